```python
import jax, jax.numpy as jnp
from jax import lax
import numpy as np

D_MODEL = 1024
BATCH = 1
SEQ = 16384
DEPTH = 2
DEC_BATCH = 128
DEC_SEQ = 4
PAST_LEN = 16384
PAGE_SIZE = 128

A_DK = 64
A_DV = 64
A_WIDTH = D_MODEL // 4
A_HEADS = A_WIDTH // A_DK
HGRN_CHUNK = 64
B_WIDTH = D_MODEL // 4
B_GROUPS = 4
CONV_WIDTH = 31
C_HEAD_DIM = 64
C_WIDTH = D_MODEL // 2
C_HEADS = C_WIDTH // C_HEAD_DIM
C_KV_HEADS = C_HEADS // 4
C_GROUP = C_HEADS // C_KV_HEADS
WINDOW = 128
ROPE_THETA = 10000.0
MIX_WIDTH = A_WIDTH + B_WIDTH + C_WIDTH
IN_COLS = 4 * A_WIDTH + 2 * B_WIDTH + C_WIDTH + 2 * C_KV_HEADS * C_HEAD_DIM
D_FF = -(-(8 * D_MODEL) // (3 * 256)) * 256
PLE_DIM = 256
EPS = 1e-6

kernel_name = 'hymba_hgrn2_conformer_swa_sink_step'


def _rmsnorm(x, g):
    xf = x.astype(jnp.float32)
    y = xf * lax.rsqrt(jnp.mean(xf * xf, axis=-1, keepdims=True) + EPS)
    return (y * g.astype(jnp.float32)).astype(x.dtype)


def _rope(x, pos):
    half = x.shape[-1] // 2
    inv = ROPE_THETA ** (-jnp.arange(half, dtype=jnp.float32) / half)
    ang = pos[:, None] * inv[None, :]
    cos = jnp.cos(ang)[:, None, :]
    sin = jnp.sin(ang)[:, None, :]
    xf = x.astype(jnp.float32)
    x1, x2 = xf[..., :half], xf[..., half:]
    return jnp.concatenate([x1 * cos - x2 * sin, x2 * cos + x1 * sin], axis=-1).astype(x.dtype)


def _hgrn_chunk(S, xs):
    q, k, v, g = xs
    C = q.shape[1]
    G = jnp.cumsum(g, axis=1)
    causal = jnp.tril(jnp.ones((C, C), dtype=bool))[None, :, :, None, None]
    decay = jnp.exp(jnp.where(causal, G[:, :, None] - G[:, None, :], -jnp.inf))
    att = jnp.einsum('bthk,btshk,bshk->bhts', q, decay, k)
    o = jnp.einsum('bhts,bshv->bthv', att, v) + jnp.einsum('bthk,bhkv->bthv', q * jnp.exp(G), S)
    G_end = G[:, -1]
    S = jnp.exp(G_end)[..., None] * S + jnp.einsum('bshk,bshv->bhkv', k * jnp.exp(G_end[:, None] - G), v)
    return S, o


def _hgrn(S0, q, k, v, g):
    B, T, H, _ = q.shape
    C = min(HGRN_CHUNK, T)
    n = T // C

    def chunks(a):
        return a.reshape(B, n, C, H, a.shape[-1]).swapaxes(0, 1)

    S, o = lax.scan(_hgrn_chunk, S0, (chunks(q), chunks(k), chunks(v), chunks(g)))
    return S, o.swapaxes(0, 1).reshape(B, T, H, -1)


def _conv_module(u, buf, w, b, ln_g, ln_b):
    B, T, Cw = u.shape
    xpad = jnp.concatenate([buf.astype(u.dtype), u], axis=1)
    y = lax.conv_general_dilated(xpad, w[:, None, :].astype(u.dtype), (1,), 'VALID',
                                 dimension_numbers=('NWC', 'WIO', 'NWC'), feature_group_count=Cw)
    y = y.astype(jnp.float32) + b.astype(jnp.float32)
    yg = y.reshape(B, T, B_GROUPS, Cw // B_GROUPS)
    mu = jnp.mean(yg, axis=-1, keepdims=True)
    var = jnp.mean(jnp.square(yg - mu), axis=-1, keepdims=True)
    y = ((yg - mu) * lax.rsqrt(var + EPS)).reshape(B, T, Cw) * ln_g.astype(jnp.float32) + ln_b.astype(jnp.float32)
    return jax.nn.silu(y).astype(u.dtype), xpad[:, -(CONV_WIDTH - 1):]


def _sink_attention(q, k, v, mask, sinks):
    s = jnp.einsum('...qhgd,...khd->...hgqk', q, k).astype(jnp.float32) * (C_HEAD_DIM ** -0.5)
    s = jnp.where(mask[..., None, None, :, :], s, -jnp.inf)
    sink = sinks.astype(jnp.float32).reshape(C_KV_HEADS, C_GROUP)[:, :, None, None]
    m = jnp.maximum(jnp.max(s, axis=-1, keepdims=True), sink)
    p = jnp.exp(s - m)
    p = p / (jnp.sum(p, axis=-1, keepdims=True) + jnp.exp(sink - m))
    return jnp.einsum('...hgqk,...khd->...qhgd', p.astype(v.dtype), v)


def _swa_prompt(q, k, v, sinks):
    B, T = q.shape[:2]
    nb = T // WINDOW
    qb = q.reshape(B, nb, WINDOW, C_KV_HEADS, C_GROUP, C_HEAD_DIM)

    def band(a):
        ab = a.reshape(B, nb, WINDOW, C_KV_HEADS, C_HEAD_DIM)
        prev = jnp.pad(ab[:, :-1], ((0, 0), (1, 0), (0, 0), (0, 0), (0, 0)))
        return jnp.concatenate([prev, ab], axis=2)

    i = jnp.arange(WINDOW)[:, None]
    j = jnp.arange(2 * WINDOW)[None, :]
    rel = WINDOW + i - j
    key_pos = jnp.arange(nb)[:, None, None] * WINDOW - WINDOW + j[None]
    mask = (rel >= 0) & (rel <= WINDOW) & (key_pos >= 0)
    o = _sink_attention(qb, band(k), band(v), mask, sinks)
    return o.reshape(B, T, C_WIDTH)


def _swa_sample(q, k, v, kbuf, vbuf, start, sinks):
    B, T = q.shape[:2]
    w_buf = kbuf.shape[1]
    k_all = jnp.concatenate([kbuf.astype(k.dtype), k], axis=1)
    v_all = jnp.concatenate([vbuf.astype(v.dtype), v], axis=1)
    q_pos = start + jnp.arange(T)
    k_pos = start - w_buf + jnp.arange(w_buf + T)
    rel = q_pos[:, None] - k_pos[None, :]
    mask = (rel >= 0) & (rel <= WINDOW)
    o = _sink_attention(q.reshape(B, T, C_KV_HEADS, C_GROUP, C_HEAD_DIM), k_all, v_all, mask, sinks)
    return o.reshape(B, T, C_WIDTH), k_all[:, -w_buf:], v_all[:, -w_buf:]


def _layer(x, p, start, s_hgrn, conv_buf, kbuf, vbuf, lb, wl):
    (w_in, a_onorm, conv_w, conv_b, conv_ln_g, conv_ln_b, q_norm, k_norm, sinks, w_out,
     norm_mix, norm_ffn, w_gate, w_up, w_down, ple_norm, w_ple_gate, w_ple_proj) = wl
    B, T, _ = x.shape
    f32 = jnp.float32
    z = _rmsnorm(x, norm_mix) @ w_in
    kv_w = C_KV_HEADS * C_HEAD_DIM
    sizes = [A_WIDTH] * 4 + [B_WIDTH] * 2 + [C_WIDTH, kv_w, kv_w]
    aq, af, ai, ag, bu, bg, cq, ck, cv = jnp.split(z, np.cumsum(sizes)[:-1].tolist(), axis=-1)

    def heads(a, d):
        return a.reshape(B, T, -1, d)

    zf = af.astype(f32)
    logf = jnp.logaddexp(jnp.log(lb), jnp.log1p(-lb) + jax.nn.log_sigmoid(zf))
    kin = (1.0 - lb) * jax.nn.sigmoid(-zf)
    if s_hgrn is None:
        s_hgrn = jnp.zeros((B, A_HEADS, A_DK, A_DV), f32)
    s_new, oa = _hgrn(s_hgrn.astype(f32), heads(aq.astype(f32), A_DK), heads(kin, A_DK),
                      heads(ai.astype(f32), A_DV), heads(logf, A_DK))
    oa = _rmsnorm(oa, a_onorm) * jax.nn.silu(heads(ag.astype(f32), A_DV))
    oa = oa.reshape(B, T, A_WIDTH).astype(x.dtype)

    u = bu * jax.nn.sigmoid(bg)
    if conv_buf is None:
        conv_buf = jnp.zeros((B, CONV_WIDTH - 1, B_WIDTH), u.dtype)
    ob, conv_new = _conv_module(u, conv_buf, conv_w, conv_b, conv_ln_g, conv_ln_b)

    pos = start + jnp.arange(T, dtype=f32)
    q = _rope(_rmsnorm(heads(cq, C_HEAD_DIM), q_norm), pos)
    k = _rope(_rmsnorm(heads(ck, C_HEAD_DIM), k_norm), pos)
    v = heads(cv, C_HEAD_DIM)
    if kbuf is None:
        oc = _swa_prompt(q, k, v, sinks)
        k_new, v_new = k[:, -WINDOW:], v[:, -WINDOW:]
    else:
        oc, k_new, v_new = _swa_sample(q, k, v, kbuf, vbuf, start, sinks)

    h = x + jnp.concatenate([oa, ob, oc], axis=-1) @ w_out
    hn = _rmsnorm(h, norm_ffn)
    h = h + (jax.nn.silu(hn @ w_gate) * (hn @ w_up)) @ w_down
    gate = jax.nn.sigmoid(_rmsnorm(h, ple_norm) @ w_ple_gate)
    h = h + gate * (p.astype(h.dtype) @ w_ple_proj)
    return h, s_new, conv_new, k_new, v_new


def setup_inputs(seed: int = 0) -> dict:
    key = jax.random.key(seed)
    ks = jax.random.split(key, 32)
    f32 = jnp.float32
    w_buf = min(WINDOW, PAST_LEN)

    def nrm(k, shape, s):
        return s * jax.random.normal(k, shape, f32)

    return {
        'x_prompt': nrm(ks[0], (BATCH, SEQ, D_MODEL), 1.0),
        'x_sample': nrm(ks[1], (DEC_BATCH, DEC_SEQ, D_MODEL), 1.0),
        'state_hgrn': nrm(ks[2], (DEPTH, DEC_BATCH, A_HEADS, A_DK, A_DV), 0.5),
        'state_conv': nrm(ks[3], (DEPTH, DEC_BATCH, CONV_WIDTH - 1, B_WIDTH), 0.5),
        'cache_swa_k': nrm(ks[4], (DEPTH, DEC_BATCH, w_buf, C_KV_HEADS, C_HEAD_DIM), 1.0),
        'cache_swa_v': nrm(ks[5], (DEPTH, DEC_BATCH, w_buf, C_KV_HEADS, C_HEAD_DIM), 1.0),
        'p_prompt': nrm(ks[6], (DEPTH, BATCH, SEQ, PLE_DIM), 1.0),
        'p_sample': nrm(ks[7], (DEPTH, DEC_BATCH, DEC_SEQ, PLE_DIM), 1.0),
        'a_lower': nrm(ks[8], (DEPTH, A_WIDTH), 1.0),
        'w_in': nrm(ks[9], (DEPTH, D_MODEL, IN_COLS), D_MODEL ** -0.5),
        'a_onorm': 1.0 + nrm(ks[10], (DEPTH, A_DV), 0.05),
        'conv_w': nrm(ks[11], (DEPTH, CONV_WIDTH, B_WIDTH), CONV_WIDTH ** -0.5),
        'conv_b': nrm(ks[12], (DEPTH, B_WIDTH), 0.01),
        'conv_ln_g': 1.0 + nrm(ks[13], (DEPTH, B_WIDTH), 0.05),
        'conv_ln_b': nrm(ks[14], (DEPTH, B_WIDTH), 0.01),
        'q_norm': 1.0 + nrm(ks[15], (DEPTH, C_HEAD_DIM), 0.05),
        'k_norm': 1.0 + nrm(ks[16], (DEPTH, C_HEAD_DIM), 0.05),
        'sinks': nrm(ks[17], (DEPTH, C_HEADS), 0.5),
        'w_out': nrm(ks[18], (DEPTH, MIX_WIDTH, D_MODEL), MIX_WIDTH ** -0.5),
        'norm_mix': 1.0 + nrm(ks[19], (DEPTH, D_MODEL), 0.05),
        'norm_ffn': 1.0 + nrm(ks[20], (DEPTH, D_MODEL), 0.05),
        'w_gate': nrm(ks[21], (DEPTH, D_MODEL, D_FF), D_MODEL ** -0.5),
        'w_up': nrm(ks[22], (DEPTH, D_MODEL, D_FF), D_MODEL ** -0.5),
        'w_down': nrm(ks[23], (DEPTH, D_FF, D_MODEL), D_FF ** -0.5),
        'ple_norm': 1.0 + nrm(ks[24], (DEPTH, D_MODEL), 0.05),
        'w_ple_gate': nrm(ks[25], (DEPTH, D_MODEL, D_MODEL), D_MODEL ** -0.5),
        'w_ple_proj': nrm(ks[26], (DEPTH, PLE_DIM, D_MODEL), PLE_DIM ** -0.5),
    }


def reference(x_prompt, x_sample, state_hgrn, state_conv, cache_swa_k, cache_swa_v, p_prompt, p_sample,
              a_lower, w_in, a_onorm, conv_w, conv_b, conv_ln_g, conv_ln_b, q_norm, k_norm, sinks, w_out,
              norm_mix, norm_ffn, w_gate, w_up, w_down, ple_norm, w_ple_gate, w_ple_proj):
    lbs = jnp.cumsum(jax.nn.softmax(a_lower.astype(jnp.float32), axis=0), axis=0)
    lbs = lbs - lbs[0:1]
    hp, hs = x_prompt, x_sample
    sp_h, sp_c, sp_k, sp_v = [], [], [], []
    ss_h, ss_c, ss_k, ss_v = [], [], [], []
    for l in range(DEPTH):
        wl = (w_in[l], a_onorm[l], conv_w[l], conv_b[l], conv_ln_g[l], conv_ln_b[l], q_norm[l], k_norm[l],
              sinks[l], w_out[l], norm_mix[l], norm_ffn[l], w_gate[l], w_up[l], w_down[l], ple_norm[l],
              w_ple_gate[l], w_ple_proj[l])
        hp, a1, a2, a3, a4 = _layer(hp, p_prompt[l], 0, None, None, None, None, lbs[l], wl)
        hs, b1, b2, b3, b4 = _layer(hs, p_sample[l], PAST_LEN, state_hgrn[l], state_conv[l],
                                    cache_swa_k[l], cache_swa_v[l], lbs[l], wl)
        sp_h.append(a1); sp_c.append(a2); sp_k.append(a3); sp_v.append(a4)
        ss_h.append(b1); ss_c.append(b2); ss_k.append(b3); ss_v.append(b4)
    return (hp, hs,
            jnp.stack(sp_h), jnp.stack(sp_c), jnp.stack(sp_k), jnp.stack(sp_v),
            jnp.stack(ss_h), jnp.stack(ss_c), jnp.stack(ss_k), jnp.stack(ss_v))
```

```python
import functools

import numpy as np
import jax
import jax.numpy as jnp
from jax import lax
from jax.experimental import pallas as pl
from jax.experimental.pallas import tpu as pltpu

f32 = jnp.float32
bf16 = jnp.bfloat16

D_MODEL = 1024
HEAD = 64
A_W = 256
B_W = 256
C_W = 512
KV_W = 128
IN_COLS = 4 * A_W + 2 * B_W + C_W + 2 * KV_W
D_FF = 2816
PLE_DIM = 256
WINDOW = 128
CONV_WIDTH = 31
ROPE_THETA = 10000.0
EPS = 1e-6

TILE = 256
CHUNK = 64
LEVELS = (32, 16, 8, 4, 2, 1)
HIST = 32
FFN_TILE = 256
VMEM_LIMIT = 52 * 1024 * 1024


def _split2(x):
    hi = x.astype(bf16)
    lo = (x - hi.astype(f32)).astype(bf16)
    return hi, lo


def _split3(x):
    hi = x.astype(bf16)
    r = x - hi.astype(f32)
    mid = r.astype(bf16)
    lo = (r - mid.astype(f32)).astype(bf16)
    return hi, mid, lo


def _seg_sum(x, e):
    hi, lo = _split2(x)
    return (jnp.dot(hi, e, preferred_element_type=f32) + jnp.dot(lo, e, preferred_element_type=f32))


def _sigmoid(x):
    return 1.0 / (1.0 + jnp.exp(-x))


def _rmsnorm_rows(x, g):
    return x * lax.rsqrt(jnp.mean(x * x, axis=-1, keepdims=True) + EPS) * g


def _head_norm(x, e, g):
    ms = _seg_sum(x * x, e) * (1.0 / HEAD)
    return x * lax.rsqrt(ms + EPS) * g


def _rope(x, cos, sin_signed):
    w = x.shape[-1]
    lane = lax.broadcasted_iota(jnp.int32, x.shape, 1)
    first = (lane % HEAD) < (HEAD // 2)
    swapped = jnp.where(first, pltpu.roll(x, w - HEAD // 2, 1), pltpu.roll(x, HEAD // 2, 1))
    return x * cos + swapped * sin_signed


def _log_forget(zf, log_lb, log_1m_lb):
    ls = jnp.minimum(zf, 0.0) - jnp.log1p(jnp.exp(-jnp.abs(zf)))
    b = log_1m_lb + ls
    return jnp.maximum(log_lb, b) + jnp.log1p(jnp.exp(-jnp.abs(log_lb - b)))


def _group_ln_silu(y, e, g, b):
    mu = _seg_sum(y, e) * (1.0 / HEAD)
    yc = y - mu
    var = _seg_sum(yc * yc, e) * (1.0 / HEAD)
    yn = yc * lax.rsqrt(var + EPS) * g + b
    return yn * _sigmoid(yn)


def _np_constants():
    lane = np.arange(256)
    e256 = (lane[:, None] // HEAD == lane[None, :] // HEAD).astype(np.float32)
    t = np.arange(TILE)
    ltri = ((t[:, None] // CHUNK == t[None, :] // CHUNK) & (t[None, :] <= t[:, None])).astype(np.float32)
    tt = np.arange(CHUNK)[:, None]
    ss = (np.arange(256) % CHUNK)[None, :]
    lvl = []
    for m in LEVELS:
        lvl.append(((tt // (2 * m) == ss // (2 * m)) & (tt % (2 * m) >= m) & (ss % (2 * m) < m)).astype(np.float32))
    lvl = np.stack(lvl)
    hm = np.stack([np.broadcast_to((lane // HEAD == h)[None, :], (CHUNK, 256)) for h in range(4)]).astype(np.float32)
    return e256, ltri, lvl, hm


def _level_ref(g, m):
    n_rows = g.shape[0]
    if 2 * m >= 8:
        g3 = g.reshape(n_rows // (2 * m), 2 * m, g.shape[1])
        return jnp.broadcast_to(g3[:, m - 1:m, :], g3.shape).reshape(g.shape)
    g3 = g.reshape(n_rows // 8, 8, g.shape[1])
    sub = lax.broadcasted_iota(jnp.int32, g3.shape, 1)
    out = None
    for blk in range(8 // (2 * m)):
        r = blk * 2 * m + m - 1
        piece = jnp.broadcast_to(g3[:, r:r + 1, :], g3.shape)
        out = piece if out is None else jnp.where(sub >= blk * 2 * m, piece, out)
    return out.reshape(g.shape)


def _hgrn_tile(q, zf, v, gate, lbp_ref, aon_ref, st_ref, e, ltri_ref, lvl_ref, hm_ref):
    log_lb, log_1m_lb, one_m_lb = lbp_ref[0:1, :], lbp_ref[1:2, :], lbp_ref[2:3, :]
    logf = _log_forget(zf, log_lb, log_1m_lb)
    kin = one_m_lb * _sigmoid(-zf)
    g3 = jnp.dot(ltri_ref[...], jnp.concatenate(_split3(logf), axis=1), preferred_element_type=f32)
    g = g3[:, 0:A_W] + g3[:, A_W:2 * A_W] + g3[:, 2 * A_W:3 * A_W]

    q_levels, k_levels = [], []
    for m in LEVELS:
        w = jnp.exp(-jnp.abs(g - _level_ref(g, m)))
        q_levels.append((q * w).astype(bf16))
        k_levels.append((kin * w).astype(bf16))
    q_inter = (q * jnp.exp(g)).astype(bf16)
    v_b = v.astype(bf16)
    diag = _seg_sum(q * kin, e)

    outs = []
    for c in range(TILE // CHUNK):
        r0, r1 = c * CHUNK, (c + 1) * CHUNK
        att = jnp.zeros((CHUNK, 4 * CHUNK), f32)
        for li in range(len(LEVELS)):
            kc = k_levels[li][r0:r1]
            rhs = jnp.concatenate([kc * hm_ref[h] for h in range(4)], axis=0)
            a = lax.dot_general(q_levels[li][r0:r1], rhs, (((1,), (1,)), ((), ())), preferred_element_type=f32)
            att = att + a * lvl_ref[li]
        vc = v_b[r0:r1]
        v_bd = jnp.concatenate([vc * hm_ref[h] for h in range(4)], axis=0)
        o = jnp.dot(att.astype(bf16), v_bd, preferred_element_type=f32)
        st = st_ref[...]
        o = o + lax.dot_general(q_inter[r0:r1], st.astype(bf16), (((1,), (1,)), ((), ())),
                                preferred_element_type=f32)
        g_end = g[r1 - 1:r1, :]
        k_hat = (kin[r0:r1] * jnp.exp(g_end - g[r0:r1])).astype(bf16)
        upd = lax.dot_general(vc, k_hat, (((0,), (0,)), ((), ())), preferred_element_type=f32)
        st_ref[...] = st * jnp.exp(g_end) + upd * e.astype(f32)
        outs.append(o)
    o = jnp.concatenate(outs, axis=0) + diag * v
    return _head_norm(o, e, aon_ref[...]) * (gate * _sigmoid(gate))


def _conv_tile(u, hist_ref, cw_ref, cvec_ref, e):
    n = u.shape[0]
    hist_ref[HIST:HIST + n, :] = u
    acc = jnp.zeros_like(u)
    lo = HIST - (CONV_WIDTH - 1)
    for r in range(8):
        offs = [o for o in range(lo, lo + CONV_WIDTH) if o % 8 == r]
        span = offs[-1] - r + n
        base = hist_ref[r:r + span, :]
        for o in offs:
            acc = acc + base[o - r:o - r + n] * cw_ref[o - lo:o - lo + 1, :]
    y = acc + cvec_ref[0:1, :]
    return _group_ln_silu(y, e, cvec_ref[1:2, :], cvec_ref[2:3, :])


def _attn_block(q_rows, kcat, vcat, sink_ref, first_key):
    lane = lax.broadcasted_iota(jnp.int32, (2 * WINDOW, KV_W), 1)
    lo_f = (lane < HEAD)
    row = lax.broadcasted_iota(jnp.int32, (4 * WINDOW, 2 * WINDOW), 0) % WINDOW
    col = lax.broadcasted_iota(jnp.int32, (4 * WINDOW, 2 * WINDOW), 1)
    valid = (col >= row) & (col <= row + WINDOW) & (col >= first_key)
    qlane = lax.broadcasted_iota(jnp.int32, (WINDOW, KV_W), 1)
    outs = []
    for g in range(2):
        k_roll = pltpu.roll(kcat, HEAD, 1)
        v_roll = pltpu.roll(vcat, HEAD, 1)
        keep = lo_f if g == 0 else jnp.logical_not(lo_f)
        k_dup = jnp.where(keep, kcat, k_roll).astype(bf16)
        v_dup = jnp.where(keep, vcat, v_roll)
        v_lo = jnp.where(lo_f, v_dup, 0.0).astype(bf16)
        v_hi = jnp.where(lo_f, 0.0, v_dup).astype(bf16)
        parts, sinks = [], []
        for p in (2 * g, 2 * g + 1):
            q2 = q_rows[:, p * KV_W:(p + 1) * KV_W]
            parts.append(jnp.where(qlane < HEAD, q2, 0.0))
            parts.append(jnp.where(qlane < HEAD, 0.0, q2))
            sinks.append(jnp.full((WINDOW, 1), sink_ref[2 * p], f32))
            sinks.append(jnp.full((WINDOW, 1), sink_ref[2 * p + 1], f32))
        lhs = jnp.concatenate(parts, axis=0).astype(bf16)
        sink = jnp.concatenate(sinks, axis=0)
        s = lax.dot_general(lhs, k_dup, (((1,), (1,)), ((), ())), preferred_element_type=f32)
        s = jnp.where(valid, s, -jnp.inf)
        mx = jnp.maximum(jnp.max(s, axis=-1, keepdims=True), sink)
        pr = jnp.exp(s - mx)
        den = jnp.sum(pr, axis=-1, keepdims=True) + jnp.exp(sink - mx)
        pr = (pr / den).astype(bf16)
        for j in range(2):
            outs.append(jnp.dot(pr[(2 * j) * WINDOW:(2 * j + 1) * WINDOW], v_lo, preferred_element_type=f32)
                        + jnp.dot(pr[(2 * j + 1) * WINDOW:(2 * j + 2) * WINDOW], v_hi, preferred_element_type=f32))
    return jnp.concatenate(outs, axis=1)


def _mixer_kernel(x_ref, cos_ref, sin_ref, win_ref, wout_ref, nmix_ref, lbp_ref, aon_ref, cw_ref, cvec_ref,
                  qk_ref, sink_ref, e_ref, ltri_ref, lvl_ref, hm_ref,
                  h_ref, st_ref, conv_ref, kn_ref, vn_ref, z_ref, hist_ref):
    i = pl.program_id(0)

    @pl.when(i == 0)
    def _():
        st_ref[...] = jnp.zeros_like(st_ref)
        hist_ref[0:HIST, :] = jnp.zeros((HIST, B_W), f32)
        kn_ref[...] = jnp.zeros_like(kn_ref)
        vn_ref[...] = jnp.zeros_like(vn_ref)

    x = x_ref[...]
    xn = _rmsnorm_rows(x, nmix_ref[...])
    z_ref[...] = jnp.dot(xn.astype(bf16), win_ref[...], preferred_element_type=f32)
    e = e_ref[...]

    oa = _hgrn_tile(z_ref[:, 0:A_W], z_ref[:, A_W:2 * A_W], z_ref[:, 2 * A_W:3 * A_W], z_ref[:, 3 * A_W:4 * A_W],
                    lbp_ref, aon_ref, st_ref, e, ltri_ref, lvl_ref, hm_ref)

    c0 = 4 * A_W
    u = z_ref[:, c0:c0 + B_W] * _sigmoid(z_ref[:, c0 + B_W:c0 + 2 * B_W])
    ob = _conv_tile(u, hist_ref, cw_ref, cvec_ref, e)
    conv_ref[...] = hist_ref[TILE + HIST - (CONV_WIDTH - 1):TILE + HIST, :]
    hist_ref[0:HIST, :] = hist_ref[TILE:TILE + HIST, :]

    c1 = c0 + 2 * B_W
    cos, sin = cos_ref[...], sin_ref[...]
    qn = jnp.concatenate([_head_norm(z_ref[:, c1 + j * 256:c1 + (j + 1) * 256], e,
                                     jnp.concatenate([qk_ref[0:1, :]] * 2, axis=1)) for j in range(2)], axis=1)
    q = _rope(qn, jnp.concatenate([cos] * 4, axis=1), jnp.concatenate([sin] * 4, axis=1)) * (HEAD ** -0.5)
    kk = _rope(_head_norm(z_ref[:, c1 + C_W:c1 + C_W + KV_W], e[0:KV_W, 0:KV_W], qk_ref[1:2, :]), cos, sin)
    vv = z_ref[:, c1 + C_W + KV_W:c1 + C_W + 2 * KV_W]
    kcat = jnp.concatenate([kn_ref[...], kk], axis=0)
    vcat = jnp.concatenate([vn_ref[...], vv], axis=0)
    oc = []
    for b in range(TILE // WINDOW):
        first_key = jnp.where(i > 0, 0, WINDOW) if b == 0 else 0
        oc.append(_attn_block(q[b * WINDOW:(b + 1) * WINDOW], kcat[b * WINDOW:(b + 2) * WINDOW],
                              vcat[b * WINDOW:(b + 2) * WINDOW], sink_ref, first_key))
    oc = jnp.concatenate(oc, axis=0)
    kn_ref[...] = kk[TILE - WINDOW:TILE]
    vn_ref[...] = vv[TILE - WINDOW:TILE]

    mix = jnp.concatenate([oa, ob, oc], axis=1).astype(bf16)
    h_ref[...] = x + jnp.dot(mix, wout_ref[...], preferred_element_type=f32)


def _const_spec(shape):
    nd = len(shape)
    return pl.BlockSpec(shape, lambda i, _n=nd: (0,) * _n)


def _prompt_mixer(x, cos, sin, wl, consts):
    t_len = x.shape[0]
    e256, ltri, lvl, hm = consts
    row_spec = lambda w: pl.BlockSpec((TILE, w), lambda i: (i, 0))
    smalls = [wl['nmix'], wl['lbp'], wl['aon'], wl['cw'], wl['cvec'], wl['qk']]
    in_specs = ([row_spec(D_MODEL), row_spec(KV_W), row_spec(KV_W), _const_spec(wl['w_in'].shape),
                 _const_spec(wl['w_out'].shape)] + [_const_spec(a.shape) for a in smalls]
                + [pl.BlockSpec(memory_space=pltpu.SMEM)]
                + [_const_spec(a.shape) for a in (e256, ltri, lvl, hm)])
    out_shape = (jax.ShapeDtypeStruct((t_len, D_MODEL), f32),
                 jax.ShapeDtypeStruct((A_W, A_W), f32),
                 jax.ShapeDtypeStruct((CONV_WIDTH - 1, B_W), f32),
                 jax.ShapeDtypeStruct((WINDOW, KV_W), f32),
                 jax.ShapeDtypeStruct((WINDOW, KV_W), f32))
    out_specs = (row_spec(D_MODEL), _const_spec((A_W, A_W)), _const_spec((CONV_WIDTH - 1, B_W)),
                 _const_spec((WINDOW, KV_W)), _const_spec((WINDOW, KV_W)))
    return pl.pallas_call(
        _mixer_kernel, grid=(t_len // TILE,), in_specs=in_specs, out_specs=out_specs, out_shape=out_shape,
        scratch_shapes=[pltpu.VMEM((TILE, IN_COLS), f32), pltpu.VMEM((TILE + HIST, B_W), f32)],
        compiler_params=pltpu.CompilerParams(dimension_semantics=("arbitrary",), vmem_limit_bytes=VMEM_LIMIT),
        name="prompt_mixer",
    )(x, cos, sin, wl['w_in'], wl['w_out'], *smalls, wl['sinks'], e256, ltri, lvl, hm)


def _ffn_kernel(h_ref, p_ref, wg_ref, wu_ref, wd_ref, wpg_ref, wpp_ref, norms_ref, y_ref):
    h = h_ref[...]
    hn = _rmsnorm_rows(h, norms_ref[0:1, :]).astype(bf16)
    gt = jnp.dot(hn, wg_ref[...], preferred_element_type=f32)
    up = jnp.dot(hn, wu_ref[...], preferred_element_type=f32)
    act = (gt * _sigmoid(gt) * up).astype(bf16)
    h = h + jnp.dot(act, wd_ref[...], preferred_element_type=f32)
    pn = _rmsnorm_rows(h, norms_ref[1:2, :]).astype(bf16)
    gate = _sigmoid(jnp.dot(pn, wpg_ref[...], preferred_element_type=f32))
    y_ref[...] = h + gate * jnp.dot(p_ref[...].astype(bf16), wpp_ref[...], preferred_element_type=f32)


def _ffn(h, p, wl):
    n = h.shape[0]
    tile = min(FFN_TILE, n)
    row_spec = lambda w: pl.BlockSpec((tile, w), lambda i: (i, 0))
    single = lambda a: pl.BlockSpec(a.shape, lambda i: (0, 0), pipeline_mode=pl.Buffered(1))
    ws = [wl['w_gate'], wl['w_up'], wl['w_down'], wl['w_ple_gate'], wl['w_ple_proj']]
    return pl.pallas_call(
        _ffn_kernel, grid=(n // tile,),
        in_specs=[row_spec(D_MODEL), row_spec(PLE_DIM)] + [single(a) for a in ws] + [_const_spec(wl['fnorms'].shape)],
        out_specs=row_spec(D_MODEL), out_shape=jax.ShapeDtypeStruct((n, D_MODEL), f32),
        compiler_params=pltpu.CompilerParams(dimension_semantics=("arbitrary",), vmem_limit_bytes=VMEM_LIMIT),
        name="ffn_ple",
    )(h, p, *ws, wl['fnorms'])


def _sample_in_kernel(x_ref, cos_ref, sin_ref, win_ref, nmix_ref, qk_ref, e_ref, z_ref, q_ref, k_ref):
    xn = _rmsnorm_rows(x_ref[...], nmix_ref[...])
    z = jnp.dot(xn.astype(bf16), win_ref[...], preferred_element_type=f32)
    z_ref[...] = z
    e = e_ref[...]
    c1 = 4 * A_W + 2 * B_W
    cos, sin = cos_ref[...], sin_ref[...]
    qn = jnp.concatenate([_head_norm(z[:, c1 + j * 256:c1 + (j + 1) * 256], e,
                                     jnp.concatenate([qk_ref[0:1, :]] * 2, axis=1)) for j in range(2)], axis=1)
    q_ref[...] = _rope(qn, jnp.concatenate([cos] * 4, axis=1), jnp.concatenate([sin] * 4, axis=1)) * (HEAD ** -0.5)
    k_ref[...] = _rope(_head_norm(z[:, c1 + C_W:c1 + C_W + KV_W], e[0:KV_W, 0:KV_W], qk_ref[1:2, :]), cos, sin)


def _sample_in(x, cos, sin, wl, e256):
    n = x.shape[0]
    args = (x, cos, sin, wl['w_in'], wl['nmix'], wl['qk'], e256)
    return pl.pallas_call(
        _sample_in_kernel, grid=(1,), in_specs=[_const_spec(a.shape) for a in args],
        out_specs=(_const_spec((n, IN_COLS)), _const_spec((n, C_W)), _const_spec((n, KV_W))),
        out_shape=(jax.ShapeDtypeStruct((n, IN_COLS), f32), jax.ShapeDtypeStruct((n, C_W), f32),
                   jax.ShapeDtypeStruct((n, KV_W), f32)),
        compiler_params=pltpu.CompilerParams(dimension_semantics=("arbitrary",), vmem_limit_bytes=VMEM_LIMIT),
        name="sample_in",
    )(*args)


def _sample_hgrn_kernel(n_tok, q_ref, f_ref, v_ref, s_ref, lbp_ref, o_ref, so_ref, st_scr, qt_scr, ft_scr, kt_scr,
                        vt_scr, ot_scr):
    n_seq = s_ref.shape[0]
    st_scr[...] = s_ref[...].T
    for t in range(n_tok):
        zf = f_ref[pl.ds(t, n_seq, stride=n_tok), :]
        logf = _log_forget(zf, lbp_ref[0:1, :], lbp_ref[1:2, :])
        ft_scr[t] = jnp.exp(logf).T
        kt_scr[t] = (lbp_ref[2:3, :] * _sigmoid(-zf)).T
        qt_scr[t] = q_ref[pl.ds(t, n_seq, stride=n_tok), :].T
        vt_scr[t] = v_ref[pl.ds(t, n_seq, stride=n_tok), :].T
        ot_scr[t] = jnp.zeros((2 * HEAD, n_seq), f32)

    def body(hk, carry):
        r0 = pl.multiple_of(hk * HEAD, HEAD)
        v0 = pl.multiple_of((hk // HEAD) * HEAD, HEAD)
        blk = st_scr[pl.ds(r0, HEAD), :]
        for t in range(n_tok):
            blk = blk * ft_scr[t, pl.ds(hk, 1), :] + kt_scr[t, pl.ds(hk, 1), :] * vt_scr[t, pl.ds(v0, HEAD), :]
            ot_scr[t, pl.ds(v0, HEAD), :] = ot_scr[t, pl.ds(v0, HEAD), :] + qt_scr[t, pl.ds(hk, 1), :] * blk
        st_scr[pl.ds(r0, HEAD), :] = blk
        return carry

    lax.fori_loop(0, 2 * HEAD, body, 0)
    so_ref[...] = st_scr[...].T
    for t in range(n_tok):
        o_ref[pl.ds(t, n_seq, stride=n_tok), :] = ot_scr[t].T


def _sample_hgrn(z, state, lbp, n_tok):
    n = z.shape[0]
    n_seq = state.shape[0]
    blk = 2 * HEAD * HEAD
    col = lambda j0: pl.BlockSpec((n, 2 * HEAD), lambda i, _j=j0: (0, _j + i))
    return pl.pallas_call(
        functools.partial(_sample_hgrn_kernel, n_tok), grid=(2,),
        in_specs=[col(0), col(2), col(4), pl.BlockSpec((n_seq, blk), lambda i: (0, i)),
                  pl.BlockSpec((3, 2 * HEAD), lambda i: (0, i))],
        out_specs=(pl.BlockSpec((n, 2 * HEAD), lambda i: (0, i)), pl.BlockSpec((n_seq, blk), lambda i: (0, i))),
        out_shape=(jax.ShapeDtypeStruct((n, A_W), f32), jax.ShapeDtypeStruct(state.shape, f32)),
        scratch_shapes=[pltpu.VMEM((blk, n_seq), f32)] + [pltpu.VMEM((n_tok, 2 * HEAD, n_seq), f32)] * 5,
        compiler_params=pltpu.CompilerParams(dimension_semantics=("arbitrary",), vmem_limit_bytes=VMEM_LIMIT),
        name="sample_hgrn",
    )(z, z, z, state, lbp)


def _sample_conv_kernel(n_tok, u0_ref, u1_ref, g0_ref, g1_ref, buf_ref, cw_ref, cvec_ref, e_ref, ob0_ref, ob1_ref,
                        new_ref):
    n_seq = buf_ref.shape[0]
    n_hist = CONV_WIDTH - 1
    xs = [buf_ref[:, j, :] for j in range(n_hist)]
    for t in range(n_tok):
        rows = pl.ds(t, n_seq, stride=n_tok)
        u = jnp.concatenate([u0_ref[rows, :], u1_ref[rows, :]], axis=1)
        g = jnp.concatenate([g0_ref[rows, :], g1_ref[rows, :]], axis=1)
        xs.append(u * _sigmoid(g))
    for t in range(n_tok):
        acc = jnp.zeros((n_seq, B_W), f32)
        for j in range(CONV_WIDTH):
            acc = acc + xs[t + j] * cw_ref[j:j + 1, :]
        y = acc + cvec_ref[0:1, :]
        ob = _group_ln_silu(y, e_ref[...], cvec_ref[1:2, :], cvec_ref[2:3, :])
        ob0_ref[pl.ds(t, n_seq, stride=n_tok), :] = ob[:, 0:KV_W]
        ob1_ref[pl.ds(t, n_seq, stride=n_tok), :] = ob[:, KV_W:B_W]
    for j in range(n_hist):
        new_ref[:, j, :] = xs[j + n_tok]


def _sample_conv(z, buf, wl, e256, n_tok):
    n = z.shape[0]
    col = lambda j: pl.BlockSpec((n, KV_W), lambda i, _j=j: (0, _j))
    half = jax.ShapeDtypeStruct((n, KV_W), f32)
    return pl.pallas_call(
        functools.partial(_sample_conv_kernel, n_tok), grid=(1,),
        in_specs=[col(8), col(9), col(10), col(11), _const_spec(buf.shape), _const_spec(wl['cw'].shape),
                  _const_spec(wl['cvec'].shape), _const_spec(e256.shape)],
        out_specs=(_const_spec((n, KV_W)), _const_spec((n, KV_W)), _const_spec(buf.shape)),
        out_shape=(half, half, jax.ShapeDtypeStruct(buf.shape, f32)),
        compiler_params=pltpu.CompilerParams(dimension_semantics=("arbitrary",), vmem_limit_bytes=VMEM_LIMIT),
        name="sample_conv",
    )(z, z, z, z, buf, wl['cw'], wl['cvec'], e256)


SEQ_BLOCK = 8


def _sample_attn_kernel(n_tok, q_ref, k_ref, v_ref, kc_ref, vc_ref, sink_ref, o_ref, kn_ref, vn_ref):
    w_buf = kc_ref.shape[1]
    n_keys = w_buf + n_tok
    lane = lax.broadcasted_iota(jnp.int32, (n_keys, KV_W), 1)
    lo_f = lane < HEAD
    qlane = lax.broadcasted_iota(jnp.int32, (n_tok, KV_W), 1)
    row_t = lax.broadcasted_iota(jnp.int32, (4 * n_tok, n_keys), 0) % n_tok
    col = lax.broadcasted_iota(jnp.int32, (4 * n_tok, n_keys), 1)
    rel = row_t + w_buf - col
    valid = (rel >= 0) & (rel <= WINDOW)
    olane = lax.broadcasted_iota(jnp.int32, (n_tok, KV_W), 1)
    for b in range(SEQ_BLOCK):
        rows = slice(b * n_tok, (b + 1) * n_tok)
        k_all = jnp.concatenate([kc_ref[b], k_ref[rows, :]], axis=0)
        v_all = jnp.concatenate([vc_ref[b], v_ref[rows, :]], axis=0)
        kn_ref[b] = k_all[n_tok:, :]
        vn_ref[b] = v_all[n_tok:, :]
        k_roll = pltpu.roll(k_all, HEAD, 1)
        v_roll = pltpu.roll(v_all, HEAD, 1)
        qb = q_ref[rows, :]
        pieces = []
        for g in range(2):
            keep = lo_f if g == 0 else jnp.logical_not(lo_f)
            k_dup = jnp.where(keep, k_all, k_roll).astype(bf16)
            v_dup = jnp.where(keep, v_all, v_roll).astype(bf16)
            parts, sinks = [], []
            for p in (2 * g, 2 * g + 1):
                q2 = qb[:, p * KV_W:(p + 1) * KV_W]
                parts.append(jnp.where(qlane < HEAD, q2, 0.0))
                parts.append(jnp.where(qlane < HEAD, 0.0, q2))
                sinks.append(jnp.full((n_tok, 1), sink_ref[2 * p], f32))
                sinks.append(jnp.full((n_tok, 1), sink_ref[2 * p + 1], f32))
            lhs = jnp.concatenate(parts, axis=0).astype(bf16)
            sink = jnp.concatenate(sinks, axis=0)
            s = lax.dot_general(lhs, k_dup, (((1,), (1,)), ((), ())), preferred_element_type=f32)
            s = jnp.where(valid, s, -jnp.inf)
            mx = jnp.maximum(jnp.max(s, axis=-1, keepdims=True), sink)
            pr = jnp.exp(s - mx)
            den = jnp.sum(pr, axis=-1, keepdims=True) + jnp.exp(sink - mx)
            o = jnp.dot((pr / den).astype(bf16), v_dup, preferred_element_type=f32)
            for j in range(2):
                pieces.append(jnp.where(olane < HEAD, o[(2 * j) * n_tok:(2 * j + 1) * n_tok],
                                        o[(2 * j + 1) * n_tok:(2 * j + 2) * n_tok]))
        o_ref[rows, :] = jnp.concatenate(pieces, axis=1)


def _sample_attn(q, k, z, kc, vc, sinks, n_tok):
    n = q.shape[0]
    n_seq = kc.shape[0]
    rows = SEQ_BLOCK * n_tok
    v_col = (4 * A_W + 2 * B_W + C_W + KV_W) // KV_W
    cache_spec = pl.BlockSpec((SEQ_BLOCK,) + kc.shape[1:], lambda i: (i, 0, 0))
    return pl.pallas_call(
        functools.partial(_sample_attn_kernel, n_tok), grid=(n_seq // SEQ_BLOCK,),
        in_specs=[pl.BlockSpec((rows, C_W), lambda i: (i, 0)), pl.BlockSpec((rows, KV_W), lambda i: (i, 0)),
                  pl.BlockSpec((rows, KV_W), lambda i: (i, v_col)), cache_spec, cache_spec,
                  pl.BlockSpec(memory_space=pltpu.SMEM)],
        out_specs=(pl.BlockSpec((rows, C_W), lambda i: (i, 0)), cache_spec, cache_spec),
        out_shape=(jax.ShapeDtypeStruct((n, C_W), f32), jax.ShapeDtypeStruct(kc.shape, f32),
                   jax.ShapeDtypeStruct(vc.shape, f32)),
        compiler_params=pltpu.CompilerParams(dimension_semantics=("arbitrary",), vmem_limit_bytes=VMEM_LIMIT),
        name="sample_attn",
    )(q, k, z, kc, vc, sinks)


def _sample_out_kernel(x_ref, oa_ref, g_ref, ob0_ref, ob1_ref, oc_ref, wout_ref, aon_ref, e_ref, h_ref):
    gate = g_ref[...]
    oa = _head_norm(oa_ref[...], e_ref[...], aon_ref[...]) * (gate * _sigmoid(gate))
    mix = jnp.concatenate([oa, ob0_ref[...], ob1_ref[...], oc_ref[...]], axis=1).astype(bf16)
    h_ref[...] = x_ref[...] + jnp.dot(mix, wout_ref[...], preferred_element_type=f32)


def _sample_out(x, oa, z, ob0, ob1, oc, wl, e256):
    n = x.shape[0]
    gate_spec = pl.BlockSpec((n, A_W), lambda i: (0, 3))
    args = (x, oa, z, ob0, ob1, oc, wl['w_out'], wl['aon'], e256)
    specs = [_const_spec(a.shape) for a in args]
    specs[2] = gate_spec
    return pl.pallas_call(
        _sample_out_kernel, grid=(1,), in_specs=specs, out_specs=_const_spec((n, D_MODEL)),
        out_shape=jax.ShapeDtypeStruct((n, D_MODEL), f32),
        compiler_params=pltpu.CompilerParams(dimension_semantics=("arbitrary",), vmem_limit_bytes=VMEM_LIMIT),
        name="sample_out",
    )(*args)


def _rope_tables(pos):
    half = HEAD // 2
    inv = ROPE_THETA ** (-jnp.arange(half, dtype=f32) / half)
    ang = pos[:, None] * inv[None, :]
    cos, sin = jnp.cos(ang), jnp.sin(ang)
    return jnp.tile(cos, (1, 4)), jnp.concatenate([-sin, sin, -sin, sin], axis=1)


def _layer_weights(l, lbs, w_in, a_onorm, conv_w, conv_b, conv_ln_g, conv_ln_b, q_norm, k_norm, sinks, w_out,
                   norm_mix, norm_ffn, w_gate, w_up, w_down, ple_norm, w_ple_gate, w_ple_proj):
    lb = lbs[l]
    return dict(
        w_in=w_in[l].astype(bf16), w_out=w_out[l].astype(bf16), w_gate=w_gate[l].astype(bf16),
        w_up=w_up[l].astype(bf16), w_down=w_down[l].astype(bf16), w_ple_gate=w_ple_gate[l].astype(bf16),
        w_ple_proj=w_ple_proj[l].astype(bf16),
        nmix=norm_mix[l][None, :],
        lbp=jnp.stack([jnp.log(lb), jnp.log1p(-lb), 1.0 - lb]),
        aon=jnp.tile(a_onorm[l], 4)[None, :],
        cw=conv_w[l],
        cvec=jnp.stack([conv_b[l], conv_ln_g[l], conv_ln_b[l]]),
        qk=jnp.stack([jnp.tile(q_norm[l], 2), jnp.tile(k_norm[l], 2)]),
        sinks=sinks[l],
        fnorms=jnp.stack([norm_ffn[l], ple_norm[l]]),
    )


def kernel(x_prompt, x_sample, state_hgrn, state_conv, cache_swa_k, cache_swa_v, p_prompt, p_sample, a_lower, w_in, a_onorm, conv_w, conv_b, conv_ln_g, conv_ln_b, q_norm, k_norm, sinks, w_out, norm_mix, norm_ffn, w_gate, w_up, w_down, ple_norm, w_ple_gate, w_ple_proj):
    depth = w_in.shape[0]
    t_len = x_prompt.shape[1]
    n_seq, n_tok = x_sample.shape[0], x_sample.shape[1]
    past_len = 16384
    assert x_prompt.shape[0] == 1 and t_len % TILE == 0

    lbs = jnp.cumsum(jax.nn.softmax(a_lower.astype(f32), axis=0), axis=0)
    lbs = lbs - lbs[0:1]
    consts_np = _np_constants()
    e256 = jnp.asarray(consts_np[0], bf16)
    consts = (e256, jnp.asarray(consts_np[1], bf16), jnp.asarray(consts_np[2], f32), jnp.asarray(consts_np[3], bf16))

    cos_p, sin_p = _rope_tables(jnp.arange(t_len, dtype=f32))
    cos_s, sin_s = _rope_tables(past_len + jnp.arange(n_tok, dtype=f32))
    cos_s, sin_s = jnp.tile(cos_s, (n_seq, 1)), jnp.tile(sin_s, (n_seq, 1))

    hp = x_prompt[0]
    hs = x_sample.reshape(n_seq * n_tok, D_MODEL)
    outs = [[] for _ in range(8)]
    for l in range(depth):
        wl = _layer_weights(l, lbs, w_in, a_onorm, conv_w, conv_b, conv_ln_g, conv_ln_b, q_norm, k_norm, sinks,
                            w_out, norm_mix, norm_ffn, w_gate, w_up, w_down, ple_norm, w_ple_gate, w_ple_proj)
        h_mid, st, conv_new, k_new, v_new = _prompt_mixer(hp, cos_p, sin_p, wl, consts)
        hp = _ffn(h_mid, p_prompt[l, 0], wl)
        s_new = jnp.stack([st[h * HEAD:(h + 1) * HEAD, h * HEAD:(h + 1) * HEAD].T for h in range(4)])
        outs[0].append(s_new[None])
        outs[1].append(conv_new[None])
        outs[2].append(k_new.reshape(1, WINDOW, 2, HEAD))
        outs[3].append(v_new.reshape(1, WINDOW, 2, HEAD))
        z, q, k = _sample_in(hs, cos_s, sin_s, wl, e256)
        oa, s_s = _sample_hgrn(z, state_hgrn[l].reshape(n_seq, -1), wl['lbp'], n_tok)
        ob0, ob1, conv_s = _sample_conv(z, state_conv[l], wl, e256, n_tok)
        w_buf = cache_swa_k.shape[2]
        oc, k_s, v_s = _sample_attn(q, k, z, cache_swa_k[l].reshape(n_seq, w_buf, KV_W),
                                    cache_swa_v[l].reshape(n_seq, w_buf, KV_W), wl['sinks'], n_tok)
        h_mid_s = _sample_out(hs, oa, z, ob0, ob1, oc, wl, e256)
        hs = _ffn(h_mid_s, p_sample[l].reshape(n_seq * n_tok, PLE_DIM), wl)
        outs[4].append(s_s.reshape(n_seq, 4, HEAD, HEAD))
        outs[5].append(conv_s)
        outs[6].append(k_s.reshape(n_seq, w_buf, 2, HEAD))
        outs[7].append(v_s.reshape(n_seq, w_buf, 2, HEAD))
    return (hp[None], hs.reshape(n_seq, n_tok, D_MODEL)) + tuple(jnp.stack(o) for o in outs)
```

```python
import functools

import numpy as np
import jax
import jax.numpy as jnp
from jax import lax
from jax.experimental import pallas as pl
from jax.experimental.pallas import tpu as pltpu

f32 = jnp.float32
bf16 = jnp.bfloat16

D_MODEL = 1024
HEAD = 64
A_W = 256
B_W = 256
C_W = 512
KV_W = 128
IN_COLS = 4 * A_W + 2 * B_W + C_W + 2 * KV_W
D_FF = 2816
PLE_DIM = 256
WINDOW = 128
CONV_WIDTH = 31
ROPE_THETA = 10000.0
EPS = 1e-6

TILE = 256
CHUNK = 64
LEVELS = (32, 16, 8, 4, 2, 1)
SAFE_EXPONENT = 80.0
HIST = 32
FFN_TILE = 256
VMEM_LIMIT = 52 * 1024 * 1024


def _split2(x):
    hi = x.astype(bf16)
    lo = (x - hi.astype(f32)).astype(bf16)
    return hi, lo


def _split3(x):
    hi = x.astype(bf16)
    r = x - hi.astype(f32)
    mid = r.astype(bf16)
    lo = (r - mid.astype(f32)).astype(bf16)
    return hi, mid, lo


def _seg_sum(x, e):
    hi, lo = _split2(x)
    return (jnp.dot(hi, e, preferred_element_type=f32) + jnp.dot(lo, e, preferred_element_type=f32))


def _sigmoid(x):
    return 1.0 / (1.0 + jnp.exp(-x))


def _rmsnorm_rows(x, g):
    return x * lax.rsqrt(jnp.mean(x * x, axis=-1, keepdims=True) + EPS) * g


def _head_norm(x, e, g):
    ms = _seg_sum(x * x, e) * (1.0 / HEAD)
    return x * lax.rsqrt(ms + EPS) * g


def _rope(x, cos, sin_signed):
    w = x.shape[-1]
    lane = lax.broadcasted_iota(jnp.int32, x.shape, 1)
    first = (lane % HEAD) < (HEAD // 2)
    swapped = jnp.where(first, pltpu.roll(x, w - HEAD // 2, 1), pltpu.roll(x, HEAD // 2, 1))
    return x * cos + swapped * sin_signed


def _log_forget(zf, log_lb, log_1m_lb):
    ls = jnp.minimum(zf, 0.0) - jnp.log(1.0 + jnp.exp(-jnp.abs(zf)))
    b = log_1m_lb + ls
    return jnp.maximum(log_lb, b) + jnp.log(1.0 + jnp.exp(-jnp.abs(log_lb - b))), ls


def _group_ln_silu(y, e, g, b):
    mu = _seg_sum(y, e) * (1.0 / HEAD)
    yc = y - mu
    var = _seg_sum(yc * yc, e) * (1.0 / HEAD)
    yn = yc * lax.rsqrt(var + EPS) * g + b
    return yn * _sigmoid(yn)


def _np_constants():
    lane = np.arange(256)
    e256 = (lane[:, None] // HEAD == lane[None, :] // HEAD).astype(np.float32)
    t = np.arange(TILE)
    ltri = ((t[:, None] // CHUNK == t[None, :] // CHUNK) & (t[None, :] <= t[:, None])).astype(np.float32)
    tt = np.arange(CHUNK)[:, None]
    ss = (np.arange(256) % CHUNK)[None, :]
    lvl = []
    for m in LEVELS:
        lvl.append(((tt // (2 * m) == ss // (2 * m)) & (tt % (2 * m) >= m) & (ss % (2 * m) < m)).astype(np.float32))
    lvl.append((ss <= tt).astype(np.float32))
    lvl = np.stack(lvl)
    hm = np.stack([np.broadcast_to((lane // HEAD == h)[None, :], (CHUNK, 256)) for h in range(4)]).astype(np.float32)
    keys = np.arange(2 * WINDOW)[:, None]
    qry = np.arange(4 * WINDOW)[None, :] % WINDOW
    band = (keys >= qry) & (keys <= qry + WINDOW)
    bias = np.where(np.stack([band & (keys >= WINDOW), band]), 0.0, -np.inf).astype(np.float32)
    return e256, ltri, lvl, hm, bias


def _level_ref(g, m):
    n_rows = g.shape[0]
    if 2 * m >= 8:
        g3 = g.reshape(n_rows // (2 * m), 2 * m, g.shape[1])
        return jnp.broadcast_to(g3[:, m - 1:m, :], g3.shape).reshape(g.shape)
    g3 = g.reshape(n_rows // 8, 8, g.shape[1])
    sub = lax.broadcasted_iota(jnp.int32, g3.shape, 1)
    out = None
    for blk in range(8 // (2 * m)):
        r = blk * 2 * m + m - 1
        piece = jnp.broadcast_to(g3[:, r:r + 1, :], g3.shape)
        out = piece if out is None else jnp.where(sub >= blk * 2 * m, piece, out)
    return out.reshape(g.shape)


def _hgrn_tile(q, zf, v, lbp_ref, st_ref, st0_ref, o_ref, e, ltri_ref, lvl_ref, hm_ref):
    log_lb, log_1m_lb, one_m_lb = lbp_ref[0:1, :], lbp_ref[1:2, :], lbp_ref[2:3, :]
    logf, ls = _log_forget(zf, log_lb, log_1m_lb)
    kin = one_m_lb * jnp.exp(ls - zf)
    g3 = jnp.dot(ltri_ref[...], jnp.concatenate(_split3(logf), axis=1), preferred_element_type=f32)
    g = g3[:, 0:A_W] + g3[:, A_W:2 * A_W] + g3[:, 2 * A_W:3 * A_W]
    v_b = v.astype(bf16)
    vt_b = v.T.astype(bf16)
    e_mid = g - _level_ref(g, CHUNK // 2)
    safe = jnp.max(jnp.abs(e_mid)) < SAFE_EXPONENT

    def expand(rows_b):
        return jnp.concatenate([rows_b * hm_ref[h] for h in range(4)], axis=0)

    def finish_chunk(c, att, q_in, k_hat):
        r0, r1 = c * CHUNK, (c + 1) * CHUNK
        st = st_ref[...]
        o = jnp.dot(att.astype(bf16), expand(v_b[r0:r1]), preferred_element_type=f32)
        o = o + lax.dot_general(q_in, st.astype(bf16) * e, (((1,), (1,)), ((), ())), preferred_element_type=f32)
        zero = jnp.zeros((CHUNK, A_W), bf16)
        pair = (c // 2) * 2 * CHUNK
        rhs = jnp.concatenate([k_hat, zero] if c % 2 == 0 else [zero, k_hat], axis=0)
        upd = jnp.dot(vt_b[:, pair:pair + 2 * CHUNK], rhs, preferred_element_type=f32)
        st_ref[...] = st * jnp.exp(g[r1 - 1:r1, :]) + upd
        o_ref[r0:r1, :] = o

    st0_ref[...] = st_ref[...]
    qw = q * jnp.exp(e_mid)
    kw = kin * jnp.exp(-e_mid)
    q_t, k_t = qw.astype(bf16), kw.astype(bf16)
    for c in range(TILE // CHUNK):
        r0, r1 = c * CHUNK, (c + 1) * CHUNK
        att = lax.dot_general(q_t[r0:r1], expand(k_t[r0:r1]), (((1,), (1,)), ((), ())), preferred_element_type=f32)
        att = jnp.where(lvl_ref[len(LEVELS)] > 0.5, att, 0.0)
        g_mid, g_end = g[r0 + CHUNK // 2 - 1:r0 + CHUNK // 2, :], g[r1 - 1:r1, :]
        finish_chunk(c, att, (qw[r0:r1] * jnp.exp(g_mid)).astype(bf16),
                     (kw[r0:r1] * jnp.exp(g_end - g_mid)).astype(bf16))

    def redo_if_unsafe():
        @pl.when(jnp.logical_not(safe))
        def _():
            st_ref[...] = st0_ref[...]
            q_levels, k_levels = [], []
            for m in LEVELS:
                w = jnp.exp(-jnp.abs(g - _level_ref(g, m)))
                q_levels.append((q * w).astype(bf16))
                k_levels.append((kin * w).astype(bf16))
            q_inter = (q * jnp.exp(g)).astype(bf16)
            for c in range(TILE // CHUNK):
                r0, r1 = c * CHUNK, (c + 1) * CHUNK
                att = jnp.zeros((CHUNK, 4 * CHUNK), f32)
                for li in range(len(LEVELS)):
                    a = lax.dot_general(q_levels[li][r0:r1], expand(k_levels[li][r0:r1]),
                                        (((1,), (1,)), ((), ())), preferred_element_type=f32)
                    att = att + a * lvl_ref[li]
                k_hat = (kin[r0:r1] * jnp.exp(g[r1 - 1:r1, :] - g[r0:r1])).astype(bf16)
                finish_chunk(c, att, q_inter[r0:r1], k_hat)
            o_ref[...] = o_ref[...] + _seg_sum(q * kin, e) * v

    return redo_if_unsafe


def _conv_tile(u, hist_ref, cw_ref, cvec_ref, e):
    n = u.shape[0]
    hist_ref[HIST:HIST + n, :] = u
    lo = HIST - (CONV_WIDTH - 1)
    y = cvec_ref[0:1, :]
    for r in range(8):
        rows = n if r == 0 else n + 8
        part = None
        for o in range(lo, lo + CONV_WIDTH):
            if o % 8 == r:
                term = hist_ref[o - r:o - r + rows, :] * cw_ref[o - lo:o - lo + 1, :]
                part = term if part is None else part + term
        y = y + part[r:r + n]
    return _group_ln_silu(y, e, cvec_ref[1:2, :], cvec_ref[2:3, :])


def _attn_block(q_t, k_blk, v_t, sink_ref, bias):
    zero = jnp.zeros((HEAD, WINDOW), f32)
    pairs = []
    for g in range(2):
        cols, sinks = [], []
        for h in range(4 * g, 4 * g + 4):
            q_h = q_t[h * HEAD:(h + 1) * HEAD, :]
            cols.append(jnp.concatenate([q_h, zero] if g == 0 else [zero, q_h], axis=0))
            sinks.append(jnp.full((1, WINDOW), sink_ref[h], f32))
        rhs = jnp.concatenate(cols, axis=1).astype(bf16)
        sink = jnp.concatenate(sinks, axis=1)
        s = jnp.dot(k_blk, rhs, preferred_element_type=f32) + bias
        mx = jnp.maximum(jnp.max(s, axis=0, keepdims=True), sink)
        pr = jnp.exp(s - mx)
        rden = 1.0 / (jnp.sum(pr, axis=0, keepdims=True) + jnp.exp(sink - mx))
        o_t = jnp.dot(v_t[g * HEAD:(g + 1) * HEAD, :], pr.astype(bf16), preferred_element_type=f32) * rden
        for j in range(2):
            pair_t = jnp.concatenate([o_t[:, (2 * j) * WINDOW:(2 * j + 1) * WINDOW],
                                      o_t[:, (2 * j + 1) * WINDOW:(2 * j + 2) * WINDOW]], axis=0)
            pairs.append(pair_t.T)
    return jnp.concatenate(pairs, axis=1)


def _mixer_kernel(x_ref, cos_ref, sin_ref, win_ref, wout_ref, nmix_ref, lbp_ref, aon_ref, cw_ref, cvec_ref,
                  qk_ref, sink_ref, e_ref, ltri_ref, lvl_ref, hm_ref, bias_ref,
                  h_ref, st_ref, conv_ref, kn_ref, vn_ref, z_ref, hist_ref, o_ref, st0_ref):
    i = pl.program_id(0)

    @pl.when(i == 0)
    def _():
        st_ref[...] = jnp.zeros_like(st_ref)
        hist_ref[0:HIST, :] = jnp.zeros((HIST, B_W), f32)
        kn_ref[...] = jnp.zeros_like(kn_ref)
        vn_ref[...] = jnp.zeros_like(vn_ref)

    x = x_ref[...]
    xn = _rmsnorm_rows(x, nmix_ref[...])
    z_ref[...] = jnp.dot(xn.astype(bf16), win_ref[...], preferred_element_type=f32)
    e = e_ref[...]

    redo_hgrn_if_unsafe = _hgrn_tile(z_ref[:, 0:A_W], z_ref[:, A_W:2 * A_W], z_ref[:, 2 * A_W:3 * A_W], lbp_ref,
                                     st_ref, st0_ref, o_ref, e, ltri_ref, lvl_ref, hm_ref)

    c0 = 4 * A_W
    u = z_ref[:, c0:c0 + B_W] * _sigmoid(z_ref[:, c0 + B_W:c0 + 2 * B_W])
    ob = _conv_tile(u, hist_ref, cw_ref, cvec_ref, e)
    conv_ref[...] = hist_ref[TILE + HIST - (CONV_WIDTH - 1):TILE + HIST, :]
    hist_ref[0:HIST, :] = hist_ref[TILE:TILE + HIST, :]

    c1 = c0 + 2 * B_W
    cos, sin = cos_ref[...], sin_ref[...]
    qn = jnp.concatenate([_head_norm(z_ref[:, c1 + j * 256:c1 + (j + 1) * 256], e,
                                     jnp.concatenate([qk_ref[0:1, :]] * 2, axis=1)) for j in range(2)], axis=1)
    q = _rope(qn, jnp.concatenate([cos] * 4, axis=1), jnp.concatenate([sin] * 4, axis=1)) * (HEAD ** -0.5)
    kk = _rope(_head_norm(z_ref[:, c1 + C_W:c1 + C_W + KV_W], e[0:KV_W, 0:KV_W], qk_ref[1:2, :]), cos, sin)
    vv = z_ref[:, c1 + C_W + KV_W:c1 + C_W + 2 * KV_W]
    kcat = jnp.concatenate([kn_ref[...], kk], axis=0)
    vcat = jnp.concatenate([vn_ref[...], vv], axis=0)
    q_t = q.T
    k_b = kcat.astype(bf16)
    v_t = vcat.T.astype(bf16)
    oc = []
    for b in range(TILE // WINDOW):
        bias = bias_ref[jnp.where(i > 0, 1, 0)] if b == 0 else bias_ref[1]
        oc.append(_attn_block(q_t[:, b * WINDOW:(b + 1) * WINDOW], k_b[b * WINDOW:(b + 2) * WINDOW],
                              v_t[:, b * WINDOW:(b + 2) * WINDOW], sink_ref, bias))
    oc = jnp.concatenate(oc, axis=0)
    kn_ref[...] = kk[TILE - WINDOW:TILE]
    vn_ref[...] = vv[TILE - WINDOW:TILE]

    redo_hgrn_if_unsafe()
    gate = z_ref[:, 3 * A_W:4 * A_W]
    oa = _head_norm(o_ref[...], e, aon_ref[...]) * (gate * _sigmoid(gate))
    mix = jnp.concatenate([oa, ob, oc], axis=1).astype(bf16)
    h_ref[...] = x + jnp.dot(mix, wout_ref[...], preferred_element_type=f32)


def _const_spec(shape):
    nd = len(shape)
    return pl.BlockSpec(shape, lambda i, _n=nd: (0,) * _n)


def _prompt_mixer(x, cos, sin, wl, consts):
    t_len = x.shape[0]
    row_spec = lambda w: pl.BlockSpec((TILE, w), lambda i: (i, 0))
    smalls = [wl['nmix'], wl['lbp'], wl['aon'], wl['cw'], wl['cvec'], wl['qk']]
    in_specs = ([row_spec(D_MODEL), row_spec(KV_W), row_spec(KV_W), _const_spec(wl['w_in'].shape),
                 _const_spec(wl['w_out'].shape)] + [_const_spec(a.shape) for a in smalls]
                + [pl.BlockSpec(memory_space=pltpu.SMEM)]
                + [_const_spec(a.shape) for a in consts])
    out_shape = (jax.ShapeDtypeStruct((t_len, D_MODEL), f32),
                 jax.ShapeDtypeStruct((A_W, A_W), f32),
                 jax.ShapeDtypeStruct((CONV_WIDTH - 1, B_W), f32),
                 jax.ShapeDtypeStruct((WINDOW, KV_W), f32),
                 jax.ShapeDtypeStruct((WINDOW, KV_W), f32))
    out_specs = (row_spec(D_MODEL), _const_spec((A_W, A_W)), _const_spec((CONV_WIDTH - 1, B_W)),
                 _const_spec((WINDOW, KV_W)), _const_spec((WINDOW, KV_W)))
    return pl.pallas_call(
        _mixer_kernel, grid=(t_len // TILE,), in_specs=in_specs, out_specs=out_specs, out_shape=out_shape,
        scratch_shapes=[pltpu.VMEM((TILE, IN_COLS), f32), pltpu.VMEM((TILE + HIST, B_W), f32),
                        pltpu.VMEM((TILE, A_W), f32), pltpu.VMEM((A_W, A_W), f32)],
        compiler_params=pltpu.CompilerParams(dimension_semantics=("arbitrary",), vmem_limit_bytes=VMEM_LIMIT),
        name="prompt_mixer",
    )(x, cos, sin, wl['w_in'], wl['w_out'], *smalls, wl['sinks'], *consts)


def _ffn_kernel(h_ref, p_ref, wg_ref, wu_ref, wd_ref, wpg_ref, wpp_ref, norms_ref, y_ref):
    h = h_ref[...]
    hn = _rmsnorm_rows(h, norms_ref[0:1, :]).astype(bf16)
    gt = jnp.dot(hn, wg_ref[...], preferred_element_type=f32)
    up = jnp.dot(hn, wu_ref[...], preferred_element_type=f32)
    act = (gt * _sigmoid(gt) * up).astype(bf16)
    h = h + jnp.dot(act, wd_ref[...], preferred_element_type=f32)
    pn = _rmsnorm_rows(h, norms_ref[1:2, :]).astype(bf16)
    gate = _sigmoid(jnp.dot(pn, wpg_ref[...], preferred_element_type=f32))
    y_ref[...] = h + gate * jnp.dot(p_ref[...].astype(bf16), wpp_ref[...], preferred_element_type=f32)


def _ffn(h, p, wl):
    n = h.shape[0]
    tile = min(FFN_TILE, n)
    row_spec = lambda w: pl.BlockSpec((tile, w), lambda i: (i, 0))
    single = lambda a: pl.BlockSpec(a.shape, lambda i: (0, 0), pipeline_mode=pl.Buffered(1))
    ws = [wl['w_gate'], wl['w_up'], wl['w_down'], wl['w_ple_gate'], wl['w_ple_proj']]
    return pl.pallas_call(
        _ffn_kernel, grid=(n // tile,),
        in_specs=[row_spec(D_MODEL), row_spec(PLE_DIM)] + [single(a) for a in ws] + [_const_spec(wl['fnorms'].shape)],
        out_specs=row_spec(D_MODEL), out_shape=jax.ShapeDtypeStruct((n, D_MODEL), f32),
        compiler_params=pltpu.CompilerParams(dimension_semantics=("arbitrary",), vmem_limit_bytes=VMEM_LIMIT),
        name="ffn_ple",
    )(h, p, *ws, wl['fnorms'])


def _sample_in_kernel(x_ref, cos_ref, sin_ref, win_ref, nmix_ref, qk_ref, e_ref, z_ref, q_ref, k_ref):
    xn = _rmsnorm_rows(x_ref[...], nmix_ref[...])
    z = jnp.dot(xn.astype(bf16), win_ref[...], preferred_element_type=f32)
    z_ref[...] = z
    e = e_ref[...]
    c1 = 4 * A_W + 2 * B_W
    cos, sin = cos_ref[...], sin_ref[...]
    qn = jnp.concatenate([_head_norm(z[:, c1 + j * 256:c1 + (j + 1) * 256], e,
                                     jnp.concatenate([qk_ref[0:1, :]] * 2, axis=1)) for j in range(2)], axis=1)
    q_ref[...] = _rope(qn, jnp.concatenate([cos] * 4, axis=1), jnp.concatenate([sin] * 4, axis=1)) * (HEAD ** -0.5)
    k_ref[...] = _rope(_head_norm(z[:, c1 + C_W:c1 + C_W + KV_W], e[0:KV_W, 0:KV_W], qk_ref[1:2, :]), cos, sin)


def _sample_in(x, cos, sin, wl, e256):
    n = x.shape[0]
    args = (x, cos, sin, wl['w_in'], wl['nmix'], wl['qk'], e256)
    return pl.pallas_call(
        _sample_in_kernel, grid=(1,), in_specs=[_const_spec(a.shape) for a in args],
        out_specs=(_const_spec((n, IN_COLS)), _const_spec((n, C_W)), _const_spec((n, KV_W))),
        out_shape=(jax.ShapeDtypeStruct((n, IN_COLS), f32), jax.ShapeDtypeStruct((n, C_W), f32),
                   jax.ShapeDtypeStruct((n, KV_W), f32)),
        compiler_params=pltpu.CompilerParams(dimension_semantics=("arbitrary",), vmem_limit_bytes=VMEM_LIMIT),
        name="sample_in",
    )(*args)


def _sample_hgrn_kernel(n_tok, q_ref, f_ref, v_ref, s_ref, lbp_ref, o_ref, so_ref, st_scr, qt_scr, ft_scr, kt_scr,
                        vt_scr, ot_scr):
    n_seq = s_ref.shape[0]
    st_scr[...] = s_ref[...].T
    for t in range(n_tok):
        zf = f_ref[pl.ds(t, n_seq, stride=n_tok), :]
        logf, ls = _log_forget(zf, lbp_ref[0:1, :], lbp_ref[1:2, :])
        ft_scr[t] = jnp.exp(logf).T
        kt_scr[t] = (lbp_ref[2:3, :] * jnp.exp(ls - zf)).T
        qt_scr[t] = q_ref[pl.ds(t, n_seq, stride=n_tok), :].T
        vt_scr[t] = v_ref[pl.ds(t, n_seq, stride=n_tok), :].T
        ot_scr[t] = jnp.zeros((2 * HEAD, n_seq), f32)

    def body(hk, carry):
        r0 = pl.multiple_of(hk * HEAD, HEAD)
        v0 = pl.multiple_of((hk // HEAD) * HEAD, HEAD)
        blk = st_scr[pl.ds(r0, HEAD), :]
        for t in range(n_tok):
            blk = blk * ft_scr[t, pl.ds(hk, 1), :] + kt_scr[t, pl.ds(hk, 1), :] * vt_scr[t, pl.ds(v0, HEAD), :]
            ot_scr[t, pl.ds(v0, HEAD), :] = ot_scr[t, pl.ds(v0, HEAD), :] + qt_scr[t, pl.ds(hk, 1), :] * blk
        st_scr[pl.ds(r0, HEAD), :] = blk
        return carry

    lax.fori_loop(0, 2 * HEAD, body, 0)
    so_ref[...] = st_scr[...].T
    for t in range(n_tok):
        o_ref[pl.ds(t, n_seq, stride=n_tok), :] = ot_scr[t].T


def _sample_hgrn(z, state, lbp, n_tok):
    n = z.shape[0]
    n_seq = state.shape[0]
    blk = 2 * HEAD * HEAD
    col = lambda j0: pl.BlockSpec((n, 2 * HEAD), lambda i, _j=j0: (0, _j + i))
    return pl.pallas_call(
        functools.partial(_sample_hgrn_kernel, n_tok), grid=(2,),
        in_specs=[col(0), col(2), col(4), pl.BlockSpec((n_seq, blk), lambda i: (0, i)),
                  pl.BlockSpec((3, 2 * HEAD), lambda i: (0, i))],
        out_specs=(pl.BlockSpec((n, 2 * HEAD), lambda i: (0, i)), pl.BlockSpec((n_seq, blk), lambda i: (0, i))),
        out_shape=(jax.ShapeDtypeStruct((n, A_W), f32), jax.ShapeDtypeStruct(state.shape, f32)),
        scratch_shapes=[pltpu.VMEM((blk, n_seq), f32)] + [pltpu.VMEM((n_tok, 2 * HEAD, n_seq), f32)] * 5,
        compiler_params=pltpu.CompilerParams(dimension_semantics=("arbitrary",), vmem_limit_bytes=VMEM_LIMIT),
        name="sample_hgrn",
    )(z, z, z, state, lbp)


def _sample_conv_kernel(n_tok, u0_ref, u1_ref, g0_ref, g1_ref, buf_ref, cw_ref, cvec_ref, e_ref, ob0_ref, ob1_ref,
                        new_ref):
    n_seq = buf_ref.shape[0]
    n_hist = CONV_WIDTH - 1
    xs = [buf_ref[:, j, :] for j in range(n_hist)]
    for t in range(n_tok):
        rows = pl.ds(t, n_seq, stride=n_tok)
        u = jnp.concatenate([u0_ref[rows, :], u1_ref[rows, :]], axis=1)
        g = jnp.concatenate([g0_ref[rows, :], g1_ref[rows, :]], axis=1)
        xs.append(u * _sigmoid(g))
    for t in range(n_tok):
        acc = jnp.zeros((n_seq, B_W), f32)
        for j in range(CONV_WIDTH):
            acc = acc + xs[t + j] * cw_ref[j:j + 1, :]
        y = acc + cvec_ref[0:1, :]
        ob = _group_ln_silu(y, e_ref[...], cvec_ref[1:2, :], cvec_ref[2:3, :])
        ob0_ref[pl.ds(t, n_seq, stride=n_tok), :] = ob[:, 0:KV_W]
        ob1_ref[pl.ds(t, n_seq, stride=n_tok), :] = ob[:, KV_W:B_W]
    for j in range(n_hist):
        new_ref[:, j, :] = xs[j + n_tok]


def _sample_conv(z, buf, wl, e256, n_tok):
    n = z.shape[0]
    col = lambda j: pl.BlockSpec((n, KV_W), lambda i, _j=j: (0, _j))
    half = jax.ShapeDtypeStruct((n, KV_W), f32)
    return pl.pallas_call(
        functools.partial(_sample_conv_kernel, n_tok), grid=(1,),
        in_specs=[col(8), col(9), col(10), col(11), _const_spec(buf.shape), _const_spec(wl['cw'].shape),
                  _const_spec(wl['cvec'].shape), _const_spec(e256.shape)],
        out_specs=(_const_spec((n, KV_W)), _const_spec((n, KV_W)), _const_spec(buf.shape)),
        out_shape=(half, half, jax.ShapeDtypeStruct(buf.shape, f32)),
        compiler_params=pltpu.CompilerParams(dimension_semantics=("arbitrary",), vmem_limit_bytes=VMEM_LIMIT),
        name="sample_conv",
    )(z, z, z, z, buf, wl['cw'], wl['cvec'], e256)


SEQ_BLOCK = 8


def _sample_attn_kernel(n_tok, q_ref, k_ref, v_ref, kc_ref, vc_ref, sink_ref, o_ref, kn_ref, vn_ref):
    w_buf = kc_ref.shape[1]
    n_keys = w_buf + n_tok
    lane = lax.broadcasted_iota(jnp.int32, (n_keys, KV_W), 1)
    lo_f = lane < HEAD
    qlane = lax.broadcasted_iota(jnp.int32, (n_tok, KV_W), 1)
    row_t = lax.broadcasted_iota(jnp.int32, (4 * n_tok, n_keys), 0) % n_tok
    col = lax.broadcasted_iota(jnp.int32, (4 * n_tok, n_keys), 1)
    rel = row_t + w_buf - col
    valid = (rel >= 0) & (rel <= WINDOW)
    olane = lax.broadcasted_iota(jnp.int32, (n_tok, KV_W), 1)
    for b in range(SEQ_BLOCK):
        rows = slice(b * n_tok, (b + 1) * n_tok)
        k_all = jnp.concatenate([kc_ref[b], k_ref[rows, :]], axis=0)
        v_all = jnp.concatenate([vc_ref[b], v_ref[rows, :]], axis=0)
        kn_ref[b] = k_all[n_tok:, :]
        vn_ref[b] = v_all[n_tok:, :]
        k_roll = pltpu.roll(k_all, HEAD, 1)
        v_roll = pltpu.roll(v_all, HEAD, 1)
        qb = q_ref[rows, :]
        pieces = []
        for g in range(2):
            keep = lo_f if g == 0 else jnp.logical_not(lo_f)
            k_dup = jnp.where(keep, k_all, k_roll).astype(bf16)
            v_dup = jnp.where(keep, v_all, v_roll).astype(bf16)
            parts, sinks = [], []
            for p in (2 * g, 2 * g + 1):
                q2 = qb[:, p * KV_W:(p + 1) * KV_W]
                parts.append(jnp.where(qlane < HEAD, q2, 0.0))
                parts.append(jnp.where(qlane < HEAD, 0.0, q2))
                sinks.append(jnp.full((n_tok, 1), sink_ref[2 * p], f32))
                sinks.append(jnp.full((n_tok, 1), sink_ref[2 * p + 1], f32))
            lhs = jnp.concatenate(parts, axis=0).astype(bf16)
            sink = jnp.concatenate(sinks, axis=0)
            s = lax.dot_general(lhs, k_dup, (((1,), (1,)), ((), ())), preferred_element_type=f32)
            s = jnp.where(valid, s, -jnp.inf)
            mx = jnp.maximum(jnp.max(s, axis=-1, keepdims=True), sink)
            pr = jnp.exp(s - mx)
            den = jnp.sum(pr, axis=-1, keepdims=True) + jnp.exp(sink - mx)
            o = jnp.dot((pr / den).astype(bf16), v_dup, preferred_element_type=f32)
            for j in range(2):
                pieces.append(jnp.where(olane < HEAD, o[(2 * j) * n_tok:(2 * j + 1) * n_tok],
                                        o[(2 * j + 1) * n_tok:(2 * j + 2) * n_tok]))
        o_ref[rows, :] = jnp.concatenate(pieces, axis=1)


def _sample_attn(q, k, z, kc, vc, sinks, n_tok):
    n = q.shape[0]
    n_seq = kc.shape[0]
    rows = SEQ_BLOCK * n_tok
    v_col = (4 * A_W + 2 * B_W + C_W + KV_W) // KV_W
    cache_spec = pl.BlockSpec((SEQ_BLOCK,) + kc.shape[1:], lambda i: (i, 0, 0))
    return pl.pallas_call(
        functools.partial(_sample_attn_kernel, n_tok), grid=(n_seq // SEQ_BLOCK,),
        in_specs=[pl.BlockSpec((rows, C_W), lambda i: (i, 0)), pl.BlockSpec((rows, KV_W), lambda i: (i, 0)),
                  pl.BlockSpec((rows, KV_W), lambda i: (i, v_col)), cache_spec, cache_spec,
                  pl.BlockSpec(memory_space=pltpu.SMEM)],
        out_specs=(pl.BlockSpec((rows, C_W), lambda i: (i, 0)), cache_spec, cache_spec),
        out_shape=(jax.ShapeDtypeStruct((n, C_W), f32), jax.ShapeDtypeStruct(kc.shape, f32),
                   jax.ShapeDtypeStruct(vc.shape, f32)),
        compiler_params=pltpu.CompilerParams(dimension_semantics=("arbitrary",), vmem_limit_bytes=VMEM_LIMIT),
        name="sample_attn",
    )(q, k, z, kc, vc, sinks)


def _sample_out_kernel(x_ref, oa_ref, g_ref, ob0_ref, ob1_ref, oc_ref, wout_ref, aon_ref, e_ref, h_ref):
    gate = g_ref[...]
    oa = _head_norm(oa_ref[...], e_ref[...], aon_ref[...]) * (gate * _sigmoid(gate))
    mix = jnp.concatenate([oa, ob0_ref[...], ob1_ref[...], oc_ref[...]], axis=1).astype(bf16)
    h_ref[...] = x_ref[...] + jnp.dot(mix, wout_ref[...], preferred_element_type=f32)


def _sample_out(x, oa, z, ob0, ob1, oc, wl, e256):
    n = x.shape[0]
    gate_spec = pl.BlockSpec((n, A_W), lambda i: (0, 3))
    args = (x, oa, z, ob0, ob1, oc, wl['w_out'], wl['aon'], e256)
    specs = [_const_spec(a.shape) for a in args]
    specs[2] = gate_spec
    return pl.pallas_call(
        _sample_out_kernel, grid=(1,), in_specs=specs, out_specs=_const_spec((n, D_MODEL)),
        out_shape=jax.ShapeDtypeStruct((n, D_MODEL), f32),
        compiler_params=pltpu.CompilerParams(dimension_semantics=("arbitrary",), vmem_limit_bytes=VMEM_LIMIT),
        name="sample_out",
    )(*args)


def _rope_tables(pos):
    half = HEAD // 2
    inv = ROPE_THETA ** (-jnp.arange(half, dtype=f32) / half)
    ang = pos[:, None] * inv[None, :]
    cos, sin = jnp.cos(ang), jnp.sin(ang)
    return jnp.tile(cos, (1, 4)), jnp.concatenate([-sin, sin, -sin, sin], axis=1)


def _layer_weights(l, lbs, w_in, a_onorm, conv_w, conv_b, conv_ln_g, conv_ln_b, q_norm, k_norm, sinks, w_out,
                   norm_mix, norm_ffn, w_gate, w_up, w_down, ple_norm, w_ple_gate, w_ple_proj):
    lb = lbs[l]
    return dict(
        w_in=w_in[l].astype(bf16), w_out=w_out[l].astype(bf16), w_gate=w_gate[l].astype(bf16),
        w_up=w_up[l].astype(bf16), w_down=w_down[l].astype(bf16), w_ple_gate=w_ple_gate[l].astype(bf16),
        w_ple_proj=w_ple_proj[l].astype(bf16),
        nmix=norm_mix[l][None, :],
        lbp=jnp.stack([jnp.log(lb), jnp.log1p(-lb), 1.0 - lb]),
        aon=jnp.tile(a_onorm[l], 4)[None, :],
        cw=conv_w[l],
        cvec=jnp.stack([conv_b[l], conv_ln_g[l], conv_ln_b[l]]),
        qk=jnp.stack([jnp.tile(q_norm[l], 2), jnp.tile(k_norm[l], 2)]),
        sinks=sinks[l],
        fnorms=jnp.stack([norm_ffn[l], ple_norm[l]]),
    )


def kernel(x_prompt, x_sample, state_hgrn, state_conv, cache_swa_k, cache_swa_v, p_prompt, p_sample, a_lower, w_in, a_onorm, conv_w, conv_b, conv_ln_g, conv_ln_b, q_norm, k_norm, sinks, w_out, norm_mix, norm_ffn, w_gate, w_up, w_down, ple_norm, w_ple_gate, w_ple_proj):
    depth = w_in.shape[0]
    t_len = x_prompt.shape[1]
    n_seq, n_tok = x_sample.shape[0], x_sample.shape[1]
    past_len = 16384
    assert x_prompt.shape[0] == 1 and t_len % TILE == 0

    lbs = jnp.cumsum(jax.nn.softmax(a_lower.astype(f32), axis=0), axis=0)
    lbs = lbs - lbs[0:1]
    consts_np = _np_constants()
    e256 = jnp.asarray(consts_np[0], bf16)
    consts = (e256, jnp.asarray(consts_np[1], bf16), jnp.asarray(consts_np[2], f32), jnp.asarray(consts_np[3], bf16),
              jnp.asarray(consts_np[4], f32))

    cos_p, sin_p = _rope_tables(jnp.arange(t_len, dtype=f32))
    cos_s, sin_s = _rope_tables(past_len + jnp.arange(n_tok, dtype=f32))
    cos_s, sin_s = jnp.tile(cos_s, (n_seq, 1)), jnp.tile(sin_s, (n_seq, 1))

    hp = x_prompt[0]
    hs = x_sample.reshape(n_seq * n_tok, D_MODEL)
    outs = [[] for _ in range(8)]
    for l in range(depth):
        wl = _layer_weights(l, lbs, w_in, a_onorm, conv_w, conv_b, conv_ln_g, conv_ln_b, q_norm, k_norm, sinks,
                            w_out, norm_mix, norm_ffn, w_gate, w_up, w_down, ple_norm, w_ple_gate, w_ple_proj)
        h_mid, st, conv_new, k_new, v_new = _prompt_mixer(hp, cos_p, sin_p, wl, consts)
        hp = _ffn(h_mid, p_prompt[l, 0], wl)
        s_new = jnp.stack([st[h * HEAD:(h + 1) * HEAD, h * HEAD:(h + 1) * HEAD].T for h in range(4)])
        outs[0].append(s_new[None])
        outs[1].append(conv_new[None])
        outs[2].append(k_new.reshape(1, WINDOW, 2, HEAD))
        outs[3].append(v_new.reshape(1, WINDOW, 2, HEAD))
        z, q, k = _sample_in(hs, cos_s, sin_s, wl, e256)
        oa, s_s = _sample_hgrn(z, state_hgrn[l].reshape(n_seq, -1), wl['lbp'], n_tok)
        ob0, ob1, conv_s = _sample_conv(z, state_conv[l], wl, e256, n_tok)
        w_buf = cache_swa_k.shape[2]
        oc, k_s, v_s = _sample_attn(q, k, z, cache_swa_k[l].reshape(n_seq, w_buf, KV_W),
                                    cache_swa_v[l].reshape(n_seq, w_buf, KV_W), wl['sinks'], n_tok)
        h_mid_s = _sample_out(hs, oa, z, ob0, ob1, oc, wl, e256)
        hs = _ffn(h_mid_s, p_sample[l].reshape(n_seq * n_tok, PLE_DIM), wl)
        outs[4].append(s_s.reshape(n_seq, 4, HEAD, HEAD))
        outs[5].append(conv_s)
        outs[6].append(k_s.reshape(n_seq, w_buf, 2, HEAD))
        outs[7].append(v_s.reshape(n_seq, w_buf, 2, HEAD))
    return (hp[None], hs.reshape(n_seq, n_tok, D_MODEL)) + tuple(jnp.stack(o) for o in outs)
```

```python
import functools

import numpy as np
import jax
import jax.numpy as jnp
from jax import lax
from jax.experimental import pallas as pl
from jax.experimental.pallas import tpu as pltpu

f32 = jnp.float32
bf16 = jnp.bfloat16

D_MODEL = 1024
HEAD = 64
A_W = 256
B_W = 256
C_W = 512
KV_W = 128
IN_COLS = 4 * A_W + 2 * B_W + C_W + 2 * KV_W
D_FF = 2816
PLE_DIM = 256
WINDOW = 128
CONV_WIDTH = 31
ROPE_THETA = 10000.0
EPS = 1e-6

TILE = 256
CHUNK = 64
LEVELS = (32, 16, 8, 4, 2, 1)
SAFE_EXPONENT = 80.0
HIST = 32
FFN_TILE = 256
VMEM_LIMIT = 52 * 1024 * 1024


def _split2(x):
    hi = x.astype(bf16)
    lo = (x - hi.astype(f32)).astype(bf16)
    return hi, lo


def _split3(x):
    hi = x.astype(bf16)
    r = x - hi.astype(f32)
    mid = r.astype(bf16)
    lo = (r - mid.astype(f32)).astype(bf16)
    return hi, mid, lo


def _seg_sum(x, e):
    hi, lo = _split2(x)
    return (jnp.dot(hi, e, preferred_element_type=f32) + jnp.dot(lo, e, preferred_element_type=f32))


def _sigmoid(x):
    return 1.0 / (1.0 + jnp.exp(-x))


def _rmsnorm_rows(x, g):
    return x * lax.rsqrt(jnp.mean(x * x, axis=-1, keepdims=True) + EPS) * g


def _head_norm(x, e, g):
    ms = _seg_sum(x * x, e) * (1.0 / HEAD)
    return x * lax.rsqrt(ms + EPS) * g


def _rope(x, cos, sin_signed):
    w = x.shape[-1]
    lane = lax.broadcasted_iota(jnp.int32, x.shape, 1)
    first = (lane % HEAD) < (HEAD // 2)
    swapped = jnp.where(first, pltpu.roll(x, w - HEAD // 2, 1), pltpu.roll(x, HEAD // 2, 1))
    return x * cos + swapped * sin_signed


def _log_forget(zf, log_lb, log_1m_lb):
    ls = jnp.minimum(zf, 0.0) - jnp.log(1.0 + jnp.exp(-jnp.abs(zf)))
    b = log_1m_lb + ls
    return jnp.maximum(log_lb, b) + jnp.log(1.0 + jnp.exp(-jnp.abs(log_lb - b))), ls


def _group_ln_silu(y, e, g, b):
    mu = _seg_sum(y, e) * (1.0 / HEAD)
    yc = y - mu
    var = _seg_sum(yc * yc, e) * (1.0 / HEAD)
    yn = yc * lax.rsqrt(var + EPS) * g + b
    return yn * _sigmoid(yn)


def _np_constants():
    lane = np.arange(256)
    e256 = (lane[:, None] // HEAD == lane[None, :] // HEAD).astype(np.float32)
    t = np.arange(TILE)
    ltri = ((t[:, None] // CHUNK == t[None, :] // CHUNK) & (t[None, :] <= t[:, None])).astype(np.float32)
    tt = np.arange(CHUNK)[:, None]
    ss = (np.arange(256) % CHUNK)[None, :]
    lvl = []
    for m in LEVELS:
        lvl.append(((tt // (2 * m) == ss // (2 * m)) & (tt % (2 * m) >= m) & (ss % (2 * m) < m)).astype(np.float32))
    lvl.append((ss <= tt).astype(np.float32))
    lvl = np.stack(lvl)
    hm = np.stack([np.broadcast_to((lane // HEAD == h)[None, :], (CHUNK, 256)) for h in range(4)]).astype(np.float32)
    keys = np.arange(2 * WINDOW)[:, None]
    qry = np.arange(4 * WINDOW)[None, :] % WINDOW
    band = (keys >= qry) & (keys <= qry + WINDOW)
    bias = np.where(np.stack([band & (keys >= WINDOW), band]), 0.0, -np.inf).astype(np.float32)
    return e256, ltri, lvl, hm, bias


def _level_ref(g, m):
    n_rows = g.shape[0]
    if 2 * m >= 8:
        g3 = g.reshape(n_rows // (2 * m), 2 * m, g.shape[1])
        return jnp.broadcast_to(g3[:, m - 1:m, :], g3.shape).reshape(g.shape)
    g3 = g.reshape(n_rows // 8, 8, g.shape[1])
    sub = lax.broadcasted_iota(jnp.int32, g3.shape, 1)
    out = None
    for blk in range(8 // (2 * m)):
        r = blk * 2 * m + m - 1
        piece = jnp.broadcast_to(g3[:, r:r + 1, :], g3.shape)
        out = piece if out is None else jnp.where(sub >= blk * 2 * m, piece, out)
    return out.reshape(g.shape)


def _hgrn_tile(q, zf, v, lbp_ref, st_ref, st0_ref, o_ref, e, ltri_ref, lvl_ref, hm_ref):
    log_lb, log_1m_lb, one_m_lb = lbp_ref[0:1, :], lbp_ref[1:2, :], lbp_ref[2:3, :]
    logf, ls = _log_forget(zf, log_lb, log_1m_lb)
    kin = one_m_lb * jnp.exp(ls - zf)
    g3 = jnp.dot(ltri_ref[...], jnp.concatenate(_split3(logf), axis=1), preferred_element_type=f32)
    g = g3[:, 0:A_W] + g3[:, A_W:2 * A_W] + g3[:, 2 * A_W:3 * A_W]
    v_b = v.astype(bf16)
    vt_b = v.T.astype(bf16)
    e_mid = g - _level_ref(g, CHUNK // 2)
    safe = jnp.max(jnp.abs(e_mid)) < SAFE_EXPONENT

    def expand(rows_b):
        return jnp.concatenate([rows_b * hm_ref[h] for h in range(4)], axis=0)

    def finish_chunk(c, att, q_in, k_hat):
        r0, r1 = c * CHUNK, (c + 1) * CHUNK
        st = st_ref[...]
        o = jnp.dot(att.astype(bf16), expand(v_b[r0:r1]), preferred_element_type=f32)
        o = o + lax.dot_general(q_in, st.astype(bf16) * e, (((1,), (1,)), ((), ())), preferred_element_type=f32)
        zero = jnp.zeros((CHUNK, A_W), bf16)
        pair = (c // 2) * 2 * CHUNK
        rhs = jnp.concatenate([k_hat, zero] if c % 2 == 0 else [zero, k_hat], axis=0)
        upd = jnp.dot(vt_b[:, pair:pair + 2 * CHUNK], rhs, preferred_element_type=f32)
        st_ref[...] = st * jnp.exp(g[r1 - 1:r1, :]) + upd
        o_ref[r0:r1, :] = o

    st0_ref[...] = st_ref[...]
    qw = q * jnp.exp(e_mid)
    kw = kin * jnp.exp(-e_mid)
    q_t, k_t = qw.astype(bf16), kw.astype(bf16)
    for c in range(TILE // CHUNK):
        r0, r1 = c * CHUNK, (c + 1) * CHUNK
        att = lax.dot_general(q_t[r0:r1], expand(k_t[r0:r1]), (((1,), (1,)), ((), ())), preferred_element_type=f32)
        att = jnp.where(lvl_ref[len(LEVELS)] > 0.5, att, 0.0)
        g_mid, g_end = g[r0 + CHUNK // 2 - 1:r0 + CHUNK // 2, :], g[r1 - 1:r1, :]
        finish_chunk(c, att, (qw[r0:r1] * jnp.exp(g_mid)).astype(bf16),
                     (kw[r0:r1] * jnp.exp(g_end - g_mid)).astype(bf16))

    def redo_if_unsafe():
        @pl.when(jnp.logical_not(safe))
        def _():
            st_ref[...] = st0_ref[...]
            q_levels, k_levels = [], []
            for m in LEVELS:
                w = jnp.exp(-jnp.abs(g - _level_ref(g, m)))
                q_levels.append((q * w).astype(bf16))
                k_levels.append((kin * w).astype(bf16))
            q_inter = (q * jnp.exp(g)).astype(bf16)
            for c in range(TILE // CHUNK):
                r0, r1 = c * CHUNK, (c + 1) * CHUNK
                att = jnp.zeros((CHUNK, 4 * CHUNK), f32)
                for li in range(len(LEVELS)):
                    a = lax.dot_general(q_levels[li][r0:r1], expand(k_levels[li][r0:r1]),
                                        (((1,), (1,)), ((), ())), preferred_element_type=f32)
                    att = att + a * lvl_ref[li]
                k_hat = (kin[r0:r1] * jnp.exp(g[r1 - 1:r1, :] - g[r0:r1])).astype(bf16)
                finish_chunk(c, att, q_inter[r0:r1], k_hat)
            o_ref[...] = o_ref[...] + _seg_sum(q * kin, e) * v

    return redo_if_unsafe


def _conv_tile(u, hist_ref, cw_ref, cvec_ref, e):
    n = u.shape[0]
    hist_ref[HIST:HIST + n, :] = u
    lo = HIST - (CONV_WIDTH - 1)
    y = cvec_ref[0:1, :]
    for r in range(8):
        rows = n if r == 0 else n + 8
        part = None
        for o in range(lo, lo + CONV_WIDTH):
            if o % 8 == r:
                term = hist_ref[o - r:o - r + rows, :] * cw_ref[o - lo:o - lo + 1, :]
                part = term if part is None else part + term
        y = y + part[r:r + n]
    return _group_ln_silu(y, e, cvec_ref[1:2, :], cvec_ref[2:3, :])


def _attn_block(q_t, k_blk, v_t, sink_ref, bias):
    zero = jnp.zeros((HEAD, WINDOW), f32)
    pairs = []
    for g in range(2):
        cols, sinks = [], []
        for h in range(4 * g, 4 * g + 4):
            q_h = q_t[h * HEAD:(h + 1) * HEAD, :]
            cols.append(jnp.concatenate([q_h, zero] if g == 0 else [zero, q_h], axis=0))
            sinks.append(jnp.full((1, WINDOW), sink_ref[h], f32))
        rhs = jnp.concatenate(cols, axis=1).astype(bf16)
        sink = jnp.concatenate(sinks, axis=1)
        s = jnp.dot(k_blk, rhs, preferred_element_type=f32) + bias
        mx = jnp.maximum(jnp.max(s, axis=0, keepdims=True), sink)
        pr = jnp.exp(s - mx)
        rden = 1.0 / (jnp.sum(pr, axis=0, keepdims=True) + jnp.exp(sink - mx))
        o_t = jnp.dot(v_t[g * HEAD:(g + 1) * HEAD, :], pr.astype(bf16), preferred_element_type=f32) * rden
        for j in range(2):
            pair_t = jnp.concatenate([o_t[:, (2 * j) * WINDOW:(2 * j + 1) * WINDOW],
                                      o_t[:, (2 * j + 1) * WINDOW:(2 * j + 2) * WINDOW]], axis=0)
            pairs.append(pair_t.T)
    return jnp.concatenate(pairs, axis=1)


def _mixer_kernel(layer, x_ref, rope_row_ref, rope_tile_ref, win_ref, wout_ref, nmix_ref, lbp_ref, aon_ref, cw_ref,
                  cvec_ref, qk_ref, sinks_ref, e_ref, ltri_ref, lvl_ref, hm_ref, bias_ref,
                  h_ref, st_ref, conv_ref, kn_ref, vn_ref, z_ref, hist_ref, o_ref, st0_ref):
    i = pl.program_id(0)
    sink_ref = sinks_ref.at[layer]

    @pl.when(i == 0)
    def _():
        st_ref[...] = jnp.zeros_like(st_ref)
        hist_ref[0:HIST, :] = jnp.zeros((HIST, B_W), f32)
        kn_ref[...] = jnp.zeros_like(kn_ref)
        vn_ref[...] = jnp.zeros_like(vn_ref)

    x = x_ref[...]
    xn = _rmsnorm_rows(x, nmix_ref[...])
    z_ref[...] = jnp.dot(xn.astype(bf16), win_ref[...], preferred_element_type=f32)
    e = e_ref[...]

    redo_hgrn_if_unsafe = _hgrn_tile(z_ref[:, 0:A_W], z_ref[:, A_W:2 * A_W], z_ref[:, 2 * A_W:3 * A_W], lbp_ref,
                                     st_ref, st0_ref, o_ref, e, ltri_ref, lvl_ref, hm_ref)

    c0 = 4 * A_W
    u = z_ref[:, c0:c0 + B_W] * _sigmoid(z_ref[:, c0 + B_W:c0 + 2 * B_W])
    ob = _conv_tile(u, hist_ref, cw_ref, cvec_ref, e)
    conv_ref[...] = hist_ref[TILE + HIST - (CONV_WIDTH - 1):TILE + HIST, :]
    hist_ref[0:HIST, :] = hist_ref[TILE:TILE + HIST, :]

    c1 = c0 + 2 * B_W
    cos_a, sin_a = rope_tile_ref[0, pl.ds(i, 1), :], rope_tile_ref[1, pl.ds(i, 1), :]
    cos = cos_a * rope_row_ref[0] - sin_a * rope_row_ref[1]
    sin = sin_a * rope_row_ref[2] + cos_a * rope_row_ref[3]
    qn = jnp.concatenate([_head_norm(z_ref[:, c1 + j * 256:c1 + (j + 1) * 256], e,
                                     jnp.concatenate([qk_ref[0:1, :]] * 2, axis=1)) for j in range(2)], axis=1)
    q = _rope(qn, jnp.concatenate([cos] * 4, axis=1), jnp.concatenate([sin] * 4, axis=1)) * (HEAD ** -0.5)
    kk = _rope(_head_norm(z_ref[:, c1 + C_W:c1 + C_W + KV_W], e[0:KV_W, 0:KV_W], qk_ref[1:2, :]), cos, sin)
    vv = z_ref[:, c1 + C_W + KV_W:c1 + C_W + 2 * KV_W]
    kcat = jnp.concatenate([kn_ref[...], kk], axis=0)
    vcat = jnp.concatenate([vn_ref[...], vv], axis=0)
    q_t = q.T
    k_b = kcat.astype(bf16)
    v_t = vcat.T.astype(bf16)
    oc = []
    for b in range(TILE // WINDOW):
        bias = bias_ref[jnp.where(i > 0, 1, 0)] if b == 0 else bias_ref[1]
        oc.append(_attn_block(q_t[:, b * WINDOW:(b + 1) * WINDOW], k_b[b * WINDOW:(b + 2) * WINDOW],
                              v_t[:, b * WINDOW:(b + 2) * WINDOW], sink_ref, bias))
    oc = jnp.concatenate(oc, axis=0)
    kn_ref[...] = kk[TILE - WINDOW:TILE]
    vn_ref[...] = vv[TILE - WINDOW:TILE]

    redo_hgrn_if_unsafe()
    gate = z_ref[:, 3 * A_W:4 * A_W]
    oa = _head_norm(o_ref[...], e, aon_ref[...]) * (gate * _sigmoid(gate))
    mix = jnp.concatenate([oa, ob, oc], axis=1).astype(bf16)
    h_ref[...] = x + jnp.dot(mix, wout_ref[...], preferred_element_type=f32)


def _const_spec(shape):
    nd = len(shape)
    return pl.BlockSpec(shape, lambda i, _n=nd: (0,) * _n)


def _layer_spec(arr, l, **kw):
    shape = arr.shape[1:]
    return pl.BlockSpec((None,) + shape, lambda i, _l=l, _n=len(shape): (_l,) + (0,) * _n, **kw)


def _prompt_mixer(x, rope, wl, l, consts):
    t_len = x.shape[0]
    row_spec = lambda w: pl.BlockSpec((TILE, w), lambda i: (i, 0))
    smalls = [wl['nmix'], wl['lbp'], wl['aon'], wl['cw'], wl['cvec'], wl['qk']]
    in_specs = ([row_spec(D_MODEL), _const_spec(rope[0].shape), _const_spec(rope[1].shape),
                 _layer_spec(wl['w_in'], l), _layer_spec(wl['w_out'], l)] + [_layer_spec(a, l) for a in smalls]
                + [pl.BlockSpec(memory_space=pltpu.SMEM)]
                + [_const_spec(a.shape) for a in consts])
    out_shape = (jax.ShapeDtypeStruct((t_len, D_MODEL), f32),
                 jax.ShapeDtypeStruct((A_W, A_W), f32),
                 jax.ShapeDtypeStruct((CONV_WIDTH - 1, B_W), f32),
                 jax.ShapeDtypeStruct((WINDOW, KV_W), f32),
                 jax.ShapeDtypeStruct((WINDOW, KV_W), f32))
    out_specs = (row_spec(D_MODEL), _const_spec((A_W, A_W)), _const_spec((CONV_WIDTH - 1, B_W)),
                 _const_spec((WINDOW, KV_W)), _const_spec((WINDOW, KV_W)))
    return pl.pallas_call(
        functools.partial(_mixer_kernel, l), grid=(t_len // TILE,), in_specs=in_specs, out_specs=out_specs,
        out_shape=out_shape,
        scratch_shapes=[pltpu.VMEM((TILE, IN_COLS), f32), pltpu.VMEM((TILE + HIST, B_W), f32),
                        pltpu.VMEM((TILE, A_W), f32), pltpu.VMEM((A_W, A_W), f32)],
        compiler_params=pltpu.CompilerParams(dimension_semantics=("arbitrary",), vmem_limit_bytes=VMEM_LIMIT),
        name="prompt_mixer",
    )(x, rope[0], rope[1], wl['w_in'], wl['w_out'], *smalls, wl['sinks'], *consts)


def _ffn_kernel(h_ref, p_ref, wg_ref, wu_ref, wd_ref, wpg_ref, wpp_ref, norms_ref, y_ref):
    h = h_ref[...]
    hn = _rmsnorm_rows(h, norms_ref[0:1, :]).astype(bf16)
    gt = jnp.dot(hn, wg_ref[...], preferred_element_type=f32)
    up = jnp.dot(hn, wu_ref[...], preferred_element_type=f32)
    act = (gt * _sigmoid(gt) * up).astype(bf16)
    h = h + jnp.dot(act, wd_ref[...], preferred_element_type=f32)
    pn = _rmsnorm_rows(h, norms_ref[1:2, :]).astype(bf16)
    gate = _sigmoid(jnp.dot(pn, wpg_ref[...], preferred_element_type=f32))
    y_ref[...] = h + gate * jnp.dot(p_ref[...].astype(bf16), wpp_ref[...], preferred_element_type=f32)


def _ffn(h, p, wl, l):
    n = h.shape[0]
    tile = min(FFN_TILE, n)
    row_spec = lambda w: pl.BlockSpec((tile, w), lambda i: (i, 0))
    ws = [wl['w_gate'], wl['w_up'], wl['w_down'], wl['w_ple_gate'], wl['w_ple_proj']]
    return pl.pallas_call(
        _ffn_kernel, grid=(n // tile,),
        in_specs=[row_spec(D_MODEL), pl.BlockSpec((None, tile, PLE_DIM), lambda i, _l=l: (_l, i, 0))]
        + [_layer_spec(a, l, pipeline_mode=pl.Buffered(1)) for a in ws] + [_layer_spec(wl['fnorms'], l)],
        out_specs=row_spec(D_MODEL), out_shape=jax.ShapeDtypeStruct((n, D_MODEL), f32),
        compiler_params=pltpu.CompilerParams(dimension_semantics=("arbitrary",), vmem_limit_bytes=VMEM_LIMIT),
        name="ffn_ple",
    )(h, p, *ws, wl['fnorms'])


N_HEADS = C_W // HEAD


def _sample_in_kernel(n_tok, x_ref, cos_ref, sin_ref, win_ref, nmix_ref, qk_ref, e_ref, z_ref, qx_ref, k_ref,
                      piece_ref):
    n_seq = x_ref.shape[0] // n_tok
    xn = _rmsnorm_rows(x_ref[...], nmix_ref[...])
    z = jnp.dot(xn.astype(bf16), win_ref[...], preferred_element_type=f32)
    z_ref[...] = z
    e = e_ref[...]
    c1 = 4 * A_W + 2 * B_W
    cos, sin = cos_ref[...], sin_ref[...]
    qn = jnp.concatenate([_head_norm(z[:, c1 + j * 256:c1 + (j + 1) * 256], e,
                                     jnp.concatenate([qk_ref[0:1, :]] * 2, axis=1)) for j in range(2)], axis=1)
    q = _rope(qn, jnp.concatenate([cos] * 4, axis=1), jnp.concatenate([sin] * 4, axis=1)) * (HEAD ** -0.5)
    k_ref[...] = _rope(_head_norm(z[:, c1 + C_W:c1 + C_W + KV_W], e[0:KV_W, 0:KV_W], qk_ref[1:2, :]), cos, sin)
    lo_f = lax.broadcasted_iota(jnp.int32, (x_ref.shape[0], KV_W), 1) < HEAD
    for h in range(N_HEADS):
        pair, g = q[:, (h // 2) * KV_W:(h // 2 + 1) * KV_W], h // 4
        own = jnp.where(lo_f if h % 2 == 0 else jnp.logical_not(lo_f), pair, 0.0)
        piece_ref[...] = own if h % 2 == g else pltpu.roll(own, HEAD, 1)
        for t in range(n_tok):
            qx_ref[pl.ds(h * n_tok + t, n_seq, stride=N_HEADS * n_tok), :] = piece_ref[pl.ds(t, n_seq, stride=n_tok), :]


def _sample_in(x, cos, sin, wl, l, e256, n_tok):
    n = x.shape[0]
    args = (x, cos, sin, wl['w_in'], wl['nmix'], wl['qk'], e256)
    specs = [_const_spec(a.shape) for a in args]
    specs[3:6] = [_layer_spec(a, l) for a in args[3:6]]
    return pl.pallas_call(
        functools.partial(_sample_in_kernel, n_tok), grid=(1,), in_specs=specs,
        out_specs=(_const_spec((n, IN_COLS)), _const_spec((n * N_HEADS, KV_W)), _const_spec((n, KV_W))),
        out_shape=(jax.ShapeDtypeStruct((n, IN_COLS), f32), jax.ShapeDtypeStruct((n * N_HEADS, KV_W), f32),
                   jax.ShapeDtypeStruct((n, KV_W), f32)),
        scratch_shapes=[pltpu.VMEM((n, KV_W), f32)],
        compiler_params=pltpu.CompilerParams(dimension_semantics=("arbitrary",), vmem_limit_bytes=VMEM_LIMIT),
        name="sample_in",
    )(*args)


def _sample_hgrn_kernel(n_tok, q_ref, f_ref, v_ref, s_ref, lbp_ref, o_ref, so_ref, st_scr, qt_scr, ft_scr, kt_scr,
                        vt_scr, ot_scr):
    n_seq = s_ref.shape[0]
    st_scr[...] = s_ref[...].T
    for t in range(n_tok):
        zf = f_ref[pl.ds(t, n_seq, stride=n_tok), :]
        logf, ls = _log_forget(zf, lbp_ref[0:1, :], lbp_ref[1:2, :])
        ft_scr[t] = jnp.exp(logf).T
        kt_scr[t] = (lbp_ref[2:3, :] * jnp.exp(ls - zf)).T
        qt_scr[t] = q_ref[pl.ds(t, n_seq, stride=n_tok), :].T
        vt_scr[t] = v_ref[pl.ds(t, n_seq, stride=n_tok), :].T
        ot_scr[t] = jnp.zeros((2 * HEAD, n_seq), f32)

    def body(hk, carry):
        r0 = pl.multiple_of(hk * HEAD, HEAD)
        v0 = pl.multiple_of((hk // HEAD) * HEAD, HEAD)
        blk = st_scr[pl.ds(r0, HEAD), :]
        for t in range(n_tok):
            blk = blk * ft_scr[t, pl.ds(hk, 1), :] + kt_scr[t, pl.ds(hk, 1), :] * vt_scr[t, pl.ds(v0, HEAD), :]
            ot_scr[t, pl.ds(v0, HEAD), :] = ot_scr[t, pl.ds(v0, HEAD), :] + qt_scr[t, pl.ds(hk, 1), :] * blk
        st_scr[pl.ds(r0, HEAD), :] = blk
        return carry

    lax.fori_loop(0, 2 * HEAD, body, 0)
    so_ref[...] = st_scr[...].T
    for t in range(n_tok):
        o_ref[pl.ds(t, n_seq, stride=n_tok), :] = ot_scr[t].T


def _sample_hgrn(z, state, lbp, l, n_tok):
    n = z.shape[0]
    n_seq = state.shape[1]
    blk = 2 * HEAD * HEAD
    col = lambda j0: pl.BlockSpec((n, 2 * HEAD), lambda i, _j=j0: (0, _j + i))
    return pl.pallas_call(
        functools.partial(_sample_hgrn_kernel, n_tok), grid=(2,),
        in_specs=[col(0), col(2), col(4), pl.BlockSpec((None, n_seq, blk), lambda i, _l=l: (_l, 0, i)),
                  pl.BlockSpec((None, 3, 2 * HEAD), lambda i, _l=l: (_l, 0, i))],
        out_specs=(pl.BlockSpec((n, 2 * HEAD), lambda i: (0, i)), pl.BlockSpec((n_seq, blk), lambda i: (0, i))),
        out_shape=(jax.ShapeDtypeStruct((n, A_W), f32), jax.ShapeDtypeStruct(state.shape[1:], f32)),
        scratch_shapes=[pltpu.VMEM((blk, n_seq), f32)] + [pltpu.VMEM((n_tok, 2 * HEAD, n_seq), f32)] * 5,
        compiler_params=pltpu.CompilerParams(dimension_semantics=("arbitrary",), vmem_limit_bytes=VMEM_LIMIT),
        name="sample_hgrn",
    )(z, z, z, state, lbp)


def _sample_conv_kernel(n_tok, u0_ref, u1_ref, g0_ref, g1_ref, buf_ref, cw_ref, cvec_ref, e_ref, ob0_ref, ob1_ref,
                        new_ref):
    n_seq = buf_ref.shape[0]
    n_hist = CONV_WIDTH - 1
    xs = [buf_ref[:, j, :] for j in range(n_hist)]
    for t in range(n_tok):
        rows = pl.ds(t, n_seq, stride=n_tok)
        u = jnp.concatenate([u0_ref[rows, :], u1_ref[rows, :]], axis=1)
        g = jnp.concatenate([g0_ref[rows, :], g1_ref[rows, :]], axis=1)
        xs.append(u * _sigmoid(g))
    for t in range(n_tok):
        acc = jnp.zeros((n_seq, B_W), f32)
        for j in range(CONV_WIDTH):
            acc = acc + xs[t + j] * cw_ref[j:j + 1, :]
        y = acc + cvec_ref[0:1, :]
        ob = _group_ln_silu(y, e_ref[...], cvec_ref[1:2, :], cvec_ref[2:3, :])
        ob0_ref[pl.ds(t, n_seq, stride=n_tok), :] = ob[:, 0:KV_W]
        ob1_ref[pl.ds(t, n_seq, stride=n_tok), :] = ob[:, KV_W:B_W]
    for j in range(n_hist):
        new_ref[:, j, :] = xs[j + n_tok]


def _sample_conv(z, buf, wl, l, e256, n_tok):
    n = z.shape[0]
    col = lambda j: pl.BlockSpec((n, KV_W), lambda i, _j=j: (0, _j))
    half = jax.ShapeDtypeStruct((n, KV_W), f32)
    return pl.pallas_call(
        functools.partial(_sample_conv_kernel, n_tok), grid=(1,),
        in_specs=[col(8), col(9), col(10), col(11), _layer_spec(buf, l), _layer_spec(wl['cw'], l),
                  _layer_spec(wl['cvec'], l), _const_spec(e256.shape)],
        out_specs=(_const_spec((n, KV_W)), _const_spec((n, KV_W)), _const_spec(buf.shape[1:])),
        out_shape=(half, half, jax.ShapeDtypeStruct(buf.shape[1:], f32)),
        compiler_params=pltpu.CompilerParams(dimension_semantics=("arbitrary",), vmem_limit_bytes=VMEM_LIMIT),
        name="sample_conv",
    )(z, z, z, z, buf, wl['cw'], wl['cvec'], e256)


SEQ_BLOCK = 8


def _sample_attn_kernel(n_tok, qx_ref, k_ref, v_ref, kc_ref, vc_ref, sink_ref, bias_ref, ox_ref, kn_ref, vn_ref):
    q_rows = N_HEADS * n_tok
    sink, bias = sink_ref[...], bias_ref[...]
    for b in range(SEQ_BLOCK):
        rows = slice(b * n_tok, (b + 1) * n_tok)
        k_all = jnp.concatenate([kc_ref[b], k_ref[rows, :]], axis=0)
        v_all = jnp.concatenate([vc_ref[b], v_ref[rows, :]], axis=0)
        kn_ref[b] = k_all[n_tok:, :]
        vn_ref[b] = v_all[n_tok:, :]
        lhs = qx_ref[b * q_rows:(b + 1) * q_rows, :].astype(bf16)
        s = lax.dot_general(lhs, k_all.astype(bf16), (((1,), (1,)), ((), ())), preferred_element_type=f32) + bias
        mx = jnp.maximum(jnp.max(s, axis=-1, keepdims=True), sink)
        pr = jnp.exp(s - mx)
        rden = 1.0 / (jnp.sum(pr, axis=-1, keepdims=True) + jnp.exp(sink - mx))
        ox_ref[b * q_rows:(b + 1) * q_rows, :] = jnp.dot(pr.astype(bf16), v_all.astype(bf16),
                                                         preferred_element_type=f32) * rden


def _sample_attn(qx, k, z, kc, vc, sink_col, bias, l, n_tok):
    n_seq = kc.shape[1]
    rows = SEQ_BLOCK * n_tok
    q_rows = SEQ_BLOCK * N_HEADS * n_tok
    v_col = (4 * A_W + 2 * B_W + C_W + KV_W) // KV_W
    cache_in = pl.BlockSpec((None, SEQ_BLOCK) + kc.shape[2:], lambda i, _l=l: (_l, i, 0, 0))
    cache_out = pl.BlockSpec((SEQ_BLOCK,) + kc.shape[2:], lambda i: (i, 0, 0))
    return pl.pallas_call(
        functools.partial(_sample_attn_kernel, n_tok), grid=(n_seq // SEQ_BLOCK,),
        in_specs=[pl.BlockSpec((q_rows, KV_W), lambda i: (i, 0)), pl.BlockSpec((rows, KV_W), lambda i: (i, 0)),
                  pl.BlockSpec((rows, KV_W), lambda i: (i, v_col)), cache_in, cache_in,
                  _layer_spec(sink_col, l), _const_spec(bias.shape)],
        out_specs=(pl.BlockSpec((q_rows, KV_W), lambda i: (i, 0)), cache_out, cache_out),
        out_shape=(jax.ShapeDtypeStruct(qx.shape, f32), jax.ShapeDtypeStruct(kc.shape[1:], f32),
                   jax.ShapeDtypeStruct(vc.shape[1:], f32)),
        compiler_params=pltpu.CompilerParams(dimension_semantics=("arbitrary",), vmem_limit_bytes=VMEM_LIMIT),
        name="sample_attn",
    )(qx, k, z, kc, vc, sink_col, bias)


def _sample_out_kernel(n_tok, x_ref, oa_ref, g_ref, ob0_ref, ob1_ref, ox_ref, wout_ref, aon_ref, e_ref, h_ref,
                       oc_ref):
    n_seq = x_ref.shape[0] // n_tok
    gate = g_ref[...]
    oa = _head_norm(oa_ref[...], e_ref[...], aon_ref[...]) * (gate * _sigmoid(gate))
    lo_f = lax.broadcasted_iota(jnp.int32, (n_seq, KV_W), 1) < HEAD
    for p in range(N_HEADS // 2):
        g = p // 2
        for t in range(n_tok):
            even = ox_ref[pl.ds((2 * p) * n_tok + t, n_seq, stride=N_HEADS * n_tok), :]
            odd = ox_ref[pl.ds((2 * p + 1) * n_tok + t, n_seq, stride=N_HEADS * n_tok), :]
            even = even if g == 0 else pltpu.roll(even, HEAD, 1)
            odd = odd if g == 1 else pltpu.roll(odd, HEAD, 1)
            oc_ref[p, pl.ds(t, n_seq, stride=n_tok), :] = jnp.where(lo_f, even, odd)
    mix = jnp.concatenate([oa, ob0_ref[...], ob1_ref[...]] + [oc_ref[p] for p in range(N_HEADS // 2)],
                          axis=1).astype(bf16)
    h_ref[...] = x_ref[...] + jnp.dot(mix, wout_ref[...], preferred_element_type=f32)


def _sample_out(x, oa, z, ob0, ob1, ox, wl, l, e256, n_tok):
    n = x.shape[0]
    gate_spec = pl.BlockSpec((n, A_W), lambda i: (0, 3))
    args = (x, oa, z, ob0, ob1, ox, wl['w_out'], wl['aon'], e256)
    specs = [_const_spec(a.shape) for a in args]
    specs[2] = gate_spec
    specs[6:8] = [_layer_spec(a, l) for a in args[6:8]]
    return pl.pallas_call(
        functools.partial(_sample_out_kernel, n_tok), grid=(1,), in_specs=specs,
        out_specs=_const_spec((n, D_MODEL)), out_shape=jax.ShapeDtypeStruct((n, D_MODEL), f32),
        scratch_shapes=[pltpu.VMEM((N_HEADS // 2, n, KV_W), f32)],
        compiler_params=pltpu.CompilerParams(dimension_semantics=("arbitrary",), vmem_limit_bytes=VMEM_LIMIT),
        name="sample_out",
    )(*args)


def _rope_cos_sin(pos):
    half = HEAD // 2
    inv = ROPE_THETA ** (-jnp.arange(half, dtype=f32) / half)
    ang = pos[:, None] * inv[None, :]
    cos, sin = jnp.cos(ang), jnp.sin(ang)
    return jnp.tile(cos, (1, 4)), jnp.concatenate([-sin, sin, -sin, sin], axis=1)


def _rope_prompt_tables(t_len):
    half = HEAD // 2
    inv = ROPE_THETA ** (-jnp.arange(half, dtype=f32) / half)
    sign = jnp.concatenate([-jnp.ones((half,), f32), jnp.ones((half,), f32)] * 2)
    ang_r = jnp.arange(TILE, dtype=f32)[:, None] * inv[None, :]
    ang_t = (jnp.arange(t_len // TILE, dtype=f32) * TILE)[:, None] * inv[None, :]
    cos_r, sin_r = jnp.tile(jnp.cos(ang_r), (1, 4)), jnp.tile(jnp.sin(ang_r), (1, 4))
    rows = jnp.stack([cos_r, sin_r, cos_r * sign, sin_r * sign])
    tiles = jnp.stack([jnp.tile(jnp.cos(ang_t), (1, 4)), jnp.tile(jnp.sin(ang_t), (1, 4))])
    return rows, tiles


def _stacked_weights(lbs, n_tok, w_in, a_onorm, conv_w, conv_b, conv_ln_g, conv_ln_b, q_norm, k_norm, sinks, w_out,
                     norm_mix, norm_ffn, w_gate, w_up, w_down, ple_norm, w_ple_gate, w_ple_proj):
    return dict(
        w_in=w_in.astype(bf16), w_out=w_out.astype(bf16), w_gate=w_gate.astype(bf16), w_up=w_up.astype(bf16),
        w_down=w_down.astype(bf16), w_ple_gate=w_ple_gate.astype(bf16), w_ple_proj=w_ple_proj.astype(bf16),
        nmix=norm_mix[:, None, :],
        lbp=jnp.stack([jnp.log(lbs), jnp.log1p(-lbs), 1.0 - lbs], axis=1),
        aon=jnp.tile(a_onorm, (1, 4))[:, None, :],
        cw=conv_w,
        cvec=jnp.stack([conv_b, conv_ln_g, conv_ln_b], axis=1),
        qk=jnp.stack([jnp.tile(q_norm, (1, 2)), jnp.tile(k_norm, (1, 2))], axis=1),
        sinks=sinks,
        sink_col=jnp.repeat(sinks, n_tok, axis=1)[:, :, None],
        fnorms=jnp.stack([norm_ffn, ple_norm], axis=1),
    )


def kernel(x_prompt, x_sample, state_hgrn, state_conv, cache_swa_k, cache_swa_v, p_prompt, p_sample, a_lower, w_in, a_onorm, conv_w, conv_b, conv_ln_g, conv_ln_b, q_norm, k_norm, sinks, w_out, norm_mix, norm_ffn, w_gate, w_up, w_down, ple_norm, w_ple_gate, w_ple_proj):
    depth = w_in.shape[0]
    t_len = x_prompt.shape[1]
    n_seq, n_tok = x_sample.shape[0], x_sample.shape[1]
    past_len = 16384
    assert x_prompt.shape[0] == 1 and t_len % TILE == 0

    lbs = jnp.cumsum(jax.nn.softmax(a_lower.astype(f32), axis=0), axis=0)
    lbs = lbs - lbs[0:1]
    consts_np = _np_constants()
    e256 = jnp.asarray(consts_np[0], bf16)
    consts = (e256, jnp.asarray(consts_np[1], bf16), jnp.asarray(consts_np[2], f32), jnp.asarray(consts_np[3], bf16),
              jnp.asarray(consts_np[4], f32))

    rope_p = _rope_prompt_tables(t_len)
    cos_s, sin_s = _rope_cos_sin(past_len + jnp.arange(n_tok, dtype=f32))
    cos_s, sin_s = jnp.tile(cos_s, (n_seq, 1)), jnp.tile(sin_s, (n_seq, 1))
    w_buf = cache_swa_k.shape[2]
    q_tok = (np.arange(N_HEADS * n_tok) % n_tok)[:, None]
    rel = q_tok + w_buf - np.arange(w_buf + n_tok)[None, :]
    bias_s = jnp.asarray(np.where((rel >= 0) & (rel <= WINDOW), 0.0, -np.inf), f32)

    wl = _stacked_weights(lbs, n_tok, w_in, a_onorm, conv_w, conv_b, conv_ln_g, conv_ln_b, q_norm, k_norm, sinks,
                          w_out, norm_mix, norm_ffn, w_gate, w_up, w_down, ple_norm, w_ple_gate, w_ple_proj)
    p_p = p_prompt.reshape(depth, t_len, PLE_DIM)
    p_s = p_sample.reshape(depth, n_seq * n_tok, PLE_DIM)
    s_hgrn = state_hgrn.reshape(depth, n_seq, -1)
    kc = cache_swa_k.reshape(depth, n_seq, w_buf, KV_W)
    vc = cache_swa_v.reshape(depth, n_seq, w_buf, KV_W)

    hp = x_prompt[0]
    hs = x_sample.reshape(n_seq * n_tok, D_MODEL)
    outs = [[] for _ in range(8)]
    for l in range(depth):
        h_mid, st, conv_new, k_new, v_new = _prompt_mixer(hp, rope_p, wl, l, consts)
        hp = _ffn(h_mid, p_p, wl, l)
        s_new = jnp.stack([st[h * HEAD:(h + 1) * HEAD, h * HEAD:(h + 1) * HEAD].T for h in range(4)])
        outs[0].append(s_new[None])
        outs[1].append(conv_new[None])
        outs[2].append(k_new.reshape(1, WINDOW, 2, HEAD))
        outs[3].append(v_new.reshape(1, WINDOW, 2, HEAD))
        z, qx, k = _sample_in(hs, cos_s, sin_s, wl, l, e256, n_tok)
        oa, s_s = _sample_hgrn(z, s_hgrn, wl['lbp'], l, n_tok)
        ob0, ob1, conv_s = _sample_conv(z, state_conv, wl, l, e256, n_tok)
        ox, k_s, v_s = _sample_attn(qx, k, z, kc, vc, wl['sink_col'], bias_s, l, n_tok)
        h_mid_s = _sample_out(hs, oa, z, ob0, ob1, ox, wl, l, e256, n_tok)
        hs = _ffn(h_mid_s, p_s, wl, l)
        outs[4].append(s_s.reshape(n_seq, 4, HEAD, HEAD))
        outs[5].append(conv_s)
        outs[6].append(k_s.reshape(n_seq, w_buf, 2, HEAD))
        outs[7].append(v_s.reshape(n_seq, w_buf, 2, HEAD))
    return (hp[None], hs.reshape(n_seq, n_tok, D_MODEL)) + tuple(jnp.stack(o) for o in outs)
```

```python
import functools

import numpy as np
import jax
import jax.numpy as jnp
from jax import lax
from jax.experimental import pallas as pl
from jax.experimental.pallas import tpu as pltpu

f32 = jnp.float32
bf16 = jnp.bfloat16

D_MODEL = 1024
HEAD = 64
A_W = 256
B_W = 256
C_W = 512
KV_W = 128
IN_COLS = 4 * A_W + 2 * B_W + C_W + 2 * KV_W
D_FF = 2816
PLE_DIM = 256
WINDOW = 128
CONV_WIDTH = 31
ROPE_THETA = 10000.0
EPS = 1e-6

TILE = 256
CHUNK = 64
LEVELS = (32, 16, 8, 4, 2, 1)
SAFE_EXPONENT = 80.0
HIST = 32
FFN_TILE = 512
LOG2E = 1.4426950408889634
VMEM_LIMIT = 52 * 1024 * 1024


def _split2(x):
    hi = x.astype(bf16)
    lo = (x - hi.astype(f32)).astype(bf16)
    return hi, lo


def _split3(x):
    hi = x.astype(bf16)
    r = x - hi.astype(f32)
    mid = r.astype(bf16)
    lo = (r - mid.astype(f32)).astype(bf16)
    return hi, mid, lo


def _seg_sum(x, e):
    hi, lo = _split2(x)
    return (jnp.dot(hi, e, preferred_element_type=f32) + jnp.dot(lo, e, preferred_element_type=f32))


def _sigmoid(x):
    return 1.0 / (1.0 + jnp.exp(-x))


def _rmsnorm_rows(x, g):
    return x * lax.rsqrt(jnp.mean(x * x, axis=-1, keepdims=True) + EPS) * g


def _seg_stat(x, e):
    return jnp.dot(x.astype(bf16), e, preferred_element_type=f32)


def _head_norm(x, e, g):
    ms = _seg_stat(x * x, e) * (1.0 / HEAD)
    return x * lax.rsqrt(ms + EPS) * g


def _rope(x, cos, sin_signed):
    w = x.shape[-1]
    lane = lax.broadcasted_iota(jnp.int32, x.shape, 1)
    first = (lane % HEAD) < (HEAD // 2)
    swapped = jnp.where(first, pltpu.roll(x, w - HEAD // 2, 1), pltpu.roll(x, HEAD // 2, 1))
    return x * cos + swapped * sin_signed


def _log_forget(zf, log_lb, log_1m_lb):
    ls = jnp.minimum(zf, 0.0) - jnp.log(1.0 + jnp.exp(-jnp.abs(zf)))
    b = log_1m_lb + ls
    return jnp.maximum(log_lb, b) + jnp.log(1.0 + jnp.exp(-jnp.abs(log_lb - b))), ls


def _group_ln_silu(y, e, g, b):
    mu = _seg_sum(y, e) * (1.0 / HEAD)
    yc = y - mu
    var = _seg_stat(yc * yc, e) * (1.0 / HEAD)
    yn = yc * lax.rsqrt(var + EPS) * g + b
    return yn * _sigmoid(yn)


def _np_constants():
    lane = np.arange(256)
    e256 = (lane[:, None] // HEAD == lane[None, :] // HEAD).astype(np.float32)
    t = np.arange(TILE)
    ltri = ((t[:, None] // CHUNK == t[None, :] // CHUNK) & (t[None, :] <= t[:, None])).astype(np.float32)
    tt = np.arange(CHUNK)[:, None]
    ss = (np.arange(256) % CHUNK)[None, :]
    lvl = []
    for m in LEVELS:
        lvl.append(((tt // (2 * m) == ss // (2 * m)) & (tt % (2 * m) >= m) & (ss % (2 * m) < m)).astype(np.float32))
    lvl.append((ss <= tt).astype(np.float32))
    lvl = np.stack(lvl)
    hm = np.stack([np.broadcast_to((lane // HEAD == h)[None, :], (CHUNK, 256)) for h in range(4)]).astype(np.float32)
    keys = np.arange(2 * WINDOW)[:, None]
    qry = np.arange(4 * WINDOW)[None, :] % WINDOW
    band = (keys >= qry) & (keys <= qry + WINDOW)
    bias = np.where(np.stack([band & (keys >= WINDOW), band]), 0.0, -np.inf).astype(np.float32)
    return e256, ltri, lvl, hm, bias


def _level_ref(g, m):
    n_rows = g.shape[0]
    if 2 * m >= 8:
        g3 = g.reshape(n_rows // (2 * m), 2 * m, g.shape[1])
        return jnp.broadcast_to(g3[:, m - 1:m, :], g3.shape).reshape(g.shape)
    g3 = g.reshape(n_rows // 8, 8, g.shape[1])
    sub = lax.broadcasted_iota(jnp.int32, g3.shape, 1)
    out = None
    for blk in range(8 // (2 * m)):
        r = blk * 2 * m + m - 1
        piece = jnp.broadcast_to(g3[:, r:r + 1, :], g3.shape)
        out = piece if out is None else jnp.where(sub >= blk * 2 * m, piece, out)
    return out.reshape(g.shape)


def _hgrn_tile(q, zf, v, lbp_ref, st_ref, st0_ref, o_ref, e, ltri_ref, lvl_ref, hm_ref):
    log_lb, log_1m_lb, one_m_lb = lbp_ref[0:1, :], lbp_ref[1:2, :], lbp_ref[2:3, :]
    logf, ls = _log_forget(zf, log_lb, log_1m_lb)
    kin = one_m_lb * jnp.exp(ls - zf)
    g3 = jnp.dot(ltri_ref[...], jnp.concatenate(_split3(logf), axis=1), preferred_element_type=f32)
    g = g3[:, 0:A_W] + g3[:, A_W:2 * A_W] + g3[:, 2 * A_W:3 * A_W]
    v_b = v.astype(bf16)
    vt_b = v.T.astype(bf16)
    e_mid = g - _level_ref(g, CHUNK // 2)
    safe = jnp.max(jnp.abs(e_mid)) < SAFE_EXPONENT

    def expand(rows_b):
        return jnp.concatenate([rows_b * hm_ref[h] for h in range(4)], axis=0)

    def finish_chunk(c, att, q_in, k_hat):
        r0, r1 = c * CHUNK, (c + 1) * CHUNK
        st = st_ref[...]
        o = jnp.dot(att.astype(bf16), expand(v_b[r0:r1]), preferred_element_type=f32)
        o = o + lax.dot_general(q_in, st.astype(bf16) * e, (((1,), (1,)), ((), ())), preferred_element_type=f32)
        zero = jnp.zeros((CHUNK, A_W), bf16)
        pair = (c // 2) * 2 * CHUNK
        rhs = jnp.concatenate([k_hat, zero] if c % 2 == 0 else [zero, k_hat], axis=0)
        upd = jnp.dot(vt_b[:, pair:pair + 2 * CHUNK], rhs, preferred_element_type=f32)
        st_ref[...] = st * jnp.exp(g[r1 - 1:r1, :]) + upd
        o_ref[r0:r1, :] = o

    st0_ref[...] = st_ref[...]
    qw = q * jnp.exp(e_mid)
    kw = kin * jnp.exp(-e_mid)
    q_t, k_t = qw.astype(bf16), kw.astype(bf16)
    for c in range(TILE // CHUNK):
        r0, r1 = c * CHUNK, (c + 1) * CHUNK
        att = lax.dot_general(q_t[r0:r1], expand(k_t[r0:r1]), (((1,), (1,)), ((), ())), preferred_element_type=f32)
        att = jnp.where(lvl_ref[len(LEVELS)] > 0.5, att, 0.0)
        g_mid, g_end = g[r0 + CHUNK // 2 - 1:r0 + CHUNK // 2, :], g[r1 - 1:r1, :]
        finish_chunk(c, att, (qw[r0:r1] * jnp.exp(g_mid)).astype(bf16),
                     (kw[r0:r1] * jnp.exp(g_end - g_mid)).astype(bf16))

    def redo_if_unsafe():
        @pl.when(jnp.logical_not(safe))
        def _():
            st_ref[...] = st0_ref[...]
            q_levels, k_levels = [], []
            for m in LEVELS:
                w = jnp.exp(-jnp.abs(g - _level_ref(g, m)))
                q_levels.append((q * w).astype(bf16))
                k_levels.append((kin * w).astype(bf16))
            q_inter = (q * jnp.exp(g)).astype(bf16)
            for c in range(TILE // CHUNK):
                r0, r1 = c * CHUNK, (c + 1) * CHUNK
                att = jnp.zeros((CHUNK, 4 * CHUNK), f32)
                for li in range(len(LEVELS)):
                    a = lax.dot_general(q_levels[li][r0:r1], expand(k_levels[li][r0:r1]),
                                        (((1,), (1,)), ((), ())), preferred_element_type=f32)
                    att = att + a * lvl_ref[li]
                k_hat = (kin[r0:r1] * jnp.exp(g[r1 - 1:r1, :] - g[r0:r1])).astype(bf16)
                finish_chunk(c, att, q_inter[r0:r1], k_hat)
            o_ref[...] = o_ref[...] + _seg_sum(q * kin, e) * v

    return redo_if_unsafe


def _conv_tile(u, hist_ref, cw_ref, cvec_ref, e):
    n = u.shape[0]
    hist_ref[HIST:HIST + n, :] = u
    lo = HIST - (CONV_WIDTH - 1)
    y = cvec_ref[0:1, :]
    for r in range(8):
        rows = n if r == 0 else n + 8
        part = None
        for o in range(lo, lo + CONV_WIDTH):
            if o % 8 == r:
                term = hist_ref[o - r:o - r + rows, :] * cw_ref[o - lo:o - lo + 1, :]
                part = term if part is None else part + term
        y = y + part[r:r + n]
    return _group_ln_silu(y, e, cvec_ref[1:2, :], cvec_ref[2:3, :])


def _attn_block(q_t, k_blk, v_t, sink_ref, bias):
    zero = jnp.zeros((HEAD, WINDOW), f32)
    pairs = []
    for g in range(2):
        cols, sinks = [], []
        for h in range(4 * g, 4 * g + 4):
            q_h = q_t[h * HEAD:(h + 1) * HEAD, :]
            cols.append(jnp.concatenate([q_h, zero] if g == 0 else [zero, q_h], axis=0))
            sinks.append(jnp.full((1, WINDOW), sink_ref[h] * LOG2E, f32))
        rhs = jnp.concatenate(cols, axis=1).astype(bf16)
        sink = jnp.concatenate(sinks, axis=1)
        s = jnp.dot(k_blk, rhs, preferred_element_type=f32) + bias
        mx = jnp.maximum(jnp.max(s, axis=0, keepdims=True), sink)
        pr = jnp.exp2(s - mx)
        rden = 1.0 / (jnp.sum(pr, axis=0, keepdims=True) + jnp.exp2(sink - mx))
        o_t = jnp.dot(v_t[g * HEAD:(g + 1) * HEAD, :], pr.astype(bf16), preferred_element_type=f32) * rden
        for j in range(2):
            pair_t = jnp.concatenate([o_t[:, (2 * j) * WINDOW:(2 * j + 1) * WINDOW],
                                      o_t[:, (2 * j + 1) * WINDOW:(2 * j + 2) * WINDOW]], axis=0)
            pairs.append(pair_t.T)
    return jnp.concatenate(pairs, axis=1)


def _mixer_kernel(layer, x_ref, rope_row_ref, rope_tile_ref, win_ref, wout_ref, nmix_ref, lbp_ref, aon_ref, cw_ref,
                  cvec_ref, qk_ref, sinks_ref, e_ref, ltri_ref, lvl_ref, hm_ref, bias_ref,
                  h_ref, st_ref, conv_ref, kn_ref, vn_ref, z_ref, hist_ref, o_ref, st0_ref):
    i = pl.program_id(0)
    sink_ref = sinks_ref.at[layer]

    @pl.when(i == 0)
    def _():
        st_ref[...] = jnp.zeros_like(st_ref)
        hist_ref[0:HIST, :] = jnp.zeros((HIST, B_W), f32)
        kn_ref[...] = jnp.zeros_like(kn_ref)
        vn_ref[...] = jnp.zeros_like(vn_ref)

    x = x_ref[...]
    xn = _rmsnorm_rows(x, nmix_ref[...])
    z_ref[...] = jnp.dot(xn.astype(bf16), win_ref[...], preferred_element_type=f32)
    e = e_ref[...]

    redo_hgrn_if_unsafe = _hgrn_tile(z_ref[:, 0:A_W], z_ref[:, A_W:2 * A_W], z_ref[:, 2 * A_W:3 * A_W], lbp_ref,
                                     st_ref, st0_ref, o_ref, e, ltri_ref, lvl_ref, hm_ref)

    c0 = 4 * A_W
    u = z_ref[:, c0:c0 + B_W] * _sigmoid(z_ref[:, c0 + B_W:c0 + 2 * B_W])
    ob = _conv_tile(u, hist_ref, cw_ref, cvec_ref, e)
    conv_ref[...] = hist_ref[TILE + HIST - (CONV_WIDTH - 1):TILE + HIST, :]
    hist_ref[0:HIST, :] = hist_ref[TILE:TILE + HIST, :]

    c1 = c0 + 2 * B_W
    cos_a, sin_a = rope_tile_ref[0, pl.ds(i, 1), :], rope_tile_ref[1, pl.ds(i, 1), :]
    cos = cos_a * rope_row_ref[0] - sin_a * rope_row_ref[1]
    sin = sin_a * rope_row_ref[2] + cos_a * rope_row_ref[3]
    qn = jnp.concatenate([_head_norm(z_ref[:, c1 + j * 256:c1 + (j + 1) * 256], e,
                                     jnp.concatenate([qk_ref[0:1, :]] * 2, axis=1)) for j in range(2)], axis=1)
    q = _rope(qn, jnp.concatenate([cos] * 4, axis=1), jnp.concatenate([sin] * 4, axis=1)) * (LOG2E * HEAD ** -0.5)
    kk = _rope(_head_norm(z_ref[:, c1 + C_W:c1 + C_W + KV_W], e[0:KV_W, 0:KV_W], qk_ref[1:2, :]), cos, sin)
    vv = z_ref[:, c1 + C_W + KV_W:c1 + C_W + 2 * KV_W]
    kcat = jnp.concatenate([kn_ref[...], kk], axis=0)
    vcat = jnp.concatenate([vn_ref[...], vv], axis=0)
    q_t = q.T
    k_b = kcat.astype(bf16)
    v_t = vcat.T.astype(bf16)
    oc = []
    for b in range(TILE // WINDOW):
        bias = bias_ref[jnp.where(i > 0, 1, 0)] if b == 0 else bias_ref[1]
        oc.append(_attn_block(q_t[:, b * WINDOW:(b + 1) * WINDOW], k_b[b * WINDOW:(b + 2) * WINDOW],
                              v_t[:, b * WINDOW:(b + 2) * WINDOW], sink_ref, bias))
    oc = jnp.concatenate(oc, axis=0)
    kn_ref[...] = kk[TILE - WINDOW:TILE]
    vn_ref[...] = vv[TILE - WINDOW:TILE]

    redo_hgrn_if_unsafe()
    gate = z_ref[:, 3 * A_W:4 * A_W]
    oa = _head_norm(o_ref[...], e, aon_ref[...]) * (gate * _sigmoid(gate))
    mix = jnp.concatenate([oa, ob, oc], axis=1).astype(bf16)
    h_ref[...] = x + jnp.dot(mix, wout_ref[...], preferred_element_type=f32)


def _const_spec(shape):
    nd = len(shape)
    return pl.BlockSpec(shape, lambda i, _n=nd: (0,) * _n)


def _layer_spec(arr, l, **kw):
    shape = arr.shape[1:]
    return pl.BlockSpec((None,) + shape, lambda i, _l=l, _n=len(shape): (_l,) + (0,) * _n, **kw)


def _prompt_mixer(x, rope, wl, l, consts):
    t_len = x.shape[0]
    row_spec = lambda w: pl.BlockSpec((TILE, w), lambda i: (i, 0))
    smalls = [wl['nmix'], wl['lbp'], wl['aon'], wl['cw'], wl['cvec'], wl['qk']]
    in_specs = ([row_spec(D_MODEL), _const_spec(rope[0].shape), _const_spec(rope[1].shape),
                 _layer_spec(wl['w_in'], l), _layer_spec(wl['w_out'], l)] + [_layer_spec(a, l) for a in smalls]
                + [pl.BlockSpec(memory_space=pltpu.SMEM)]
                + [_const_spec(a.shape) for a in consts])
    out_shape = (jax.ShapeDtypeStruct((t_len, D_MODEL), f32),
                 jax.ShapeDtypeStruct((A_W, A_W), f32),
                 jax.ShapeDtypeStruct((CONV_WIDTH - 1, B_W), f32),
                 jax.ShapeDtypeStruct((WINDOW, KV_W), f32),
                 jax.ShapeDtypeStruct((WINDOW, KV_W), f32))
    out_specs = (row_spec(D_MODEL), _const_spec((A_W, A_W)), _const_spec((CONV_WIDTH - 1, B_W)),
                 _const_spec((WINDOW, KV_W)), _const_spec((WINDOW, KV_W)))
    return pl.pallas_call(
        functools.partial(_mixer_kernel, l), grid=(t_len // TILE,), in_specs=in_specs, out_specs=out_specs,
        out_shape=out_shape,
        scratch_shapes=[pltpu.VMEM((TILE, IN_COLS), f32), pltpu.VMEM((TILE + HIST, B_W), f32),
                        pltpu.VMEM((TILE, A_W), f32), pltpu.VMEM((A_W, A_W), f32)],
        compiler_params=pltpu.CompilerParams(dimension_semantics=("arbitrary",), vmem_limit_bytes=VMEM_LIMIT),
        name="prompt_mixer",
    )(x, rope[0], rope[1], wl['w_in'], wl['w_out'], *smalls, wl['sinks'], *consts)


def _ffn_kernel(h_ref, p_ref, wg_ref, wu_ref, wd_ref, wpg_ref, wpp_ref, norms_ref, y_ref):
    h = h_ref[...]
    hn = _rmsnorm_rows(h, norms_ref[0:1, :]).astype(bf16)
    gt = jnp.dot(hn, wg_ref[...], preferred_element_type=f32)
    up = jnp.dot(hn, wu_ref[...], preferred_element_type=f32)
    act = (gt * _sigmoid(gt) * up).astype(bf16)
    h = h + jnp.dot(act, wd_ref[...], preferred_element_type=f32)
    pn = _rmsnorm_rows(h, norms_ref[1:2, :]).astype(bf16)
    gate = _sigmoid(jnp.dot(pn, wpg_ref[...], preferred_element_type=f32))
    y_ref[...] = h + gate * jnp.dot(p_ref[...].astype(bf16), wpp_ref[...], preferred_element_type=f32)


def _ffn(h, p, wl, l):
    n = h.shape[0]
    tile = min(FFN_TILE, n)
    row_spec = lambda w: pl.BlockSpec((tile, w), lambda i: (i, 0))
    ws = [wl['w_gate'], wl['w_up'], wl['w_down'], wl['w_ple_gate'], wl['w_ple_proj']]
    return pl.pallas_call(
        _ffn_kernel, grid=(n // tile,),
        in_specs=[row_spec(D_MODEL), pl.BlockSpec((None, tile, PLE_DIM), lambda i, _l=l: (_l, i, 0))]
        + [_layer_spec(a, l, pipeline_mode=pl.Buffered(1)) for a in ws] + [_layer_spec(wl['fnorms'], l)],
        out_specs=row_spec(D_MODEL), out_shape=jax.ShapeDtypeStruct((n, D_MODEL), f32),
        compiler_params=pltpu.CompilerParams(dimension_semantics=("arbitrary",), vmem_limit_bytes=VMEM_LIMIT),
        name="ffn_ple",
    )(h, p, *ws, wl['fnorms'])


N_HEADS = C_W // HEAD


def _sample_in_kernel(n_tok, x_ref, cos_ref, sin_ref, win_ref, nmix_ref, qk_ref, e_ref, z_ref, qx_ref, k_ref,
                      piece_ref):
    n_seq = x_ref.shape[0] // n_tok
    xn = _rmsnorm_rows(x_ref[...], nmix_ref[...])
    z = jnp.dot(xn.astype(bf16), win_ref[...], preferred_element_type=f32)
    z_ref[...] = z
    e = e_ref[...]
    c1 = 4 * A_W + 2 * B_W
    cos, sin = cos_ref[...], sin_ref[...]
    qn = jnp.concatenate([_head_norm(z[:, c1 + j * 256:c1 + (j + 1) * 256], e,
                                     jnp.concatenate([qk_ref[0:1, :]] * 2, axis=1)) for j in range(2)], axis=1)
    q = _rope(qn, jnp.concatenate([cos] * 4, axis=1), jnp.concatenate([sin] * 4, axis=1)) * (HEAD ** -0.5)
    k_ref[...] = _rope(_head_norm(z[:, c1 + C_W:c1 + C_W + KV_W], e[0:KV_W, 0:KV_W], qk_ref[1:2, :]), cos, sin)
    lo_f = lax.broadcasted_iota(jnp.int32, (x_ref.shape[0], KV_W), 1) < HEAD
    for h in range(N_HEADS):
        pair, g = q[:, (h // 2) * KV_W:(h // 2 + 1) * KV_W], h // 4
        own = jnp.where(lo_f if h % 2 == 0 else jnp.logical_not(lo_f), pair, 0.0)
        piece_ref[...] = own if h % 2 == g else pltpu.roll(own, HEAD, 1)
        for t in range(n_tok):
            qx_ref[pl.ds(h * n_tok + t, n_seq, stride=N_HEADS * n_tok), :] = piece_ref[pl.ds(t, n_seq, stride=n_tok), :]


def _sample_in(x, cos, sin, wl, l, e256, n_tok):
    n = x.shape[0]
    args = (x, cos, sin, wl['w_in'], wl['nmix'], wl['qk'], e256)
    specs = [_const_spec(a.shape) for a in args]
    specs[3:6] = [_layer_spec(a, l) for a in args[3:6]]
    return pl.pallas_call(
        functools.partial(_sample_in_kernel, n_tok), grid=(1,), in_specs=specs,
        out_specs=(_const_spec((n, IN_COLS)), _const_spec((n * N_HEADS, KV_W)), _const_spec((n, KV_W))),
        out_shape=(jax.ShapeDtypeStruct((n, IN_COLS), f32), jax.ShapeDtypeStruct((n * N_HEADS, KV_W), f32),
                   jax.ShapeDtypeStruct((n, KV_W), f32)),
        scratch_shapes=[pltpu.VMEM((n, KV_W), f32)],
        compiler_params=pltpu.CompilerParams(dimension_semantics=("arbitrary",), vmem_limit_bytes=VMEM_LIMIT),
        name="sample_in",
    )(*args)


def _sample_hgrn_kernel(n_tok, q_ref, f_ref, v_ref, s_ref, lbp_ref, o_ref, so_ref, st_scr, qt_scr, ft_scr, kt_scr,
                        vt_scr, ot_scr):
    n_seq = s_ref.shape[0]
    st_scr[...] = s_ref[...].T
    for t in range(n_tok):
        zf = f_ref[pl.ds(t, n_seq, stride=n_tok), :]
        logf, ls = _log_forget(zf, lbp_ref[0:1, :], lbp_ref[1:2, :])
        ft_scr[t] = jnp.exp(logf).T
        kt_scr[t] = (lbp_ref[2:3, :] * jnp.exp(ls - zf)).T
        qt_scr[t] = q_ref[pl.ds(t, n_seq, stride=n_tok), :].T
        vt_scr[t] = v_ref[pl.ds(t, n_seq, stride=n_tok), :].T
        ot_scr[t] = jnp.zeros((2 * HEAD, n_seq), f32)

    def body(hk, carry):
        r0 = pl.multiple_of(hk * HEAD, HEAD)
        v0 = pl.multiple_of((hk // HEAD) * HEAD, HEAD)
        blk = st_scr[pl.ds(r0, HEAD), :]
        for t in range(n_tok):
            blk = blk * ft_scr[t, pl.ds(hk, 1), :] + kt_scr[t, pl.ds(hk, 1), :] * vt_scr[t, pl.ds(v0, HEAD), :]
            ot_scr[t, pl.ds(v0, HEAD), :] = ot_scr[t, pl.ds(v0, HEAD), :] + qt_scr[t, pl.ds(hk, 1), :] * blk
        st_scr[pl.ds(r0, HEAD), :] = blk
        return carry

    lax.fori_loop(0, 2 * HEAD, body, 0)
    so_ref[...] = st_scr[...].T
    for t in range(n_tok):
        o_ref[pl.ds(t, n_seq, stride=n_tok), :] = ot_scr[t].T


def _sample_hgrn(z, state, lbp, l, n_tok):
    n = z.shape[0]
    n_seq = state.shape[1]
    blk = 2 * HEAD * HEAD
    col = lambda j0: pl.BlockSpec((n, 2 * HEAD), lambda i, _j=j0: (0, _j + i))
    return pl.pallas_call(
        functools.partial(_sample_hgrn_kernel, n_tok), grid=(2,),
        in_specs=[col(0), col(2), col(4), pl.BlockSpec((None, n_seq, blk), lambda i, _l=l: (_l, 0, i)),
                  pl.BlockSpec((None, 3, 2 * HEAD), lambda i, _l=l: (_l, 0, i))],
        out_specs=(pl.BlockSpec((n, 2 * HEAD), lambda i: (0, i)), pl.BlockSpec((n_seq, blk), lambda i: (0, i))),
        out_shape=(jax.ShapeDtypeStruct((n, A_W), f32), jax.ShapeDtypeStruct(state.shape[1:], f32)),
        scratch_shapes=[pltpu.VMEM((blk, n_seq), f32)] + [pltpu.VMEM((n_tok, 2 * HEAD, n_seq), f32)] * 5,
        compiler_params=pltpu.CompilerParams(dimension_semantics=("arbitrary",), vmem_limit_bytes=VMEM_LIMIT),
        name="sample_hgrn",
    )(z, z, z, state, lbp)


def _sample_conv_kernel(n_tok, u0_ref, u1_ref, g0_ref, g1_ref, buf_ref, cw_ref, cvec_ref, e_ref, ob0_ref, ob1_ref,
                        new_ref):
    n_seq = buf_ref.shape[0]
    n_hist = CONV_WIDTH - 1
    us = []
    for t in range(n_tok):
        rows = pl.ds(t, n_seq, stride=n_tok)
        u = jnp.concatenate([u0_ref[rows, :], u1_ref[rows, :]], axis=1)
        g = jnp.concatenate([g0_ref[rows, :], g1_ref[rows, :]], axis=1)
        us.append(u * _sigmoid(g))
    buf = buf_ref[...]
    for t in range(n_tok):
        taps = cw_ref[0:n_hist - t, :]
        if t:
            taps = jnp.concatenate([jnp.zeros((t, B_W), f32), taps], axis=0)
        y = jnp.sum(buf * taps[None], axis=1) + cvec_ref[0:1, :]
        for s in range(t + 1):
            j = n_hist + s - t
            y = y + us[s] * cw_ref[j:j + 1, :]
        ob = _group_ln_silu(y, e_ref[...], cvec_ref[1:2, :], cvec_ref[2:3, :])
        ob0_ref[pl.ds(t, n_seq, stride=n_tok), :] = ob[:, 0:KV_W]
        ob1_ref[pl.ds(t, n_seq, stride=n_tok), :] = ob[:, KV_W:B_W]
    new_ref[:, 0:n_hist - n_tok, :] = buf_ref[:, n_tok:n_hist, :]
    for s in range(n_tok):
        new_ref[:, n_hist - n_tok + s, :] = us[s]


def _sample_conv(z, buf, wl, l, e256, n_tok):
    n = z.shape[0]
    col = lambda j: pl.BlockSpec((n, KV_W), lambda i, _j=j: (0, _j))
    half = jax.ShapeDtypeStruct((n, KV_W), f32)
    return pl.pallas_call(
        functools.partial(_sample_conv_kernel, n_tok), grid=(1,),
        in_specs=[col(8), col(9), col(10), col(11), _layer_spec(buf, l), _layer_spec(wl['cw'], l),
                  _layer_spec(wl['cvec'], l), _const_spec(e256.shape)],
        out_specs=(_const_spec((n, KV_W)), _const_spec((n, KV_W)), _const_spec(buf.shape[1:])),
        out_shape=(half, half, jax.ShapeDtypeStruct(buf.shape[1:], f32)),
        compiler_params=pltpu.CompilerParams(dimension_semantics=("arbitrary",), vmem_limit_bytes=VMEM_LIMIT),
        name="sample_conv",
    )(z, z, z, z, buf, wl['cw'], wl['cvec'], e256)


SEQ_BLOCK = 8


def _sample_attn_kernel(n_tok, qx_ref, k_ref, v_ref, kc_ref, vc_ref, sink_ref, bias_ref, ox_ref, kn_ref, vn_ref):
    q_rows = N_HEADS * n_tok
    sink, bias = sink_ref[...], bias_ref[...]
    for b in range(SEQ_BLOCK):
        rows = slice(b * n_tok, (b + 1) * n_tok)
        k_all = jnp.concatenate([kc_ref[b], k_ref[rows, :]], axis=0)
        v_all = jnp.concatenate([vc_ref[b], v_ref[rows, :]], axis=0)
        kn_ref[b] = k_all[n_tok:, :]
        vn_ref[b] = v_all[n_tok:, :]
        lhs = qx_ref[b * q_rows:(b + 1) * q_rows, :].astype(bf16)
        s = lax.dot_general(lhs, k_all.astype(bf16), (((1,), (1,)), ((), ())), preferred_element_type=f32) + bias
        mx = jnp.maximum(jnp.max(s, axis=-1, keepdims=True), sink)
        pr = jnp.exp(s - mx)
        rden = 1.0 / (jnp.sum(pr, axis=-1, keepdims=True) + jnp.exp(sink - mx))
        ox_ref[b * q_rows:(b + 1) * q_rows, :] = jnp.dot(pr.astype(bf16), v_all.astype(bf16),
                                                         preferred_element_type=f32) * rden


def _sample_attn(qx, k, z, kc, vc, sink_col, bias, l, n_tok):
    n_seq = kc.shape[1]
    rows = SEQ_BLOCK * n_tok
    q_rows = SEQ_BLOCK * N_HEADS * n_tok
    v_col = (4 * A_W + 2 * B_W + C_W + KV_W) // KV_W
    cache_in = pl.BlockSpec((None, SEQ_BLOCK) + kc.shape[2:], lambda i, _l=l: (_l, i, 0, 0))
    cache_out = pl.BlockSpec((SEQ_BLOCK,) + kc.shape[2:], lambda i: (i, 0, 0))
    return pl.pallas_call(
        functools.partial(_sample_attn_kernel, n_tok), grid=(n_seq // SEQ_BLOCK,),
        in_specs=[pl.BlockSpec((q_rows, KV_W), lambda i: (i, 0)), pl.BlockSpec((rows, KV_W), lambda i: (i, 0)),
                  pl.BlockSpec((rows, KV_W), lambda i: (i, v_col)), cache_in, cache_in,
                  _layer_spec(sink_col, l), _const_spec(bias.shape)],
        out_specs=(pl.BlockSpec((q_rows, KV_W), lambda i: (i, 0)), cache_out, cache_out),
        out_shape=(jax.ShapeDtypeStruct(qx.shape, f32), jax.ShapeDtypeStruct(kc.shape[1:], f32),
                   jax.ShapeDtypeStruct(vc.shape[1:], f32)),
        compiler_params=pltpu.CompilerParams(dimension_semantics=("arbitrary",), vmem_limit_bytes=VMEM_LIMIT),
        name="sample_attn",
    )(qx, k, z, kc, vc, sink_col, bias)


def _sample_out_kernel(n_tok, x_ref, oa_ref, g_ref, ob0_ref, ob1_ref, ox_ref, wout_ref, aon_ref, e_ref, h_ref,
                       oc_ref):
    n_seq = x_ref.shape[0] // n_tok
    gate = g_ref[...]
    oa = _head_norm(oa_ref[...], e_ref[...], aon_ref[...]) * (gate * _sigmoid(gate))
    lo_f = lax.broadcasted_iota(jnp.int32, (n_seq, KV_W), 1) < HEAD
    for p in range(N_HEADS // 2):
        g = p // 2
        for t in range(n_tok):
            even = ox_ref[pl.ds((2 * p) * n_tok + t, n_seq, stride=N_HEADS * n_tok), :]
            odd = ox_ref[pl.ds((2 * p + 1) * n_tok + t, n_seq, stride=N_HEADS * n_tok), :]
            even = even if g == 0 else pltpu.roll(even, HEAD, 1)
            odd = odd if g == 1 else pltpu.roll(odd, HEAD, 1)
            oc_ref[p, pl.ds(t, n_seq, stride=n_tok), :] = jnp.where(lo_f, even, odd)
    mix = jnp.concatenate([oa, ob0_ref[...], ob1_ref[...]] + [oc_ref[p] for p in range(N_HEADS // 2)],
                          axis=1).astype(bf16)
    h_ref[...] = x_ref[...] + jnp.dot(mix, wout_ref[...], preferred_element_type=f32)


def _sample_out(x, oa, z, ob0, ob1, ox, wl, l, e256, n_tok):
    n = x.shape[0]
    gate_spec = pl.BlockSpec((n, A_W), lambda i: (0, 3))
    args = (x, oa, z, ob0, ob1, ox, wl['w_out'], wl['aon'], e256)
    specs = [_const_spec(a.shape) for a in args]
    specs[2] = gate_spec
    specs[6:8] = [_layer_spec(a, l) for a in args[6:8]]
    return pl.pallas_call(
        functools.partial(_sample_out_kernel, n_tok), grid=(1,), in_specs=specs,
        out_specs=_const_spec((n, D_MODEL)), out_shape=jax.ShapeDtypeStruct((n, D_MODEL), f32),
        scratch_shapes=[pltpu.VMEM((N_HEADS // 2, n, KV_W), f32)],
        compiler_params=pltpu.CompilerParams(dimension_semantics=("arbitrary",), vmem_limit_bytes=VMEM_LIMIT),
        name="sample_out",
    )(*args)


def _rope_cos_sin(pos):
    half = HEAD // 2
    inv = ROPE_THETA ** (-jnp.arange(half, dtype=f32) / half)
    ang = pos[:, None] * inv[None, :]
    cos, sin = jnp.cos(ang), jnp.sin(ang)
    return jnp.tile(cos, (1, 4)), jnp.concatenate([-sin, sin, -sin, sin], axis=1)


def _rope_prompt_tables(t_len):
    half = HEAD // 2
    inv = ROPE_THETA ** (-jnp.arange(half, dtype=f32) / half)
    sign = jnp.concatenate([-jnp.ones((half,), f32), jnp.ones((half,), f32)] * 2)
    ang_r = jnp.arange(TILE, dtype=f32)[:, None] * inv[None, :]
    ang_t = (jnp.arange(t_len // TILE, dtype=f32) * TILE)[:, None] * inv[None, :]
    cos_r, sin_r = jnp.tile(jnp.cos(ang_r), (1, 4)), jnp.tile(jnp.sin(ang_r), (1, 4))
    rows = jnp.stack([cos_r, sin_r, cos_r * sign, sin_r * sign])
    tiles = jnp.stack([jnp.tile(jnp.cos(ang_t), (1, 4)), jnp.tile(jnp.sin(ang_t), (1, 4))])
    return rows, tiles


def _stacked_weights(lbs, n_tok, w_in, a_onorm, conv_w, conv_b, conv_ln_g, conv_ln_b, q_norm, k_norm, sinks, w_out,
                     norm_mix, norm_ffn, w_gate, w_up, w_down, ple_norm, w_ple_gate, w_ple_proj):
    return dict(
        w_in=w_in.astype(bf16), w_out=w_out.astype(bf16), w_gate=w_gate.astype(bf16), w_up=w_up.astype(bf16),
        w_down=w_down.astype(bf16), w_ple_gate=w_ple_gate.astype(bf16), w_ple_proj=w_ple_proj.astype(bf16),
        nmix=norm_mix[:, None, :],
        lbp=jnp.stack([jnp.log(lbs), jnp.log1p(-lbs), 1.0 - lbs], axis=1),
        aon=jnp.tile(a_onorm, (1, 4))[:, None, :],
        cw=conv_w,
        cvec=jnp.stack([conv_b, conv_ln_g, conv_ln_b], axis=1),
        qk=jnp.stack([jnp.tile(q_norm, (1, 2)), jnp.tile(k_norm, (1, 2))], axis=1),
        sinks=sinks,
        sink_col=jnp.repeat(sinks, n_tok, axis=1)[:, :, None],
        fnorms=jnp.stack([norm_ffn, ple_norm], axis=1),
    )


def kernel(x_prompt, x_sample, state_hgrn, state_conv, cache_swa_k, cache_swa_v, p_prompt, p_sample, a_lower, w_in, a_onorm, conv_w, conv_b, conv_ln_g, conv_ln_b, q_norm, k_norm, sinks, w_out, norm_mix, norm_ffn, w_gate, w_up, w_down, ple_norm, w_ple_gate, w_ple_proj):
    depth = w_in.shape[0]
    t_len = x_prompt.shape[1]
    n_seq, n_tok = x_sample.shape[0], x_sample.shape[1]
    past_len = 16384
    assert x_prompt.shape[0] == 1 and t_len % TILE == 0

    lbs = jnp.cumsum(jax.nn.softmax(a_lower.astype(f32), axis=0), axis=0)
    lbs = lbs - lbs[0:1]
    consts_np = _np_constants()
    e256 = jnp.asarray(consts_np[0], bf16)
    consts = (e256, jnp.asarray(consts_np[1], bf16), jnp.asarray(consts_np[2], f32), jnp.asarray(consts_np[3], bf16),
              jnp.asarray(consts_np[4], f32))

    rope_p = _rope_prompt_tables(t_len)
    cos_s, sin_s = _rope_cos_sin(past_len + jnp.arange(n_tok, dtype=f32))
    cos_s, sin_s = jnp.tile(cos_s, (n_seq, 1)), jnp.tile(sin_s, (n_seq, 1))
    w_buf = cache_swa_k.shape[2]
    q_tok = (np.arange(N_HEADS * n_tok) % n_tok)[:, None]
    rel = q_tok + w_buf - np.arange(w_buf + n_tok)[None, :]
    bias_s = jnp.asarray(np.where((rel >= 0) & (rel <= WINDOW), 0.0, -np.inf), f32)

    wl = _stacked_weights(lbs, n_tok, w_in, a_onorm, conv_w, conv_b, conv_ln_g, conv_ln_b, q_norm, k_norm, sinks,
                          w_out, norm_mix, norm_ffn, w_gate, w_up, w_down, ple_norm, w_ple_gate, w_ple_proj)
    p_p = p_prompt.reshape(depth, t_len, PLE_DIM)
    p_s = p_sample.reshape(depth, n_seq * n_tok, PLE_DIM)
    s_hgrn = state_hgrn.reshape(depth, n_seq, -1)
    kc = cache_swa_k.reshape(depth, n_seq, w_buf, KV_W)
    vc = cache_swa_v.reshape(depth, n_seq, w_buf, KV_W)

    hp = x_prompt[0]
    hs = x_sample.reshape(n_seq * n_tok, D_MODEL)
    outs = [[] for _ in range(8)]
    for l in range(depth):
        h_mid, st, conv_new, k_new, v_new = _prompt_mixer(hp, rope_p, wl, l, consts)
        hp = _ffn(h_mid, p_p, wl, l)
        s_new = jnp.stack([st[h * HEAD:(h + 1) * HEAD, h * HEAD:(h + 1) * HEAD].T for h in range(4)])
        outs[0].append(s_new[None])
        outs[1].append(conv_new[None])
        outs[2].append(k_new.reshape(1, WINDOW, 2, HEAD))
        outs[3].append(v_new.reshape(1, WINDOW, 2, HEAD))
        z, qx, k = _sample_in(hs, cos_s, sin_s, wl, l, e256, n_tok)
        oa, s_s = _sample_hgrn(z, s_hgrn, wl['lbp'], l, n_tok)
        ob0, ob1, conv_s = _sample_conv(z, state_conv, wl, l, e256, n_tok)
        ox, k_s, v_s = _sample_attn(qx, k, z, kc, vc, wl['sink_col'], bias_s, l, n_tok)
        h_mid_s = _sample_out(hs, oa, z, ob0, ob1, ox, wl, l, e256, n_tok)
        hs = _ffn(h_mid_s, p_s, wl, l)
        outs[4].append(s_s.reshape(n_seq, 4, HEAD, HEAD))
        outs[5].append(conv_s)
        outs[6].append(k_s.reshape(n_seq, w_buf, 2, HEAD))
        outs[7].append(v_s.reshape(n_seq, w_buf, 2, HEAD))
    return (hp[None], hs.reshape(n_seq, n_tok, D_MODEL)) + tuple(jnp.stack(o) for o in outs)
```

```python
import functools

import numpy as np
import jax
import jax.numpy as jnp
from jax import lax
from jax.experimental import pallas as pl
from jax.experimental.pallas import tpu as pltpu

f32 = jnp.float32
bf16 = jnp.bfloat16

D_MODEL = 1024
HEAD = 64
A_W = 256
B_W = 256
C_W = 512
KV_W = 128
IN_COLS = 4 * A_W + 2 * B_W + C_W + 2 * KV_W
D_FF = 2816
PLE_DIM = 256
WINDOW = 128
CONV_WIDTH = 31
ROPE_THETA = 10000.0
EPS = 1e-6

TILE = 256
CHUNK = 64
LEVELS = (32, 16, 8, 4, 2, 1)
SAFE_EXPONENT = 80.0
HIST = 32
FFN_TILE = 512
LOG2E = 1.4426950408889634
VMEM_LIMIT = 52 * 1024 * 1024


def _split2(x):
    hi = x.astype(bf16)
    lo = (x - hi.astype(f32)).astype(bf16)
    return hi, lo


def _split3(x):
    hi = x.astype(bf16)
    r = x - hi.astype(f32)
    mid = r.astype(bf16)
    lo = (r - mid.astype(f32)).astype(bf16)
    return hi, mid, lo


def _seg_sum(x, e):
    hi, lo = _split2(x)
    return (jnp.dot(hi, e, preferred_element_type=f32) + jnp.dot(lo, e, preferred_element_type=f32))


def _sigmoid(x):
    return 1.0 / (1.0 + jnp.exp(-x))


def _rmsnorm_rows(x, g):
    return x * lax.rsqrt(jnp.mean(x * x, axis=-1, keepdims=True) + EPS) * g


def _seg_stat(x, e):
    return jnp.dot(x.astype(bf16), e, preferred_element_type=f32)


def _head_norm(x, e, g):
    ms = _seg_stat(x * x, e) * (1.0 / HEAD)
    return x * lax.rsqrt(ms + EPS) * g


def _rope(x, cos, sin_signed):
    w = x.shape[-1]
    lane = lax.broadcasted_iota(jnp.int32, x.shape, 1)
    first = (lane % HEAD) < (HEAD // 2)
    swapped = jnp.where(first, pltpu.roll(x, w - HEAD // 2, 1), pltpu.roll(x, HEAD // 2, 1))
    return x * cos + swapped * sin_signed


def _log_forget(zf, log_lb, log_1m_lb):
    ls = jnp.minimum(zf, 0.0) - jnp.log(1.0 + jnp.exp(-jnp.abs(zf)))
    b = log_1m_lb + ls
    return jnp.maximum(log_lb, b) + jnp.log(1.0 + jnp.exp(-jnp.abs(log_lb - b))), ls


def _group_ln_silu(y, e, g, b):
    mu = _seg_sum(y, e) * (1.0 / HEAD)
    yc = y - mu
    var = _seg_stat(yc * yc, e) * (1.0 / HEAD)
    yn = yc * lax.rsqrt(var + EPS) * g + b
    return yn * _sigmoid(yn)


def _np_constants():
    lane = np.arange(256)
    e256 = (lane[:, None] // HEAD == lane[None, :] // HEAD).astype(np.float32)
    t = np.arange(TILE)
    ltri = ((t[:, None] // CHUNK == t[None, :] // CHUNK) & (t[None, :] <= t[:, None])).astype(np.float32)
    tt = np.arange(CHUNK)[:, None]
    ss = (np.arange(256) % CHUNK)[None, :]
    lvl = []
    for m in LEVELS:
        lvl.append(((tt // (2 * m) == ss // (2 * m)) & (tt % (2 * m) >= m) & (ss % (2 * m) < m)).astype(np.float32))
    lvl.append((ss <= tt).astype(np.float32))
    lvl = np.stack(lvl)
    hm = np.stack([np.broadcast_to((lane // HEAD == h)[None, :], (CHUNK, 256)) for h in range(4)]).astype(np.float32)
    keys = np.arange(2 * WINDOW)[:, None]
    qry = np.arange(4 * WINDOW)[None, :] % WINDOW
    band = (keys >= qry) & (keys <= qry + WINDOW)
    bias = np.where(np.stack([band & (keys >= WINDOW), band]), 0.0, -np.inf).astype(np.float32)
    return e256, ltri, lvl, hm, bias


def _level_ref(g, m):
    n_rows = g.shape[0]
    if 2 * m >= 8:
        g3 = g.reshape(n_rows // (2 * m), 2 * m, g.shape[1])
        return jnp.broadcast_to(g3[:, m - 1:m, :], g3.shape).reshape(g.shape)
    g3 = g.reshape(n_rows // 8, 8, g.shape[1])
    sub = lax.broadcasted_iota(jnp.int32, g3.shape, 1)
    out = None
    for blk in range(8 // (2 * m)):
        r = blk * 2 * m + m - 1
        piece = jnp.broadcast_to(g3[:, r:r + 1, :], g3.shape)
        out = piece if out is None else jnp.where(sub >= blk * 2 * m, piece, out)
    return out.reshape(g.shape)


def _hgrn_tile(q, zf, v, lbp_ref, st_ref, st0_ref, o_ref, e, ltri_ref, lvl_ref, hm_ref):
    log_lb, log_1m_lb, one_m_lb = lbp_ref[0:1, :], lbp_ref[1:2, :], lbp_ref[2:3, :]
    logf, ls = _log_forget(zf, log_lb, log_1m_lb)
    kin = one_m_lb * jnp.exp(ls - zf)
    g3 = jnp.dot(ltri_ref[...], jnp.concatenate(_split3(logf), axis=1), preferred_element_type=f32)
    g = g3[:, 0:A_W] + g3[:, A_W:2 * A_W] + g3[:, 2 * A_W:3 * A_W]
    v_b = v.astype(bf16)
    vt_b = v.T.astype(bf16)
    e_mid = g - _level_ref(g, CHUNK // 2)
    safe = jnp.max(jnp.abs(e_mid)) < SAFE_EXPONENT

    def expand(rows_b):
        return jnp.concatenate([rows_b * hm_ref[h] for h in range(4)], axis=0)

    def finish_chunk(c, att, q_in, k_hat):
        r0, r1 = c * CHUNK, (c + 1) * CHUNK
        st = st_ref[...]
        o = jnp.dot(att.astype(bf16), expand(v_b[r0:r1]), preferred_element_type=f32)
        o = o + lax.dot_general(q_in, st.astype(bf16) * e, (((1,), (1,)), ((), ())), preferred_element_type=f32)
        zero = jnp.zeros((CHUNK, A_W), bf16)
        pair = (c // 2) * 2 * CHUNK
        rhs = jnp.concatenate([k_hat, zero] if c % 2 == 0 else [zero, k_hat], axis=0)
        upd = jnp.dot(vt_b[:, pair:pair + 2 * CHUNK], rhs, preferred_element_type=f32)
        st_ref[...] = st * jnp.exp(g[r1 - 1:r1, :]) + upd
        o_ref[r0:r1, :] = o

    st0_ref[...] = st_ref[...]
    qw = q * jnp.exp(e_mid)
    kw = kin * jnp.exp(-e_mid)
    q_t, k_t = qw.astype(bf16), kw.astype(bf16)
    chunks = [(c * CHUNK, (c + 1) * CHUNK) for c in range(TILE // CHUNK)]
    atts = [lax.dot_general(q_t[r0:r1], expand(k_t[r0:r1]), (((1,), (1,)), ((), ())), preferred_element_type=f32)
            for r0, r1 in chunks]
    atts = [jnp.where(lvl_ref[len(LEVELS)] > 0.5, a, 0.0) for a in atts]
    for c, (r0, r1) in enumerate(chunks):
        g_mid, g_end = g[r0 + CHUNK // 2 - 1:r0 + CHUNK // 2, :], g[r1 - 1:r1, :]
        finish_chunk(c, atts[c], (qw[r0:r1] * jnp.exp(g_mid)).astype(bf16),
                     (kw[r0:r1] * jnp.exp(g_end - g_mid)).astype(bf16))

    def redo_if_unsafe():
        @pl.when(jnp.logical_not(safe))
        def _():
            st_ref[...] = st0_ref[...]
            q_levels, k_levels = [], []
            for m in LEVELS:
                w = jnp.exp(-jnp.abs(g - _level_ref(g, m)))
                q_levels.append((q * w).astype(bf16))
                k_levels.append((kin * w).astype(bf16))
            q_inter = (q * jnp.exp(g)).astype(bf16)
            for c in range(TILE // CHUNK):
                r0, r1 = c * CHUNK, (c + 1) * CHUNK
                att = jnp.zeros((CHUNK, 4 * CHUNK), f32)
                for li in range(len(LEVELS)):
                    a = lax.dot_general(q_levels[li][r0:r1], expand(k_levels[li][r0:r1]),
                                        (((1,), (1,)), ((), ())), preferred_element_type=f32)
                    att = att + a * lvl_ref[li]
                k_hat = (kin[r0:r1] * jnp.exp(g[r1 - 1:r1, :] - g[r0:r1])).astype(bf16)
                finish_chunk(c, att, q_inter[r0:r1], k_hat)
            o_ref[...] = o_ref[...] + _seg_sum(q * kin, e) * v

    return redo_if_unsafe


def _conv_tile(u, hist_ref, cw_ref, cvec_ref, e):
    n = u.shape[0]
    hist_ref[HIST:HIST + n, :] = u
    lo = HIST - (CONV_WIDTH - 1)
    y = cvec_ref[0:1, :]
    for r in range(8):
        rows = n if r == 0 else n + 8
        part = None
        for o in range(lo, lo + CONV_WIDTH):
            if o % 8 == r:
                term = hist_ref[o - r:o - r + rows, :] * cw_ref[o - lo:o - lo + 1, :]
                part = term if part is None else part + term
        y = y + part[r:r + n]
    return _group_ln_silu(y, e, cvec_ref[1:2, :], cvec_ref[2:3, :])


def _attn_tile(q_t, k_b, v_t, sink_ref, biases):
    zero = jnp.zeros((HEAD, WINDOW), f32)
    inst = [(b, g) for b in range(TILE // WINDOW) for g in range(2)]
    sinks = [jnp.concatenate([jnp.full((1, WINDOW), sink_ref[h] * LOG2E, f32) for h in range(4 * g, 4 * g + 4)],
                             axis=1) for g in range(2)]

    def rhs(b, g):
        cols = []
        for h in range(4 * g, 4 * g + 4):
            q_h = q_t[h * HEAD:(h + 1) * HEAD, b * WINDOW:(b + 1) * WINDOW]
            cols.append(jnp.concatenate([q_h, zero] if g == 0 else [zero, q_h], axis=0))
        return jnp.concatenate(cols, axis=1).astype(bf16)

    s = [jnp.dot(k_b[b * WINDOW:(b + 2) * WINDOW], rhs(b, g), preferred_element_type=f32) + biases[b]
         for b, g in inst]
    mx = [jnp.maximum(jnp.max(s[n], axis=0, keepdims=True), sinks[g]) for n, (b, g) in enumerate(inst)]
    pr = [jnp.exp2(s[n] - mx[n]) for n in range(len(inst))]
    rden = [1.0 / (jnp.sum(pr[n], axis=0, keepdims=True) + jnp.exp2(sinks[g] - mx[n]))
            for n, (b, g) in enumerate(inst)]
    o_t = [jnp.dot(v_t[g * HEAD:(g + 1) * HEAD, b * WINDOW:(b + 2) * WINDOW], pr[n].astype(bf16),
                   preferred_element_type=f32) * rden[n] for n, (b, g) in enumerate(inst)]
    rows = []
    for b in range(TILE // WINDOW):
        pairs = []
        for g in range(2):
            o = o_t[2 * b + g]
            for j in range(2):
                pair_t = jnp.concatenate([o[:, (2 * j) * WINDOW:(2 * j + 1) * WINDOW],
                                          o[:, (2 * j + 1) * WINDOW:(2 * j + 2) * WINDOW]], axis=0)
                pairs.append(pair_t.T)
        rows.append(jnp.concatenate(pairs, axis=1))
    return jnp.concatenate(rows, axis=0)


def _mixer_kernel(layer, x_ref, rope_row_ref, rope_tile_ref, win_ref, wout_ref, nmix_ref, lbp_ref, aon_ref, cw_ref,
                  cvec_ref, qk_ref, sinks_ref, e_ref, ltri_ref, lvl_ref, hm_ref, bias_ref,
                  h_ref, st_ref, conv_ref, kn_ref, vn_ref, z_ref, hist_ref, o_ref, st0_ref):
    i = pl.program_id(0)
    sink_ref = sinks_ref.at[layer]

    @pl.when(i == 0)
    def _():
        st_ref[...] = jnp.zeros_like(st_ref)
        hist_ref[0:HIST, :] = jnp.zeros((HIST, B_W), f32)
        kn_ref[...] = jnp.zeros_like(kn_ref)
        vn_ref[...] = jnp.zeros_like(vn_ref)

    x = x_ref[...]
    xn = _rmsnorm_rows(x, nmix_ref[...])
    z_ref[...] = jnp.dot(xn.astype(bf16), win_ref[...], preferred_element_type=f32)
    e = e_ref[...]

    redo_hgrn_if_unsafe = _hgrn_tile(z_ref[:, 0:A_W], z_ref[:, A_W:2 * A_W], z_ref[:, 2 * A_W:3 * A_W], lbp_ref,
                                     st_ref, st0_ref, o_ref, e, ltri_ref, lvl_ref, hm_ref)

    c0 = 4 * A_W
    u = z_ref[:, c0:c0 + B_W] * _sigmoid(z_ref[:, c0 + B_W:c0 + 2 * B_W])
    ob = _conv_tile(u, hist_ref, cw_ref, cvec_ref, e)
    conv_ref[...] = hist_ref[TILE + HIST - (CONV_WIDTH - 1):TILE + HIST, :]
    hist_ref[0:HIST, :] = hist_ref[TILE:TILE + HIST, :]

    c1 = c0 + 2 * B_W
    cos_a, sin_a = rope_tile_ref[0, pl.ds(i, 1), :], rope_tile_ref[1, pl.ds(i, 1), :]
    cos = cos_a * rope_row_ref[0] - sin_a * rope_row_ref[1]
    sin = sin_a * rope_row_ref[2] + cos_a * rope_row_ref[3]
    qn = jnp.concatenate([_head_norm(z_ref[:, c1 + j * 256:c1 + (j + 1) * 256], e,
                                     jnp.concatenate([qk_ref[0:1, :]] * 2, axis=1)) for j in range(2)], axis=1)
    q = _rope(qn, jnp.concatenate([cos] * 4, axis=1), jnp.concatenate([sin] * 4, axis=1)) * (LOG2E * HEAD ** -0.5)
    kk = _rope(_head_norm(z_ref[:, c1 + C_W:c1 + C_W + KV_W], e[0:KV_W, 0:KV_W], qk_ref[1:2, :]), cos, sin)
    vv = z_ref[:, c1 + C_W + KV_W:c1 + C_W + 2 * KV_W]
    kcat = jnp.concatenate([kn_ref[...], kk], axis=0)
    vcat = jnp.concatenate([vn_ref[...], vv], axis=0)
    q_t = q.T
    k_b = kcat.astype(bf16)
    v_t = vcat.T.astype(bf16)
    biases = [bias_ref[jnp.where(i > 0, 1, 0)]] + [bias_ref[1]] * (TILE // WINDOW - 1)
    oc = _attn_tile(q_t, k_b, v_t, sink_ref, biases)
    kn_ref[...] = kk[TILE - WINDOW:TILE]
    vn_ref[...] = vv[TILE - WINDOW:TILE]

    redo_hgrn_if_unsafe()
    gate = z_ref[:, 3 * A_W:4 * A_W]
    oa = _head_norm(o_ref[...], e, aon_ref[...]) * (gate * _sigmoid(gate))
    mix = jnp.concatenate([oa, ob, oc], axis=1).astype(bf16)
    h_ref[...] = x + jnp.dot(mix, wout_ref[...], preferred_element_type=f32)


def _const_spec(shape):
    nd = len(shape)
    return pl.BlockSpec(shape, lambda i, _n=nd: (0,) * _n)


def _layer_spec(arr, l, **kw):
    shape = arr.shape[1:]
    return pl.BlockSpec((None,) + shape, lambda i, _l=l, _n=len(shape): (_l,) + (0,) * _n, **kw)


def _prompt_mixer(x, rope, wl, l, consts):
    t_len = x.shape[0]
    row_spec = lambda w: pl.BlockSpec((TILE, w), lambda i: (i, 0))
    smalls = [wl['nmix'], wl['lbp'], wl['aon'], wl['cw'], wl['cvec'], wl['qk']]
    in_specs = ([row_spec(D_MODEL), _const_spec(rope[0].shape), _const_spec(rope[1].shape),
                 _layer_spec(wl['w_in'], l), _layer_spec(wl['w_out'], l)] + [_layer_spec(a, l) for a in smalls]
                + [pl.BlockSpec(memory_space=pltpu.SMEM)]
                + [_const_spec(a.shape) for a in consts])
    out_shape = (jax.ShapeDtypeStruct((t_len, D_MODEL), f32),
                 jax.ShapeDtypeStruct((A_W, A_W), f32),
                 jax.ShapeDtypeStruct((CONV_WIDTH - 1, B_W), f32),
                 jax.ShapeDtypeStruct((WINDOW, KV_W), f32),
                 jax.ShapeDtypeStruct((WINDOW, KV_W), f32))
    out_specs = (row_spec(D_MODEL), _const_spec((A_W, A_W)), _const_spec((CONV_WIDTH - 1, B_W)),
                 _const_spec((WINDOW, KV_W)), _const_spec((WINDOW, KV_W)))
    return pl.pallas_call(
        functools.partial(_mixer_kernel, l), grid=(t_len // TILE,), in_specs=in_specs, out_specs=out_specs,
        out_shape=out_shape,
        scratch_shapes=[pltpu.VMEM((TILE, IN_COLS), f32), pltpu.VMEM((TILE + HIST, B_W), f32),
                        pltpu.VMEM((TILE, A_W), f32), pltpu.VMEM((A_W, A_W), f32)],
        compiler_params=pltpu.CompilerParams(dimension_semantics=("arbitrary",), vmem_limit_bytes=VMEM_LIMIT),
        name="prompt_mixer",
    )(x, rope[0], rope[1], wl['w_in'], wl['w_out'], *smalls, wl['sinks'], *consts)


def _ffn_kernel(h_ref, p_ref, wg_ref, wu_ref, wd_ref, wpg_ref, wpp_ref, norms_ref, y_ref):
    h = h_ref[...]
    hn = _rmsnorm_rows(h, norms_ref[0:1, :]).astype(bf16)
    gt = jnp.dot(hn, wg_ref[...], preferred_element_type=f32)
    up = jnp.dot(hn, wu_ref[...], preferred_element_type=f32)
    act = (gt * _sigmoid(gt) * up).astype(bf16)
    h = h + jnp.dot(act, wd_ref[...], preferred_element_type=f32)
    pn = _rmsnorm_rows(h, norms_ref[1:2, :]).astype(bf16)
    gate = _sigmoid(jnp.dot(pn, wpg_ref[...], preferred_element_type=f32))
    y_ref[...] = h + gate * jnp.dot(p_ref[...].astype(bf16), wpp_ref[...], preferred_element_type=f32)


def _ffn(h, p, wl, l):
    n = h.shape[0]
    tile = min(FFN_TILE, n)
    assert n % tile == 0
    row_spec = lambda w: pl.BlockSpec((tile, w), lambda i: (i, 0))
    ws = [wl['w_gate'], wl['w_up'], wl['w_down'], wl['w_ple_gate'], wl['w_ple_proj']]
    return pl.pallas_call(
        _ffn_kernel, grid=(n // tile,),
        in_specs=[row_spec(D_MODEL), pl.BlockSpec((None, tile, PLE_DIM), lambda i, _l=l: (_l, i, 0))]
        + [_layer_spec(a, l, pipeline_mode=pl.Buffered(1)) for a in ws] + [_layer_spec(wl['fnorms'], l)],
        out_specs=row_spec(D_MODEL), out_shape=jax.ShapeDtypeStruct((n, D_MODEL), f32),
        compiler_params=pltpu.CompilerParams(dimension_semantics=("arbitrary",), vmem_limit_bytes=VMEM_LIMIT),
        name="ffn_ple",
    )(h, p, *ws, wl['fnorms'])


N_HEADS = C_W // HEAD


def _sample_in_kernel(n_tok, x_ref, cos_ref, sin_ref, win_ref, nmix_ref, qk_ref, e_ref, z_ref, qx_ref, k_ref,
                      piece_ref):
    n_seq = x_ref.shape[0] // n_tok
    xn = _rmsnorm_rows(x_ref[...], nmix_ref[...])
    z = jnp.dot(xn.astype(bf16), win_ref[...], preferred_element_type=f32)
    z_ref[...] = z
    e = e_ref[...]
    c1 = 4 * A_W + 2 * B_W
    cos, sin = cos_ref[...], sin_ref[...]
    qn = jnp.concatenate([_head_norm(z[:, c1 + j * 256:c1 + (j + 1) * 256], e,
                                     jnp.concatenate([qk_ref[0:1, :]] * 2, axis=1)) for j in range(2)], axis=1)
    q = _rope(qn, jnp.concatenate([cos] * 4, axis=1), jnp.concatenate([sin] * 4, axis=1)) * (HEAD ** -0.5)
    k_ref[...] = _rope(_head_norm(z[:, c1 + C_W:c1 + C_W + KV_W], e[0:KV_W, 0:KV_W], qk_ref[1:2, :]), cos, sin)
    lo_f = lax.broadcasted_iota(jnp.int32, (x_ref.shape[0], KV_W), 1) < HEAD
    for h in range(N_HEADS):
        pair, g = q[:, (h // 2) * KV_W:(h // 2 + 1) * KV_W], h // 4
        own = jnp.where(lo_f if h % 2 == 0 else jnp.logical_not(lo_f), pair, 0.0)
        piece_ref[...] = own if h % 2 == g else pltpu.roll(own, HEAD, 1)
        for t in range(n_tok):
            qx_ref[pl.ds(h * n_tok + t, n_seq, stride=N_HEADS * n_tok), :] = piece_ref[pl.ds(t, n_seq, stride=n_tok), :]


def _sample_in(x, cos, sin, wl, l, e256, n_tok):
    n = x.shape[0]
    args = (x, cos, sin, wl['w_in'], wl['nmix'], wl['qk'], e256)
    specs = [_const_spec(a.shape) for a in args]
    specs[3:6] = [_layer_spec(a, l) for a in args[3:6]]
    return pl.pallas_call(
        functools.partial(_sample_in_kernel, n_tok), grid=(1,), in_specs=specs,
        out_specs=(_const_spec((n, IN_COLS)), _const_spec((n * N_HEADS, KV_W)), _const_spec((n, KV_W))),
        out_shape=(jax.ShapeDtypeStruct((n, IN_COLS), f32), jax.ShapeDtypeStruct((n * N_HEADS, KV_W), f32),
                   jax.ShapeDtypeStruct((n, KV_W), f32)),
        scratch_shapes=[pltpu.VMEM((n, KV_W), f32)],
        compiler_params=pltpu.CompilerParams(dimension_semantics=("arbitrary",), vmem_limit_bytes=VMEM_LIMIT),
        name="sample_in",
    )(*args)


def _sample_hgrn_kernel(n_tok, q_ref, f_ref, v_ref, s_ref, lbp_ref, o_ref, so_ref, st_scr, qt_scr, ft_scr, kt_scr,
                        vt_scr, ot_scr):
    n_seq = s_ref.shape[0]
    st_scr[...] = s_ref[...].T
    for t in range(n_tok):
        zf = f_ref[pl.ds(t, n_seq, stride=n_tok), :]
        logf, ls = _log_forget(zf, lbp_ref[0:1, :], lbp_ref[1:2, :])
        ft_scr[t] = jnp.exp(logf).T
        kt_scr[t] = (lbp_ref[2:3, :] * jnp.exp(ls - zf)).T
        qt_scr[t] = q_ref[pl.ds(t, n_seq, stride=n_tok), :].T
        vt_scr[t] = v_ref[pl.ds(t, n_seq, stride=n_tok), :].T
        ot_scr[t] = jnp.zeros((2 * HEAD, n_seq), f32)

    def body(hk, carry):
        r0 = pl.multiple_of(hk * HEAD, HEAD)
        v0 = pl.multiple_of((hk // HEAD) * HEAD, HEAD)
        blk = st_scr[pl.ds(r0, HEAD), :]
        for t in range(n_tok):
            blk = blk * ft_scr[t, pl.ds(hk, 1), :] + kt_scr[t, pl.ds(hk, 1), :] * vt_scr[t, pl.ds(v0, HEAD), :]
            ot_scr[t, pl.ds(v0, HEAD), :] = ot_scr[t, pl.ds(v0, HEAD), :] + qt_scr[t, pl.ds(hk, 1), :] * blk
        st_scr[pl.ds(r0, HEAD), :] = blk
        return carry

    lax.fori_loop(0, 2 * HEAD, body, 0)
    so_ref[...] = st_scr[...].T
    for t in range(n_tok):
        o_ref[pl.ds(t, n_seq, stride=n_tok), :] = ot_scr[t].T


def _sample_hgrn(z, state, lbp, l, n_tok):
    n = z.shape[0]
    n_seq = state.shape[1]
    blk = 2 * HEAD * HEAD
    col = lambda j0: pl.BlockSpec((n, 2 * HEAD), lambda i, _j=j0: (0, _j + i))
    return pl.pallas_call(
        functools.partial(_sample_hgrn_kernel, n_tok), grid=(2,),
        in_specs=[col(0), col(2), col(4), pl.BlockSpec((None, n_seq, blk), lambda i, _l=l: (_l, 0, i)),
                  pl.BlockSpec((None, 3, 2 * HEAD), lambda i, _l=l: (_l, 0, i))],
        out_specs=(pl.BlockSpec((n, 2 * HEAD), lambda i: (0, i)), pl.BlockSpec((n_seq, blk), lambda i: (0, i))),
        out_shape=(jax.ShapeDtypeStruct((n, A_W), f32), jax.ShapeDtypeStruct(state.shape[1:], f32)),
        scratch_shapes=[pltpu.VMEM((blk, n_seq), f32)] + [pltpu.VMEM((n_tok, 2 * HEAD, n_seq), f32)] * 5,
        compiler_params=pltpu.CompilerParams(dimension_semantics=("arbitrary",), vmem_limit_bytes=VMEM_LIMIT),
        name="sample_hgrn",
    )(z, z, z, state, lbp)


def _sample_conv_kernel(n_tok, u0_ref, u1_ref, g0_ref, g1_ref, buf_ref, cw_ref, cvec_ref, e_ref, ob0_ref, ob1_ref,
                        new_ref):
    n_seq = buf_ref.shape[0]
    n_hist = CONV_WIDTH - 1
    us = []
    for t in range(n_tok):
        rows = pl.ds(t, n_seq, stride=n_tok)
        u = jnp.concatenate([u0_ref[rows, :], u1_ref[rows, :]], axis=1)
        g = jnp.concatenate([g0_ref[rows, :], g1_ref[rows, :]], axis=1)
        us.append(u * _sigmoid(g))
    buf = buf_ref[...]
    for t in range(n_tok):
        taps = cw_ref[0:n_hist - t, :]
        if t:
            taps = jnp.concatenate([jnp.zeros((t, B_W), f32), taps], axis=0)
        y = jnp.sum(buf * taps[None], axis=1) + cvec_ref[0:1, :]
        for s in range(t + 1):
            j = n_hist + s - t
            y = y + us[s] * cw_ref[j:j + 1, :]
        ob = _group_ln_silu(y, e_ref[...], cvec_ref[1:2, :], cvec_ref[2:3, :])
        ob0_ref[pl.ds(t, n_seq, stride=n_tok), :] = ob[:, 0:KV_W]
        ob1_ref[pl.ds(t, n_seq, stride=n_tok), :] = ob[:, KV_W:B_W]
    new_ref[:, 0:n_hist - n_tok, :] = buf_ref[:, n_tok:n_hist, :]
    for s in range(n_tok):
        new_ref[:, n_hist - n_tok + s, :] = us[s]


def _sample_conv(z, buf, wl, l, e256, n_tok):
    n = z.shape[0]
    col = lambda j: pl.BlockSpec((n, KV_W), lambda i, _j=j: (0, _j))
    half = jax.ShapeDtypeStruct((n, KV_W), f32)
    return pl.pallas_call(
        functools.partial(_sample_conv_kernel, n_tok), grid=(1,),
        in_specs=[col(8), col(9), col(10), col(11), _layer_spec(buf, l), _layer_spec(wl['cw'], l),
                  _layer_spec(wl['cvec'], l), _const_spec(e256.shape)],
        out_specs=(_const_spec((n, KV_W)), _const_spec((n, KV_W)), _const_spec(buf.shape[1:])),
        out_shape=(half, half, jax.ShapeDtypeStruct(buf.shape[1:], f32)),
        compiler_params=pltpu.CompilerParams(dimension_semantics=("arbitrary",), vmem_limit_bytes=VMEM_LIMIT),
        name="sample_conv",
    )(z, z, z, z, buf, wl['cw'], wl['cvec'], e256)


SEQ_BLOCK = 8


def _sample_attn_kernel(n_tok, qx_ref, k_ref, v_ref, kc_ref, vc_ref, sink_ref, bias_ref, ox_ref, kn_ref, vn_ref):
    q_rows = N_HEADS * n_tok
    sink, bias = sink_ref[...], bias_ref[...]
    seqs = range(SEQ_BLOCK)
    k_all = [jnp.concatenate([kc_ref[b], k_ref[b * n_tok:(b + 1) * n_tok, :]], axis=0) for b in seqs]
    v_all = [jnp.concatenate([vc_ref[b], v_ref[b * n_tok:(b + 1) * n_tok, :]], axis=0) for b in seqs]
    s = [lax.dot_general(qx_ref[b * q_rows:(b + 1) * q_rows, :].astype(bf16), k_all[b].astype(bf16),
                         (((1,), (1,)), ((), ())), preferred_element_type=f32) + bias for b in seqs]
    mx = [jnp.maximum(jnp.max(s[b], axis=-1, keepdims=True), sink) for b in seqs]
    pr = [jnp.exp(s[b] - mx[b]) for b in seqs]
    rden = [1.0 / (jnp.sum(pr[b], axis=-1, keepdims=True) + jnp.exp(sink - mx[b])) for b in seqs]
    for b in seqs:
        ox_ref[b * q_rows:(b + 1) * q_rows, :] = jnp.dot(pr[b].astype(bf16), v_all[b].astype(bf16),
                                                         preferred_element_type=f32) * rden[b]
    for b in seqs:
        kn_ref[b] = k_all[b][n_tok:, :]
        vn_ref[b] = v_all[b][n_tok:, :]


def _sample_attn(qx, k, z, kc, vc, sink_col, bias, l, n_tok):
    n_seq = kc.shape[1]
    rows = SEQ_BLOCK * n_tok
    q_rows = SEQ_BLOCK * N_HEADS * n_tok
    v_col = (4 * A_W + 2 * B_W + C_W + KV_W) // KV_W
    cache_in = pl.BlockSpec((None, SEQ_BLOCK) + kc.shape[2:], lambda i, _l=l: (_l, i, 0, 0))
    cache_out = pl.BlockSpec((SEQ_BLOCK,) + kc.shape[2:], lambda i: (i, 0, 0))
    return pl.pallas_call(
        functools.partial(_sample_attn_kernel, n_tok), grid=(n_seq // SEQ_BLOCK,),
        in_specs=[pl.BlockSpec((q_rows, KV_W), lambda i: (i, 0)), pl.BlockSpec((rows, KV_W), lambda i: (i, 0)),
                  pl.BlockSpec((rows, KV_W), lambda i: (i, v_col)), cache_in, cache_in,
                  _layer_spec(sink_col, l), _const_spec(bias.shape)],
        out_specs=(pl.BlockSpec((q_rows, KV_W), lambda i: (i, 0)), cache_out, cache_out),
        out_shape=(jax.ShapeDtypeStruct(qx.shape, f32), jax.ShapeDtypeStruct(kc.shape[1:], f32),
                   jax.ShapeDtypeStruct(vc.shape[1:], f32)),
        compiler_params=pltpu.CompilerParams(dimension_semantics=("arbitrary",), vmem_limit_bytes=VMEM_LIMIT),
        name="sample_attn",
    )(qx, k, z, kc, vc, sink_col, bias)


def _sample_out_kernel(n_tok, x_ref, oa_ref, g_ref, ob0_ref, ob1_ref, ox_ref, wout_ref, aon_ref, e_ref, h_ref,
                       oc_ref):
    n_seq = x_ref.shape[0] // n_tok
    gate = g_ref[...]
    oa = _head_norm(oa_ref[...], e_ref[...], aon_ref[...]) * (gate * _sigmoid(gate))
    lo_f = lax.broadcasted_iota(jnp.int32, (n_seq, KV_W), 1) < HEAD
    for p in range(N_HEADS // 2):
        g = p // 2
        for t in range(n_tok):
            even = ox_ref[pl.ds((2 * p) * n_tok + t, n_seq, stride=N_HEADS * n_tok), :]
            odd = ox_ref[pl.ds((2 * p + 1) * n_tok + t, n_seq, stride=N_HEADS * n_tok), :]
            even = even if g == 0 else pltpu.roll(even, HEAD, 1)
            odd = odd if g == 1 else pltpu.roll(odd, HEAD, 1)
            oc_ref[p, pl.ds(t, n_seq, stride=n_tok), :] = jnp.where(lo_f, even, odd)
    mix = jnp.concatenate([oa, ob0_ref[...], ob1_ref[...]] + [oc_ref[p] for p in range(N_HEADS // 2)],
                          axis=1).astype(bf16)
    h_ref[...] = x_ref[...] + jnp.dot(mix, wout_ref[...], preferred_element_type=f32)


def _sample_out(x, oa, z, ob0, ob1, ox, wl, l, e256, n_tok):
    n = x.shape[0]
    gate_spec = pl.BlockSpec((n, A_W), lambda i: (0, 3))
    args = (x, oa, z, ob0, ob1, ox, wl['w_out'], wl['aon'], e256)
    specs = [_const_spec(a.shape) for a in args]
    specs[2] = gate_spec
    specs[6:8] = [_layer_spec(a, l) for a in args[6:8]]
    return pl.pallas_call(
        functools.partial(_sample_out_kernel, n_tok), grid=(1,), in_specs=specs,
        out_specs=_const_spec((n, D_MODEL)), out_shape=jax.ShapeDtypeStruct((n, D_MODEL), f32),
        scratch_shapes=[pltpu.VMEM((N_HEADS // 2, n, KV_W), f32)],
        compiler_params=pltpu.CompilerParams(dimension_semantics=("arbitrary",), vmem_limit_bytes=VMEM_LIMIT),
        name="sample_out",
    )(*args)


def _rope_cos_sin(pos):
    half = HEAD // 2
    inv = ROPE_THETA ** (-jnp.arange(half, dtype=f32) / half)
    ang = pos[:, None] * inv[None, :]
    cos, sin = jnp.cos(ang), jnp.sin(ang)
    return jnp.tile(cos, (1, 4)), jnp.concatenate([-sin, sin, -sin, sin], axis=1)


def _rope_prompt_tables(t_len):
    half = HEAD // 2
    inv = ROPE_THETA ** (-jnp.arange(half, dtype=f32) / half)
    sign = jnp.concatenate([-jnp.ones((half,), f32), jnp.ones((half,), f32)] * 2)
    ang_r = jnp.arange(TILE, dtype=f32)[:, None] * inv[None, :]
    ang_t = (jnp.arange(t_len // TILE, dtype=f32) * TILE)[:, None] * inv[None, :]
    cos_r, sin_r = jnp.tile(jnp.cos(ang_r), (1, 4)), jnp.tile(jnp.sin(ang_r), (1, 4))
    rows = jnp.stack([cos_r, sin_r, cos_r * sign, sin_r * sign])
    tiles = jnp.stack([jnp.tile(jnp.cos(ang_t), (1, 4)), jnp.tile(jnp.sin(ang_t), (1, 4))])
    return rows, tiles


def _stacked_weights(lbs, n_tok, w_in, a_onorm, conv_w, conv_b, conv_ln_g, conv_ln_b, q_norm, k_norm, sinks, w_out,
                     norm_mix, norm_ffn, w_gate, w_up, w_down, ple_norm, w_ple_gate, w_ple_proj):
    return dict(
        w_in=w_in.astype(bf16), w_out=w_out.astype(bf16), w_gate=w_gate.astype(bf16), w_up=w_up.astype(bf16),
        w_down=w_down.astype(bf16), w_ple_gate=w_ple_gate.astype(bf16), w_ple_proj=w_ple_proj.astype(bf16),
        nmix=norm_mix[:, None, :],
        lbp=jnp.stack([jnp.log(lbs), jnp.log1p(-lbs), 1.0 - lbs], axis=1),
        aon=jnp.tile(a_onorm, (1, 4))[:, None, :],
        cw=conv_w,
        cvec=jnp.stack([conv_b, conv_ln_g, conv_ln_b], axis=1),
        qk=jnp.stack([jnp.tile(q_norm, (1, 2)), jnp.tile(k_norm, (1, 2))], axis=1),
        sinks=sinks,
        sink_col=jnp.repeat(sinks, n_tok, axis=1)[:, :, None],
        fnorms=jnp.stack([norm_ffn, ple_norm], axis=1),
    )


def kernel(x_prompt, x_sample, state_hgrn, state_conv, cache_swa_k, cache_swa_v, p_prompt, p_sample, a_lower, w_in, a_onorm, conv_w, conv_b, conv_ln_g, conv_ln_b, q_norm, k_norm, sinks, w_out, norm_mix, norm_ffn, w_gate, w_up, w_down, ple_norm, w_ple_gate, w_ple_proj):
    depth = w_in.shape[0]
    t_len = x_prompt.shape[1]
    n_seq, n_tok = x_sample.shape[0], x_sample.shape[1]
    past_len = 16384
    assert x_prompt.shape[0] == 1 and t_len % TILE == 0

    lbs = jnp.cumsum(jax.nn.softmax(a_lower.astype(f32), axis=0), axis=0)
    lbs = lbs - lbs[0:1]
    consts_np = _np_constants()
    e256 = jnp.asarray(consts_np[0], bf16)
    consts = (e256, jnp.asarray(consts_np[1], bf16), jnp.asarray(consts_np[2], f32), jnp.asarray(consts_np[3], bf16),
              jnp.asarray(consts_np[4], f32))

    rope_p = _rope_prompt_tables(t_len)
    cos_s, sin_s = _rope_cos_sin(past_len + jnp.arange(n_tok, dtype=f32))
    cos_s, sin_s = jnp.tile(cos_s, (n_seq, 1)), jnp.tile(sin_s, (n_seq, 1))
    w_buf = cache_swa_k.shape[2]
    q_tok = (np.arange(N_HEADS * n_tok) % n_tok)[:, None]
    rel = q_tok + w_buf - np.arange(w_buf + n_tok)[None, :]
    bias_s = jnp.asarray(np.where((rel >= 0) & (rel <= WINDOW), 0.0, -np.inf), f32)

    wl = _stacked_weights(lbs, n_tok, w_in, a_onorm, conv_w, conv_b, conv_ln_g, conv_ln_b, q_norm, k_norm, sinks,
                          w_out, norm_mix, norm_ffn, w_gate, w_up, w_down, ple_norm, w_ple_gate, w_ple_proj)
    p_p = p_prompt.reshape(depth, t_len, PLE_DIM)
    p_s = p_sample.reshape(depth, n_seq * n_tok, PLE_DIM)
    s_hgrn = state_hgrn.reshape(depth, n_seq, -1)
    kc = cache_swa_k.reshape(depth, n_seq, w_buf, KV_W)
    vc = cache_swa_v.reshape(depth, n_seq, w_buf, KV_W)

    hp = x_prompt[0]
    hs = x_sample.reshape(n_seq * n_tok, D_MODEL)
    outs = [[] for _ in range(8)]
    for l in range(depth):
        h_mid, st, conv_new, k_new, v_new = _prompt_mixer(hp, rope_p, wl, l, consts)
        hp = _ffn(h_mid, p_p, wl, l)
        s_new = jnp.stack([st[h * HEAD:(h + 1) * HEAD, h * HEAD:(h + 1) * HEAD].T for h in range(4)])
        outs[0].append(s_new[None])
        outs[1].append(conv_new[None])
        outs[2].append(k_new.reshape(1, WINDOW, 2, HEAD))
        outs[3].append(v_new.reshape(1, WINDOW, 2, HEAD))
        z, qx, k = _sample_in(hs, cos_s, sin_s, wl, l, e256, n_tok)
        oa, s_s = _sample_hgrn(z, s_hgrn, wl['lbp'], l, n_tok)
        ob0, ob1, conv_s = _sample_conv(z, state_conv, wl, l, e256, n_tok)
        ox, k_s, v_s = _sample_attn(qx, k, z, kc, vc, wl['sink_col'], bias_s, l, n_tok)
        h_mid_s = _sample_out(hs, oa, z, ob0, ob1, ox, wl, l, e256, n_tok)
        hs = _ffn(h_mid_s, p_s, wl, l)
        outs[4].append(s_s.reshape(n_seq, 4, HEAD, HEAD))
        outs[5].append(conv_s)
        outs[6].append(k_s.reshape(n_seq, w_buf, 2, HEAD))
        outs[7].append(v_s.reshape(n_seq, w_buf, 2, HEAD))
    return (hp[None], hs.reshape(n_seq, n_tok, D_MODEL)) + tuple(jnp.stack(o) for o in outs)
```

```python
import functools

import numpy as np
import jax
import jax.numpy as jnp
from jax import lax
from jax.experimental import pallas as pl
from jax.experimental.pallas import tpu as pltpu

f32 = jnp.float32
bf16 = jnp.bfloat16

D_MODEL = 1024
HEAD = 64
A_W = 256
B_W = 256
C_W = 512
KV_W = 128
IN_COLS = 4 * A_W + 2 * B_W + C_W + 2 * KV_W
D_FF = 2816
PLE_DIM = 256
WINDOW = 128
CONV_WIDTH = 31
ROPE_THETA = 10000.0
EPS = 1e-6

TILE = 512
CUM_ROWS = 256
CHUNK = 64
LEVELS = (32, 16, 8, 4, 2, 1)
SAFE_EXPONENT = 80.0
HIST = 32
FFN_TILE = 512
LOG2E = 1.4426950408889634
VMEM_LIMIT = 52 * 1024 * 1024


def _split2(x):
    hi = x.astype(bf16)
    lo = (x - hi.astype(f32)).astype(bf16)
    return hi, lo


def _split3(x):
    hi = x.astype(bf16)
    r = x - hi.astype(f32)
    mid = r.astype(bf16)
    lo = (r - mid.astype(f32)).astype(bf16)
    return hi, mid, lo


def _seg_sum(x, e):
    hi, lo = _split2(x)
    return (jnp.dot(hi, e, preferred_element_type=f32) + jnp.dot(lo, e, preferred_element_type=f32))


def _sigmoid(x):
    return 1.0 / (1.0 + jnp.exp(-x))


def _rmsnorm_rows(x, g):
    return x * lax.rsqrt(jnp.mean(x * x, axis=-1, keepdims=True) + EPS) * g


def _seg_stat(x, e):
    return jnp.dot(x.astype(bf16), e, preferred_element_type=f32)


def _head_norm(x, e, g):
    ms = _seg_stat(x * x, e) * (1.0 / HEAD)
    return x * lax.rsqrt(ms + EPS) * g


def _rope(x, cos, sin_signed):
    w = x.shape[-1]
    lane = lax.broadcasted_iota(jnp.int32, x.shape, 1)
    first = (lane % HEAD) < (HEAD // 2)
    swapped = jnp.where(first, pltpu.roll(x, w - HEAD // 2, 1), pltpu.roll(x, HEAD // 2, 1))
    return x * cos + swapped * sin_signed


def _log_forget(zf, log_lb, log_1m_lb):
    ls = jnp.minimum(zf, 0.0) - jnp.log(1.0 + jnp.exp(-jnp.abs(zf)))
    b = log_1m_lb + ls
    return jnp.maximum(log_lb, b) + jnp.log(1.0 + jnp.exp(-jnp.abs(log_lb - b))), ls


def _group_ln_silu(y, e, g, b):
    mu = _seg_sum(y, e) * (1.0 / HEAD)
    yc = y - mu
    var = _seg_stat(yc * yc, e) * (1.0 / HEAD)
    yn = yc * lax.rsqrt(var + EPS) * g + b
    return yn * _sigmoid(yn)


def _np_constants():
    lane = np.arange(256)
    e256 = (lane[:, None] // HEAD == lane[None, :] // HEAD).astype(np.float32)
    t = np.arange(CUM_ROWS)
    ltri = ((t[:, None] // CHUNK == t[None, :] // CHUNK) & (t[None, :] <= t[:, None])).astype(np.float32)
    tt = np.arange(CHUNK)[:, None]
    ss = (np.arange(256) % CHUNK)[None, :]
    lvl = []
    for m in LEVELS:
        lvl.append(((tt // (2 * m) == ss // (2 * m)) & (tt % (2 * m) >= m) & (ss % (2 * m) < m)).astype(np.float32))
    lvl.append((ss <= tt).astype(np.float32))
    lvl = np.stack(lvl)
    hm = np.stack([np.broadcast_to((lane // HEAD == h)[None, :], (CHUNK, 256)) for h in range(4)]).astype(np.float32)
    keys = np.arange(2 * WINDOW)[:, None]
    qry = np.arange(4 * WINDOW)[None, :] % WINDOW
    band = (keys >= qry) & (keys <= qry + WINDOW)
    bias = np.where(np.stack([band & (keys >= WINDOW), band]), 0.0, -np.inf).astype(np.float32)
    return e256, ltri, lvl, hm, bias


def _level_ref(g, m):
    n_rows = g.shape[0]
    if 2 * m >= 8:
        g3 = g.reshape(n_rows // (2 * m), 2 * m, g.shape[1])
        return jnp.broadcast_to(g3[:, m - 1:m, :], g3.shape).reshape(g.shape)
    g3 = g.reshape(n_rows // 8, 8, g.shape[1])
    sub = lax.broadcasted_iota(jnp.int32, g3.shape, 1)
    out = None
    for blk in range(8 // (2 * m)):
        r = blk * 2 * m + m - 1
        piece = jnp.broadcast_to(g3[:, r:r + 1, :], g3.shape)
        out = piece if out is None else jnp.where(sub >= blk * 2 * m, piece, out)
    return out.reshape(g.shape)


def _hgrn_tile(q, zf, v, lbp_ref, st_ref, st0_ref, o_ref, e, ltri_ref, lvl_ref, hm_ref):
    log_lb, log_1m_lb, one_m_lb = lbp_ref[0:1, :], lbp_ref[1:2, :], lbp_ref[2:3, :]
    logf, ls = _log_forget(zf, log_lb, log_1m_lb)
    kin = one_m_lb * jnp.exp(ls - zf)
    parts = jnp.concatenate(_split3(logf), axis=1)
    g3 = jnp.concatenate([jnp.dot(ltri_ref[...], parts[r:r + CUM_ROWS], preferred_element_type=f32)
                          for r in range(0, TILE, CUM_ROWS)], axis=0)
    g = g3[:, 0:A_W] + g3[:, A_W:2 * A_W] + g3[:, 2 * A_W:3 * A_W]
    v_b = v.astype(bf16)
    vt_b = v.T.astype(bf16)
    e_mid = g - _level_ref(g, CHUNK // 2)
    safe = jnp.max(jnp.abs(e_mid)) < SAFE_EXPONENT

    def expand(rows_b):
        return jnp.concatenate([rows_b * hm_ref[h] for h in range(4)], axis=0)

    def finish_chunk(c, att, q_in, k_hat):
        r0, r1 = c * CHUNK, (c + 1) * CHUNK
        st = st_ref[...]
        o = jnp.dot(att.astype(bf16), expand(v_b[r0:r1]), preferred_element_type=f32)
        o = o + lax.dot_general(q_in, st.astype(bf16) * e, (((1,), (1,)), ((), ())), preferred_element_type=f32)
        zero = jnp.zeros((CHUNK, A_W), bf16)
        pair = (c // 2) * 2 * CHUNK
        rhs = jnp.concatenate([k_hat, zero] if c % 2 == 0 else [zero, k_hat], axis=0)
        upd = jnp.dot(vt_b[:, pair:pair + 2 * CHUNK], rhs, preferred_element_type=f32)
        st_ref[...] = st * jnp.exp(g[r1 - 1:r1, :]) + upd
        o_ref[r0:r1, :] = o

    st0_ref[...] = st_ref[...]
    qw = q * jnp.exp(e_mid)
    kw = kin * jnp.exp(-e_mid)
    q_t, k_t = qw.astype(bf16), kw.astype(bf16)
    chunks = [(c * CHUNK, (c + 1) * CHUNK) for c in range(TILE // CHUNK)]
    atts = [lax.dot_general(q_t[r0:r1], expand(k_t[r0:r1]), (((1,), (1,)), ((), ())), preferred_element_type=f32)
            for r0, r1 in chunks]
    atts = [jnp.where(lvl_ref[len(LEVELS)] > 0.5, a, 0.0) for a in atts]
    for c, (r0, r1) in enumerate(chunks):
        g_mid, g_end = g[r0 + CHUNK // 2 - 1:r0 + CHUNK // 2, :], g[r1 - 1:r1, :]
        finish_chunk(c, atts[c], (qw[r0:r1] * jnp.exp(g_mid)).astype(bf16),
                     (kw[r0:r1] * jnp.exp(g_end - g_mid)).astype(bf16))

    def redo_if_unsafe():
        @pl.when(jnp.logical_not(safe))
        def _():
            st_ref[...] = st0_ref[...]
            q_levels, k_levels = [], []
            for m in LEVELS:
                w = jnp.exp(-jnp.abs(g - _level_ref(g, m)))
                q_levels.append((q * w).astype(bf16))
                k_levels.append((kin * w).astype(bf16))
            q_inter = (q * jnp.exp(g)).astype(bf16)
            for c in range(TILE // CHUNK):
                r0, r1 = c * CHUNK, (c + 1) * CHUNK
                att = jnp.zeros((CHUNK, 4 * CHUNK), f32)
                for li in range(len(LEVELS)):
                    a = lax.dot_general(q_levels[li][r0:r1], expand(k_levels[li][r0:r1]),
                                        (((1,), (1,)), ((), ())), preferred_element_type=f32)
                    att = att + a * lvl_ref[li]
                k_hat = (kin[r0:r1] * jnp.exp(g[r1 - 1:r1, :] - g[r0:r1])).astype(bf16)
                finish_chunk(c, att, q_inter[r0:r1], k_hat)
            o_ref[...] = o_ref[...] + _seg_sum(q * kin, e) * v

    return redo_if_unsafe


def _conv_tile(u, hist_ref, cw_ref, cvec_ref, e):
    n = u.shape[0]
    hist_ref[HIST:HIST + n, :] = u
    lo = HIST - (CONV_WIDTH - 1)
    y = cvec_ref[0:1, :]
    for r in range(8):
        rows = n if r == 0 else n + 8
        part = None
        for o in range(lo, lo + CONV_WIDTH):
            if o % 8 == r:
                term = hist_ref[o - r:o - r + rows, :] * cw_ref[o - lo:o - lo + 1, :]
                part = term if part is None else part + term
        y = y + part[r:r + n]
    return _group_ln_silu(y, e, cvec_ref[1:2, :], cvec_ref[2:3, :])


def _attn_tile(q_t, k_b, v_t, sink_ref, biases):
    zero = jnp.zeros((HEAD, WINDOW), f32)
    inst = [(b, g) for b in range(TILE // WINDOW) for g in range(2)]
    sinks = [jnp.concatenate([jnp.full((1, WINDOW), sink_ref[h] * LOG2E, f32) for h in range(4 * g, 4 * g + 4)],
                             axis=1) for g in range(2)]

    def rhs(b, g):
        cols = []
        for h in range(4 * g, 4 * g + 4):
            q_h = q_t[h * HEAD:(h + 1) * HEAD, b * WINDOW:(b + 1) * WINDOW]
            cols.append(jnp.concatenate([q_h, zero] if g == 0 else [zero, q_h], axis=0))
        return jnp.concatenate(cols, axis=1).astype(bf16)

    s = [jnp.dot(k_b[b * WINDOW:(b + 2) * WINDOW], rhs(b, g), preferred_element_type=f32) + biases[b]
         for b, g in inst]
    mx = [jnp.maximum(jnp.max(s[n], axis=0, keepdims=True), sinks[g]) for n, (b, g) in enumerate(inst)]
    pr = [jnp.exp2(s[n] - mx[n]) for n in range(len(inst))]
    rden = [1.0 / (jnp.sum(pr[n], axis=0, keepdims=True) + jnp.exp2(sinks[g] - mx[n]))
            for n, (b, g) in enumerate(inst)]
    o_t = [jnp.dot(v_t[g * HEAD:(g + 1) * HEAD, b * WINDOW:(b + 2) * WINDOW], pr[n].astype(bf16),
                   preferred_element_type=f32) * rden[n] for n, (b, g) in enumerate(inst)]
    rows = []
    for b in range(TILE // WINDOW):
        pairs = []
        for g in range(2):
            o = o_t[2 * b + g]
            for j in range(2):
                pair_t = jnp.concatenate([o[:, (2 * j) * WINDOW:(2 * j + 1) * WINDOW],
                                          o[:, (2 * j + 1) * WINDOW:(2 * j + 2) * WINDOW]], axis=0)
                pairs.append(pair_t.T)
        rows.append(jnp.concatenate(pairs, axis=1))
    return jnp.concatenate(rows, axis=0)


def _mixer_kernel(layer, x_ref, rope_row_ref, rope_tile_ref, win_ref, wout_ref, nmix_ref, lbp_ref, aon_ref, cw_ref,
                  cvec_ref, qk_ref, sinks_ref, e_ref, ltri_ref, lvl_ref, hm_ref, bias_ref,
                  h_ref, st_ref, conv_ref, kn_ref, vn_ref, z_ref, hist_ref, o_ref, st0_ref):
    i = pl.program_id(0)
    sink_ref = sinks_ref.at[layer]

    @pl.when(i == 0)
    def _():
        st_ref[...] = jnp.zeros_like(st_ref)
        hist_ref[0:HIST, :] = jnp.zeros((HIST, B_W), f32)
        kn_ref[...] = jnp.zeros_like(kn_ref)
        vn_ref[...] = jnp.zeros_like(vn_ref)

    x = x_ref[...]
    xn = _rmsnorm_rows(x, nmix_ref[...])
    z_ref[...] = jnp.dot(xn.astype(bf16), win_ref[...], preferred_element_type=f32)
    e = e_ref[...]

    redo_hgrn_if_unsafe = _hgrn_tile(z_ref[:, 0:A_W], z_ref[:, A_W:2 * A_W], z_ref[:, 2 * A_W:3 * A_W], lbp_ref,
                                     st_ref, st0_ref, o_ref, e, ltri_ref, lvl_ref, hm_ref)

    c0 = 4 * A_W
    u = z_ref[:, c0:c0 + B_W] * _sigmoid(z_ref[:, c0 + B_W:c0 + 2 * B_W])
    ob = _conv_tile(u, hist_ref, cw_ref, cvec_ref, e)
    conv_ref[...] = hist_ref[TILE + HIST - (CONV_WIDTH - 1):TILE + HIST, :]
    hist_ref[0:HIST, :] = hist_ref[TILE:TILE + HIST, :]

    c1 = c0 + 2 * B_W
    cos_a, sin_a = rope_tile_ref[0, pl.ds(i, 1), :], rope_tile_ref[1, pl.ds(i, 1), :]
    cos = cos_a * rope_row_ref[0] - sin_a * rope_row_ref[1]
    sin = sin_a * rope_row_ref[2] + cos_a * rope_row_ref[3]
    qn = jnp.concatenate([_head_norm(z_ref[:, c1 + j * 256:c1 + (j + 1) * 256], e,
                                     jnp.concatenate([qk_ref[0:1, :]] * 2, axis=1)) for j in range(2)], axis=1)
    q = _rope(qn, jnp.concatenate([cos] * 4, axis=1), jnp.concatenate([sin] * 4, axis=1)) * (LOG2E * HEAD ** -0.5)
    kk = _rope(_head_norm(z_ref[:, c1 + C_W:c1 + C_W + KV_W], e[0:KV_W, 0:KV_W], qk_ref[1:2, :]), cos, sin)
    vv = z_ref[:, c1 + C_W + KV_W:c1 + C_W + 2 * KV_W]
    kcat = jnp.concatenate([kn_ref[...], kk], axis=0)
    vcat = jnp.concatenate([vn_ref[...], vv], axis=0)
    q_t = q.T
    k_b = kcat.astype(bf16)
    v_t = vcat.T.astype(bf16)
    biases = [bias_ref[jnp.where(i > 0, 1, 0)]] + [bias_ref[1]] * (TILE // WINDOW - 1)
    oc = _attn_tile(q_t, k_b, v_t, sink_ref, biases)
    kn_ref[...] = kk[TILE - WINDOW:TILE]
    vn_ref[...] = vv[TILE - WINDOW:TILE]

    redo_hgrn_if_unsafe()
    gate = z_ref[:, 3 * A_W:4 * A_W]
    oa = _head_norm(o_ref[...], e, aon_ref[...]) * (gate * _sigmoid(gate))
    mix = jnp.concatenate([oa, ob, oc], axis=1).astype(bf16)
    h_ref[...] = x + jnp.dot(mix, wout_ref[...], preferred_element_type=f32)


def _const_spec(shape):
    nd = len(shape)
    return pl.BlockSpec(shape, lambda i, _n=nd: (0,) * _n)


def _layer_spec(arr, l, **kw):
    shape = arr.shape[1:]
    return pl.BlockSpec((None,) + shape, lambda i, _l=l, _n=len(shape): (_l,) + (0,) * _n, **kw)


def _prompt_mixer(x, rope, wl, l, consts):
    t_len = x.shape[0]
    row_spec = lambda w: pl.BlockSpec((TILE, w), lambda i: (i, 0))
    smalls = [wl['nmix'], wl['lbp'], wl['aon'], wl['cw'], wl['cvec'], wl['qk']]
    in_specs = ([row_spec(D_MODEL), _const_spec(rope[0].shape), _const_spec(rope[1].shape),
                 _layer_spec(wl['w_in'], l), _layer_spec(wl['w_out'], l)] + [_layer_spec(a, l) for a in smalls]
                + [pl.BlockSpec(memory_space=pltpu.SMEM)]
                + [_const_spec(a.shape) for a in consts])
    out_shape = (jax.ShapeDtypeStruct((t_len, D_MODEL), f32),
                 jax.ShapeDtypeStruct((A_W, A_W), f32),
                 jax.ShapeDtypeStruct((CONV_WIDTH - 1, B_W), f32),
                 jax.ShapeDtypeStruct((WINDOW, KV_W), f32),
                 jax.ShapeDtypeStruct((WINDOW, KV_W), f32))
    out_specs = (row_spec(D_MODEL), _const_spec((A_W, A_W)), _const_spec((CONV_WIDTH - 1, B_W)),
                 _const_spec((WINDOW, KV_W)), _const_spec((WINDOW, KV_W)))
    return pl.pallas_call(
        functools.partial(_mixer_kernel, l), grid=(t_len // TILE,), in_specs=in_specs, out_specs=out_specs,
        out_shape=out_shape,
        scratch_shapes=[pltpu.VMEM((TILE, IN_COLS), f32), pltpu.VMEM((TILE + HIST, B_W), f32),
                        pltpu.VMEM((TILE, A_W), f32), pltpu.VMEM((A_W, A_W), f32)],
        compiler_params=pltpu.CompilerParams(dimension_semantics=("arbitrary",), vmem_limit_bytes=VMEM_LIMIT),
        name="prompt_mixer",
    )(x, rope[0], rope[1], wl['w_in'], wl['w_out'], *smalls, wl['sinks'], *consts)


def _ffn_kernel(h_ref, p_ref, wg_ref, wu_ref, wd_ref, wpg_ref, wpp_ref, norms_ref, y_ref):
    h = h_ref[...]
    hn = _rmsnorm_rows(h, norms_ref[0:1, :]).astype(bf16)
    gt = jnp.dot(hn, wg_ref[...], preferred_element_type=f32)
    up = jnp.dot(hn, wu_ref[...], preferred_element_type=f32)
    act = (gt * _sigmoid(gt) * up).astype(bf16)
    h = h + jnp.dot(act, wd_ref[...], preferred_element_type=f32)
    pn = _rmsnorm_rows(h, norms_ref[1:2, :]).astype(bf16)
    gate = _sigmoid(jnp.dot(pn, wpg_ref[...], preferred_element_type=f32))
    y_ref[...] = h + gate * jnp.dot(p_ref[...].astype(bf16), wpp_ref[...], preferred_element_type=f32)


def _ffn(h, p, wl, l):
    n = h.shape[0]
    tile = min(FFN_TILE, n)
    assert n % tile == 0
    row_spec = lambda w: pl.BlockSpec((tile, w), lambda i: (i, 0))
    ws = [wl['w_gate'], wl['w_up'], wl['w_down'], wl['w_ple_gate'], wl['w_ple_proj']]
    return pl.pallas_call(
        _ffn_kernel, grid=(n // tile,),
        in_specs=[row_spec(D_MODEL), pl.BlockSpec((None, tile, PLE_DIM), lambda i, _l=l: (_l, i, 0))]
        + [_layer_spec(a, l, pipeline_mode=pl.Buffered(1)) for a in ws] + [_layer_spec(wl['fnorms'], l)],
        out_specs=row_spec(D_MODEL), out_shape=jax.ShapeDtypeStruct((n, D_MODEL), f32),
        compiler_params=pltpu.CompilerParams(dimension_semantics=("arbitrary",), vmem_limit_bytes=VMEM_LIMIT),
        name="ffn_ple",
    )(h, p, *ws, wl['fnorms'])


N_HEADS = C_W // HEAD


def _sample_in_kernel(n_tok, x_ref, cos_ref, sin_ref, win_ref, nmix_ref, qk_ref, e_ref, z_ref, qx_ref, k_ref,
                      piece_ref):
    n_seq = x_ref.shape[0] // n_tok
    xn = _rmsnorm_rows(x_ref[...], nmix_ref[...])
    z = jnp.dot(xn.astype(bf16), win_ref[...], preferred_element_type=f32)
    z_ref[...] = z
    e = e_ref[...]
    c1 = 4 * A_W + 2 * B_W
    cos, sin = cos_ref[...], sin_ref[...]
    qn = jnp.concatenate([_head_norm(z[:, c1 + j * 256:c1 + (j + 1) * 256], e,
                                     jnp.concatenate([qk_ref[0:1, :]] * 2, axis=1)) for j in range(2)], axis=1)
    q = _rope(qn, jnp.concatenate([cos] * 4, axis=1), jnp.concatenate([sin] * 4, axis=1)) * (HEAD ** -0.5)
    k_ref[...] = _rope(_head_norm(z[:, c1 + C_W:c1 + C_W + KV_W], e[0:KV_W, 0:KV_W], qk_ref[1:2, :]), cos, sin)
    lo_f = lax.broadcasted_iota(jnp.int32, (x_ref.shape[0], KV_W), 1) < HEAD
    for h in range(N_HEADS):
        pair, g = q[:, (h // 2) * KV_W:(h // 2 + 1) * KV_W], h // 4
        own = jnp.where(lo_f if h % 2 == 0 else jnp.logical_not(lo_f), pair, 0.0)
        piece_ref[...] = own if h % 2 == g else pltpu.roll(own, HEAD, 1)
        for t in range(n_tok):
            qx_ref[pl.ds(h * n_tok + t, n_seq, stride=N_HEADS * n_tok), :] = piece_ref[pl.ds(t, n_seq, stride=n_tok), :]


def _sample_in(x, cos, sin, wl, l, e256, n_tok):
    n = x.shape[0]
    args = (x, cos, sin, wl['w_in'], wl['nmix'], wl['qk'], e256)
    specs = [_const_spec(a.shape) for a in args]
    specs[3:6] = [_layer_spec(a, l) for a in args[3:6]]
    return pl.pallas_call(
        functools.partial(_sample_in_kernel, n_tok), grid=(1,), in_specs=specs,
        out_specs=(_const_spec((n, IN_COLS)), _const_spec((n * N_HEADS, KV_W)), _const_spec((n, KV_W))),
        out_shape=(jax.ShapeDtypeStruct((n, IN_COLS), f32), jax.ShapeDtypeStruct((n * N_HEADS, KV_W), f32),
                   jax.ShapeDtypeStruct((n, KV_W), f32)),
        scratch_shapes=[pltpu.VMEM((n, KV_W), f32)],
        compiler_params=pltpu.CompilerParams(dimension_semantics=("arbitrary",), vmem_limit_bytes=VMEM_LIMIT),
        name="sample_in",
    )(*args)


def _sample_hgrn_kernel(n_tok, q_ref, f_ref, v_ref, s_ref, lbp_ref, o_ref, so_ref, st_scr, qt_scr, ft_scr, kt_scr,
                        vt_scr, ot_scr):
    n_seq = s_ref.shape[0]
    st_scr[...] = s_ref[...].T
    for t in range(n_tok):
        zf = f_ref[pl.ds(t, n_seq, stride=n_tok), :]
        logf, ls = _log_forget(zf, lbp_ref[0:1, :], lbp_ref[1:2, :])
        ft_scr[t] = jnp.exp(logf).T
        kt_scr[t] = (lbp_ref[2:3, :] * jnp.exp(ls - zf)).T
        qt_scr[t] = q_ref[pl.ds(t, n_seq, stride=n_tok), :].T
        vt_scr[t] = v_ref[pl.ds(t, n_seq, stride=n_tok), :].T
        ot_scr[t] = jnp.zeros((2 * HEAD, n_seq), f32)

    def body(hk, carry):
        r0 = pl.multiple_of(hk * HEAD, HEAD)
        v0 = pl.multiple_of((hk // HEAD) * HEAD, HEAD)
        blk = st_scr[pl.ds(r0, HEAD), :]
        for t in range(n_tok):
            blk = blk * ft_scr[t, pl.ds(hk, 1), :] + kt_scr[t, pl.ds(hk, 1), :] * vt_scr[t, pl.ds(v0, HEAD), :]
            ot_scr[t, pl.ds(v0, HEAD), :] = ot_scr[t, pl.ds(v0, HEAD), :] + qt_scr[t, pl.ds(hk, 1), :] * blk
        st_scr[pl.ds(r0, HEAD), :] = blk
        return carry

    lax.fori_loop(0, 2 * HEAD, body, 0, unroll=2)
    so_ref[...] = st_scr[...].T
    for t in range(n_tok):
        o_ref[pl.ds(t, n_seq, stride=n_tok), :] = ot_scr[t].T


def _sample_hgrn(z, state, lbp, l, n_tok):
    n = z.shape[0]
    n_seq = state.shape[1]
    blk = 2 * HEAD * HEAD
    col = lambda j0: pl.BlockSpec((n, 2 * HEAD), lambda i, _j=j0: (0, _j + i))
    return pl.pallas_call(
        functools.partial(_sample_hgrn_kernel, n_tok), grid=(2,),
        in_specs=[col(0), col(2), col(4), pl.BlockSpec((None, n_seq, blk), lambda i, _l=l: (_l, 0, i)),
                  pl.BlockSpec((None, 3, 2 * HEAD), lambda i, _l=l: (_l, 0, i))],
        out_specs=(pl.BlockSpec((n, 2 * HEAD), lambda i: (0, i)), pl.BlockSpec((n_seq, blk), lambda i: (0, i))),
        out_shape=(jax.ShapeDtypeStruct((n, A_W), f32), jax.ShapeDtypeStruct(state.shape[1:], f32)),
        scratch_shapes=[pltpu.VMEM((blk, n_seq), f32)] + [pltpu.VMEM((n_tok, 2 * HEAD, n_seq), f32)] * 5,
        compiler_params=pltpu.CompilerParams(dimension_semantics=("arbitrary",), vmem_limit_bytes=VMEM_LIMIT),
        name="sample_hgrn",
    )(z, z, z, state, lbp)


def _sample_conv_kernel(n_tok, u0_ref, u1_ref, g0_ref, g1_ref, buf_ref, cw_ref, cvec_ref, e_ref, ob0_ref, ob1_ref,
                        new_ref):
    n_seq = buf_ref.shape[0]
    n_hist = CONV_WIDTH - 1
    us = []
    for t in range(n_tok):
        rows = pl.ds(t, n_seq, stride=n_tok)
        u = jnp.concatenate([u0_ref[rows, :], u1_ref[rows, :]], axis=1)
        g = jnp.concatenate([g0_ref[rows, :], g1_ref[rows, :]], axis=1)
        us.append(u * _sigmoid(g))
    buf = buf_ref[...]
    for t in range(n_tok):
        taps = cw_ref[0:n_hist - t, :]
        if t:
            taps = jnp.concatenate([jnp.zeros((t, B_W), f32), taps], axis=0)
        y = jnp.sum(buf * taps[None], axis=1) + cvec_ref[0:1, :]
        for s in range(t + 1):
            j = n_hist + s - t
            y = y + us[s] * cw_ref[j:j + 1, :]
        ob = _group_ln_silu(y, e_ref[...], cvec_ref[1:2, :], cvec_ref[2:3, :])
        ob0_ref[pl.ds(t, n_seq, stride=n_tok), :] = ob[:, 0:KV_W]
        ob1_ref[pl.ds(t, n_seq, stride=n_tok), :] = ob[:, KV_W:B_W]
    new_ref[:, 0:n_hist - n_tok, :] = buf_ref[:, n_tok:n_hist, :]
    for s in range(n_tok):
        new_ref[:, n_hist - n_tok + s, :] = us[s]


def _sample_conv(z, buf, wl, l, e256, n_tok):
    n = z.shape[0]
    col = lambda j: pl.BlockSpec((n, KV_W), lambda i, _j=j: (0, _j))
    half = jax.ShapeDtypeStruct((n, KV_W), f32)
    return pl.pallas_call(
        functools.partial(_sample_conv_kernel, n_tok), grid=(1,),
        in_specs=[col(8), col(9), col(10), col(11), _layer_spec(buf, l), _layer_spec(wl['cw'], l),
                  _layer_spec(wl['cvec'], l), _const_spec(e256.shape)],
        out_specs=(_const_spec((n, KV_W)), _const_spec((n, KV_W)), _const_spec(buf.shape[1:])),
        out_shape=(half, half, jax.ShapeDtypeStruct(buf.shape[1:], f32)),
        compiler_params=pltpu.CompilerParams(dimension_semantics=("arbitrary",), vmem_limit_bytes=VMEM_LIMIT),
        name="sample_conv",
    )(z, z, z, z, buf, wl['cw'], wl['cvec'], e256)


SEQ_BLOCK = 8


def _sample_attn_kernel(n_tok, qx_ref, k_ref, v_ref, kc_ref, vc_ref, sink_ref, bias_ref, ox_ref, kn_ref, vn_ref):
    q_rows = N_HEADS * n_tok
    sink, bias = sink_ref[...], bias_ref[...]
    seqs = range(SEQ_BLOCK)
    k_all = [jnp.concatenate([kc_ref[b], k_ref[b * n_tok:(b + 1) * n_tok, :]], axis=0) for b in seqs]
    v_all = [jnp.concatenate([vc_ref[b], v_ref[b * n_tok:(b + 1) * n_tok, :]], axis=0) for b in seqs]
    s = [lax.dot_general(qx_ref[b * q_rows:(b + 1) * q_rows, :].astype(bf16), k_all[b].astype(bf16),
                         (((1,), (1,)), ((), ())), preferred_element_type=f32) + bias for b in seqs]
    mx = [jnp.maximum(jnp.max(s[b], axis=-1, keepdims=True), sink) for b in seqs]
    pr = [jnp.exp(s[b] - mx[b]) for b in seqs]
    rden = [1.0 / (jnp.sum(pr[b], axis=-1, keepdims=True) + jnp.exp(sink - mx[b])) for b in seqs]
    for b in seqs:
        ox_ref[b * q_rows:(b + 1) * q_rows, :] = jnp.dot(pr[b].astype(bf16), v_all[b].astype(bf16),
                                                         preferred_element_type=f32) * rden[b]
    for b in seqs:
        kn_ref[b] = k_all[b][n_tok:, :]
        vn_ref[b] = v_all[b][n_tok:, :]


def _sample_attn(qx, k, z, kc, vc, sink_col, bias, l, n_tok):
    n_seq = kc.shape[1]
    rows = SEQ_BLOCK * n_tok
    q_rows = SEQ_BLOCK * N_HEADS * n_tok
    v_col = (4 * A_W + 2 * B_W + C_W + KV_W) // KV_W
    cache_in = pl.BlockSpec((None, SEQ_BLOCK) + kc.shape[2:], lambda i, _l=l: (_l, i, 0, 0))
    cache_out = pl.BlockSpec((SEQ_BLOCK,) + kc.shape[2:], lambda i: (i, 0, 0))
    return pl.pallas_call(
        functools.partial(_sample_attn_kernel, n_tok), grid=(n_seq // SEQ_BLOCK,),
        in_specs=[pl.BlockSpec((q_rows, KV_W), lambda i: (i, 0)), pl.BlockSpec((rows, KV_W), lambda i: (i, 0)),
                  pl.BlockSpec((rows, KV_W), lambda i: (i, v_col)), cache_in, cache_in,
                  _layer_spec(sink_col, l), _const_spec(bias.shape)],
        out_specs=(pl.BlockSpec((q_rows, KV_W), lambda i: (i, 0)), cache_out, cache_out),
        out_shape=(jax.ShapeDtypeStruct(qx.shape, f32), jax.ShapeDtypeStruct(kc.shape[1:], f32),
                   jax.ShapeDtypeStruct(vc.shape[1:], f32)),
        compiler_params=pltpu.CompilerParams(dimension_semantics=("arbitrary",), vmem_limit_bytes=VMEM_LIMIT),
        name="sample_attn",
    )(qx, k, z, kc, vc, sink_col, bias)


def _sample_out_kernel(n_tok, x_ref, oa_ref, g_ref, ob0_ref, ob1_ref, ox_ref, wout_ref, aon_ref, e_ref, h_ref,
                       oc_ref):
    n_seq = x_ref.shape[0] // n_tok
    gate = g_ref[...]
    oa = _head_norm(oa_ref[...], e_ref[...], aon_ref[...]) * (gate * _sigmoid(gate))
    lo_f = lax.broadcasted_iota(jnp.int32, (n_seq, KV_W), 1) < HEAD
    for p in range(N_HEADS // 2):
        g = p // 2
        for t in range(n_tok):
            even = ox_ref[pl.ds((2 * p) * n_tok + t, n_seq, stride=N_HEADS * n_tok), :]
            odd = ox_ref[pl.ds((2 * p + 1) * n_tok + t, n_seq, stride=N_HEADS * n_tok), :]
            even = even if g == 0 else pltpu.roll(even, HEAD, 1)
            odd = odd if g == 1 else pltpu.roll(odd, HEAD, 1)
            oc_ref[p, pl.ds(t, n_seq, stride=n_tok), :] = jnp.where(lo_f, even, odd)
    mix = jnp.concatenate([oa, ob0_ref[...], ob1_ref[...]] + [oc_ref[p] for p in range(N_HEADS // 2)],
                          axis=1).astype(bf16)
    h_ref[...] = x_ref[...] + jnp.dot(mix, wout_ref[...], preferred_element_type=f32)


def _sample_out(x, oa, z, ob0, ob1, ox, wl, l, e256, n_tok):
    n = x.shape[0]
    gate_spec = pl.BlockSpec((n, A_W), lambda i: (0, 3))
    args = (x, oa, z, ob0, ob1, ox, wl['w_out'], wl['aon'], e256)
    specs = [_const_spec(a.shape) for a in args]
    specs[2] = gate_spec
    specs[6:8] = [_layer_spec(a, l) for a in args[6:8]]
    return pl.pallas_call(
        functools.partial(_sample_out_kernel, n_tok), grid=(1,), in_specs=specs,
        out_specs=_const_spec((n, D_MODEL)), out_shape=jax.ShapeDtypeStruct((n, D_MODEL), f32),
        scratch_shapes=[pltpu.VMEM((N_HEADS // 2, n, KV_W), f32)],
        compiler_params=pltpu.CompilerParams(dimension_semantics=("arbitrary",), vmem_limit_bytes=VMEM_LIMIT),
        name="sample_out",
    )(*args)


def _rope_cos_sin(pos):
    half = HEAD // 2
    inv = ROPE_THETA ** (-jnp.arange(half, dtype=f32) / half)
    ang = pos[:, None] * inv[None, :]
    cos, sin = jnp.cos(ang), jnp.sin(ang)
    return jnp.tile(cos, (1, 4)), jnp.concatenate([-sin, sin, -sin, sin], axis=1)


def _rope_prompt_tables(t_len):
    half = HEAD // 2
    inv = ROPE_THETA ** (-jnp.arange(half, dtype=f32) / half)
    sign = jnp.concatenate([-jnp.ones((half,), f32), jnp.ones((half,), f32)] * 2)
    ang_r = jnp.arange(TILE, dtype=f32)[:, None] * inv[None, :]
    ang_t = (jnp.arange(t_len // TILE, dtype=f32) * TILE)[:, None] * inv[None, :]
    cos_r, sin_r = jnp.tile(jnp.cos(ang_r), (1, 4)), jnp.tile(jnp.sin(ang_r), (1, 4))
    rows = jnp.stack([cos_r, sin_r, cos_r * sign, sin_r * sign])
    tiles = jnp.stack([jnp.tile(jnp.cos(ang_t), (1, 4)), jnp.tile(jnp.sin(ang_t), (1, 4))])
    return rows, tiles


def _stacked_weights(lbs, n_tok, w_in, a_onorm, conv_w, conv_b, conv_ln_g, conv_ln_b, q_norm, k_norm, sinks, w_out,
                     norm_mix, norm_ffn, w_gate, w_up, w_down, ple_norm, w_ple_gate, w_ple_proj):
    return dict(
        w_in=w_in.astype(bf16), w_out=w_out.astype(bf16), w_gate=w_gate.astype(bf16), w_up=w_up.astype(bf16),
        w_down=w_down.astype(bf16), w_ple_gate=w_ple_gate.astype(bf16), w_ple_proj=w_ple_proj.astype(bf16),
        nmix=norm_mix[:, None, :],
        lbp=jnp.stack([jnp.log(lbs), jnp.log1p(-lbs), 1.0 - lbs], axis=1),
        aon=jnp.tile(a_onorm, (1, 4))[:, None, :],
        cw=conv_w,
        cvec=jnp.stack([conv_b, conv_ln_g, conv_ln_b], axis=1),
        qk=jnp.stack([jnp.tile(q_norm, (1, 2)), jnp.tile(k_norm, (1, 2))], axis=1),
        sinks=sinks,
        sink_col=jnp.repeat(sinks, n_tok, axis=1)[:, :, None],
        fnorms=jnp.stack([norm_ffn, ple_norm], axis=1),
    )


def kernel(x_prompt, x_sample, state_hgrn, state_conv, cache_swa_k, cache_swa_v, p_prompt, p_sample, a_lower, w_in, a_onorm, conv_w, conv_b, conv_ln_g, conv_ln_b, q_norm, k_norm, sinks, w_out, norm_mix, norm_ffn, w_gate, w_up, w_down, ple_norm, w_ple_gate, w_ple_proj):
    depth = w_in.shape[0]
    t_len = x_prompt.shape[1]
    n_seq, n_tok = x_sample.shape[0], x_sample.shape[1]
    past_len = 16384
    assert x_prompt.shape[0] == 1 and t_len % TILE == 0

    lbs = jnp.cumsum(jax.nn.softmax(a_lower.astype(f32), axis=0), axis=0)
    lbs = lbs - lbs[0:1]
    consts_np = _np_constants()
    e256 = jnp.asarray(consts_np[0], bf16)
    consts = (e256, jnp.asarray(consts_np[1], bf16), jnp.asarray(consts_np[2], f32), jnp.asarray(consts_np[3], bf16),
              jnp.asarray(consts_np[4], f32))

    rope_p = _rope_prompt_tables(t_len)
    cos_s, sin_s = _rope_cos_sin(past_len + jnp.arange(n_tok, dtype=f32))
    cos_s, sin_s = jnp.tile(cos_s, (n_seq, 1)), jnp.tile(sin_s, (n_seq, 1))
    w_buf = cache_swa_k.shape[2]
    q_tok = (np.arange(N_HEADS * n_tok) % n_tok)[:, None]
    rel = q_tok + w_buf - np.arange(w_buf + n_tok)[None, :]
    bias_s = jnp.asarray(np.where((rel >= 0) & (rel <= WINDOW), 0.0, -np.inf), f32)

    wl = _stacked_weights(lbs, n_tok, w_in, a_onorm, conv_w, conv_b, conv_ln_g, conv_ln_b, q_norm, k_norm, sinks,
                          w_out, norm_mix, norm_ffn, w_gate, w_up, w_down, ple_norm, w_ple_gate, w_ple_proj)
    p_p = p_prompt.reshape(depth, t_len, PLE_DIM)
    p_s = p_sample.reshape(depth, n_seq * n_tok, PLE_DIM)
    s_hgrn = state_hgrn.reshape(depth, n_seq, -1)
    kc = cache_swa_k.reshape(depth, n_seq, w_buf, KV_W)
    vc = cache_swa_v.reshape(depth, n_seq, w_buf, KV_W)

    hp = x_prompt[0]
    hs = x_sample.reshape(n_seq * n_tok, D_MODEL)
    outs = [[] for _ in range(8)]
    for l in range(depth):
        h_mid, st, conv_new, k_new, v_new = _prompt_mixer(hp, rope_p, wl, l, consts)
        hp = _ffn(h_mid, p_p, wl, l)
        s_new = jnp.stack([st[h * HEAD:(h + 1) * HEAD, h * HEAD:(h + 1) * HEAD].T for h in range(4)])
        outs[0].append(s_new[None])
        outs[1].append(conv_new[None])
        outs[2].append(k_new.reshape(1, WINDOW, 2, HEAD))
        outs[3].append(v_new.reshape(1, WINDOW, 2, HEAD))
        z, qx, k = _sample_in(hs, cos_s, sin_s, wl, l, e256, n_tok)
        oa, s_s = _sample_hgrn(z, s_hgrn, wl['lbp'], l, n_tok)
        ob0, ob1, conv_s = _sample_conv(z, state_conv, wl, l, e256, n_tok)
        ox, k_s, v_s = _sample_attn(qx, k, z, kc, vc, wl['sink_col'], bias_s, l, n_tok)
        h_mid_s = _sample_out(hs, oa, z, ob0, ob1, ox, wl, l, e256, n_tok)
        hs = _ffn(h_mid_s, p_s, wl, l)
        outs[4].append(s_s.reshape(n_seq, 4, HEAD, HEAD))
        outs[5].append(conv_s)
        outs[6].append(k_s.reshape(n_seq, w_buf, 2, HEAD))
        outs[7].append(v_s.reshape(n_seq, w_buf, 2, HEAD))
    return (hp[None], hs.reshape(n_seq, n_tok, D_MODEL)) + tuple(jnp.stack(o) for o in outs)
```

```python
import functools

import numpy as np
import jax
import jax.numpy as jnp
from jax import lax
from jax.experimental import pallas as pl
from jax.experimental.pallas import tpu as pltpu

f32 = jnp.float32
bf16 = jnp.bfloat16

D_MODEL = 1024
HEAD = 64
A_W = 256
B_W = 256
C_W = 512
KV_W = 128
IN_COLS = 4 * A_W + 2 * B_W + C_W + 2 * KV_W
D_FF = 2816
PLE_DIM = 256
WINDOW = 128
CONV_WIDTH = 31
ROPE_THETA = 10000.0
EPS = 1e-6

TILE = 512
CUM_ROWS = 256
CHUNK = 64
LEVELS = (32, 16, 8, 4, 2, 1)
SAFE_EXPONENT = 80.0
HIST = 32
FFN_TILE = 512
FFN_SLABS = 4
LOG2E = 1.4426950408889634
VMEM_LIMIT = 52 * 1024 * 1024


def _split2(x):
    hi = x.astype(bf16)
    lo = (x - hi.astype(f32)).astype(bf16)
    return hi, lo


def _split3(x):
    hi = x.astype(bf16)
    r = x - hi.astype(f32)
    mid = r.astype(bf16)
    lo = (r - mid.astype(f32)).astype(bf16)
    return hi, mid, lo


def _seg_sum(x, e):
    hi, lo = _split2(x)
    return (jnp.dot(hi, e, preferred_element_type=f32) + jnp.dot(lo, e, preferred_element_type=f32))


def _sigmoid(x):
    return 1.0 / (1.0 + jnp.exp(-x))


def _rmsnorm_rows(x, g):
    return x * lax.rsqrt(jnp.mean(x * x, axis=-1, keepdims=True) + EPS) * g


def _seg_stat(x, e):
    return jnp.dot(x.astype(bf16), e, preferred_element_type=f32)


def _head_norm(x, e, g):
    ms = _seg_stat(x * x, e) * (1.0 / HEAD)
    return x * lax.rsqrt(ms + EPS) * g


def _rope(x, cos, sin_signed):
    w = x.shape[-1]
    lane = lax.broadcasted_iota(jnp.int32, x.shape, 1)
    first = (lane % HEAD) < (HEAD // 2)
    swapped = jnp.where(first, pltpu.roll(x, w - HEAD // 2, 1), pltpu.roll(x, HEAD // 2, 1))
    return x * cos + swapped * sin_signed


def _log_forget(zf, log_lb, log_1m_lb):
    ls = jnp.minimum(zf, 0.0) - jnp.log(1.0 + jnp.exp(-jnp.abs(zf)))
    b = log_1m_lb + ls
    return jnp.maximum(log_lb, b) + jnp.log(1.0 + jnp.exp(-jnp.abs(log_lb - b))), ls


def _group_ln_silu(y, e, g, b):
    mu = _seg_sum(y, e) * (1.0 / HEAD)
    yc = y - mu
    var = _seg_stat(yc * yc, e) * (1.0 / HEAD)
    yn = yc * lax.rsqrt(var + EPS) * g + b
    return yn * _sigmoid(yn)


def _np_constants():
    lane = np.arange(256)
    e256 = (lane[:, None] // HEAD == lane[None, :] // HEAD).astype(np.float32)
    t = np.arange(CUM_ROWS)
    ltri = ((t[:, None] // CHUNK == t[None, :] // CHUNK) & (t[None, :] <= t[:, None])).astype(np.float32)
    tt = np.arange(CHUNK)[:, None]
    ss = (np.arange(256) % CHUNK)[None, :]
    lvl = []
    for m in LEVELS:
        lvl.append(((tt // (2 * m) == ss // (2 * m)) & (tt % (2 * m) >= m) & (ss % (2 * m) < m)).astype(np.float32))
    lvl.append((ss <= tt).astype(np.float32))
    lvl = np.stack(lvl)
    hm = np.stack([np.broadcast_to((lane // HEAD == h)[None, :], (CHUNK, 256)) for h in range(4)]).astype(np.float32)
    keys = np.arange(2 * WINDOW)[:, None]
    qry = np.arange(4 * WINDOW)[None, :] % WINDOW
    band = (keys >= qry) & (keys <= qry + WINDOW)
    bias = np.where(np.stack([band & (keys >= WINDOW), band]), 0.0, -np.inf).astype(np.float32)
    return e256, ltri, lvl, hm, bias


def _level_ref(g, m):
    n_rows = g.shape[0]
    if 2 * m >= 8:
        g3 = g.reshape(n_rows // (2 * m), 2 * m, g.shape[1])
        return jnp.broadcast_to(g3[:, m - 1:m, :], g3.shape).reshape(g.shape)
    g3 = g.reshape(n_rows // 8, 8, g.shape[1])
    sub = lax.broadcasted_iota(jnp.int32, g3.shape, 1)
    out = None
    for blk in range(8 // (2 * m)):
        r = blk * 2 * m + m - 1
        piece = jnp.broadcast_to(g3[:, r:r + 1, :], g3.shape)
        out = piece if out is None else jnp.where(sub >= blk * 2 * m, piece, out)
    return out.reshape(g.shape)


def _hgrn_tile(q, zf, v, lbp_ref, st_ref, st0_ref, o_ref, e, ltri_ref, lvl_ref, hm_ref):
    log_lb, log_1m_lb, one_m_lb = lbp_ref[0:1, :], lbp_ref[1:2, :], lbp_ref[2:3, :]
    logf, ls = _log_forget(zf, log_lb, log_1m_lb)
    kin = one_m_lb * jnp.exp(ls - zf)
    parts = jnp.concatenate(_split3(logf), axis=1)
    g3 = jnp.concatenate([jnp.dot(ltri_ref[...], parts[r:r + CUM_ROWS], preferred_element_type=f32)
                          for r in range(0, TILE, CUM_ROWS)], axis=0)
    g = g3[:, 0:A_W] + g3[:, A_W:2 * A_W] + g3[:, 2 * A_W:3 * A_W]
    v_b = v.astype(bf16)
    vt_b = v.T.astype(bf16)
    e_mid = g - _level_ref(g, CHUNK // 2)
    safe = jnp.max(jnp.abs(e_mid)) < SAFE_EXPONENT

    def expand(rows_b):
        return jnp.concatenate([rows_b * hm_ref[h] for h in range(4)], axis=0)

    def finish_chunk(c, att, q_in, k_hat):
        r0, r1 = c * CHUNK, (c + 1) * CHUNK
        st = st_ref[...]
        o = jnp.dot(att.astype(bf16), expand(v_b[r0:r1]), preferred_element_type=f32)
        o = o + lax.dot_general(q_in, st.astype(bf16) * e, (((1,), (1,)), ((), ())), preferred_element_type=f32)
        zero = jnp.zeros((CHUNK, A_W), bf16)
        pair = (c // 2) * 2 * CHUNK
        rhs = jnp.concatenate([k_hat, zero] if c % 2 == 0 else [zero, k_hat], axis=0)
        upd = jnp.dot(vt_b[:, pair:pair + 2 * CHUNK], rhs, preferred_element_type=f32)
        st_ref[...] = st * jnp.exp(g[r1 - 1:r1, :]) + upd
        o_ref[r0:r1, :] = o

    st0_ref[...] = st_ref[...]
    qw = q * jnp.exp(e_mid)
    kw = kin * jnp.exp(-e_mid)
    q_t, k_t = qw.astype(bf16), kw.astype(bf16)
    chunks = [(c * CHUNK, (c + 1) * CHUNK) for c in range(TILE // CHUNK)]
    atts = [lax.dot_general(q_t[r0:r1], expand(k_t[r0:r1]), (((1,), (1,)), ((), ())), preferred_element_type=f32)
            for r0, r1 in chunks]
    atts = [jnp.where(lvl_ref[len(LEVELS)] > 0.5, a, 0.0) for a in atts]
    for c, (r0, r1) in enumerate(chunks):
        g_mid, g_end = g[r0 + CHUNK // 2 - 1:r0 + CHUNK // 2, :], g[r1 - 1:r1, :]
        finish_chunk(c, atts[c], (qw[r0:r1] * jnp.exp(g_mid)).astype(bf16),
                     (kw[r0:r1] * jnp.exp(g_end - g_mid)).astype(bf16))

    def redo_if_unsafe():
        @pl.when(jnp.logical_not(safe))
        def _():
            st_ref[...] = st0_ref[...]
            q_levels, k_levels = [], []
            for m in LEVELS:
                w = jnp.exp(-jnp.abs(g - _level_ref(g, m)))
                q_levels.append((q * w).astype(bf16))
                k_levels.append((kin * w).astype(bf16))
            q_inter = (q * jnp.exp(g)).astype(bf16)
            for c in range(TILE // CHUNK):
                r0, r1 = c * CHUNK, (c + 1) * CHUNK
                att = jnp.zeros((CHUNK, 4 * CHUNK), f32)
                for li in range(len(LEVELS)):
                    a = lax.dot_general(q_levels[li][r0:r1], expand(k_levels[li][r0:r1]),
                                        (((1,), (1,)), ((), ())), preferred_element_type=f32)
                    att = att + a * lvl_ref[li]
                k_hat = (kin[r0:r1] * jnp.exp(g[r1 - 1:r1, :] - g[r0:r1])).astype(bf16)
                finish_chunk(c, att, q_inter[r0:r1], k_hat)
            o_ref[...] = o_ref[...] + _seg_sum(q * kin, e) * v

    return redo_if_unsafe


def _conv_tile(u, hist_ref, cw_ref, cvec_ref, e):
    n = u.shape[0]
    hist_ref[HIST:HIST + n, :] = u
    lo = HIST - (CONV_WIDTH - 1)
    y = cvec_ref[0:1, :]
    for r in range(8):
        rows = n if r == 0 else n + 8
        part = None
        for o in range(lo, lo + CONV_WIDTH):
            if o % 8 == r:
                term = hist_ref[o - r:o - r + rows, :] * cw_ref[o - lo:o - lo + 1, :]
                part = term if part is None else part + term
        y = y + part[r:r + n]
    return _group_ln_silu(y, e, cvec_ref[1:2, :], cvec_ref[2:3, :])


def _attn_tile(q_t, k_b, v_t, sink_ref, biases):
    zero = jnp.zeros((HEAD, WINDOW), f32)
    inst = [(b, g) for b in range(TILE // WINDOW) for g in range(2)]
    sinks = [jnp.concatenate([jnp.full((1, WINDOW), sink_ref[h] * LOG2E, f32) for h in range(4 * g, 4 * g + 4)],
                             axis=1) for g in range(2)]

    def rhs(b, g):
        cols = []
        for h in range(4 * g, 4 * g + 4):
            q_h = q_t[h * HEAD:(h + 1) * HEAD, b * WINDOW:(b + 1) * WINDOW]
            cols.append(jnp.concatenate([q_h, zero] if g == 0 else [zero, q_h], axis=0))
        return jnp.concatenate(cols, axis=1).astype(bf16)

    s = [jnp.dot(k_b[b * WINDOW:(b + 2) * WINDOW], rhs(b, g), preferred_element_type=f32) + biases[b]
         for b, g in inst]
    mx = [jnp.maximum(jnp.max(s[n], axis=0, keepdims=True), sinks[g]) for n, (b, g) in enumerate(inst)]
    pr = [jnp.exp2(s[n] - mx[n]) for n in range(len(inst))]
    rden = [1.0 / (jnp.sum(pr[n], axis=0, keepdims=True) + jnp.exp2(sinks[g] - mx[n]))
            for n, (b, g) in enumerate(inst)]
    o_t = [jnp.dot(v_t[g * HEAD:(g + 1) * HEAD, b * WINDOW:(b + 2) * WINDOW], pr[n].astype(bf16),
                   preferred_element_type=f32) * rden[n] for n, (b, g) in enumerate(inst)]
    rows = []
    for b in range(TILE // WINDOW):
        pairs = []
        for g in range(2):
            o = o_t[2 * b + g]
            for j in range(2):
                pair_t = jnp.concatenate([o[:, (2 * j) * WINDOW:(2 * j + 1) * WINDOW],
                                          o[:, (2 * j + 1) * WINDOW:(2 * j + 2) * WINDOW]], axis=0)
                pairs.append(pair_t.T)
        rows.append(jnp.concatenate(pairs, axis=1))
    return jnp.concatenate(rows, axis=0)


def _mixer_kernel(layer, x_ref, rope_row_ref, rope_tile_ref, win_ref, wout_ref, nmix_ref, lbp_ref, aon_ref, cw_ref,
                  cvec_ref, qk_ref, sinks_ref, e_ref, ltri_ref, lvl_ref, hm_ref, bias_ref,
                  h_ref, st_ref, conv_ref, kn_ref, vn_ref, z_ref, hist_ref, o_ref, st0_ref):
    i = pl.program_id(0)
    sink_ref = sinks_ref.at[layer]

    @pl.when(i == 0)
    def _():
        st_ref[...] = jnp.zeros_like(st_ref)
        hist_ref[0:HIST, :] = jnp.zeros((HIST, B_W), f32)
        kn_ref[...] = jnp.zeros_like(kn_ref)
        vn_ref[...] = jnp.zeros_like(vn_ref)

    x = x_ref[...]
    xn = _rmsnorm_rows(x, nmix_ref[...])
    z_ref[...] = jnp.dot(xn.astype(bf16), win_ref[...], preferred_element_type=f32)
    e = e_ref[...]

    redo_hgrn_if_unsafe = _hgrn_tile(z_ref[:, 0:A_W], z_ref[:, A_W:2 * A_W], z_ref[:, 2 * A_W:3 * A_W], lbp_ref,
                                     st_ref, st0_ref, o_ref, e, ltri_ref, lvl_ref, hm_ref)

    c0 = 4 * A_W
    u = z_ref[:, c0:c0 + B_W] * _sigmoid(z_ref[:, c0 + B_W:c0 + 2 * B_W])
    ob = _conv_tile(u, hist_ref, cw_ref, cvec_ref, e)
    conv_ref[...] = hist_ref[TILE + HIST - (CONV_WIDTH - 1):TILE + HIST, :]
    hist_ref[0:HIST, :] = hist_ref[TILE:TILE + HIST, :]

    c1 = c0 + 2 * B_W
    cos_a, sin_a = rope_tile_ref[0, pl.ds(i, 1), :], rope_tile_ref[1, pl.ds(i, 1), :]
    cos = cos_a * rope_row_ref[0] - sin_a * rope_row_ref[1]
    sin = sin_a * rope_row_ref[2] + cos_a * rope_row_ref[3]
    qn = jnp.concatenate([_head_norm(z_ref[:, c1 + j * 256:c1 + (j + 1) * 256], e,
                                     jnp.concatenate([qk_ref[0:1, :]] * 2, axis=1)) for j in range(2)], axis=1)
    q = _rope(qn, jnp.concatenate([cos] * 4, axis=1), jnp.concatenate([sin] * 4, axis=1)) * (LOG2E * HEAD ** -0.5)
    kk = _rope(_head_norm(z_ref[:, c1 + C_W:c1 + C_W + KV_W], e[0:KV_W, 0:KV_W], qk_ref[1:2, :]), cos, sin)
    vv = z_ref[:, c1 + C_W + KV_W:c1 + C_W + 2 * KV_W]
    kcat = jnp.concatenate([kn_ref[...], kk], axis=0)
    vcat = jnp.concatenate([vn_ref[...], vv], axis=0)
    q_t = q.T
    k_b = kcat.astype(bf16)
    v_t = vcat.T.astype(bf16)
    biases = [bias_ref[jnp.where(i > 0, 1, 0)]] + [bias_ref[1]] * (TILE // WINDOW - 1)
    oc = _attn_tile(q_t, k_b, v_t, sink_ref, biases)
    kn_ref[...] = kk[TILE - WINDOW:TILE]
    vn_ref[...] = vv[TILE - WINDOW:TILE]

    redo_hgrn_if_unsafe()
    gate = z_ref[:, 3 * A_W:4 * A_W]
    oa = _head_norm(o_ref[...], e, aon_ref[...]) * (gate * _sigmoid(gate))
    mix = jnp.concatenate([oa, ob, oc], axis=1).astype(bf16)
    h_ref[...] = x + jnp.dot(mix, wout_ref[...], preferred_element_type=f32)


def _const_spec(shape):
    nd = len(shape)
    return pl.BlockSpec(shape, lambda i, _n=nd: (0,) * _n)


def _layer_spec(arr, l, **kw):
    shape = arr.shape[1:]
    return pl.BlockSpec((None,) + shape, lambda i, _l=l, _n=len(shape): (_l,) + (0,) * _n, **kw)


def _prompt_mixer(x, rope, wl, l, consts):
    t_len = x.shape[0]
    row_spec = lambda w: pl.BlockSpec((TILE, w), lambda i: (i, 0))
    smalls = [wl['nmix'], wl['lbp'], wl['aon'], wl['cw'], wl['cvec'], wl['qk']]
    in_specs = ([row_spec(D_MODEL), _const_spec(rope[0].shape), _const_spec(rope[1].shape),
                 _layer_spec(wl['w_in'], l), _layer_spec(wl['w_out'], l)] + [_layer_spec(a, l) for a in smalls]
                + [pl.BlockSpec(memory_space=pltpu.SMEM)]
                + [_const_spec(a.shape) for a in consts])
    out_shape = (jax.ShapeDtypeStruct((t_len, D_MODEL), f32),
                 jax.ShapeDtypeStruct((A_W, A_W), f32),
                 jax.ShapeDtypeStruct((CONV_WIDTH - 1, B_W), f32),
                 jax.ShapeDtypeStruct((WINDOW, KV_W), f32),
                 jax.ShapeDtypeStruct((WINDOW, KV_W), f32))
    out_specs = (row_spec(D_MODEL), _const_spec((A_W, A_W)), _const_spec((CONV_WIDTH - 1, B_W)),
                 _const_spec((WINDOW, KV_W)), _const_spec((WINDOW, KV_W)))
    return pl.pallas_call(
        functools.partial(_mixer_kernel, l), grid=(t_len // TILE,), in_specs=in_specs, out_specs=out_specs,
        out_shape=out_shape,
        scratch_shapes=[pltpu.VMEM((TILE, IN_COLS), f32), pltpu.VMEM((TILE + HIST, B_W), f32),
                        pltpu.VMEM((TILE, A_W), f32), pltpu.VMEM((A_W, A_W), f32)],
        compiler_params=pltpu.CompilerParams(dimension_semantics=("arbitrary",), vmem_limit_bytes=VMEM_LIMIT),
        name="prompt_mixer",
    )(x, rope[0], rope[1], wl['w_in'], wl['w_out'], *smalls, wl['sinks'], *consts)


def _ffn_kernel(h_ref, p_ref, wg_ref, wu_ref, wd_ref, wpg_ref, wpp_ref, norms_ref, y_ref):
    n = h_ref.shape[0]
    slabs = [slice(r, r + n // FFN_SLABS) for r in range(0, n, n // FFN_SLABS)]
    h = [h_ref[s, :] for s in slabs]
    hn = [_rmsnorm_rows(a, norms_ref[0:1, :]).astype(bf16) for a in h]
    gt = [jnp.dot(a, wg_ref[...], preferred_element_type=f32) for a in hn]
    up = [jnp.dot(a, wu_ref[...], preferred_element_type=f32) for a in hn]
    act = [(g * _sigmoid(g) * u).astype(bf16) for g, u in zip(gt, up)]
    h = [a + jnp.dot(b, wd_ref[...], preferred_element_type=f32) for a, b in zip(h, act)]
    pn = [_rmsnorm_rows(a, norms_ref[1:2, :]).astype(bf16) for a in h]
    gate = [_sigmoid(jnp.dot(a, wpg_ref[...], preferred_element_type=f32)) for a in pn]
    for s, a, g in zip(slabs, h, gate):
        y_ref[s, :] = a + g * jnp.dot(p_ref[s, :].astype(bf16), wpp_ref[...], preferred_element_type=f32)


def _ffn(h, p, wl, l):
    n = h.shape[0]
    tile = min(FFN_TILE, n)
    assert n % tile == 0
    row_spec = lambda w: pl.BlockSpec((tile, w), lambda i: (i, 0))
    ws = [wl['w_gate'], wl['w_up'], wl['w_down'], wl['w_ple_gate'], wl['w_ple_proj']]
    return pl.pallas_call(
        _ffn_kernel, grid=(n // tile,),
        in_specs=[row_spec(D_MODEL), pl.BlockSpec((None, tile, PLE_DIM), lambda i, _l=l: (_l, i, 0))]
        + [_layer_spec(a, l, pipeline_mode=pl.Buffered(1)) for a in ws] + [_layer_spec(wl['fnorms'], l)],
        out_specs=row_spec(D_MODEL), out_shape=jax.ShapeDtypeStruct((n, D_MODEL), f32),
        compiler_params=pltpu.CompilerParams(dimension_semantics=("arbitrary",), vmem_limit_bytes=VMEM_LIMIT),
        name="ffn_ple",
    )(h, p, *ws, wl['fnorms'])


N_HEADS = C_W // HEAD


def _sample_in_kernel(n_tok, x_ref, cos_ref, sin_ref, win_ref, nmix_ref, qk_ref, e_ref, z_ref, qx_ref, k_ref,
                      piece_ref):
    n_seq = x_ref.shape[0] // n_tok
    xn = _rmsnorm_rows(x_ref[...], nmix_ref[...])
    z = jnp.dot(xn.astype(bf16), win_ref[...], preferred_element_type=f32)
    z_ref[...] = z
    e = e_ref[...]
    c1 = 4 * A_W + 2 * B_W
    cos, sin = cos_ref[...], sin_ref[...]
    qn = jnp.concatenate([_head_norm(z[:, c1 + j * 256:c1 + (j + 1) * 256], e,
                                     jnp.concatenate([qk_ref[0:1, :]] * 2, axis=1)) for j in range(2)], axis=1)
    q = _rope(qn, jnp.concatenate([cos] * 4, axis=1), jnp.concatenate([sin] * 4, axis=1)) * (HEAD ** -0.5)
    k_ref[...] = _rope(_head_norm(z[:, c1 + C_W:c1 + C_W + KV_W], e[0:KV_W, 0:KV_W], qk_ref[1:2, :]), cos, sin)
    lo_f = lax.broadcasted_iota(jnp.int32, (x_ref.shape[0], KV_W), 1) < HEAD
    for h in range(N_HEADS):
        pair, g = q[:, (h // 2) * KV_W:(h // 2 + 1) * KV_W], h // 4
        own = jnp.where(lo_f if h % 2 == 0 else jnp.logical_not(lo_f), pair, 0.0)
        piece_ref[...] = own if h % 2 == g else pltpu.roll(own, HEAD, 1)
        for t in range(n_tok):
            qx_ref[pl.ds(h * n_tok + t, n_seq, stride=N_HEADS * n_tok), :] = piece_ref[pl.ds(t, n_seq, stride=n_tok), :]


def _sample_in(x, cos, sin, wl, l, e256, n_tok):
    n = x.shape[0]
    args = (x, cos, sin, wl['w_in'], wl['nmix'], wl['qk'], e256)
    specs = [_const_spec(a.shape) for a in args]
    specs[3:6] = [_layer_spec(a, l) for a in args[3:6]]
    return pl.pallas_call(
        functools.partial(_sample_in_kernel, n_tok), grid=(1,), in_specs=specs,
        out_specs=(_const_spec((n, IN_COLS)), _const_spec((n * N_HEADS, KV_W)), _const_spec((n, KV_W))),
        out_shape=(jax.ShapeDtypeStruct((n, IN_COLS), f32), jax.ShapeDtypeStruct((n * N_HEADS, KV_W), f32),
                   jax.ShapeDtypeStruct((n, KV_W), f32)),
        scratch_shapes=[pltpu.VMEM((n, KV_W), f32)],
        compiler_params=pltpu.CompilerParams(dimension_semantics=("arbitrary",), vmem_limit_bytes=VMEM_LIMIT),
        name="sample_in",
    )(*args)


def _sample_hgrn_kernel(n_tok, q_ref, f_ref, v_ref, s_ref, lbp_ref, o_ref, so_ref, st_scr, qt_scr, ft_scr, kt_scr,
                        vt_scr, ot_scr):
    n_seq = s_ref.shape[0]
    st_scr[...] = s_ref[...].T
    for t in range(n_tok):
        zf = f_ref[pl.ds(t, n_seq, stride=n_tok), :]
        logf, ls = _log_forget(zf, lbp_ref[0:1, :], lbp_ref[1:2, :])
        ft_scr[t] = jnp.exp(logf).T
        kt_scr[t] = (lbp_ref[2:3, :] * jnp.exp(ls - zf)).T
        qt_scr[t] = q_ref[pl.ds(t, n_seq, stride=n_tok), :].T
        vt_scr[t] = v_ref[pl.ds(t, n_seq, stride=n_tok), :].T
        ot_scr[t] = jnp.zeros((2 * HEAD, n_seq), f32)

    def body(hk, carry):
        r0 = pl.multiple_of(hk * HEAD, HEAD)
        v0 = pl.multiple_of((hk // HEAD) * HEAD, HEAD)
        blk = st_scr[pl.ds(r0, HEAD), :]
        for t in range(n_tok):
            blk = blk * ft_scr[t, pl.ds(hk, 1), :] + kt_scr[t, pl.ds(hk, 1), :] * vt_scr[t, pl.ds(v0, HEAD), :]
            ot_scr[t, pl.ds(v0, HEAD), :] = ot_scr[t, pl.ds(v0, HEAD), :] + qt_scr[t, pl.ds(hk, 1), :] * blk
        st_scr[pl.ds(r0, HEAD), :] = blk
        return carry

    lax.fori_loop(0, 2 * HEAD, body, 0, unroll=2)
    so_ref[...] = st_scr[...].T
    for t in range(n_tok):
        o_ref[pl.ds(t, n_seq, stride=n_tok), :] = ot_scr[t].T


def _sample_hgrn(z, state, lbp, l, n_tok):
    n = z.shape[0]
    n_seq = state.shape[1]
    blk = 2 * HEAD * HEAD
    col = lambda j0: pl.BlockSpec((n, 2 * HEAD), lambda i, _j=j0: (0, _j + i))
    return pl.pallas_call(
        functools.partial(_sample_hgrn_kernel, n_tok), grid=(2,),
        in_specs=[col(0), col(2), col(4), pl.BlockSpec((None, n_seq, blk), lambda i, _l=l: (_l, 0, i)),
                  pl.BlockSpec((None, 3, 2 * HEAD), lambda i, _l=l: (_l, 0, i))],
        out_specs=(pl.BlockSpec((n, 2 * HEAD), lambda i: (0, i)), pl.BlockSpec((n_seq, blk), lambda i: (0, i))),
        out_shape=(jax.ShapeDtypeStruct((n, A_W), f32), jax.ShapeDtypeStruct(state.shape[1:], f32)),
        scratch_shapes=[pltpu.VMEM((blk, n_seq), f32)] + [pltpu.VMEM((n_tok, 2 * HEAD, n_seq), f32)] * 5,
        compiler_params=pltpu.CompilerParams(dimension_semantics=("arbitrary",), vmem_limit_bytes=VMEM_LIMIT),
        name="sample_hgrn",
    )(z, z, z, state, lbp)


def _sample_conv_kernel(n_tok, u0_ref, u1_ref, g0_ref, g1_ref, buf_ref, cw_ref, cvec_ref, e_ref, ob0_ref, ob1_ref,
                        new_ref):
    n_seq = buf_ref.shape[0]
    n_hist = CONV_WIDTH - 1
    us = []
    for t in range(n_tok):
        rows = pl.ds(t, n_seq, stride=n_tok)
        u = jnp.concatenate([u0_ref[rows, :], u1_ref[rows, :]], axis=1)
        g = jnp.concatenate([g0_ref[rows, :], g1_ref[rows, :]], axis=1)
        us.append(u * _sigmoid(g))
    buf = buf_ref[...]
    for t in range(n_tok):
        taps = cw_ref[0:n_hist - t, :]
        if t:
            taps = jnp.concatenate([jnp.zeros((t, B_W), f32), taps], axis=0)
        y = jnp.sum(buf * taps[None], axis=1) + cvec_ref[0:1, :]
        for s in range(t + 1):
            j = n_hist + s - t
            y = y + us[s] * cw_ref[j:j + 1, :]
        ob = _group_ln_silu(y, e_ref[...], cvec_ref[1:2, :], cvec_ref[2:3, :])
        ob0_ref[pl.ds(t, n_seq, stride=n_tok), :] = ob[:, 0:KV_W]
        ob1_ref[pl.ds(t, n_seq, stride=n_tok), :] = ob[:, KV_W:B_W]
    new_ref[:, 0:n_hist - n_tok, :] = buf_ref[:, n_tok:n_hist, :]
    for s in range(n_tok):
        new_ref[:, n_hist - n_tok + s, :] = us[s]


def _sample_conv(z, buf, wl, l, e256, n_tok):
    n = z.shape[0]
    col = lambda j: pl.BlockSpec((n, KV_W), lambda i, _j=j: (0, _j))
    half = jax.ShapeDtypeStruct((n, KV_W), f32)
    return pl.pallas_call(
        functools.partial(_sample_conv_kernel, n_tok), grid=(1,),
        in_specs=[col(8), col(9), col(10), col(11), _layer_spec(buf, l), _layer_spec(wl['cw'], l),
                  _layer_spec(wl['cvec'], l), _const_spec(e256.shape)],
        out_specs=(_const_spec((n, KV_W)), _const_spec((n, KV_W)), _const_spec(buf.shape[1:])),
        out_shape=(half, half, jax.ShapeDtypeStruct(buf.shape[1:], f32)),
        compiler_params=pltpu.CompilerParams(dimension_semantics=("arbitrary",), vmem_limit_bytes=VMEM_LIMIT),
        name="sample_conv",
    )(z, z, z, z, buf, wl['cw'], wl['cvec'], e256)


SEQ_BLOCK = 16


def _sample_attn_kernel(n_tok, qx_ref, k_ref, v_ref, kc_ref, vc_ref, sink_ref, bias_ref, ox_ref, kn_ref, vn_ref):
    q_rows = N_HEADS * n_tok
    sink, bias = sink_ref[...], bias_ref[...]
    seqs = range(SEQ_BLOCK)
    k_all = [jnp.concatenate([kc_ref[b], k_ref[b * n_tok:(b + 1) * n_tok, :]], axis=0) for b in seqs]
    v_all = [jnp.concatenate([vc_ref[b], v_ref[b * n_tok:(b + 1) * n_tok, :]], axis=0) for b in seqs]
    s = [lax.dot_general(qx_ref[b * q_rows:(b + 1) * q_rows, :].astype(bf16), k_all[b].astype(bf16),
                         (((1,), (1,)), ((), ())), preferred_element_type=f32) + bias for b in seqs]
    mx = [jnp.maximum(jnp.max(s[b], axis=-1, keepdims=True), sink) for b in seqs]
    pr = [jnp.exp(s[b] - mx[b]) for b in seqs]
    rden = [1.0 / (jnp.sum(pr[b], axis=-1, keepdims=True) + jnp.exp(sink - mx[b])) for b in seqs]
    for b in seqs:
        ox_ref[b * q_rows:(b + 1) * q_rows, :] = jnp.dot(pr[b].astype(bf16), v_all[b].astype(bf16),
                                                         preferred_element_type=f32) * rden[b]
    for b in seqs:
        kn_ref[b] = k_all[b][n_tok:, :]
        vn_ref[b] = v_all[b][n_tok:, :]


def _sample_attn(qx, k, z, kc, vc, sink_col, bias, l, n_tok):
    n_seq = kc.shape[1]
    rows = SEQ_BLOCK * n_tok
    q_rows = SEQ_BLOCK * N_HEADS * n_tok
    v_col = (4 * A_W + 2 * B_W + C_W + KV_W) // KV_W
    cache_in = pl.BlockSpec((None, SEQ_BLOCK) + kc.shape[2:], lambda i, _l=l: (_l, i, 0, 0))
    cache_out = pl.BlockSpec((SEQ_BLOCK,) + kc.shape[2:], lambda i: (i, 0, 0))
    return pl.pallas_call(
        functools.partial(_sample_attn_kernel, n_tok), grid=(n_seq // SEQ_BLOCK,),
        in_specs=[pl.BlockSpec((q_rows, KV_W), lambda i: (i, 0)), pl.BlockSpec((rows, KV_W), lambda i: (i, 0)),
                  pl.BlockSpec((rows, KV_W), lambda i: (i, v_col)), cache_in, cache_in,
                  _layer_spec(sink_col, l), _const_spec(bias.shape)],
        out_specs=(pl.BlockSpec((q_rows, KV_W), lambda i: (i, 0)), cache_out, cache_out),
        out_shape=(jax.ShapeDtypeStruct(qx.shape, f32), jax.ShapeDtypeStruct(kc.shape[1:], f32),
                   jax.ShapeDtypeStruct(vc.shape[1:], f32)),
        compiler_params=pltpu.CompilerParams(dimension_semantics=("arbitrary",), vmem_limit_bytes=VMEM_LIMIT),
        name="sample_attn",
    )(qx, k, z, kc, vc, sink_col, bias)


def _sample_out_kernel(n_tok, x_ref, oa_ref, g_ref, ob0_ref, ob1_ref, ox_ref, wout_ref, aon_ref, e_ref, h_ref,
                       oc_ref):
    n_seq = x_ref.shape[0] // n_tok
    gate = g_ref[...]
    oa = _head_norm(oa_ref[...], e_ref[...], aon_ref[...]) * (gate * _sigmoid(gate))
    lo_f = lax.broadcasted_iota(jnp.int32, (n_seq, KV_W), 1) < HEAD
    for p in range(N_HEADS // 2):
        g = p // 2
        for t in range(n_tok):
            even = ox_ref[pl.ds((2 * p) * n_tok + t, n_seq, stride=N_HEADS * n_tok), :]
            odd = ox_ref[pl.ds((2 * p + 1) * n_tok + t, n_seq, stride=N_HEADS * n_tok), :]
            even = even if g == 0 else pltpu.roll(even, HEAD, 1)
            odd = odd if g == 1 else pltpu.roll(odd, HEAD, 1)
            oc_ref[p, pl.ds(t, n_seq, stride=n_tok), :] = jnp.where(lo_f, even, odd)
    mix = jnp.concatenate([oa, ob0_ref[...], ob1_ref[...]] + [oc_ref[p] for p in range(N_HEADS // 2)],
                          axis=1).astype(bf16)
    h_ref[...] = x_ref[...] + jnp.dot(mix, wout_ref[...], preferred_element_type=f32)


def _sample_out(x, oa, z, ob0, ob1, ox, wl, l, e256, n_tok):
    n = x.shape[0]
    gate_spec = pl.BlockSpec((n, A_W), lambda i: (0, 3))
    args = (x, oa, z, ob0, ob1, ox, wl['w_out'], wl['aon'], e256)
    specs = [_const_spec(a.shape) for a in args]
    specs[2] = gate_spec
    specs[6:8] = [_layer_spec(a, l) for a in args[6:8]]
    return pl.pallas_call(
        functools.partial(_sample_out_kernel, n_tok), grid=(1,), in_specs=specs,
        out_specs=_const_spec((n, D_MODEL)), out_shape=jax.ShapeDtypeStruct((n, D_MODEL), f32),
        scratch_shapes=[pltpu.VMEM((N_HEADS // 2, n, KV_W), f32)],
        compiler_params=pltpu.CompilerParams(dimension_semantics=("arbitrary",), vmem_limit_bytes=VMEM_LIMIT),
        name="sample_out",
    )(*args)


def _rope_cos_sin(pos):
    half = HEAD // 2
    inv = ROPE_THETA ** (-jnp.arange(half, dtype=f32) / half)
    ang = pos[:, None] * inv[None, :]
    cos, sin = jnp.cos(ang), jnp.sin(ang)
    return jnp.tile(cos, (1, 4)), jnp.concatenate([-sin, sin, -sin, sin], axis=1)


def _rope_prompt_tables(t_len):
    half = HEAD // 2
    inv = ROPE_THETA ** (-jnp.arange(half, dtype=f32) / half)
    sign = jnp.concatenate([-jnp.ones((half,), f32), jnp.ones((half,), f32)] * 2)
    ang_r = jnp.arange(TILE, dtype=f32)[:, None] * inv[None, :]
    ang_t = (jnp.arange(t_len // TILE, dtype=f32) * TILE)[:, None] * inv[None, :]
    cos_r, sin_r = jnp.tile(jnp.cos(ang_r), (1, 4)), jnp.tile(jnp.sin(ang_r), (1, 4))
    rows = jnp.stack([cos_r, sin_r, cos_r * sign, sin_r * sign])
    tiles = jnp.stack([jnp.tile(jnp.cos(ang_t), (1, 4)), jnp.tile(jnp.sin(ang_t), (1, 4))])
    return rows, tiles


def _stacked_weights(lbs, n_tok, w_in, a_onorm, conv_w, conv_b, conv_ln_g, conv_ln_b, q_norm, k_norm, sinks, w_out,
                     norm_mix, norm_ffn, w_gate, w_up, w_down, ple_norm, w_ple_gate, w_ple_proj):
    return dict(
        w_in=w_in.astype(bf16), w_out=w_out.astype(bf16), w_gate=w_gate.astype(bf16), w_up=w_up.astype(bf16),
        w_down=w_down.astype(bf16), w_ple_gate=w_ple_gate.astype(bf16), w_ple_proj=w_ple_proj.astype(bf16),
        nmix=norm_mix[:, None, :],
        lbp=jnp.stack([jnp.log(lbs), jnp.log1p(-lbs), 1.0 - lbs], axis=1),
        aon=jnp.tile(a_onorm, (1, 4))[:, None, :],
        cw=conv_w,
        cvec=jnp.stack([conv_b, conv_ln_g, conv_ln_b], axis=1),
        qk=jnp.stack([jnp.tile(q_norm, (1, 2)), jnp.tile(k_norm, (1, 2))], axis=1),
        sinks=sinks,
        sink_col=jnp.repeat(sinks, n_tok, axis=1)[:, :, None],
        fnorms=jnp.stack([norm_ffn, ple_norm], axis=1),
    )


def kernel(x_prompt, x_sample, state_hgrn, state_conv, cache_swa_k, cache_swa_v, p_prompt, p_sample, a_lower, w_in, a_onorm, conv_w, conv_b, conv_ln_g, conv_ln_b, q_norm, k_norm, sinks, w_out, norm_mix, norm_ffn, w_gate, w_up, w_down, ple_norm, w_ple_gate, w_ple_proj):
    depth = w_in.shape[0]
    t_len = x_prompt.shape[1]
    n_seq, n_tok = x_sample.shape[0], x_sample.shape[1]
    past_len = 16384
    assert x_prompt.shape[0] == 1 and t_len % TILE == 0

    lbs = jnp.cumsum(jax.nn.softmax(a_lower.astype(f32), axis=0), axis=0)
    lbs = lbs - lbs[0:1]
    consts_np = _np_constants()
    e256 = jnp.asarray(consts_np[0], bf16)
    consts = (e256, jnp.asarray(consts_np[1], bf16), jnp.asarray(consts_np[2], f32), jnp.asarray(consts_np[3], bf16),
              jnp.asarray(consts_np[4], f32))

    rope_p = _rope_prompt_tables(t_len)
    cos_s, sin_s = _rope_cos_sin(past_len + jnp.arange(n_tok, dtype=f32))
    cos_s, sin_s = jnp.tile(cos_s, (n_seq, 1)), jnp.tile(sin_s, (n_seq, 1))
    w_buf = cache_swa_k.shape[2]
    q_tok = (np.arange(N_HEADS * n_tok) % n_tok)[:, None]
    rel = q_tok + w_buf - np.arange(w_buf + n_tok)[None, :]
    bias_s = jnp.asarray(np.where((rel >= 0) & (rel <= WINDOW), 0.0, -np.inf), f32)

    wl = _stacked_weights(lbs, n_tok, w_in, a_onorm, conv_w, conv_b, conv_ln_g, conv_ln_b, q_norm, k_norm, sinks,
                          w_out, norm_mix, norm_ffn, w_gate, w_up, w_down, ple_norm, w_ple_gate, w_ple_proj)
    p_p = p_prompt.reshape(depth, t_len, PLE_DIM)
    p_s = p_sample.reshape(depth, n_seq * n_tok, PLE_DIM)
    s_hgrn = state_hgrn.reshape(depth, n_seq, -1)
    kc = cache_swa_k.reshape(depth, n_seq, w_buf, KV_W)
    vc = cache_swa_v.reshape(depth, n_seq, w_buf, KV_W)

    hp = x_prompt[0]
    hs = x_sample.reshape(n_seq * n_tok, D_MODEL)
    outs = [[] for _ in range(8)]
    for l in range(depth):
        h_mid, st, conv_new, k_new, v_new = _prompt_mixer(hp, rope_p, wl, l, consts)
        hp = _ffn(h_mid, p_p, wl, l)
        s_new = jnp.stack([st[h * HEAD:(h + 1) * HEAD, h * HEAD:(h + 1) * HEAD].T for h in range(4)])
        outs[0].append(s_new[None])
        outs[1].append(conv_new[None])
        outs[2].append(k_new.reshape(1, WINDOW, 2, HEAD))
        outs[3].append(v_new.reshape(1, WINDOW, 2, HEAD))
        z, qx, k = _sample_in(hs, cos_s, sin_s, wl, l, e256, n_tok)
        oa, s_s = _sample_hgrn(z, s_hgrn, wl['lbp'], l, n_tok)
        ob0, ob1, conv_s = _sample_conv(z, state_conv, wl, l, e256, n_tok)
        ox, k_s, v_s = _sample_attn(qx, k, z, kc, vc, wl['sink_col'], bias_s, l, n_tok)
        h_mid_s = _sample_out(hs, oa, z, ob0, ob1, ox, wl, l, e256, n_tok)
        hs = _ffn(h_mid_s, p_s, wl, l)
        outs[4].append(s_s.reshape(n_seq, 4, HEAD, HEAD))
        outs[5].append(conv_s)
        outs[6].append(k_s.reshape(n_seq, w_buf, 2, HEAD))
        outs[7].append(v_s.reshape(n_seq, w_buf, 2, HEAD))
    return (hp[None], hs.reshape(n_seq, n_tok, D_MODEL)) + tuple(jnp.stack(o) for o in outs)
```

```python
import functools

import numpy as np
import jax
import jax.numpy as jnp
from jax import lax
from jax.experimental import pallas as pl
from jax.experimental.pallas import tpu as pltpu

f32 = jnp.float32
bf16 = jnp.bfloat16

D_MODEL = 1024
HEAD = 64
A_W = 256
B_W = 256
C_W = 512
KV_W = 128
IN_COLS = 4 * A_W + 2 * B_W + C_W + 2 * KV_W
D_FF = 2816
PLE_DIM = 256
WINDOW = 128
CONV_WIDTH = 31
ROPE_THETA = 10000.0
EPS = 1e-6

TILE = 512
CUM_ROWS = 256
CHUNK = 64
LEVELS = (32, 16, 8, 4, 2, 1)
SAFE_EXPONENT = 80.0
HIST = 32
FFN_TILE = 512
FFN_SLABS = 4
LOG2E = 1.4426950408889634
VMEM_LIMIT = 52 * 1024 * 1024
LAYER_VMEM_LIMIT = 58 * 1024 * 1024


def _split2(x):
    hi = x.astype(bf16)
    lo = (x - hi.astype(f32)).astype(bf16)
    return hi, lo


def _split3(x):
    hi = x.astype(bf16)
    r = x - hi.astype(f32)
    mid = r.astype(bf16)
    lo = (r - mid.astype(f32)).astype(bf16)
    return hi, mid, lo


def _seg_sum(x, e):
    hi, lo = _split2(x)
    return (jnp.dot(hi, e, preferred_element_type=f32) + jnp.dot(lo, e, preferred_element_type=f32))


def _sigmoid(x):
    return 1.0 / (1.0 + jnp.exp(-x))


def _rmsnorm_rows(x, g):
    return x * lax.rsqrt(jnp.mean(x * x, axis=-1, keepdims=True) + EPS) * g


def _seg_stat(x, e):
    return jnp.dot(x.astype(bf16), e, preferred_element_type=f32)


def _head_norm(x, e, g):
    ms = _seg_stat(x * x, e) * (1.0 / HEAD)
    return x * lax.rsqrt(ms + EPS) * g


def _rope(x, cos, sin_signed):
    w = x.shape[-1]
    lane = lax.broadcasted_iota(jnp.int32, x.shape, 1)
    first = (lane % HEAD) < (HEAD // 2)
    swapped = jnp.where(first, pltpu.roll(x, w - HEAD // 2, 1), pltpu.roll(x, HEAD // 2, 1))
    return x * cos + swapped * sin_signed


def _log_forget(zf, log_lb, log_1m_lb):
    ls = jnp.minimum(zf, 0.0) - jnp.log(1.0 + jnp.exp(-jnp.abs(zf)))
    b = log_1m_lb + ls
    return jnp.maximum(log_lb, b) + jnp.log(1.0 + jnp.exp(-jnp.abs(log_lb - b))), ls


def _group_ln_silu(y, e, g, b):
    mu = _seg_sum(y, e) * (1.0 / HEAD)
    yc = y - mu
    var = _seg_stat(yc * yc, e) * (1.0 / HEAD)
    yn = yc * lax.rsqrt(var + EPS) * g + b
    return yn * _sigmoid(yn)


def _np_constants():
    lane = np.arange(256)
    e256 = (lane[:, None] // HEAD == lane[None, :] // HEAD).astype(np.float32)
    t = np.arange(CUM_ROWS)
    ltri = ((t[:, None] // CHUNK == t[None, :] // CHUNK) & (t[None, :] <= t[:, None])).astype(np.float32)
    tt = np.arange(CHUNK)[:, None]
    ss = (np.arange(256) % CHUNK)[None, :]
    lvl = []
    for m in LEVELS:
        lvl.append(((tt // (2 * m) == ss // (2 * m)) & (tt % (2 * m) >= m) & (ss % (2 * m) < m)).astype(np.float32))
    lvl.append((ss <= tt).astype(np.float32))
    lvl = np.stack(lvl)
    hm = np.stack([np.broadcast_to((lane // HEAD == h)[None, :], (CHUNK, 256)) for h in range(4)]).astype(np.float32)
    keys = np.arange(2 * WINDOW)[:, None]
    qry = np.arange(4 * WINDOW)[None, :] % WINDOW
    band = (keys >= qry) & (keys <= qry + WINDOW)
    bias = np.where(np.stack([band & (keys >= WINDOW), band]), 0.0, -np.inf).astype(np.float32)
    return e256, ltri, lvl, hm, bias


def _level_ref(g, m):
    n_rows = g.shape[0]
    if 2 * m >= 8:
        g3 = g.reshape(n_rows // (2 * m), 2 * m, g.shape[1])
        return jnp.broadcast_to(g3[:, m - 1:m, :], g3.shape).reshape(g.shape)
    g3 = g.reshape(n_rows // 8, 8, g.shape[1])
    sub = lax.broadcasted_iota(jnp.int32, g3.shape, 1)
    out = None
    for blk in range(8 // (2 * m)):
        r = blk * 2 * m + m - 1
        piece = jnp.broadcast_to(g3[:, r:r + 1, :], g3.shape)
        out = piece if out is None else jnp.where(sub >= blk * 2 * m, piece, out)
    return out.reshape(g.shape)


def _hgrn_tile(q, zf, v, lbp_ref, st_ref, st0_ref, o_ref, e, ltri_ref, lvl_ref, hm_ref):
    log_lb, log_1m_lb, one_m_lb = lbp_ref[0:1, :], lbp_ref[1:2, :], lbp_ref[2:3, :]
    logf, ls = _log_forget(zf, log_lb, log_1m_lb)
    kin = one_m_lb * jnp.exp(ls - zf)
    parts = jnp.concatenate(_split3(logf), axis=1)
    g3 = jnp.concatenate([jnp.dot(ltri_ref[...], parts[r:r + CUM_ROWS], preferred_element_type=f32)
                          for r in range(0, TILE, CUM_ROWS)], axis=0)
    g = g3[:, 0:A_W] + g3[:, A_W:2 * A_W] + g3[:, 2 * A_W:3 * A_W]
    v_b = v.astype(bf16)
    vt_b = v.T.astype(bf16)
    e_mid = g - _level_ref(g, CHUNK // 2)
    safe = jnp.max(jnp.abs(e_mid)) < SAFE_EXPONENT

    def expand(rows_b):
        return jnp.concatenate([rows_b * hm_ref[h] for h in range(4)], axis=0)

    def finish_chunk(c, att, q_in, k_hat):
        r0, r1 = c * CHUNK, (c + 1) * CHUNK
        st = st_ref[...]
        o = jnp.dot(att.astype(bf16), expand(v_b[r0:r1]), preferred_element_type=f32)
        o = o + lax.dot_general(q_in, st.astype(bf16) * e, (((1,), (1,)), ((), ())), preferred_element_type=f32)
        zero = jnp.zeros((CHUNK, A_W), bf16)
        pair = (c // 2) * 2 * CHUNK
        rhs = jnp.concatenate([k_hat, zero] if c % 2 == 0 else [zero, k_hat], axis=0)
        upd = jnp.dot(vt_b[:, pair:pair + 2 * CHUNK], rhs, preferred_element_type=f32)
        st_ref[...] = st * jnp.exp(g[r1 - 1:r1, :]) + upd
        o_ref[r0:r1, :] = o

    st0_ref[...] = st_ref[...]
    qw = q * jnp.exp(e_mid)
    kw = kin * jnp.exp(-e_mid)
    q_t, k_t = qw.astype(bf16), kw.astype(bf16)
    chunks = [(c * CHUNK, (c + 1) * CHUNK) for c in range(TILE // CHUNK)]
    atts = [lax.dot_general(q_t[r0:r1], expand(k_t[r0:r1]), (((1,), (1,)), ((), ())), preferred_element_type=f32)
            for r0, r1 in chunks]
    atts = [jnp.where(lvl_ref[len(LEVELS)] > 0.5, a, 0.0) for a in atts]
    for c, (r0, r1) in enumerate(chunks):
        g_mid, g_end = g[r0 + CHUNK // 2 - 1:r0 + CHUNK // 2, :], g[r1 - 1:r1, :]
        finish_chunk(c, atts[c], (qw[r0:r1] * jnp.exp(g_mid)).astype(bf16),
                     (kw[r0:r1] * jnp.exp(g_end - g_mid)).astype(bf16))

    def redo_if_unsafe():
        @pl.when(jnp.logical_not(safe))
        def _():
            st_ref[...] = st0_ref[...]
            q_levels, k_levels = [], []
            for m in LEVELS:
                w = jnp.exp(-jnp.abs(g - _level_ref(g, m)))
                q_levels.append((q * w).astype(bf16))
                k_levels.append((kin * w).astype(bf16))
            q_inter = (q * jnp.exp(g)).astype(bf16)
            for c in range(TILE // CHUNK):
                r0, r1 = c * CHUNK, (c + 1) * CHUNK
                att = jnp.zeros((CHUNK, 4 * CHUNK), f32)
                for li in range(len(LEVELS)):
                    a = lax.dot_general(q_levels[li][r0:r1], expand(k_levels[li][r0:r1]),
                                        (((1,), (1,)), ((), ())), preferred_element_type=f32)
                    att = att + a * lvl_ref[li]
                k_hat = (kin[r0:r1] * jnp.exp(g[r1 - 1:r1, :] - g[r0:r1])).astype(bf16)
                finish_chunk(c, att, q_inter[r0:r1], k_hat)
            o_ref[...] = o_ref[...] + _seg_sum(q * kin, e) * v

    return redo_if_unsafe


def _conv_tile(u, hist_ref, cw_ref, cvec_ref, e):
    n = u.shape[0]
    hist_ref[HIST:HIST + n, :] = u
    lo = HIST - (CONV_WIDTH - 1)
    y = cvec_ref[0:1, :]
    for r in range(8):
        rows = n if r == 0 else n + 8
        part = None
        for o in range(lo, lo + CONV_WIDTH):
            if o % 8 == r:
                term = hist_ref[o - r:o - r + rows, :] * cw_ref[o - lo:o - lo + 1, :]
                part = term if part is None else part + term
        y = y + part[r:r + n]
    return _group_ln_silu(y, e, cvec_ref[1:2, :], cvec_ref[2:3, :])


def _attn_tile(q_t, k_b, v_t, sink_ref, biases):
    zero = jnp.zeros((HEAD, WINDOW), f32)
    inst = [(b, g) for b in range(TILE // WINDOW) for g in range(2)]
    sinks = [jnp.concatenate([jnp.full((1, WINDOW), sink_ref[h] * LOG2E, f32) for h in range(4 * g, 4 * g + 4)],
                             axis=1) for g in range(2)]

    def rhs(b, g):
        cols = []
        for h in range(4 * g, 4 * g + 4):
            q_h = q_t[h * HEAD:(h + 1) * HEAD, b * WINDOW:(b + 1) * WINDOW]
            cols.append(jnp.concatenate([q_h, zero] if g == 0 else [zero, q_h], axis=0))
        return jnp.concatenate(cols, axis=1).astype(bf16)

    s = [jnp.dot(k_b[b * WINDOW:(b + 2) * WINDOW], rhs(b, g), preferred_element_type=f32) + biases[b]
         for b, g in inst]
    mx = [jnp.maximum(jnp.max(s[n], axis=0, keepdims=True), sinks[g]) for n, (b, g) in enumerate(inst)]
    pr = [jnp.exp2(s[n] - mx[n]) for n in range(len(inst))]
    rden = [1.0 / (jnp.sum(pr[n], axis=0, keepdims=True) + jnp.exp2(sinks[g] - mx[n]))
            for n, (b, g) in enumerate(inst)]
    o_t = [jnp.dot(v_t[g * HEAD:(g + 1) * HEAD, b * WINDOW:(b + 2) * WINDOW], pr[n].astype(bf16),
                   preferred_element_type=f32) * rden[n] for n, (b, g) in enumerate(inst)]
    rows = []
    for b in range(TILE // WINDOW):
        pairs = []
        for g in range(2):
            o = o_t[2 * b + g]
            for j in range(2):
                pair_t = jnp.concatenate([o[:, (2 * j) * WINDOW:(2 * j + 1) * WINDOW],
                                          o[:, (2 * j + 1) * WINDOW:(2 * j + 2) * WINDOW]], axis=0)
                pairs.append(pair_t.T)
        rows.append(jnp.concatenate(pairs, axis=1))
    return jnp.concatenate(rows, axis=0)


def _mixer_kernel(layer, x_ref, rope_row_ref, rope_tile_ref, win_ref, wout_ref, nmix_ref, lbp_ref, aon_ref, cw_ref,
                  cvec_ref, qk_ref, sinks_ref, e_ref, ltri_ref, lvl_ref, hm_ref, bias_ref,
                  h_ref, st_ref, conv_ref, kn_ref, vn_ref, z_ref, hist_ref, o_ref, st0_ref):
    i = pl.program_id(0)
    sink_ref = sinks_ref.at[layer]

    @pl.when(i == 0)
    def _():
        st_ref[...] = jnp.zeros_like(st_ref)
        hist_ref[0:HIST, :] = jnp.zeros((HIST, B_W), f32)
        kn_ref[...] = jnp.zeros_like(kn_ref)
        vn_ref[...] = jnp.zeros_like(vn_ref)

    x = x_ref[...]
    xn = _rmsnorm_rows(x, nmix_ref[...])
    z_ref[...] = jnp.dot(xn.astype(bf16), win_ref[...], preferred_element_type=f32)
    e = e_ref[...]

    redo_hgrn_if_unsafe = _hgrn_tile(z_ref[:, 0:A_W], z_ref[:, A_W:2 * A_W], z_ref[:, 2 * A_W:3 * A_W], lbp_ref,
                                     st_ref, st0_ref, o_ref, e, ltri_ref, lvl_ref, hm_ref)

    c0 = 4 * A_W
    u = z_ref[:, c0:c0 + B_W] * _sigmoid(z_ref[:, c0 + B_W:c0 + 2 * B_W])
    ob = _conv_tile(u, hist_ref, cw_ref, cvec_ref, e)
    conv_ref[...] = hist_ref[TILE + HIST - (CONV_WIDTH - 1):TILE + HIST, :]
    hist_ref[0:HIST, :] = hist_ref[TILE:TILE + HIST, :]

    c1 = c0 + 2 * B_W
    cos_a, sin_a = rope_tile_ref[0, pl.ds(i, 1), :], rope_tile_ref[1, pl.ds(i, 1), :]
    cos = cos_a * rope_row_ref[0] - sin_a * rope_row_ref[1]
    sin = sin_a * rope_row_ref[2] + cos_a * rope_row_ref[3]
    qn = jnp.concatenate([_head_norm(z_ref[:, c1 + j * 256:c1 + (j + 1) * 256], e,
                                     jnp.concatenate([qk_ref[0:1, :]] * 2, axis=1)) for j in range(2)], axis=1)
    q = _rope(qn, jnp.concatenate([cos] * 4, axis=1), jnp.concatenate([sin] * 4, axis=1)) * (LOG2E * HEAD ** -0.5)
    kk = _rope(_head_norm(z_ref[:, c1 + C_W:c1 + C_W + KV_W], e[0:KV_W, 0:KV_W], qk_ref[1:2, :]), cos, sin)
    vv = z_ref[:, c1 + C_W + KV_W:c1 + C_W + 2 * KV_W]
    kcat = jnp.concatenate([kn_ref[...], kk], axis=0)
    vcat = jnp.concatenate([vn_ref[...], vv], axis=0)
    q_t = q.T
    k_b = kcat.astype(bf16)
    v_t = vcat.T.astype(bf16)
    biases = [bias_ref[jnp.where(i > 0, 1, 0)]] + [bias_ref[1]] * (TILE // WINDOW - 1)
    oc = _attn_tile(q_t, k_b, v_t, sink_ref, biases)
    kn_ref[...] = kk[TILE - WINDOW:TILE]
    vn_ref[...] = vv[TILE - WINDOW:TILE]

    redo_hgrn_if_unsafe()
    gate = z_ref[:, 3 * A_W:4 * A_W]
    oa = _head_norm(o_ref[...], e, aon_ref[...]) * (gate * _sigmoid(gate))
    mix = jnp.concatenate([oa, ob, oc], axis=1).astype(bf16)
    h_ref[...] = x + jnp.dot(mix, wout_ref[...], preferred_element_type=f32)


def _const_spec(shape):
    nd = len(shape)
    return pl.BlockSpec(shape, lambda i, _n=nd: (0,) * _n)


def _layer_spec(arr, l, **kw):
    shape = arr.shape[1:]
    return pl.BlockSpec((None,) + shape, lambda i, _l=l, _n=len(shape): (_l,) + (0,) * _n, **kw)


def _prompt_mixer(x, rope, wl, l, consts):
    t_len = x.shape[0]
    row_spec = lambda w: pl.BlockSpec((TILE, w), lambda i: (i, 0))
    smalls = [wl['nmix'], wl['lbp'], wl['aon'], wl['cw'], wl['cvec'], wl['qk']]
    in_specs = ([row_spec(D_MODEL), _const_spec(rope[0].shape), _const_spec(rope[1].shape),
                 _layer_spec(wl['w_in'], l), _layer_spec(wl['w_out'], l)] + [_layer_spec(a, l) for a in smalls]
                + [pl.BlockSpec(memory_space=pltpu.SMEM)]
                + [_const_spec(a.shape) for a in consts])
    out_shape = (jax.ShapeDtypeStruct((t_len, D_MODEL), f32),
                 jax.ShapeDtypeStruct((A_W, A_W), f32),
                 jax.ShapeDtypeStruct((CONV_WIDTH - 1, B_W), f32),
                 jax.ShapeDtypeStruct((WINDOW, KV_W), f32),
                 jax.ShapeDtypeStruct((WINDOW, KV_W), f32))
    out_specs = (row_spec(D_MODEL), _const_spec((A_W, A_W)), _const_spec((CONV_WIDTH - 1, B_W)),
                 _const_spec((WINDOW, KV_W)), _const_spec((WINDOW, KV_W)))
    return pl.pallas_call(
        functools.partial(_mixer_kernel, l), grid=(t_len // TILE,), in_specs=in_specs, out_specs=out_specs,
        out_shape=out_shape,
        scratch_shapes=[pltpu.VMEM((TILE, IN_COLS), f32), pltpu.VMEM((TILE + HIST, B_W), f32),
                        pltpu.VMEM((TILE, A_W), f32), pltpu.VMEM((A_W, A_W), f32)],
        compiler_params=pltpu.CompilerParams(dimension_semantics=("arbitrary",), vmem_limit_bytes=VMEM_LIMIT),
        name="prompt_mixer",
    )(x, rope[0], rope[1], wl['w_in'], wl['w_out'], *smalls, wl['sinks'], *consts)


def _ffn_kernel(h_ref, p_ref, wg_ref, wu_ref, wd_ref, wpg_ref, wpp_ref, norms_ref, y_ref):
    n = h_ref.shape[0]
    slabs = [slice(r, r + n // FFN_SLABS) for r in range(0, n, n // FFN_SLABS)]
    h = [h_ref[s, :] for s in slabs]
    hn = [_rmsnorm_rows(a, norms_ref[0:1, :]).astype(bf16) for a in h]
    gt = [jnp.dot(a, wg_ref[...], preferred_element_type=f32) for a in hn]
    up = [jnp.dot(a, wu_ref[...], preferred_element_type=f32) for a in hn]
    act = [(g * _sigmoid(g) * u).astype(bf16) for g, u in zip(gt, up)]
    h = [a + jnp.dot(b, wd_ref[...], preferred_element_type=f32) for a, b in zip(h, act)]
    pn = [_rmsnorm_rows(a, norms_ref[1:2, :]).astype(bf16) for a in h]
    gate = [_sigmoid(jnp.dot(a, wpg_ref[...], preferred_element_type=f32)) for a in pn]
    for s, a, g in zip(slabs, h, gate):
        y_ref[s, :] = a + g * jnp.dot(p_ref[s, :].astype(bf16), wpp_ref[...], preferred_element_type=f32)


def _ffn(h, p, wl, l):
    n = h.shape[0]
    tile = min(FFN_TILE, n)
    assert n % tile == 0
    row_spec = lambda w: pl.BlockSpec((tile, w), lambda i: (i, 0))
    ws = [wl['w_gate'], wl['w_up'], wl['w_down'], wl['w_ple_gate'], wl['w_ple_proj']]
    return pl.pallas_call(
        _ffn_kernel, grid=(n // tile,),
        in_specs=[row_spec(D_MODEL), pl.BlockSpec((None, tile, PLE_DIM), lambda i, _l=l: (_l, i, 0))]
        + [_layer_spec(a, l, pipeline_mode=pl.Buffered(1)) for a in ws] + [_layer_spec(wl['fnorms'], l)],
        out_specs=row_spec(D_MODEL), out_shape=jax.ShapeDtypeStruct((n, D_MODEL), f32),
        compiler_params=pltpu.CompilerParams(dimension_semantics=("arbitrary",), vmem_limit_bytes=VMEM_LIMIT),
        name="ffn_ple",
    )(h, p, *ws, wl['fnorms'])


N_MIXER_INPUTS = 17


def _layer_kernel(layer, *refs):
    mixer_in, (p_ref, wg_ref, wu_ref, wd_ref, wpg_ref, wpp_ref, norms_ref) = refs[:N_MIXER_INPUTS], refs[17:24]
    y_ref, st_ref, conv_ref, kn_ref, vn_ref = refs[24:29]
    z_ref, hist_ref, o_ref, st0_ref, mid_ref = refs[29:]
    _mixer_kernel(layer, *mixer_in, mid_ref, st_ref, conv_ref, kn_ref, vn_ref, z_ref, hist_ref, o_ref, st0_ref)
    _ffn_kernel(mid_ref, p_ref, wg_ref, wu_ref, wd_ref, wpg_ref, wpp_ref, norms_ref, y_ref)


def _prompt_layer(x, p, rope, wl, l, consts):
    t_len = x.shape[0]
    row_spec = lambda w: pl.BlockSpec((TILE, w), lambda i: (i, 0))
    once = dict(pipeline_mode=pl.Buffered(1))
    smalls = [wl['nmix'], wl['lbp'], wl['aon'], wl['cw'], wl['cvec'], wl['qk']]
    ffn_ws = [wl['w_gate'], wl['w_up'], wl['w_down'], wl['w_ple_gate'], wl['w_ple_proj']]
    in_specs = ([row_spec(D_MODEL), _const_spec(rope[0].shape), _const_spec(rope[1].shape),
                 _layer_spec(wl['w_in'], l, **once), _layer_spec(wl['w_out'], l, **once)]
                + [_layer_spec(a, l) for a in smalls] + [pl.BlockSpec(memory_space=pltpu.SMEM)]
                + [_const_spec(a.shape) for a in consts]
                + [pl.BlockSpec((None, TILE, PLE_DIM), lambda i, _l=l: (_l, i, 0))]
                + [_layer_spec(a, l, **once) for a in ffn_ws] + [_layer_spec(wl['fnorms'], l)])
    assert len(in_specs) == N_MIXER_INPUTS + 7
    out_shape = (jax.ShapeDtypeStruct((t_len, D_MODEL), f32),
                 jax.ShapeDtypeStruct((A_W, A_W), f32),
                 jax.ShapeDtypeStruct((CONV_WIDTH - 1, B_W), f32),
                 jax.ShapeDtypeStruct((WINDOW, KV_W), f32),
                 jax.ShapeDtypeStruct((WINDOW, KV_W), f32))
    out_specs = (row_spec(D_MODEL), _const_spec((A_W, A_W)), _const_spec((CONV_WIDTH - 1, B_W)),
                 _const_spec((WINDOW, KV_W)), _const_spec((WINDOW, KV_W)))
    return pl.pallas_call(
        functools.partial(_layer_kernel, l), grid=(t_len // TILE,), in_specs=in_specs, out_specs=out_specs,
        out_shape=out_shape,
        scratch_shapes=[pltpu.VMEM((TILE, IN_COLS), f32), pltpu.VMEM((TILE + HIST, B_W), f32),
                        pltpu.VMEM((TILE, A_W), f32), pltpu.VMEM((A_W, A_W), f32),
                        pltpu.VMEM((TILE, D_MODEL), f32)],
        compiler_params=pltpu.CompilerParams(dimension_semantics=("arbitrary",),
                                             vmem_limit_bytes=LAYER_VMEM_LIMIT),
        name="prompt_layer",
    )(x, rope[0], rope[1], wl['w_in'], wl['w_out'], *smalls, wl['sinks'], *consts, p, *ffn_ws, wl['fnorms'])


N_HEADS = C_W // HEAD


def _sample_in_kernel(n_tok, x_ref, cos_ref, sin_ref, win_ref, nmix_ref, qk_ref, e_ref, z_ref, qx_ref, k_ref,
                      piece_ref):
    n_seq = x_ref.shape[0] // n_tok
    xn = _rmsnorm_rows(x_ref[...], nmix_ref[...])
    z = jnp.dot(xn.astype(bf16), win_ref[...], preferred_element_type=f32)
    z_ref[...] = z
    e = e_ref[...]
    c1 = 4 * A_W + 2 * B_W
    cos, sin = cos_ref[...], sin_ref[...]
    qn = jnp.concatenate([_head_norm(z[:, c1 + j * 256:c1 + (j + 1) * 256], e,
                                     jnp.concatenate([qk_ref[0:1, :]] * 2, axis=1)) for j in range(2)], axis=1)
    q = _rope(qn, jnp.concatenate([cos] * 4, axis=1), jnp.concatenate([sin] * 4, axis=1)) * (HEAD ** -0.5)
    k_ref[...] = _rope(_head_norm(z[:, c1 + C_W:c1 + C_W + KV_W], e[0:KV_W, 0:KV_W], qk_ref[1:2, :]), cos, sin)
    lo_f = lax.broadcasted_iota(jnp.int32, (x_ref.shape[0], KV_W), 1) < HEAD
    for h in range(N_HEADS):
        pair, g = q[:, (h // 2) * KV_W:(h // 2 + 1) * KV_W], h // 4
        own = jnp.where(lo_f if h % 2 == 0 else jnp.logical_not(lo_f), pair, 0.0)
        piece_ref[...] = own if h % 2 == g else pltpu.roll(own, HEAD, 1)
        for t in range(n_tok):
            qx_ref[pl.ds(h * n_tok + t, n_seq, stride=N_HEADS * n_tok), :] = piece_ref[pl.ds(t, n_seq, stride=n_tok), :]


def _sample_in(x, cos, sin, wl, l, e256, n_tok):
    n = x.shape[0]
    args = (x, cos, sin, wl['w_in'], wl['nmix'], wl['qk'], e256)
    specs = [_const_spec(a.shape) for a in args]
    specs[3:6] = [_layer_spec(a, l) for a in args[3:6]]
    return pl.pallas_call(
        functools.partial(_sample_in_kernel, n_tok), grid=(1,), in_specs=specs,
        out_specs=(_const_spec((n, IN_COLS)), _const_spec((n * N_HEADS, KV_W)), _const_spec((n, KV_W))),
        out_shape=(jax.ShapeDtypeStruct((n, IN_COLS), f32), jax.ShapeDtypeStruct((n * N_HEADS, KV_W), f32),
                   jax.ShapeDtypeStruct((n, KV_W), f32)),
        scratch_shapes=[pltpu.VMEM((n, KV_W), f32)],
        compiler_params=pltpu.CompilerParams(dimension_semantics=("arbitrary",), vmem_limit_bytes=VMEM_LIMIT),
        name="sample_in",
    )(*args)


def _sample_hgrn_kernel(n_tok, q_ref, f_ref, v_ref, s_ref, lbp_ref, o_ref, so_ref, st_scr, qt_scr, ft_scr, kt_scr,
                        vt_scr, ot_scr):
    n_seq = s_ref.shape[0]
    st_scr[...] = s_ref[...].T
    for t in range(n_tok):
        zf = f_ref[pl.ds(t, n_seq, stride=n_tok), :]
        logf, ls = _log_forget(zf, lbp_ref[0:1, :], lbp_ref[1:2, :])
        ft_scr[t] = jnp.exp(logf).T
        kt_scr[t] = (lbp_ref[2:3, :] * jnp.exp(ls - zf)).T
        qt_scr[t] = q_ref[pl.ds(t, n_seq, stride=n_tok), :].T
        vt_scr[t] = v_ref[pl.ds(t, n_seq, stride=n_tok), :].T
        ot_scr[t] = jnp.zeros((2 * HEAD, n_seq), f32)

    def body(hk, carry):
        r0 = pl.multiple_of(hk * HEAD, HEAD)
        v0 = pl.multiple_of((hk // HEAD) * HEAD, HEAD)
        blk = st_scr[pl.ds(r0, HEAD), :]
        for t in range(n_tok):
            blk = blk * ft_scr[t, pl.ds(hk, 1), :] + kt_scr[t, pl.ds(hk, 1), :] * vt_scr[t, pl.ds(v0, HEAD), :]
            ot_scr[t, pl.ds(v0, HEAD), :] = ot_scr[t, pl.ds(v0, HEAD), :] + qt_scr[t, pl.ds(hk, 1), :] * blk
        st_scr[pl.ds(r0, HEAD), :] = blk
        return carry

    lax.fori_loop(0, 2 * HEAD, body, 0, unroll=2)
    so_ref[...] = st_scr[...].T
    for t in range(n_tok):
        o_ref[pl.ds(t, n_seq, stride=n_tok), :] = ot_scr[t].T


def _sample_hgrn(z, state, lbp, l, n_tok):
    n = z.shape[0]
    n_seq = state.shape[1]
    blk = 2 * HEAD * HEAD
    col = lambda j0: pl.BlockSpec((n, 2 * HEAD), lambda i, _j=j0: (0, _j + i))
    return pl.pallas_call(
        functools.partial(_sample_hgrn_kernel, n_tok), grid=(2,),
        in_specs=[col(0), col(2), col(4), pl.BlockSpec((None, n_seq, blk), lambda i, _l=l: (_l, 0, i)),
                  pl.BlockSpec((None, 3, 2 * HEAD), lambda i, _l=l: (_l, 0, i))],
        out_specs=(pl.BlockSpec((n, 2 * HEAD), lambda i: (0, i)), pl.BlockSpec((n_seq, blk), lambda i: (0, i))),
        out_shape=(jax.ShapeDtypeStruct((n, A_W), f32), jax.ShapeDtypeStruct(state.shape[1:], f32)),
        scratch_shapes=[pltpu.VMEM((blk, n_seq), f32)] + [pltpu.VMEM((n_tok, 2 * HEAD, n_seq), f32)] * 5,
        compiler_params=pltpu.CompilerParams(dimension_semantics=("arbitrary",), vmem_limit_bytes=VMEM_LIMIT),
        name="sample_hgrn",
    )(z, z, z, state, lbp)


def _sample_conv_kernel(n_tok, u0_ref, u1_ref, g0_ref, g1_ref, buf_ref, cw_ref, cvec_ref, e_ref, ob0_ref, ob1_ref,
                        new_ref):
    n_seq = buf_ref.shape[0]
    n_hist = CONV_WIDTH - 1
    us = []
    for t in range(n_tok):
        rows = pl.ds(t, n_seq, stride=n_tok)
        u = jnp.concatenate([u0_ref[rows, :], u1_ref[rows, :]], axis=1)
        g = jnp.concatenate([g0_ref[rows, :], g1_ref[rows, :]], axis=1)
        us.append(u * _sigmoid(g))
    buf = buf_ref[...]
    for t in range(n_tok):
        taps = cw_ref[0:n_hist - t, :]
        if t:
            taps = jnp.concatenate([jnp.zeros((t, B_W), f32), taps], axis=0)
        y = jnp.sum(buf * taps[None], axis=1) + cvec_ref[0:1, :]
        for s in range(t + 1):
            j = n_hist + s - t
            y = y + us[s] * cw_ref[j:j + 1, :]
        ob = _group_ln_silu(y, e_ref[...], cvec_ref[1:2, :], cvec_ref[2:3, :])
        ob0_ref[pl.ds(t, n_seq, stride=n_tok), :] = ob[:, 0:KV_W]
        ob1_ref[pl.ds(t, n_seq, stride=n_tok), :] = ob[:, KV_W:B_W]
    new_ref[:, 0:n_hist - n_tok, :] = buf_ref[:, n_tok:n_hist, :]
    for s in range(n_tok):
        new_ref[:, n_hist - n_tok + s, :] = us[s]


def _sample_conv(z, buf, wl, l, e256, n_tok):
    n = z.shape[0]
    col = lambda j: pl.BlockSpec((n, KV_W), lambda i, _j=j: (0, _j))
    half = jax.ShapeDtypeStruct((n, KV_W), f32)
    return pl.pallas_call(
        functools.partial(_sample_conv_kernel, n_tok), grid=(1,),
        in_specs=[col(8), col(9), col(10), col(11), _layer_spec(buf, l), _layer_spec(wl['cw'], l),
                  _layer_spec(wl['cvec'], l), _const_spec(e256.shape)],
        out_specs=(_const_spec((n, KV_W)), _const_spec((n, KV_W)), _const_spec(buf.shape[1:])),
        out_shape=(half, half, jax.ShapeDtypeStruct(buf.shape[1:], f32)),
        compiler_params=pltpu.CompilerParams(dimension_semantics=("arbitrary",), vmem_limit_bytes=VMEM_LIMIT),
        name="sample_conv",
    )(z, z, z, z, buf, wl['cw'], wl['cvec'], e256)


SEQ_BLOCK = 16


def _sample_attn_kernel(n_tok, qx_ref, k_ref, v_ref, kc_ref, vc_ref, sink_ref, bias_ref, ox_ref, kn_ref, vn_ref):
    q_rows = N_HEADS * n_tok
    sink, bias = sink_ref[...], bias_ref[...]
    seqs = range(SEQ_BLOCK)
    k_all = [jnp.concatenate([kc_ref[b], k_ref[b * n_tok:(b + 1) * n_tok, :]], axis=0) for b in seqs]
    v_all = [jnp.concatenate([vc_ref[b], v_ref[b * n_tok:(b + 1) * n_tok, :]], axis=0) for b in seqs]
    s = [lax.dot_general(qx_ref[b * q_rows:(b + 1) * q_rows, :].astype(bf16), k_all[b].astype(bf16),
                         (((1,), (1,)), ((), ())), preferred_element_type=f32) + bias for b in seqs]
    mx = [jnp.maximum(jnp.max(s[b], axis=-1, keepdims=True), sink) for b in seqs]
    pr = [jnp.exp(s[b] - mx[b]) for b in seqs]
    rden = [1.0 / (jnp.sum(pr[b], axis=-1, keepdims=True) + jnp.exp(sink - mx[b])) for b in seqs]
    for b in seqs:
        ox_ref[b * q_rows:(b + 1) * q_rows, :] = jnp.dot(pr[b].astype(bf16), v_all[b].astype(bf16),
                                                         preferred_element_type=f32) * rden[b]
    for b in seqs:
        kn_ref[b] = k_all[b][n_tok:, :]
        vn_ref[b] = v_all[b][n_tok:, :]


def _sample_attn(qx, k, z, kc, vc, sink_col, bias, l, n_tok):
    n_seq = kc.shape[1]
    assert n_seq % SEQ_BLOCK == 0
    rows = SEQ_BLOCK * n_tok
    q_rows = SEQ_BLOCK * N_HEADS * n_tok
    v_col = (4 * A_W + 2 * B_W + C_W + KV_W) // KV_W
    cache_in = pl.BlockSpec((None, SEQ_BLOCK) + kc.shape[2:], lambda i, _l=l: (_l, i, 0, 0))
    cache_out = pl.BlockSpec((SEQ_BLOCK,) + kc.shape[2:], lambda i: (i, 0, 0))
    return pl.pallas_call(
        functools.partial(_sample_attn_kernel, n_tok), grid=(n_seq // SEQ_BLOCK,),
        in_specs=[pl.BlockSpec((q_rows, KV_W), lambda i: (i, 0)), pl.BlockSpec((rows, KV_W), lambda i: (i, 0)),
                  pl.BlockSpec((rows, KV_W), lambda i: (i, v_col)), cache_in, cache_in,
                  _layer_spec(sink_col, l), _const_spec(bias.shape)],
        out_specs=(pl.BlockSpec((q_rows, KV_W), lambda i: (i, 0)), cache_out, cache_out),
        out_shape=(jax.ShapeDtypeStruct(qx.shape, f32), jax.ShapeDtypeStruct(kc.shape[1:], f32),
                   jax.ShapeDtypeStruct(vc.shape[1:], f32)),
        compiler_params=pltpu.CompilerParams(dimension_semantics=("arbitrary",), vmem_limit_bytes=VMEM_LIMIT),
        name="sample_attn",
    )(qx, k, z, kc, vc, sink_col, bias)


def _sample_out_kernel(n_tok, x_ref, oa_ref, g_ref, ob0_ref, ob1_ref, ox_ref, wout_ref, aon_ref, e_ref, h_ref,
                       oc_ref):
    n_seq = x_ref.shape[0] // n_tok
    gate = g_ref[...]
    oa = _head_norm(oa_ref[...], e_ref[...], aon_ref[...]) * (gate * _sigmoid(gate))
    lo_f = lax.broadcasted_iota(jnp.int32, (n_seq, KV_W), 1) < HEAD
    for p in range(N_HEADS // 2):
        g = p // 2
        for t in range(n_tok):
            even = ox_ref[pl.ds((2 * p) * n_tok + t, n_seq, stride=N_HEADS * n_tok), :]
            odd = ox_ref[pl.ds((2 * p + 1) * n_tok + t, n_seq, stride=N_HEADS * n_tok), :]
            even = even if g == 0 else pltpu.roll(even, HEAD, 1)
            odd = odd if g == 1 else pltpu.roll(odd, HEAD, 1)
            oc_ref[p, pl.ds(t, n_seq, stride=n_tok), :] = jnp.where(lo_f, even, odd)
    mix = jnp.concatenate([oa, ob0_ref[...], ob1_ref[...]] + [oc_ref[p] for p in range(N_HEADS // 2)],
                          axis=1).astype(bf16)
    h_ref[...] = x_ref[...] + jnp.dot(mix, wout_ref[...], preferred_element_type=f32)


def _sample_out(x, oa, z, ob0, ob1, ox, wl, l, e256, n_tok):
    n = x.shape[0]
    gate_spec = pl.BlockSpec((n, A_W), lambda i: (0, 3))
    args = (x, oa, z, ob0, ob1, ox, wl['w_out'], wl['aon'], e256)
    specs = [_const_spec(a.shape) for a in args]
    specs[2] = gate_spec
    specs[6:8] = [_layer_spec(a, l) for a in args[6:8]]
    return pl.pallas_call(
        functools.partial(_sample_out_kernel, n_tok), grid=(1,), in_specs=specs,
        out_specs=_const_spec((n, D_MODEL)), out_shape=jax.ShapeDtypeStruct((n, D_MODEL), f32),
        scratch_shapes=[pltpu.VMEM((N_HEADS // 2, n, KV_W), f32)],
        compiler_params=pltpu.CompilerParams(dimension_semantics=("arbitrary",), vmem_limit_bytes=VMEM_LIMIT),
        name="sample_out",
    )(*args)


def _rope_cos_sin(pos):
    half = HEAD // 2
    inv = ROPE_THETA ** (-jnp.arange(half, dtype=f32) / half)
    ang = pos[:, None] * inv[None, :]
    cos, sin = jnp.cos(ang), jnp.sin(ang)
    return jnp.tile(cos, (1, 4)), jnp.concatenate([-sin, sin, -sin, sin], axis=1)


def _rope_prompt_tables(t_len):
    half = HEAD // 2
    inv = ROPE_THETA ** (-jnp.arange(half, dtype=f32) / half)
    sign = jnp.concatenate([-jnp.ones((half,), f32), jnp.ones((half,), f32)] * 2)
    ang_r = jnp.arange(TILE, dtype=f32)[:, None] * inv[None, :]
    ang_t = (jnp.arange(t_len // TILE, dtype=f32) * TILE)[:, None] * inv[None, :]
    cos_r, sin_r = jnp.tile(jnp.cos(ang_r), (1, 4)), jnp.tile(jnp.sin(ang_r), (1, 4))
    rows = jnp.stack([cos_r, sin_r, cos_r * sign, sin_r * sign])
    tiles = jnp.stack([jnp.tile(jnp.cos(ang_t), (1, 4)), jnp.tile(jnp.sin(ang_t), (1, 4))])
    return rows, tiles


def _stacked_weights(lbs, n_tok, w_in, a_onorm, conv_w, conv_b, conv_ln_g, conv_ln_b, q_norm, k_norm, sinks, w_out,
                     norm_mix, norm_ffn, w_gate, w_up, w_down, ple_norm, w_ple_gate, w_ple_proj):
    return dict(
        w_in=w_in.astype(bf16), w_out=w_out.astype(bf16), w_gate=w_gate.astype(bf16), w_up=w_up.astype(bf16),
        w_down=w_down.astype(bf16), w_ple_gate=w_ple_gate.astype(bf16), w_ple_proj=w_ple_proj.astype(bf16),
        nmix=norm_mix[:, None, :],
        lbp=jnp.stack([jnp.log(lbs), jnp.log1p(-lbs), 1.0 - lbs], axis=1),
        aon=jnp.tile(a_onorm, (1, 4))[:, None, :],
        cw=conv_w,
        cvec=jnp.stack([conv_b, conv_ln_g, conv_ln_b], axis=1),
        qk=jnp.stack([jnp.tile(q_norm, (1, 2)), jnp.tile(k_norm, (1, 2))], axis=1),
        sinks=sinks,
        sink_col=jnp.repeat(sinks, n_tok, axis=1)[:, :, None],
        fnorms=jnp.stack([norm_ffn, ple_norm], axis=1),
    )


def kernel(x_prompt, x_sample, state_hgrn, state_conv, cache_swa_k, cache_swa_v, p_prompt, p_sample, a_lower, w_in, a_onorm, conv_w, conv_b, conv_ln_g, conv_ln_b, q_norm, k_norm, sinks, w_out, norm_mix, norm_ffn, w_gate, w_up, w_down, ple_norm, w_ple_gate, w_ple_proj):
    depth = w_in.shape[0]
    t_len = x_prompt.shape[1]
    n_seq, n_tok = x_sample.shape[0], x_sample.shape[1]
    past_len = 16384
    assert x_prompt.shape[0] == 1 and t_len % TILE == 0

    lbs = jnp.cumsum(jax.nn.softmax(a_lower.astype(f32), axis=0), axis=0)
    lbs = lbs - lbs[0:1]
    consts_np = _np_constants()
    e256 = jnp.asarray(consts_np[0], bf16)
    consts = (e256, jnp.asarray(consts_np[1], bf16), jnp.asarray(consts_np[2], f32), jnp.asarray(consts_np[3], bf16),
              jnp.asarray(consts_np[4], f32))

    rope_p = _rope_prompt_tables(t_len)
    cos_s, sin_s = _rope_cos_sin(past_len + jnp.arange(n_tok, dtype=f32))
    cos_s, sin_s = jnp.tile(cos_s, (n_seq, 1)), jnp.tile(sin_s, (n_seq, 1))
    w_buf = cache_swa_k.shape[2]
    q_tok = (np.arange(N_HEADS * n_tok) % n_tok)[:, None]
    rel = q_tok + w_buf - np.arange(w_buf + n_tok)[None, :]
    bias_s = jnp.asarray(np.where((rel >= 0) & (rel <= WINDOW), 0.0, -np.inf), f32)

    wl = _stacked_weights(lbs, n_tok, w_in, a_onorm, conv_w, conv_b, conv_ln_g, conv_ln_b, q_norm, k_norm, sinks,
                          w_out, norm_mix, norm_ffn, w_gate, w_up, w_down, ple_norm, w_ple_gate, w_ple_proj)
    p_p = p_prompt.reshape(depth, t_len, PLE_DIM)
    p_s = p_sample.reshape(depth, n_seq * n_tok, PLE_DIM)
    s_hgrn = state_hgrn.reshape(depth, n_seq, -1)
    kc = cache_swa_k.reshape(depth, n_seq, w_buf, KV_W)
    vc = cache_swa_v.reshape(depth, n_seq, w_buf, KV_W)

    hp = x_prompt[0]
    hs = x_sample.reshape(n_seq * n_tok, D_MODEL)
    outs = [[] for _ in range(8)]
    for l in range(depth):
        hp, st, conv_new, k_new, v_new = _prompt_layer(hp, p_p, rope_p, wl, l, consts)
        s_new = jnp.stack([st[h * HEAD:(h + 1) * HEAD, h * HEAD:(h + 1) * HEAD].T for h in range(4)])
        outs[0].append(s_new[None])
        outs[1].append(conv_new[None])
        outs[2].append(k_new.reshape(1, WINDOW, 2, HEAD))
        outs[3].append(v_new.reshape(1, WINDOW, 2, HEAD))
        z, qx, k = _sample_in(hs, cos_s, sin_s, wl, l, e256, n_tok)
        oa, s_s = _sample_hgrn(z, s_hgrn, wl['lbp'], l, n_tok)
        ob0, ob1, conv_s = _sample_conv(z, state_conv, wl, l, e256, n_tok)
        ox, k_s, v_s = _sample_attn(qx, k, z, kc, vc, wl['sink_col'], bias_s, l, n_tok)
        h_mid_s = _sample_out(hs, oa, z, ob0, ob1, ox, wl, l, e256, n_tok)
        hs = _ffn(h_mid_s, p_s, wl, l)
        outs[4].append(s_s.reshape(n_seq, 4, HEAD, HEAD))
        outs[5].append(conv_s)
        outs[6].append(k_s.reshape(n_seq, w_buf, 2, HEAD))
        outs[7].append(v_s.reshape(n_seq, w_buf, 2, HEAD))
    return (hp[None], hs.reshape(n_seq, n_tok, D_MODEL)) + tuple(jnp.stack(o) for o in outs)
```

```python
import functools

import numpy as np
import jax
import jax.numpy as jnp
from jax import lax
from jax.experimental import pallas as pl
from jax.experimental.pallas import tpu as pltpu

f32 = jnp.float32
bf16 = jnp.bfloat16

D_MODEL = 1024
HEAD = 64
A_W = 256
B_W = 256
C_W = 512
KV_W = 128
IN_COLS = 4 * A_W + 2 * B_W + C_W + 2 * KV_W
D_FF = 2816
PLE_DIM = 256
WINDOW = 128
CONV_WIDTH = 31
ROPE_THETA = 10000.0
EPS = 1e-6

TILE = 512
CUM_ROWS = 256
CHUNK = 64
LEVELS = (32, 16, 8, 4, 2, 1)
SAFE_EXPONENT = 80.0
HIST = 32
FFN_TILE = 512
FFN_SLABS = 4
LOG2E = 1.4426950408889634
VMEM_LIMIT = 52 * 1024 * 1024
LAYER_VMEM_LIMIT = 58 * 1024 * 1024


def _split2(x):
    hi = x.astype(bf16)
    lo = (x - hi.astype(f32)).astype(bf16)
    return hi, lo


def _split3(x):
    hi = x.astype(bf16)
    r = x - hi.astype(f32)
    mid = r.astype(bf16)
    lo = (r - mid.astype(f32)).astype(bf16)
    return hi, mid, lo


def _seg_sum(x, e):
    hi, lo = _split2(x)
    return (jnp.dot(hi, e, preferred_element_type=f32) + jnp.dot(lo, e, preferred_element_type=f32))


def _sigmoid(x):
    return 1.0 / (1.0 + jnp.exp(-x))


def _rmsnorm_rows(x, g):
    return x * lax.rsqrt(jnp.mean(x * x, axis=-1, keepdims=True) + EPS) * g


def _seg_stat(x, e):
    return jnp.dot(x.astype(bf16), e, preferred_element_type=f32)


def _head_norm(x, e, g):
    ms = _seg_stat(x * x, e) * (1.0 / HEAD)
    return x * lax.rsqrt(ms + EPS) * g


def _rope(x, cos, sin_signed):
    w = x.shape[-1]
    lane = lax.broadcasted_iota(jnp.int32, x.shape, 1)
    first = (lane % HEAD) < (HEAD // 2)
    swapped = jnp.where(first, pltpu.roll(x, w - HEAD // 2, 1), pltpu.roll(x, HEAD // 2, 1))
    return x * cos + swapped * sin_signed


def _log_forget(zf, log_lb, log_1m_lb):
    ls = jnp.minimum(zf, 0.0) - jnp.log(1.0 + jnp.exp(-jnp.abs(zf)))
    b = log_1m_lb + ls
    return jnp.maximum(log_lb, b) + jnp.log(1.0 + jnp.exp(-jnp.abs(log_lb - b))), ls


def _group_ln_silu(y, e, g, b):
    mu = _seg_sum(y, e) * (1.0 / HEAD)
    yc = y - mu
    var = _seg_stat(yc * yc, e) * (1.0 / HEAD)
    yn = yc * lax.rsqrt(var + EPS) * g + b
    return yn * _sigmoid(yn)


def _np_constants():
    lane = np.arange(256)
    e256 = (lane[:, None] // HEAD == lane[None, :] // HEAD).astype(np.float32)
    t = np.arange(CUM_ROWS)
    ltri = ((t[:, None] // CHUNK == t[None, :] // CHUNK) & (t[None, :] <= t[:, None])).astype(np.float32)
    tt = np.arange(CHUNK)[:, None]
    ss = (np.arange(256) % CHUNK)[None, :]
    lvl = []
    for m in LEVELS:
        lvl.append(((tt // (2 * m) == ss // (2 * m)) & (tt % (2 * m) >= m) & (ss % (2 * m) < m)).astype(np.float32))
    lvl.append((ss <= tt).astype(np.float32))
    lvl = np.stack(lvl)
    hm = np.stack([np.broadcast_to((lane // HEAD == h)[None, :], (CHUNK, 256)) for h in range(4)]).astype(np.float32)
    keys = np.arange(2 * WINDOW)[:, None]
    qry = np.arange(4 * WINDOW)[None, :] % WINDOW
    band = (keys >= qry) & (keys <= qry + WINDOW)
    bias = np.where(np.stack([band & (keys >= WINDOW), band]), 0.0, -np.inf).astype(np.float32)
    return e256, ltri, lvl, hm, bias


def _level_ref(g, m):
    n_rows = g.shape[0]
    if 2 * m >= 8:
        g3 = g.reshape(n_rows // (2 * m), 2 * m, g.shape[1])
        return jnp.broadcast_to(g3[:, m - 1:m, :], g3.shape).reshape(g.shape)
    g3 = g.reshape(n_rows // 8, 8, g.shape[1])
    sub = lax.broadcasted_iota(jnp.int32, g3.shape, 1)
    out = None
    for blk in range(8 // (2 * m)):
        r = blk * 2 * m + m - 1
        piece = jnp.broadcast_to(g3[:, r:r + 1, :], g3.shape)
        out = piece if out is None else jnp.where(sub >= blk * 2 * m, piece, out)
    return out.reshape(g.shape)


def _hgrn_tile(q, zf, v, lbp_ref, st_ref, st0_ref, o_ref, e, ltri_ref, lvl_ref, hm_ref):
    log_lb, log_1m_lb, one_m_lb = lbp_ref[0:1, :], lbp_ref[1:2, :], lbp_ref[2:3, :]
    logf, ls = _log_forget(zf, log_lb, log_1m_lb)
    kin = one_m_lb * jnp.exp(ls - zf)
    parts = jnp.concatenate(_split3(logf), axis=1)
    g3 = jnp.concatenate([jnp.dot(ltri_ref[...], parts[r:r + CUM_ROWS], preferred_element_type=f32)
                          for r in range(0, TILE, CUM_ROWS)], axis=0)
    g = g3[:, 0:A_W] + g3[:, A_W:2 * A_W] + g3[:, 2 * A_W:3 * A_W]
    v_b = v.astype(bf16)
    vt_b = v.T.astype(bf16)
    e_mid = g - _level_ref(g, CHUNK // 2)
    safe = jnp.max(jnp.abs(e_mid)) < SAFE_EXPONENT

    def expand(rows_b):
        return jnp.concatenate([rows_b * hm_ref[h] for h in range(4)], axis=0)

    def finish_chunk(c, att, q_in, k_hat):
        r0, r1 = c * CHUNK, (c + 1) * CHUNK
        st = st_ref[...]
        o = jnp.dot(att.astype(bf16), expand(v_b[r0:r1]), preferred_element_type=f32)
        o = o + lax.dot_general(q_in, st.astype(bf16) * e, (((1,), (1,)), ((), ())), preferred_element_type=f32)
        zero = jnp.zeros((CHUNK, A_W), bf16)
        pair = (c // 2) * 2 * CHUNK
        rhs = jnp.concatenate([k_hat, zero] if c % 2 == 0 else [zero, k_hat], axis=0)
        upd = jnp.dot(vt_b[:, pair:pair + 2 * CHUNK], rhs, preferred_element_type=f32)
        st_ref[...] = st * jnp.exp(g[r1 - 1:r1, :]) + upd
        o_ref[r0:r1, :] = o

    st0_ref[...] = st_ref[...]
    qw = q * jnp.exp(e_mid)
    kw = kin * jnp.exp(-e_mid)
    q_t, k_t = qw.astype(bf16), kw.astype(bf16)
    chunks = [(c * CHUNK, (c + 1) * CHUNK) for c in range(TILE // CHUNK)]
    atts = [lax.dot_general(q_t[r0:r1], expand(k_t[r0:r1]), (((1,), (1,)), ((), ())), preferred_element_type=f32)
            for r0, r1 in chunks]
    atts = [jnp.where(lvl_ref[len(LEVELS)] > 0.5, a, 0.0) for a in atts]
    for c, (r0, r1) in enumerate(chunks):
        g_mid, g_end = g[r0 + CHUNK // 2 - 1:r0 + CHUNK // 2, :], g[r1 - 1:r1, :]
        finish_chunk(c, atts[c], (qw[r0:r1] * jnp.exp(g_mid)).astype(bf16),
                     (kw[r0:r1] * jnp.exp(g_end - g_mid)).astype(bf16))

    def redo_if_unsafe():
        @pl.when(jnp.logical_not(safe))
        def _():
            st_ref[...] = st0_ref[...]
            q_levels, k_levels = [], []
            for m in LEVELS:
                w = jnp.exp(-jnp.abs(g - _level_ref(g, m)))
                q_levels.append((q * w).astype(bf16))
                k_levels.append((kin * w).astype(bf16))
            q_inter = (q * jnp.exp(g)).astype(bf16)
            for c in range(TILE // CHUNK):
                r0, r1 = c * CHUNK, (c + 1) * CHUNK
                att = jnp.zeros((CHUNK, 4 * CHUNK), f32)
                for li in range(len(LEVELS)):
                    a = lax.dot_general(q_levels[li][r0:r1], expand(k_levels[li][r0:r1]),
                                        (((1,), (1,)), ((), ())), preferred_element_type=f32)
                    att = att + a * lvl_ref[li]
                k_hat = (kin[r0:r1] * jnp.exp(g[r1 - 1:r1, :] - g[r0:r1])).astype(bf16)
                finish_chunk(c, att, q_inter[r0:r1], k_hat)
            o_ref[...] = o_ref[...] + _seg_sum(q * kin, e) * v

    return redo_if_unsafe


def _conv_tile(u, hist_ref, cw_ref, cvec_ref, e):
    n = u.shape[0]
    hist_ref[HIST:HIST + n, :] = u
    lo = HIST - (CONV_WIDTH - 1)
    y = cvec_ref[0:1, :]
    for r in range(8):
        rows = n if r == 0 else n + 8
        part = None
        for o in range(lo, lo + CONV_WIDTH):
            if o % 8 == r:
                term = hist_ref[o - r:o - r + rows, :] * cw_ref[o - lo:o - lo + 1, :]
                part = term if part is None else part + term
        y = y + part[r:r + n]
    return _group_ln_silu(y, e, cvec_ref[1:2, :], cvec_ref[2:3, :])


def _attn_tile(q_t, k_b, v_t, sink_ref, biases):
    zero = jnp.zeros((HEAD, WINDOW), f32)
    inst = [(b, g) for b in range(TILE // WINDOW) for g in range(2)]
    sinks = [jnp.concatenate([jnp.full((1, WINDOW), sink_ref[h] * LOG2E, f32) for h in range(4 * g, 4 * g + 4)],
                             axis=1) for g in range(2)]

    def rhs(b, g):
        cols = []
        for h in range(4 * g, 4 * g + 4):
            q_h = q_t[h * HEAD:(h + 1) * HEAD, b * WINDOW:(b + 1) * WINDOW]
            cols.append(jnp.concatenate([q_h, zero] if g == 0 else [zero, q_h], axis=0))
        return jnp.concatenate(cols, axis=1).astype(bf16)

    s = [jnp.dot(k_b[b * WINDOW:(b + 2) * WINDOW], rhs(b, g), preferred_element_type=f32) + biases[b]
         for b, g in inst]
    mx = [jnp.maximum(jnp.max(s[n], axis=0, keepdims=True), sinks[g]) for n, (b, g) in enumerate(inst)]
    pr = [jnp.exp2(s[n] - mx[n]) for n in range(len(inst))]
    rden = [1.0 / (jnp.sum(pr[n], axis=0, keepdims=True) + jnp.exp2(sinks[g] - mx[n]))
            for n, (b, g) in enumerate(inst)]
    o_t = [jnp.dot(v_t[g * HEAD:(g + 1) * HEAD, b * WINDOW:(b + 2) * WINDOW], pr[n].astype(bf16),
                   preferred_element_type=f32) * rden[n] for n, (b, g) in enumerate(inst)]
    rows = []
    for b in range(TILE // WINDOW):
        pairs = []
        for g in range(2):
            o = o_t[2 * b + g]
            for j in range(2):
                pair_t = jnp.concatenate([o[:, (2 * j) * WINDOW:(2 * j + 1) * WINDOW],
                                          o[:, (2 * j + 1) * WINDOW:(2 * j + 2) * WINDOW]], axis=0)
                pairs.append(pair_t.T)
        rows.append(jnp.concatenate(pairs, axis=1))
    return jnp.concatenate(rows, axis=0)


def _mixer_kernel(layer, x_ref, rope_row_ref, rope_tile_ref, win_ref, wout_ref, nmix_ref, lbp_ref, aon_ref, cw_ref,
                  cvec_ref, qk_ref, sinks_ref, e_ref, ltri_ref, lvl_ref, hm_ref, bias_ref,
                  h_ref, st_ref, conv_ref, kn_ref, vn_ref, z_ref, hist_ref, o_ref, st0_ref):
    i = pl.program_id(0)
    sink_ref = sinks_ref.at[layer]

    @pl.when(i == 0)
    def _():
        st_ref[...] = jnp.zeros_like(st_ref)
        hist_ref[0:HIST, :] = jnp.zeros((HIST, B_W), f32)
        kn_ref[...] = jnp.zeros_like(kn_ref)
        vn_ref[...] = jnp.zeros_like(vn_ref)

    x = x_ref[...]
    xn = _rmsnorm_rows(x, nmix_ref[...])
    z_ref[...] = jnp.dot(xn.astype(bf16), win_ref[...], preferred_element_type=f32)
    e = e_ref[...]

    redo_hgrn_if_unsafe = _hgrn_tile(z_ref[:, 0:A_W], z_ref[:, A_W:2 * A_W], z_ref[:, 2 * A_W:3 * A_W], lbp_ref,
                                     st_ref, st0_ref, o_ref, e, ltri_ref, lvl_ref, hm_ref)

    c0 = 4 * A_W
    u = z_ref[:, c0:c0 + B_W] * _sigmoid(z_ref[:, c0 + B_W:c0 + 2 * B_W])
    ob = _conv_tile(u, hist_ref, cw_ref, cvec_ref, e)
    conv_ref[...] = hist_ref[TILE + HIST - (CONV_WIDTH - 1):TILE + HIST, :]
    hist_ref[0:HIST, :] = hist_ref[TILE:TILE + HIST, :]

    c1 = c0 + 2 * B_W
    cos_a, sin_a = rope_tile_ref[0, pl.ds(i, 1), :], rope_tile_ref[1, pl.ds(i, 1), :]
    cos = cos_a * rope_row_ref[0] - sin_a * rope_row_ref[1]
    sin = sin_a * rope_row_ref[2] + cos_a * rope_row_ref[3]
    qn = jnp.concatenate([_head_norm(z_ref[:, c1 + j * 256:c1 + (j + 1) * 256], e,
                                     jnp.concatenate([qk_ref[0:1, :]] * 2, axis=1)) for j in range(2)], axis=1)
    q = _rope(qn, jnp.concatenate([cos] * 4, axis=1), jnp.concatenate([sin] * 4, axis=1)) * (LOG2E * HEAD ** -0.5)
    kk = _rope(_head_norm(z_ref[:, c1 + C_W:c1 + C_W + KV_W], e[0:KV_W, 0:KV_W], qk_ref[1:2, :]), cos, sin)
    vv = z_ref[:, c1 + C_W + KV_W:c1 + C_W + 2 * KV_W]
    kcat = jnp.concatenate([kn_ref[...], kk], axis=0)
    vcat = jnp.concatenate([vn_ref[...], vv], axis=0)
    q_t = q.T
    k_b = kcat.astype(bf16)
    v_t = vcat.T.astype(bf16)
    biases = [bias_ref[jnp.where(i > 0, 1, 0)]] + [bias_ref[1]] * (TILE // WINDOW - 1)
    oc = _attn_tile(q_t, k_b, v_t, sink_ref, biases)
    kn_ref[...] = kk[TILE - WINDOW:TILE]
    vn_ref[...] = vv[TILE - WINDOW:TILE]

    redo_hgrn_if_unsafe()
    gate = z_ref[:, 3 * A_W:4 * A_W]
    oa = _head_norm(o_ref[...], e, aon_ref[...]) * (gate * _sigmoid(gate))
    mix = jnp.concatenate([oa, ob, oc], axis=1).astype(bf16)
    h_ref[...] = x + jnp.dot(mix, wout_ref[...], preferred_element_type=f32)


def _const_spec(shape):
    nd = len(shape)
    return pl.BlockSpec(shape, lambda i, _n=nd: (0,) * _n)


def _layer_spec(arr, l, **kw):
    shape = arr.shape[1:]
    return pl.BlockSpec((None,) + shape, lambda i, _l=l, _n=len(shape): (_l,) + (0,) * _n, **kw)


def _ffn_kernel(h_ref, p_ref, wg_ref, wu_ref, wd_ref, wpg_ref, wpp_ref, norms_ref, y_ref):
    n = h_ref.shape[0]
    slabs = [slice(r, r + n // FFN_SLABS) for r in range(0, n, n // FFN_SLABS)]
    h = [h_ref[s, :] for s in slabs]
    hn = [_rmsnorm_rows(a, norms_ref[0:1, :]).astype(bf16) for a in h]
    gt = [jnp.dot(a, wg_ref[...], preferred_element_type=f32) for a in hn]
    up = [jnp.dot(a, wu_ref[...], preferred_element_type=f32) for a in hn]
    act = [(g * _sigmoid(g) * u).astype(bf16) for g, u in zip(gt, up)]
    h = [a + jnp.dot(b, wd_ref[...], preferred_element_type=f32) for a, b in zip(h, act)]
    pn = [_rmsnorm_rows(a, norms_ref[1:2, :]).astype(bf16) for a in h]
    gate = [_sigmoid(jnp.dot(a, wpg_ref[...], preferred_element_type=f32)) for a in pn]
    for s, a, g in zip(slabs, h, gate):
        y_ref[s, :] = a + g * jnp.dot(p_ref[s, :].astype(bf16), wpp_ref[...], preferred_element_type=f32)


def _ffn(h, p, wl, l):
    n = h.shape[0]
    tile = min(FFN_TILE, n)
    assert n % tile == 0
    row_spec = lambda w: pl.BlockSpec((tile, w), lambda i: (i, 0))
    ws = [wl['w_gate'], wl['w_up'], wl['w_down'], wl['w_ple_gate'], wl['w_ple_proj']]
    return pl.pallas_call(
        _ffn_kernel, grid=(n // tile,),
        in_specs=[row_spec(D_MODEL), pl.BlockSpec((None, tile, PLE_DIM), lambda i, _l=l: (_l, i, 0))]
        + [_layer_spec(a, l, pipeline_mode=pl.Buffered(1)) for a in ws] + [_layer_spec(wl['fnorms'], l)],
        out_specs=row_spec(D_MODEL), out_shape=jax.ShapeDtypeStruct((n, D_MODEL), f32),
        compiler_params=pltpu.CompilerParams(dimension_semantics=("arbitrary",), vmem_limit_bytes=VMEM_LIMIT),
        name="ffn_ple",
    )(h, p, *ws, wl['fnorms'])


N_MIXER_INPUTS = 17


def _layer_kernel(layer, *refs):
    mixer_in, (p_ref, wg_ref, wu_ref, wd_ref, wpg_ref, wpp_ref, norms_ref) = refs[:N_MIXER_INPUTS], refs[17:24]
    y_ref, st_ref, conv_ref, kn_ref, vn_ref = refs[24:29]
    z_ref, hist_ref, o_ref, st0_ref, mid_ref = refs[29:]
    _mixer_kernel(layer, *mixer_in, mid_ref, st_ref, conv_ref, kn_ref, vn_ref, z_ref, hist_ref, o_ref, st0_ref)
    _ffn_kernel(mid_ref, p_ref, wg_ref, wu_ref, wd_ref, wpg_ref, wpp_ref, norms_ref, y_ref)


def _prompt_layer(x, p, rope, wl, l, consts):
    t_len = x.shape[0]
    row_spec = lambda w: pl.BlockSpec((TILE, w), lambda i: (i, 0))
    once = dict(pipeline_mode=pl.Buffered(1))
    smalls = [wl['nmix'], wl['lbp'], wl['aon'], wl['cw'], wl['cvec'], wl['qk']]
    ffn_ws = [wl['w_gate'], wl['w_up'], wl['w_down'], wl['w_ple_gate'], wl['w_ple_proj']]
    in_specs = ([row_spec(D_MODEL), _const_spec(rope[0].shape), _const_spec(rope[1].shape),
                 _layer_spec(wl['w_in'], l, **once), _layer_spec(wl['w_out'], l, **once)]
                + [_layer_spec(a, l) for a in smalls] + [pl.BlockSpec(memory_space=pltpu.SMEM)]
                + [_const_spec(a.shape) for a in consts]
                + [pl.BlockSpec((None, TILE, PLE_DIM), lambda i, _l=l: (_l, i, 0))]
                + [_layer_spec(a, l, **once) for a in ffn_ws] + [_layer_spec(wl['fnorms'], l)])
    assert len(in_specs) == N_MIXER_INPUTS + 7
    out_shape = (jax.ShapeDtypeStruct((t_len, D_MODEL), f32),
                 jax.ShapeDtypeStruct((A_W, A_W), f32),
                 jax.ShapeDtypeStruct((CONV_WIDTH - 1, B_W), f32),
                 jax.ShapeDtypeStruct((WINDOW, KV_W), f32),
                 jax.ShapeDtypeStruct((WINDOW, KV_W), f32))
    out_specs = (row_spec(D_MODEL), _const_spec((A_W, A_W)), _const_spec((CONV_WIDTH - 1, B_W)),
                 _const_spec((WINDOW, KV_W)), _const_spec((WINDOW, KV_W)))
    return pl.pallas_call(
        functools.partial(_layer_kernel, l), grid=(t_len // TILE,), in_specs=in_specs, out_specs=out_specs,
        out_shape=out_shape,
        scratch_shapes=[pltpu.VMEM((TILE, IN_COLS), f32), pltpu.VMEM((TILE + HIST, B_W), f32),
                        pltpu.VMEM((TILE, A_W), f32), pltpu.VMEM((A_W, A_W), f32),
                        pltpu.VMEM((TILE, D_MODEL), f32)],
        compiler_params=pltpu.CompilerParams(dimension_semantics=("arbitrary",),
                                             vmem_limit_bytes=LAYER_VMEM_LIMIT),
        name="prompt_layer",
    )(x, rope[0], rope[1], wl['w_in'], wl['w_out'], *smalls, wl['sinks'], *consts, p, *ffn_ws, wl['fnorms'])


N_HEADS = C_W // HEAD


def _sample_in_kernel(n_tok, x_ref, cos_ref, sin_ref, win_ref, nmix_ref, qk_ref, e_ref, z_ref, qx_ref, k_ref,
                      piece_ref):
    n_seq = x_ref.shape[0] // n_tok
    xn = _rmsnorm_rows(x_ref[...], nmix_ref[...])
    z = jnp.dot(xn.astype(bf16), win_ref[...], preferred_element_type=f32)
    z_ref[...] = z
    e = e_ref[...]
    c1 = 4 * A_W + 2 * B_W
    cos, sin = cos_ref[...], sin_ref[...]
    qn = jnp.concatenate([_head_norm(z[:, c1 + j * 256:c1 + (j + 1) * 256], e,
                                     jnp.concatenate([qk_ref[0:1, :]] * 2, axis=1)) for j in range(2)], axis=1)
    q = _rope(qn, jnp.concatenate([cos] * 4, axis=1), jnp.concatenate([sin] * 4, axis=1)) * (HEAD ** -0.5)
    k_ref[...] = _rope(_head_norm(z[:, c1 + C_W:c1 + C_W + KV_W], e[0:KV_W, 0:KV_W], qk_ref[1:2, :]), cos, sin)
    lo_f = lax.broadcasted_iota(jnp.int32, (x_ref.shape[0], KV_W), 1) < HEAD
    for h in range(N_HEADS):
        pair, g = q[:, (h // 2) * KV_W:(h // 2 + 1) * KV_W], h // 4
        own = jnp.where(lo_f if h % 2 == 0 else jnp.logical_not(lo_f), pair, 0.0)
        piece_ref[...] = own if h % 2 == g else pltpu.roll(own, HEAD, 1)
        for t in range(n_tok):
            qx_ref[pl.ds(h * n_tok + t, n_seq, stride=N_HEADS * n_tok), :] = piece_ref[pl.ds(t, n_seq, stride=n_tok), :]


def _sample_in(x, cos, sin, wl, l, e256, n_tok):
    n = x.shape[0]
    args = (x, cos, sin, wl['w_in'], wl['nmix'], wl['qk'], e256)
    specs = [_const_spec(a.shape) for a in args]
    specs[3:6] = [_layer_spec(a, l) for a in args[3:6]]
    return pl.pallas_call(
        functools.partial(_sample_in_kernel, n_tok), grid=(1,), in_specs=specs,
        out_specs=(_const_spec((n, IN_COLS)), _const_spec((n * N_HEADS, KV_W)), _const_spec((n, KV_W))),
        out_shape=(jax.ShapeDtypeStruct((n, IN_COLS), f32), jax.ShapeDtypeStruct((n * N_HEADS, KV_W), f32),
                   jax.ShapeDtypeStruct((n, KV_W), f32)),
        scratch_shapes=[pltpu.VMEM((n, KV_W), f32)],
        compiler_params=pltpu.CompilerParams(dimension_semantics=("arbitrary",), vmem_limit_bytes=VMEM_LIMIT),
        name="sample_in",
    )(*args)


def _sample_hgrn_kernel(n_tok, q_ref, f_ref, v_ref, s_ref, lbp_ref, o_ref, so_ref, st_scr, qt_scr, ft_scr, kt_scr,
                        vt_scr, ot_scr):
    n_seq = s_ref.shape[0]
    st_scr[...] = s_ref[...].T
    for t in range(n_tok):
        zf = f_ref[pl.ds(t, n_seq, stride=n_tok), :]
        logf, ls = _log_forget(zf, lbp_ref[0:1, :], lbp_ref[1:2, :])
        ft_scr[t] = jnp.exp(logf).T
        kt_scr[t] = (lbp_ref[2:3, :] * jnp.exp(ls - zf)).T
        qt_scr[t] = q_ref[pl.ds(t, n_seq, stride=n_tok), :].T
        vt_scr[t] = v_ref[pl.ds(t, n_seq, stride=n_tok), :].T
        ot_scr[t] = jnp.zeros((2 * HEAD, n_seq), f32)

    def body(hk, carry):
        r0 = pl.multiple_of(hk * HEAD, HEAD)
        v0 = pl.multiple_of((hk // HEAD) * HEAD, HEAD)
        blk = st_scr[pl.ds(r0, HEAD), :]
        for t in range(n_tok):
            blk = blk * ft_scr[t, pl.ds(hk, 1), :] + kt_scr[t, pl.ds(hk, 1), :] * vt_scr[t, pl.ds(v0, HEAD), :]
            ot_scr[t, pl.ds(v0, HEAD), :] = ot_scr[t, pl.ds(v0, HEAD), :] + qt_scr[t, pl.ds(hk, 1), :] * blk
        st_scr[pl.ds(r0, HEAD), :] = blk
        return carry

    lax.fori_loop(0, 2 * HEAD, body, 0, unroll=4)
    so_ref[...] = st_scr[...].T
    for t in range(n_tok):
        o_ref[pl.ds(t, n_seq, stride=n_tok), :] = ot_scr[t].T


def _sample_hgrn(z, state, lbp, l, n_tok):
    n = z.shape[0]
    n_seq = state.shape[1]
    blk = 2 * HEAD * HEAD
    col = lambda j0: pl.BlockSpec((n, 2 * HEAD), lambda i, _j=j0: (0, _j + i))
    return pl.pallas_call(
        functools.partial(_sample_hgrn_kernel, n_tok), grid=(2,),
        in_specs=[col(0), col(2), col(4), pl.BlockSpec((None, n_seq, blk), lambda i, _l=l: (_l, 0, i)),
                  pl.BlockSpec((None, 3, 2 * HEAD), lambda i, _l=l: (_l, 0, i))],
        out_specs=(pl.BlockSpec((n, 2 * HEAD), lambda i: (0, i)), pl.BlockSpec((n_seq, blk), lambda i: (0, i))),
        out_shape=(jax.ShapeDtypeStruct((n, A_W), f32), jax.ShapeDtypeStruct(state.shape[1:], f32)),
        scratch_shapes=[pltpu.VMEM((blk, n_seq), f32)] + [pltpu.VMEM((n_tok, 2 * HEAD, n_seq), f32)] * 5,
        compiler_params=pltpu.CompilerParams(dimension_semantics=("arbitrary",), vmem_limit_bytes=VMEM_LIMIT),
        name="sample_hgrn",
    )(z, z, z, state, lbp)


def _sample_conv_kernel(n_tok, u0_ref, u1_ref, g0_ref, g1_ref, buf_ref, cw_ref, cvec_ref, e_ref, ob0_ref, ob1_ref,
                        new_ref):
    n_seq = buf_ref.shape[0]
    n_hist = CONV_WIDTH - 1
    us = []
    for t in range(n_tok):
        rows = pl.ds(t, n_seq, stride=n_tok)
        u = jnp.concatenate([u0_ref[rows, :], u1_ref[rows, :]], axis=1)
        g = jnp.concatenate([g0_ref[rows, :], g1_ref[rows, :]], axis=1)
        us.append(u * _sigmoid(g))
    buf = buf_ref[...]
    for t in range(n_tok):
        taps = cw_ref[0:n_hist - t, :]
        if t:
            taps = jnp.concatenate([jnp.zeros((t, B_W), f32), taps], axis=0)
        y = jnp.sum(buf * taps[None], axis=1) + cvec_ref[0:1, :]
        for s in range(t + 1):
            j = n_hist + s - t
            y = y + us[s] * cw_ref[j:j + 1, :]
        ob = _group_ln_silu(y, e_ref[...], cvec_ref[1:2, :], cvec_ref[2:3, :])
        ob0_ref[pl.ds(t, n_seq, stride=n_tok), :] = ob[:, 0:KV_W]
        ob1_ref[pl.ds(t, n_seq, stride=n_tok), :] = ob[:, KV_W:B_W]
    new_ref[:, 0:n_hist - n_tok, :] = buf_ref[:, n_tok:n_hist, :]
    for s in range(n_tok):
        new_ref[:, n_hist - n_tok + s, :] = us[s]


def _sample_conv(z, buf, wl, l, e256, n_tok):
    n = z.shape[0]
    col = lambda j: pl.BlockSpec((n, KV_W), lambda i, _j=j: (0, _j))
    half = jax.ShapeDtypeStruct((n, KV_W), f32)
    return pl.pallas_call(
        functools.partial(_sample_conv_kernel, n_tok), grid=(1,),
        in_specs=[col(8), col(9), col(10), col(11), _layer_spec(buf, l), _layer_spec(wl['cw'], l),
                  _layer_spec(wl['cvec'], l), _const_spec(e256.shape)],
        out_specs=(_const_spec((n, KV_W)), _const_spec((n, KV_W)), _const_spec(buf.shape[1:])),
        out_shape=(half, half, jax.ShapeDtypeStruct(buf.shape[1:], f32)),
        compiler_params=pltpu.CompilerParams(dimension_semantics=("arbitrary",), vmem_limit_bytes=VMEM_LIMIT),
        name="sample_conv",
    )(z, z, z, z, buf, wl['cw'], wl['cvec'], e256)


SEQ_BLOCK = 32


def _sample_attn_kernel(n_tok, qx_ref, k_ref, v_ref, kc_ref, vc_ref, sink_ref, bias_ref, ox_ref, kn_ref, vn_ref):
    q_rows = N_HEADS * n_tok
    sink, bias = sink_ref[...], bias_ref[...]
    seqs = range(SEQ_BLOCK)
    k_all = [jnp.concatenate([kc_ref[b], k_ref[b * n_tok:(b + 1) * n_tok, :]], axis=0) for b in seqs]
    v_all = [jnp.concatenate([vc_ref[b], v_ref[b * n_tok:(b + 1) * n_tok, :]], axis=0) for b in seqs]
    s = [lax.dot_general(qx_ref[b * q_rows:(b + 1) * q_rows, :].astype(bf16), k_all[b].astype(bf16),
                         (((1,), (1,)), ((), ())), preferred_element_type=f32) + bias for b in seqs]
    mx = [jnp.maximum(jnp.max(s[b], axis=-1, keepdims=True), sink) for b in seqs]
    pr = [jnp.exp(s[b] - mx[b]) for b in seqs]
    rden = [1.0 / (jnp.sum(pr[b], axis=-1, keepdims=True) + jnp.exp(sink - mx[b])) for b in seqs]
    for b in seqs:
        ox_ref[b * q_rows:(b + 1) * q_rows, :] = jnp.dot(pr[b].astype(bf16), v_all[b].astype(bf16),
                                                         preferred_element_type=f32) * rden[b]
    for b in seqs:
        kn_ref[b] = k_all[b][n_tok:, :]
        vn_ref[b] = v_all[b][n_tok:, :]


def _sample_attn(qx, k, z, kc, vc, sink_col, bias, l, n_tok):
    n_seq = kc.shape[1]
    assert n_seq % SEQ_BLOCK == 0
    rows = SEQ_BLOCK * n_tok
    q_rows = SEQ_BLOCK * N_HEADS * n_tok
    v_col = (4 * A_W + 2 * B_W + C_W + KV_W) // KV_W
    cache_in = pl.BlockSpec((None, SEQ_BLOCK) + kc.shape[2:], lambda i, _l=l: (_l, i, 0, 0))
    cache_out = pl.BlockSpec((SEQ_BLOCK,) + kc.shape[2:], lambda i: (i, 0, 0))
    return pl.pallas_call(
        functools.partial(_sample_attn_kernel, n_tok), grid=(n_seq // SEQ_BLOCK,),
        in_specs=[pl.BlockSpec((q_rows, KV_W), lambda i: (i, 0)), pl.BlockSpec((rows, KV_W), lambda i: (i, 0)),
                  pl.BlockSpec((rows, KV_W), lambda i: (i, v_col)), cache_in, cache_in,
                  _layer_spec(sink_col, l), _const_spec(bias.shape)],
        out_specs=(pl.BlockSpec((q_rows, KV_W), lambda i: (i, 0)), cache_out, cache_out),
        out_shape=(jax.ShapeDtypeStruct(qx.shape, f32), jax.ShapeDtypeStruct(kc.shape[1:], f32),
                   jax.ShapeDtypeStruct(vc.shape[1:], f32)),
        compiler_params=pltpu.CompilerParams(dimension_semantics=("arbitrary",), vmem_limit_bytes=VMEM_LIMIT),
        name="sample_attn",
    )(qx, k, z, kc, vc, sink_col, bias)


def _sample_out_kernel(n_tok, x_ref, oa_ref, g_ref, ob0_ref, ob1_ref, ox_ref, wout_ref, aon_ref, e_ref, h_ref,
                       oc_ref):
    n_seq = x_ref.shape[0] // n_tok
    gate = g_ref[...]
    oa = _head_norm(oa_ref[...], e_ref[...], aon_ref[...]) * (gate * _sigmoid(gate))
    lo_f = lax.broadcasted_iota(jnp.int32, (n_seq, KV_W), 1) < HEAD
    for p in range(N_HEADS // 2):
        g = p // 2
        for t in range(n_tok):
            even = ox_ref[pl.ds((2 * p) * n_tok + t, n_seq, stride=N_HEADS * n_tok), :]
            odd = ox_ref[pl.ds((2 * p + 1) * n_tok + t, n_seq, stride=N_HEADS * n_tok), :]
            even = even if g == 0 else pltpu.roll(even, HEAD, 1)
            odd = odd if g == 1 else pltpu.roll(odd, HEAD, 1)
            oc_ref[p, pl.ds(t, n_seq, stride=n_tok), :] = jnp.where(lo_f, even, odd)
    mix = jnp.concatenate([oa, ob0_ref[...], ob1_ref[...]] + [oc_ref[p] for p in range(N_HEADS // 2)],
                          axis=1).astype(bf16)
    h_ref[...] = x_ref[...] + jnp.dot(mix, wout_ref[...], preferred_element_type=f32)


def _sample_out(x, oa, z, ob0, ob1, ox, wl, l, e256, n_tok):
    n = x.shape[0]
    gate_spec = pl.BlockSpec((n, A_W), lambda i: (0, 3))
    args = (x, oa, z, ob0, ob1, ox, wl['w_out'], wl['aon'], e256)
    specs = [_const_spec(a.shape) for a in args]
    specs[2] = gate_spec
    specs[6:8] = [_layer_spec(a, l) for a in args[6:8]]
    return pl.pallas_call(
        functools.partial(_sample_out_kernel, n_tok), grid=(1,), in_specs=specs,
        out_specs=_const_spec((n, D_MODEL)), out_shape=jax.ShapeDtypeStruct((n, D_MODEL), f32),
        scratch_shapes=[pltpu.VMEM((N_HEADS // 2, n, KV_W), f32)],
        compiler_params=pltpu.CompilerParams(dimension_semantics=("arbitrary",), vmem_limit_bytes=VMEM_LIMIT),
        name="sample_out",
    )(*args)


def _rope_cos_sin(pos):
    half = HEAD // 2
    inv = ROPE_THETA ** (-jnp.arange(half, dtype=f32) / half)
    ang = pos[:, None] * inv[None, :]
    cos, sin = jnp.cos(ang), jnp.sin(ang)
    return jnp.tile(cos, (1, 4)), jnp.concatenate([-sin, sin, -sin, sin], axis=1)


def _rope_prompt_tables(t_len):
    half = HEAD // 2
    inv = ROPE_THETA ** (-jnp.arange(half, dtype=f32) / half)
    sign = jnp.concatenate([-jnp.ones((half,), f32), jnp.ones((half,), f32)] * 2)
    ang_r = jnp.arange(TILE, dtype=f32)[:, None] * inv[None, :]
    ang_t = (jnp.arange(t_len // TILE, dtype=f32) * TILE)[:, None] * inv[None, :]
    cos_r, sin_r = jnp.tile(jnp.cos(ang_r), (1, 4)), jnp.tile(jnp.sin(ang_r), (1, 4))
    rows = jnp.stack([cos_r, sin_r, cos_r * sign, sin_r * sign])
    tiles = jnp.stack([jnp.tile(jnp.cos(ang_t), (1, 4)), jnp.tile(jnp.sin(ang_t), (1, 4))])
    return rows, tiles


def _stacked_weights(lbs, n_tok, w_in, a_onorm, conv_w, conv_b, conv_ln_g, conv_ln_b, q_norm, k_norm, sinks, w_out,
                     norm_mix, norm_ffn, w_gate, w_up, w_down, ple_norm, w_ple_gate, w_ple_proj):
    return dict(
        w_in=w_in.astype(bf16), w_out=w_out.astype(bf16), w_gate=w_gate.astype(bf16), w_up=w_up.astype(bf16),
        w_down=w_down.astype(bf16), w_ple_gate=w_ple_gate.astype(bf16), w_ple_proj=w_ple_proj.astype(bf16),
        nmix=norm_mix[:, None, :],
        lbp=jnp.stack([jnp.log(lbs), jnp.log1p(-lbs), 1.0 - lbs], axis=1),
        aon=jnp.tile(a_onorm, (1, 4))[:, None, :],
        cw=conv_w,
        cvec=jnp.stack([conv_b, conv_ln_g, conv_ln_b], axis=1),
        qk=jnp.stack([jnp.tile(q_norm, (1, 2)), jnp.tile(k_norm, (1, 2))], axis=1),
        sinks=sinks,
        sink_col=jnp.repeat(sinks, n_tok, axis=1)[:, :, None],
        fnorms=jnp.stack([norm_ffn, ple_norm], axis=1),
    )


def kernel(x_prompt, x_sample, state_hgrn, state_conv, cache_swa_k, cache_swa_v, p_prompt, p_sample, a_lower, w_in, a_onorm, conv_w, conv_b, conv_ln_g, conv_ln_b, q_norm, k_norm, sinks, w_out, norm_mix, norm_ffn, w_gate, w_up, w_down, ple_norm, w_ple_gate, w_ple_proj):
    depth = w_in.shape[0]
    t_len = x_prompt.shape[1]
    n_seq, n_tok = x_sample.shape[0], x_sample.shape[1]
    past_len = 16384
    assert x_prompt.shape[0] == 1 and t_len % TILE == 0

    lbs = jnp.cumsum(jax.nn.softmax(a_lower.astype(f32), axis=0), axis=0)
    lbs = lbs - lbs[0:1]
    consts_np = _np_constants()
    e256 = jnp.asarray(consts_np[0], bf16)
    consts = (e256, jnp.asarray(consts_np[1], bf16), jnp.asarray(consts_np[2], f32), jnp.asarray(consts_np[3], bf16),
              jnp.asarray(consts_np[4], f32))

    rope_p = _rope_prompt_tables(t_len)
    cos_s, sin_s = _rope_cos_sin(past_len + jnp.arange(n_tok, dtype=f32))
    cos_s, sin_s = jnp.tile(cos_s, (n_seq, 1)), jnp.tile(sin_s, (n_seq, 1))
    w_buf = cache_swa_k.shape[2]
    q_tok = (np.arange(N_HEADS * n_tok) % n_tok)[:, None]
    rel = q_tok + w_buf - np.arange(w_buf + n_tok)[None, :]
    bias_s = jnp.asarray(np.where((rel >= 0) & (rel <= WINDOW), 0.0, -np.inf), f32)

    wl = _stacked_weights(lbs, n_tok, w_in, a_onorm, conv_w, conv_b, conv_ln_g, conv_ln_b, q_norm, k_norm, sinks,
                          w_out, norm_mix, norm_ffn, w_gate, w_up, w_down, ple_norm, w_ple_gate, w_ple_proj)
    p_p = p_prompt.reshape(depth, t_len, PLE_DIM)
    p_s = p_sample.reshape(depth, n_seq * n_tok, PLE_DIM)
    s_hgrn = state_hgrn.reshape(depth, n_seq, -1)
    kc = cache_swa_k.reshape(depth, n_seq, w_buf, KV_W)
    vc = cache_swa_v.reshape(depth, n_seq, w_buf, KV_W)

    hp = x_prompt[0]
    hs = x_sample.reshape(n_seq * n_tok, D_MODEL)
    outs = [[] for _ in range(8)]
    for l in range(depth):
        hp, st, conv_new, k_new, v_new = _prompt_layer(hp, p_p, rope_p, wl, l, consts)
        s_new = jnp.stack([st[h * HEAD:(h + 1) * HEAD, h * HEAD:(h + 1) * HEAD].T for h in range(4)])
        outs[0].append(s_new[None])
        outs[1].append(conv_new[None])
        outs[2].append(k_new.reshape(1, WINDOW, 2, HEAD))
        outs[3].append(v_new.reshape(1, WINDOW, 2, HEAD))
        z, qx, k = _sample_in(hs, cos_s, sin_s, wl, l, e256, n_tok)
        oa, s_s = _sample_hgrn(z, s_hgrn, wl['lbp'], l, n_tok)
        ob0, ob1, conv_s = _sample_conv(z, state_conv, wl, l, e256, n_tok)
        ox, k_s, v_s = _sample_attn(qx, k, z, kc, vc, wl['sink_col'], bias_s, l, n_tok)
        h_mid_s = _sample_out(hs, oa, z, ob0, ob1, ox, wl, l, e256, n_tok)
        hs = _ffn(h_mid_s, p_s, wl, l)
        outs[4].append(s_s.reshape(n_seq, 4, HEAD, HEAD))
        outs[5].append(conv_s)
        outs[6].append(k_s.reshape(n_seq, w_buf, 2, HEAD))
        outs[7].append(v_s.reshape(n_seq, w_buf, 2, HEAD))
    return (hp[None], hs.reshape(n_seq, n_tok, D_MODEL)) + tuple(jnp.stack(o) for o in outs)
```

```python
import functools

import numpy as np
import jax
import jax.numpy as jnp
from jax import lax
from jax.experimental import pallas as pl
from jax.experimental.pallas import tpu as pltpu

f32 = jnp.float32
bf16 = jnp.bfloat16

D_MODEL = 1024
HEAD = 64
A_W = 256
B_W = 256
C_W = 512
KV_W = 128
IN_COLS = 4 * A_W + 2 * B_W + C_W + 2 * KV_W
D_FF = 2816
PLE_DIM = 256
WINDOW = 128
CONV_WIDTH = 31
ROPE_THETA = 10000.0
EPS = 1e-6

TILE = 512
CUM_ROWS = 256
CHUNK = 64
LEVELS = (32, 16, 8, 4, 2, 1)
SAFE_EXPONENT = 80.0
HIST = 32
FFN_TILE = 512
FFN_SLABS = 4
LOG2E = 1.4426950408889634
VMEM_LIMIT = 52 * 1024 * 1024
LAYER_VMEM_LIMIT = 58 * 1024 * 1024


def _split2(x):
    hi = x.astype(bf16)
    lo = (x - hi.astype(f32)).astype(bf16)
    return hi, lo


def _split3(x):
    hi = x.astype(bf16)
    r = x - hi.astype(f32)
    mid = r.astype(bf16)
    lo = (r - mid.astype(f32)).astype(bf16)
    return hi, mid, lo


def _seg_sum(x, e):
    hi, lo = _split2(x)
    return (jnp.dot(hi, e, preferred_element_type=f32) + jnp.dot(lo, e, preferred_element_type=f32))


def _sigmoid(x):
    return 1.0 / (1.0 + jnp.exp(-x))


def _rmsnorm_rows(x, g):
    return x * lax.rsqrt(jnp.mean(x * x, axis=-1, keepdims=True) + EPS) * g


def _seg_stat(x, e):
    return jnp.dot(x.astype(bf16), e, preferred_element_type=f32)


def _head_norm(x, e, g):
    ms = _seg_stat(x * x, e) * (1.0 / HEAD)
    return x * lax.rsqrt(ms + EPS) * g


def _rope(x, cos, sin_signed):
    w = x.shape[-1]
    lane = lax.broadcasted_iota(jnp.int32, x.shape, 1)
    first = (lane % HEAD) < (HEAD // 2)
    swapped = jnp.where(first, pltpu.roll(x, w - HEAD // 2, 1), pltpu.roll(x, HEAD // 2, 1))
    return x * cos + swapped * sin_signed


def _log_forget(zf, log_lb, log_1m_lb):
    ls = jnp.minimum(zf, 0.0) - jnp.log(1.0 + jnp.exp(-jnp.abs(zf)))
    b = log_1m_lb + ls
    return jnp.maximum(log_lb, b) + jnp.log(1.0 + jnp.exp(-jnp.abs(log_lb - b))), ls


def _group_ln_silu(y, e, g, b):
    mu = _seg_sum(y, e) * (1.0 / HEAD)
    yc = y - mu
    var = _seg_stat(yc * yc, e) * (1.0 / HEAD)
    yn = yc * lax.rsqrt(var + EPS) * g + b
    return yn * _sigmoid(yn)


def _np_constants():
    lane = np.arange(256)
    e256 = (lane[:, None] // HEAD == lane[None, :] // HEAD).astype(np.float32)
    t = np.arange(CUM_ROWS)
    ltri = ((t[:, None] // CHUNK == t[None, :] // CHUNK) & (t[None, :] <= t[:, None])).astype(np.float32)
    tt = np.arange(CHUNK)[:, None]
    ss = (np.arange(256) % CHUNK)[None, :]
    lvl = []
    for m in LEVELS:
        lvl.append(((tt // (2 * m) == ss // (2 * m)) & (tt % (2 * m) >= m) & (ss % (2 * m) < m)).astype(np.float32))
    lvl.append((ss <= tt).astype(np.float32))
    lvl = np.stack(lvl)
    hm = np.stack([np.broadcast_to((lane // HEAD == h)[None, :], (CHUNK, 256)) for h in range(4)]).astype(np.float32)
    keys = np.arange(2 * WINDOW)[:, None]
    qry = np.arange(4 * WINDOW)[None, :] % WINDOW
    band = (keys >= qry) & (keys <= qry + WINDOW)
    bias = np.where(np.stack([band & (keys >= WINDOW), band]), 0.0, -np.inf).astype(np.float32)
    return e256, ltri, lvl, hm, bias


def _level_ref(g, m):
    n_rows = g.shape[0]
    if 2 * m >= 8:
        g3 = g.reshape(n_rows // (2 * m), 2 * m, g.shape[1])
        return jnp.broadcast_to(g3[:, m - 1:m, :], g3.shape).reshape(g.shape)
    g3 = g.reshape(n_rows // 8, 8, g.shape[1])
    sub = lax.broadcasted_iota(jnp.int32, g3.shape, 1)
    out = None
    for blk in range(8 // (2 * m)):
        r = blk * 2 * m + m - 1
        piece = jnp.broadcast_to(g3[:, r:r + 1, :], g3.shape)
        out = piece if out is None else jnp.where(sub >= blk * 2 * m, piece, out)
    return out.reshape(g.shape)


def _hgrn_tile(q, zf, v, lbp_ref, st_ref, st0_ref, o_ref, e, ltri_ref, lvl_ref, hm_ref):
    log_lb, log_1m_lb, one_m_lb = lbp_ref[0:1, :], lbp_ref[1:2, :], lbp_ref[2:3, :]
    logf, ls = _log_forget(zf, log_lb, log_1m_lb)
    kin = one_m_lb * jnp.exp(ls - zf)
    parts = jnp.concatenate(_split3(logf), axis=1)
    g3 = jnp.concatenate([jnp.dot(ltri_ref[...], parts[r:r + CUM_ROWS], preferred_element_type=f32)
                          for r in range(0, TILE, CUM_ROWS)], axis=0)
    g = g3[:, 0:A_W] + g3[:, A_W:2 * A_W] + g3[:, 2 * A_W:3 * A_W]
    v_b = v.astype(bf16)
    vt_b = v.T.astype(bf16)
    e_mid = g - _level_ref(g, CHUNK // 2)
    safe = jnp.max(jnp.abs(e_mid)) < SAFE_EXPONENT

    def expand(rows_b):
        return jnp.concatenate([rows_b * hm_ref[h] for h in range(4)], axis=0)

    def finish_chunk(c, att, q_in, k_hat):
        r0, r1 = c * CHUNK, (c + 1) * CHUNK
        st = st_ref[...]
        o = jnp.dot(att.astype(bf16), expand(v_b[r0:r1]), preferred_element_type=f32)
        o = o + lax.dot_general(q_in, st.astype(bf16) * e, (((1,), (1,)), ((), ())), preferred_element_type=f32)
        zero = jnp.zeros((CHUNK, A_W), bf16)
        pair = (c // 2) * 2 * CHUNK
        rhs = jnp.concatenate([k_hat, zero] if c % 2 == 0 else [zero, k_hat], axis=0)
        upd = jnp.dot(vt_b[:, pair:pair + 2 * CHUNK], rhs, preferred_element_type=f32)
        st_ref[...] = st * jnp.exp(g[r1 - 1:r1, :]) + upd
        o_ref[r0:r1, :] = o

    st0_ref[...] = st_ref[...]
    qw = q * jnp.exp(e_mid)
    kw = kin * jnp.exp(-e_mid)
    q_t, k_t = qw.astype(bf16), kw.astype(bf16)
    chunks = [(c * CHUNK, (c + 1) * CHUNK) for c in range(TILE // CHUNK)]
    atts = [lax.dot_general(q_t[r0:r1], expand(k_t[r0:r1]), (((1,), (1,)), ((), ())), preferred_element_type=f32)
            for r0, r1 in chunks]
    atts = [jnp.where(lvl_ref[len(LEVELS)] > 0.5, a, 0.0) for a in atts]
    for c, (r0, r1) in enumerate(chunks):
        g_mid, g_end = g[r0 + CHUNK // 2 - 1:r0 + CHUNK // 2, :], g[r1 - 1:r1, :]
        finish_chunk(c, atts[c], (qw[r0:r1] * jnp.exp(g_mid)).astype(bf16),
                     (kw[r0:r1] * jnp.exp(g_end - g_mid)).astype(bf16))

    def redo_if_unsafe():
        @pl.when(jnp.logical_not(safe))
        def _():
            st_ref[...] = st0_ref[...]
            q_levels, k_levels = [], []
            for m in LEVELS:
                w = jnp.exp(-jnp.abs(g - _level_ref(g, m)))
                q_levels.append((q * w).astype(bf16))
                k_levels.append((kin * w).astype(bf16))
            q_inter = (q * jnp.exp(g)).astype(bf16)
            for c in range(TILE // CHUNK):
                r0, r1 = c * CHUNK, (c + 1) * CHUNK
                att = jnp.zeros((CHUNK, 4 * CHUNK), f32)
                for li in range(len(LEVELS)):
                    a = lax.dot_general(q_levels[li][r0:r1], expand(k_levels[li][r0:r1]),
                                        (((1,), (1,)), ((), ())), preferred_element_type=f32)
                    att = att + a * lvl_ref[li]
                k_hat = (kin[r0:r1] * jnp.exp(g[r1 - 1:r1, :] - g[r0:r1])).astype(bf16)
                finish_chunk(c, att, q_inter[r0:r1], k_hat)
            o_ref[...] = o_ref[...] + _seg_sum(q * kin, e) * v

    return redo_if_unsafe


def _conv_tile(u, hist_ref, cw_ref, cvec_ref, e):
    n = u.shape[0]
    hist_ref[HIST:HIST + n, :] = u
    lo = HIST - (CONV_WIDTH - 1)
    y = cvec_ref[0:1, :]
    for r in range(8):
        rows = n if r == 0 else n + 8
        part = None
        for o in range(lo, lo + CONV_WIDTH):
            if o % 8 == r:
                term = hist_ref[o - r:o - r + rows, :] * cw_ref[o - lo:o - lo + 1, :]
                part = term if part is None else part + term
        y = y + part[r:r + n]
    return _group_ln_silu(y, e, cvec_ref[1:2, :], cvec_ref[2:3, :])


def _attn_tile(q_t, k_b, v_t, sink_ref, biases):
    zero = jnp.zeros((HEAD, WINDOW), f32)
    inst = [(b, g) for b in range(TILE // WINDOW) for g in range(2)]
    sinks = [jnp.concatenate([jnp.full((1, WINDOW), sink_ref[h] * LOG2E, f32) for h in range(4 * g, 4 * g + 4)],
                             axis=1) for g in range(2)]

    def rhs(b, g):
        cols = []
        for h in range(4 * g, 4 * g + 4):
            q_h = q_t[h * HEAD:(h + 1) * HEAD, b * WINDOW:(b + 1) * WINDOW]
            cols.append(jnp.concatenate([q_h, zero] if g == 0 else [zero, q_h], axis=0))
        return jnp.concatenate(cols, axis=1).astype(bf16)

    s = [jnp.dot(k_b[b * WINDOW:(b + 2) * WINDOW], rhs(b, g), preferred_element_type=f32) + biases[b]
         for b, g in inst]
    mx = [jnp.maximum(jnp.max(s[n], axis=0, keepdims=True), sinks[g]) for n, (b, g) in enumerate(inst)]
    pr = [jnp.exp2(s[n] - mx[n]) for n in range(len(inst))]
    rden = [1.0 / (jnp.sum(pr[n], axis=0, keepdims=True) + jnp.exp2(sinks[g] - mx[n]))
            for n, (b, g) in enumerate(inst)]
    o_t = [jnp.dot(v_t[g * HEAD:(g + 1) * HEAD, b * WINDOW:(b + 2) * WINDOW], pr[n].astype(bf16),
                   preferred_element_type=f32) * rden[n] for n, (b, g) in enumerate(inst)]
    rows = []
    for b in range(TILE // WINDOW):
        pairs = []
        for g in range(2):
            o = o_t[2 * b + g]
            for j in range(2):
                pair_t = jnp.concatenate([o[:, (2 * j) * WINDOW:(2 * j + 1) * WINDOW],
                                          o[:, (2 * j + 1) * WINDOW:(2 * j + 2) * WINDOW]], axis=0)
                pairs.append(pair_t.T)
        rows.append(jnp.concatenate(pairs, axis=1))
    return jnp.concatenate(rows, axis=0)


def _mixer_kernel(layer, x_ref, rope_row_ref, rope_tile_ref, win_ref, wout_ref, nmix_ref, lbp_ref, aon_ref, cw_ref,
                  cvec_ref, qk_ref, sinks_ref, e_ref, ltri_ref, lvl_ref, hm_ref, bias_ref,
                  h_ref, st_ref, conv_ref, kn_ref, vn_ref, z_ref, hist_ref, o_ref, st0_ref):
    i = pl.program_id(0)
    sink_ref = sinks_ref.at[layer]

    @pl.when(i == 0)
    def _():
        st_ref[...] = jnp.zeros_like(st_ref)
        hist_ref[0:HIST, :] = jnp.zeros((HIST, B_W), f32)
        kn_ref[...] = jnp.zeros_like(kn_ref)
        vn_ref[...] = jnp.zeros_like(vn_ref)

    x = x_ref[...]
    xn = _rmsnorm_rows(x, nmix_ref[...])
    z_ref[...] = jnp.dot(xn.astype(bf16), win_ref[...], preferred_element_type=f32)
    e = e_ref[...]

    redo_hgrn_if_unsafe = _hgrn_tile(z_ref[:, 0:A_W], z_ref[:, A_W:2 * A_W], z_ref[:, 2 * A_W:3 * A_W], lbp_ref,
                                     st_ref, st0_ref, o_ref, e, ltri_ref, lvl_ref, hm_ref)

    c0 = 4 * A_W
    u = z_ref[:, c0:c0 + B_W] * _sigmoid(z_ref[:, c0 + B_W:c0 + 2 * B_W])
    ob = _conv_tile(u, hist_ref, cw_ref, cvec_ref, e)
    conv_ref[...] = hist_ref[TILE + HIST - (CONV_WIDTH - 1):TILE + HIST, :]
    hist_ref[0:HIST, :] = hist_ref[TILE:TILE + HIST, :]

    c1 = c0 + 2 * B_W
    cos_a, sin_a = rope_tile_ref[0, pl.ds(i, 1), :], rope_tile_ref[1, pl.ds(i, 1), :]
    cos = cos_a * rope_row_ref[0] - sin_a * rope_row_ref[1]
    sin = sin_a * rope_row_ref[2] + cos_a * rope_row_ref[3]
    qn = jnp.concatenate([_head_norm(z_ref[:, c1 + j * 256:c1 + (j + 1) * 256], e,
                                     jnp.concatenate([qk_ref[0:1, :]] * 2, axis=1)) for j in range(2)], axis=1)
    q = _rope(qn, jnp.concatenate([cos] * 4, axis=1), jnp.concatenate([sin] * 4, axis=1)) * (LOG2E * HEAD ** -0.5)
    kk = _rope(_head_norm(z_ref[:, c1 + C_W:c1 + C_W + KV_W], e[0:KV_W, 0:KV_W], qk_ref[1:2, :]), cos, sin)
    vv = z_ref[:, c1 + C_W + KV_W:c1 + C_W + 2 * KV_W]
    kcat = jnp.concatenate([kn_ref[...], kk], axis=0)
    vcat = jnp.concatenate([vn_ref[...], vv], axis=0)
    q_t = q.T
    k_b = kcat.astype(bf16)
    v_t = vcat.T.astype(bf16)
    biases = [bias_ref[jnp.where(i > 0, 1, 0)]] + [bias_ref[1]] * (TILE // WINDOW - 1)
    oc = _attn_tile(q_t, k_b, v_t, sink_ref, biases)
    kn_ref[...] = kk[TILE - WINDOW:TILE]
    vn_ref[...] = vv[TILE - WINDOW:TILE]

    redo_hgrn_if_unsafe()
    gate = z_ref[:, 3 * A_W:4 * A_W]
    oa = _head_norm(o_ref[...], e, aon_ref[...]) * (gate * _sigmoid(gate))
    mix = jnp.concatenate([oa, ob, oc], axis=1).astype(bf16)
    h_ref[...] = x + jnp.dot(mix, wout_ref[...], preferred_element_type=f32)


def _const_spec(shape):
    nd = len(shape)
    return pl.BlockSpec(shape, lambda i, _n=nd: (0,) * _n)


def _layer_spec(arr, l, **kw):
    shape = arr.shape[1:]
    return pl.BlockSpec((None,) + shape, lambda i, _l=l, _n=len(shape): (_l,) + (0,) * _n, **kw)


def _ffn_kernel(h_ref, p_ref, wg_ref, wu_ref, wd_ref, wpg_ref, wpp_ref, norms_ref, y_ref):
    n = h_ref.shape[0]
    slabs = [slice(r, r + n // FFN_SLABS) for r in range(0, n, n // FFN_SLABS)]
    h = [h_ref[s, :] for s in slabs]
    hn = [_rmsnorm_rows(a, norms_ref[0:1, :]).astype(bf16) for a in h]
    gt = [jnp.dot(a, wg_ref[...], preferred_element_type=f32) for a in hn]
    up = [jnp.dot(a, wu_ref[...], preferred_element_type=f32) for a in hn]
    act = [(g * _sigmoid(g) * u).astype(bf16) for g, u in zip(gt, up)]
    h = [a + jnp.dot(b, wd_ref[...], preferred_element_type=f32) for a, b in zip(h, act)]
    pn = [_rmsnorm_rows(a, norms_ref[1:2, :]).astype(bf16) for a in h]
    gate = [_sigmoid(jnp.dot(a, wpg_ref[...], preferred_element_type=f32)) for a in pn]
    for s, a, g in zip(slabs, h, gate):
        y_ref[s, :] = a + g * jnp.dot(p_ref[s, :].astype(bf16), wpp_ref[...], preferred_element_type=f32)


def _ffn(h, p, wl, l):
    n = h.shape[0]
    tile = min(FFN_TILE, n)
    assert n % tile == 0
    row_spec = lambda w: pl.BlockSpec((tile, w), lambda i: (i, 0))
    ws = [wl['w_gate'], wl['w_up'], wl['w_down'], wl['w_ple_gate'], wl['w_ple_proj']]
    return pl.pallas_call(
        _ffn_kernel, grid=(n // tile,),
        in_specs=[row_spec(D_MODEL), pl.BlockSpec((None, tile, PLE_DIM), lambda i, _l=l: (_l, i, 0))]
        + [_layer_spec(a, l, pipeline_mode=pl.Buffered(1)) for a in ws] + [_layer_spec(wl['fnorms'], l)],
        out_specs=row_spec(D_MODEL), out_shape=jax.ShapeDtypeStruct((n, D_MODEL), f32),
        compiler_params=pltpu.CompilerParams(dimension_semantics=("arbitrary",), vmem_limit_bytes=VMEM_LIMIT),
        name="ffn_ple",
    )(h, p, *ws, wl['fnorms'])


N_MIXER_INPUTS = 17


def _layer_kernel(layer, *refs):
    mixer_in, (p_ref, wg_ref, wu_ref, wd_ref, wpg_ref, wpp_ref, norms_ref) = refs[:N_MIXER_INPUTS], refs[17:24]
    y_ref, st_ref, conv_ref, kn_ref, vn_ref = refs[24:29]
    z_ref, hist_ref, o_ref, st0_ref, mid_ref = refs[29:]
    _mixer_kernel(layer, *mixer_in, mid_ref, st_ref, conv_ref, kn_ref, vn_ref, z_ref, hist_ref, o_ref, st0_ref)
    _ffn_kernel(mid_ref, p_ref, wg_ref, wu_ref, wd_ref, wpg_ref, wpp_ref, norms_ref, y_ref)


def _prompt_layer(x, p, rope, wl, l, consts):
    t_len = x.shape[0]
    row_spec = lambda w: pl.BlockSpec((TILE, w), lambda i: (i, 0))
    once = dict(pipeline_mode=pl.Buffered(1))
    smalls = [wl['nmix'], wl['lbp'], wl['aon'], wl['cw'], wl['cvec'], wl['qk']]
    ffn_ws = [wl['w_gate'], wl['w_up'], wl['w_down'], wl['w_ple_gate'], wl['w_ple_proj']]
    in_specs = ([row_spec(D_MODEL), _const_spec(rope[0].shape), _const_spec(rope[1].shape),
                 _layer_spec(wl['w_in'], l, **once), _layer_spec(wl['w_out'], l, **once)]
                + [_layer_spec(a, l) for a in smalls] + [pl.BlockSpec(memory_space=pltpu.SMEM)]
                + [_const_spec(a.shape) for a in consts]
                + [pl.BlockSpec((None, TILE, PLE_DIM), lambda i, _l=l: (_l, i, 0))]
                + [_layer_spec(a, l, **once) for a in ffn_ws] + [_layer_spec(wl['fnorms'], l)])
    assert len(in_specs) == N_MIXER_INPUTS + 7
    out_shape = (jax.ShapeDtypeStruct((t_len, D_MODEL), f32),
                 jax.ShapeDtypeStruct((A_W, A_W), f32),
                 jax.ShapeDtypeStruct((CONV_WIDTH - 1, B_W), f32),
                 jax.ShapeDtypeStruct((WINDOW, KV_W), f32),
                 jax.ShapeDtypeStruct((WINDOW, KV_W), f32))
    out_specs = (row_spec(D_MODEL), _const_spec((A_W, A_W)), _const_spec((CONV_WIDTH - 1, B_W)),
                 _const_spec((WINDOW, KV_W)), _const_spec((WINDOW, KV_W)))
    return pl.pallas_call(
        functools.partial(_layer_kernel, l), grid=(t_len // TILE,), in_specs=in_specs, out_specs=out_specs,
        out_shape=out_shape,
        scratch_shapes=[pltpu.VMEM((TILE, IN_COLS), f32), pltpu.VMEM((TILE + HIST, B_W), f32),
                        pltpu.VMEM((TILE, A_W), f32), pltpu.VMEM((A_W, A_W), f32),
                        pltpu.VMEM((TILE, D_MODEL), f32)],
        compiler_params=pltpu.CompilerParams(dimension_semantics=("arbitrary",),
                                             vmem_limit_bytes=LAYER_VMEM_LIMIT),
        name="prompt_layer",
    )(x, rope[0], rope[1], wl['w_in'], wl['w_out'], *smalls, wl['sinks'], *consts, p, *ffn_ws, wl['fnorms'])


N_HEADS = C_W // HEAD


def _sample_in_kernel(n_tok, x_ref, cos_ref, sin_ref, win_ref, nmix_ref, qk_ref, e_ref, z_ref, qx_ref, k_ref,
                      piece_ref):
    n_seq = x_ref.shape[0] // n_tok
    xn = _rmsnorm_rows(x_ref[...], nmix_ref[...])
    z = jnp.dot(xn.astype(bf16), win_ref[...], preferred_element_type=f32)
    z_ref[...] = z
    e = e_ref[...]
    c1 = 4 * A_W + 2 * B_W
    cos, sin = cos_ref[...], sin_ref[...]
    qn = jnp.concatenate([_head_norm(z[:, c1 + j * 256:c1 + (j + 1) * 256], e,
                                     jnp.concatenate([qk_ref[0:1, :]] * 2, axis=1)) for j in range(2)], axis=1)
    q = _rope(qn, jnp.concatenate([cos] * 4, axis=1), jnp.concatenate([sin] * 4, axis=1)) * (HEAD ** -0.5)
    k_ref[...] = _rope(_head_norm(z[:, c1 + C_W:c1 + C_W + KV_W], e[0:KV_W, 0:KV_W], qk_ref[1:2, :]), cos, sin)
    lo_f = lax.broadcasted_iota(jnp.int32, (x_ref.shape[0], KV_W), 1) < HEAD
    for h in range(N_HEADS):
        pair, g = q[:, (h // 2) * KV_W:(h // 2 + 1) * KV_W], h // 4
        own = jnp.where(lo_f if h % 2 == 0 else jnp.logical_not(lo_f), pair, 0.0)
        piece_ref[...] = own if h % 2 == g else pltpu.roll(own, HEAD, 1)
        for t in range(n_tok):
            qx_ref[pl.ds(h * n_tok + t, n_seq, stride=N_HEADS * n_tok), :] = piece_ref[pl.ds(t, n_seq, stride=n_tok), :]


def _sample_in(x, cos, sin, wl, l, e256, n_tok):
    n = x.shape[0]
    args = (x, cos, sin, wl['w_in'], wl['nmix'], wl['qk'], e256)
    specs = [_const_spec(a.shape) for a in args]
    specs[3:6] = [_layer_spec(a, l) for a in args[3:6]]
    return pl.pallas_call(
        functools.partial(_sample_in_kernel, n_tok), grid=(1,), in_specs=specs,
        out_specs=(_const_spec((n, IN_COLS)), _const_spec((n * N_HEADS, KV_W)), _const_spec((n, KV_W))),
        out_shape=(jax.ShapeDtypeStruct((n, IN_COLS), f32), jax.ShapeDtypeStruct((n * N_HEADS, KV_W), f32),
                   jax.ShapeDtypeStruct((n, KV_W), f32)),
        scratch_shapes=[pltpu.VMEM((n, KV_W), f32)],
        compiler_params=pltpu.CompilerParams(dimension_semantics=("arbitrary",), vmem_limit_bytes=VMEM_LIMIT),
        name="sample_in",
    )(*args)


def _sample_hgrn_kernel(n_tok, q_ref, f_ref, v_ref, s_ref, lbp_ref, o_ref, so_ref, qt_scr, ft_scr, kt_scr, vt_scr,
                        ot_scr):
    n_seq = s_ref.shape[1]
    for t in range(n_tok):
        zf = f_ref[pl.ds(t, n_seq, stride=n_tok), :]
        logf, ls = _log_forget(zf, lbp_ref[0:1, :], lbp_ref[1:2, :])
        ft_scr[t] = jnp.exp(logf).T
        kt_scr[t] = (lbp_ref[2:3, :] * jnp.exp(ls - zf)).T
        qt_scr[t] = q_ref[pl.ds(t, n_seq, stride=n_tok), :].T
        vt_scr[t] = v_ref[pl.ds(t, n_seq, stride=n_tok), :].T
        ot_scr[t] = jnp.zeros((2 * HEAD, n_seq), f32)

    def body(hk, carry):
        r0 = pl.multiple_of(hk * HEAD, HEAD)
        v0 = pl.multiple_of((hk // HEAD) * HEAD, HEAD)
        blk = s_ref[pl.ds(r0, HEAD), :]
        for t in range(n_tok):
            blk = blk * ft_scr[t, pl.ds(hk, 1), :] + kt_scr[t, pl.ds(hk, 1), :] * vt_scr[t, pl.ds(v0, HEAD), :]
            ot_scr[t, pl.ds(v0, HEAD), :] = ot_scr[t, pl.ds(v0, HEAD), :] + qt_scr[t, pl.ds(hk, 1), :] * blk
        so_ref[pl.ds(r0, HEAD), :] = blk
        return carry

    lax.fori_loop(0, 2 * HEAD, body, 0, unroll=4)
    for t in range(n_tok):
        o_ref[pl.ds(t, n_seq, stride=n_tok), :] = ot_scr[t].T


def _sample_hgrn(z, state, lbp, l, n_tok):
    n = z.shape[0]
    n_seq = state.shape[2]
    blk = 2 * HEAD * HEAD
    col = lambda j0: pl.BlockSpec((n, 2 * HEAD), lambda i, _j=j0: (0, _j + i))
    return pl.pallas_call(
        functools.partial(_sample_hgrn_kernel, n_tok), grid=(2,),
        in_specs=[col(0), col(2), col(4), pl.BlockSpec((None, blk, n_seq), lambda i, _l=l: (_l, i, 0)),
                  pl.BlockSpec((None, 3, 2 * HEAD), lambda i, _l=l: (_l, 0, i))],
        out_specs=(pl.BlockSpec((n, 2 * HEAD), lambda i: (0, i)), pl.BlockSpec((blk, n_seq), lambda i: (i, 0))),
        out_shape=(jax.ShapeDtypeStruct((n, A_W), f32), jax.ShapeDtypeStruct(state.shape[1:], f32)),
        scratch_shapes=[pltpu.VMEM((n_tok, 2 * HEAD, n_seq), f32)] * 5,
        compiler_params=pltpu.CompilerParams(dimension_semantics=("arbitrary",), vmem_limit_bytes=VMEM_LIMIT),
        name="sample_hgrn",
    )(z, z, z, state, lbp)


def _sample_conv_kernel(n_tok, first, u0_ref, u1_ref, g0_ref, g1_ref, buf_ref, cw_ref, cvec_ref, e_ref, *rest):
    ob0_ref, ob1_ref, new_ref = rest[-3:]
    slots = [new_ref.at[d] for d in range(new_ref.shape[0])] if first else [new_ref]
    n_seq = buf_ref.shape[0]
    n_hist = CONV_WIDTH - 1
    us = []
    for t in range(n_tok):
        rows = pl.ds(t, n_seq, stride=n_tok)
        u = jnp.concatenate([u0_ref[rows, :], u1_ref[rows, :]], axis=1)
        g = jnp.concatenate([g0_ref[rows, :], g1_ref[rows, :]], axis=1)
        us.append(u * _sigmoid(g))
    buf = buf_ref[...]
    for t in range(n_tok):
        taps = cw_ref[0:n_hist - t, :]
        if t:
            taps = jnp.concatenate([jnp.zeros((t, B_W), f32), taps], axis=0)
        y = jnp.sum(buf * taps[None], axis=1) + cvec_ref[0:1, :]
        for s in range(t + 1):
            j = n_hist + s - t
            y = y + us[s] * cw_ref[j:j + 1, :]
        ob = _group_ln_silu(y, e_ref[...], cvec_ref[1:2, :], cvec_ref[2:3, :])
        ob0_ref[pl.ds(t, n_seq, stride=n_tok), :] = ob[:, 0:KV_W]
        ob1_ref[pl.ds(t, n_seq, stride=n_tok), :] = ob[:, KV_W:B_W]
    for dst in slots:
        dst[:, 0:n_hist - n_tok, :] = buf_ref[:, n_tok:n_hist, :]
        for s in range(n_tok):
            dst[:, n_hist - n_tok + s, :] = us[s]


def _sample_conv(z, buf, new_all, wl, l, e256, n_tok):
    n = z.shape[0]
    col = lambda j: pl.BlockSpec((n, KV_W), lambda i, _j=j: (0, _j))
    half = jax.ShapeDtypeStruct((n, KV_W), f32)
    in_specs = [col(8), col(9), col(10), col(11), _layer_spec(buf, l), _layer_spec(wl['cw'], l),
                _layer_spec(wl['cvec'], l), _const_spec(e256.shape)]
    args = [z, z, z, z, buf, wl['cw'], wl['cvec'], e256]
    if new_all is None:
        new_spec, aliases = _const_spec(buf.shape), {}
    else:
        in_specs.append(pl.BlockSpec(memory_space=pl.ANY))
        args.append(new_all)
        new_spec, aliases = _layer_spec(buf, l), {len(args) - 1: 2}
    return pl.pallas_call(
        functools.partial(_sample_conv_kernel, n_tok, new_all is None), grid=(1,), in_specs=in_specs,
        out_specs=(_const_spec((n, KV_W)), _const_spec((n, KV_W)), new_spec),
        out_shape=(half, half, jax.ShapeDtypeStruct(buf.shape, f32)),
        input_output_aliases=aliases,
        compiler_params=pltpu.CompilerParams(dimension_semantics=("arbitrary",), vmem_limit_bytes=VMEM_LIMIT),
        name="sample_conv",
    )(*args)


SEQ_BLOCK = 32


def _sample_attn_kernel(n_tok, qx_ref, k_ref, v_ref, kc_ref, vc_ref, sink_ref, bias_ref, ox_ref, kn_ref, vn_ref):
    q_rows = N_HEADS * n_tok
    sink, bias = sink_ref[...], bias_ref[...]
    seqs = range(SEQ_BLOCK)
    k_all = [jnp.concatenate([kc_ref[b], k_ref[b * n_tok:(b + 1) * n_tok, :]], axis=0) for b in seqs]
    v_all = [jnp.concatenate([vc_ref[b], v_ref[b * n_tok:(b + 1) * n_tok, :]], axis=0) for b in seqs]
    s = [lax.dot_general(qx_ref[b * q_rows:(b + 1) * q_rows, :].astype(bf16), k_all[b].astype(bf16),
                         (((1,), (1,)), ((), ())), preferred_element_type=f32) + bias for b in seqs]
    mx = [jnp.maximum(jnp.max(s[b], axis=-1, keepdims=True), sink) for b in seqs]
    pr = [jnp.exp(s[b] - mx[b]) for b in seqs]
    rden = [1.0 / (jnp.sum(pr[b], axis=-1, keepdims=True) + jnp.exp(sink - mx[b])) for b in seqs]
    for b in seqs:
        ox_ref[b * q_rows:(b + 1) * q_rows, :] = jnp.dot(pr[b].astype(bf16), v_all[b].astype(bf16),
                                                         preferred_element_type=f32) * rden[b]
    for b in seqs:
        kn_ref[b] = k_all[b][n_tok:, :]
        vn_ref[b] = v_all[b][n_tok:, :]


def _sample_attn(qx, k, z, kc, vc, sink_col, bias, l, n_tok):
    n_seq = kc.shape[1]
    assert n_seq % SEQ_BLOCK == 0
    rows = SEQ_BLOCK * n_tok
    q_rows = SEQ_BLOCK * N_HEADS * n_tok
    v_col = (4 * A_W + 2 * B_W + C_W + KV_W) // KV_W
    cache_in = pl.BlockSpec((None, SEQ_BLOCK) + kc.shape[2:], lambda i, _l=l: (_l, i, 0, 0))
    cache_out = pl.BlockSpec((SEQ_BLOCK,) + kc.shape[2:], lambda i: (i, 0, 0))
    return pl.pallas_call(
        functools.partial(_sample_attn_kernel, n_tok), grid=(n_seq // SEQ_BLOCK,),
        in_specs=[pl.BlockSpec((q_rows, KV_W), lambda i: (i, 0)), pl.BlockSpec((rows, KV_W), lambda i: (i, 0)),
                  pl.BlockSpec((rows, KV_W), lambda i: (i, v_col)), cache_in, cache_in,
                  _layer_spec(sink_col, l), _const_spec(bias.shape)],
        out_specs=(pl.BlockSpec((q_rows, KV_W), lambda i: (i, 0)), cache_out, cache_out),
        out_shape=(jax.ShapeDtypeStruct(qx.shape, f32), jax.ShapeDtypeStruct(kc.shape[1:], f32),
                   jax.ShapeDtypeStruct(vc.shape[1:], f32)),
        compiler_params=pltpu.CompilerParams(dimension_semantics=("arbitrary",), vmem_limit_bytes=VMEM_LIMIT),
        name="sample_attn",
    )(qx, k, z, kc, vc, sink_col, bias)


def _sample_out_kernel(n_tok, x_ref, oa_ref, g_ref, ob0_ref, ob1_ref, ox_ref, wout_ref, aon_ref, e_ref, h_ref,
                       oc_ref):
    n_seq = x_ref.shape[0] // n_tok
    gate = g_ref[...]
    oa = _head_norm(oa_ref[...], e_ref[...], aon_ref[...]) * (gate * _sigmoid(gate))
    lo_f = lax.broadcasted_iota(jnp.int32, (n_seq, KV_W), 1) < HEAD
    for p in range(N_HEADS // 2):
        g = p // 2
        for t in range(n_tok):
            even = ox_ref[pl.ds((2 * p) * n_tok + t, n_seq, stride=N_HEADS * n_tok), :]
            odd = ox_ref[pl.ds((2 * p + 1) * n_tok + t, n_seq, stride=N_HEADS * n_tok), :]
            even = even if g == 0 else pltpu.roll(even, HEAD, 1)
            odd = odd if g == 1 else pltpu.roll(odd, HEAD, 1)
            oc_ref[p, pl.ds(t, n_seq, stride=n_tok), :] = jnp.where(lo_f, even, odd)
    mix = jnp.concatenate([oa, ob0_ref[...], ob1_ref[...]] + [oc_ref[p] for p in range(N_HEADS // 2)],
                          axis=1).astype(bf16)
    h_ref[...] = x_ref[...] + jnp.dot(mix, wout_ref[...], preferred_element_type=f32)


def _sample_out(x, oa, z, ob0, ob1, ox, wl, l, e256, n_tok):
    n = x.shape[0]
    gate_spec = pl.BlockSpec((n, A_W), lambda i: (0, 3))
    args = (x, oa, z, ob0, ob1, ox, wl['w_out'], wl['aon'], e256)
    specs = [_const_spec(a.shape) for a in args]
    specs[2] = gate_spec
    specs[6:8] = [_layer_spec(a, l) for a in args[6:8]]
    return pl.pallas_call(
        functools.partial(_sample_out_kernel, n_tok), grid=(1,), in_specs=specs,
        out_specs=_const_spec((n, D_MODEL)), out_shape=jax.ShapeDtypeStruct((n, D_MODEL), f32),
        scratch_shapes=[pltpu.VMEM((N_HEADS // 2, n, KV_W), f32)],
        compiler_params=pltpu.CompilerParams(dimension_semantics=("arbitrary",), vmem_limit_bytes=VMEM_LIMIT),
        name="sample_out",
    )(*args)


def _rope_cos_sin(pos):
    half = HEAD // 2
    inv = ROPE_THETA ** (-jnp.arange(half, dtype=f32) / half)
    ang = pos[:, None] * inv[None, :]
    cos, sin = jnp.cos(ang), jnp.sin(ang)
    return jnp.tile(cos, (1, 4)), jnp.concatenate([-sin, sin, -sin, sin], axis=1)


def _rope_prompt_tables(t_len):
    half = HEAD // 2
    inv = ROPE_THETA ** (-jnp.arange(half, dtype=f32) / half)
    sign = jnp.concatenate([-jnp.ones((half,), f32), jnp.ones((half,), f32)] * 2)
    ang_r = jnp.arange(TILE, dtype=f32)[:, None] * inv[None, :]
    ang_t = (jnp.arange(t_len // TILE, dtype=f32) * TILE)[:, None] * inv[None, :]
    cos_r, sin_r = jnp.tile(jnp.cos(ang_r), (1, 4)), jnp.tile(jnp.sin(ang_r), (1, 4))
    rows = jnp.stack([cos_r, sin_r, cos_r * sign, sin_r * sign])
    tiles = jnp.stack([jnp.tile(jnp.cos(ang_t), (1, 4)), jnp.tile(jnp.sin(ang_t), (1, 4))])
    return rows, tiles


def _stacked_weights(lbs, n_tok, w_in, a_onorm, conv_w, conv_b, conv_ln_g, conv_ln_b, q_norm, k_norm, sinks, w_out,
                     norm_mix, norm_ffn, w_gate, w_up, w_down, ple_norm, w_ple_gate, w_ple_proj):
    return dict(
        w_in=w_in.astype(bf16), w_out=w_out.astype(bf16), w_gate=w_gate.astype(bf16), w_up=w_up.astype(bf16),
        w_down=w_down.astype(bf16), w_ple_gate=w_ple_gate.astype(bf16), w_ple_proj=w_ple_proj.astype(bf16),
        nmix=norm_mix[:, None, :],
        lbp=jnp.stack([jnp.log(lbs), jnp.log1p(-lbs), 1.0 - lbs], axis=1),
        aon=jnp.tile(a_onorm, (1, 4))[:, None, :],
        cw=conv_w,
        cvec=jnp.stack([conv_b, conv_ln_g, conv_ln_b], axis=1),
        qk=jnp.stack([jnp.tile(q_norm, (1, 2)), jnp.tile(k_norm, (1, 2))], axis=1),
        sinks=sinks,
        sink_col=jnp.repeat(sinks, n_tok, axis=1)[:, :, None],
        fnorms=jnp.stack([norm_ffn, ple_norm], axis=1),
    )


def kernel(x_prompt, x_sample, state_hgrn, state_conv, cache_swa_k, cache_swa_v, p_prompt, p_sample, a_lower, w_in, a_onorm, conv_w, conv_b, conv_ln_g, conv_ln_b, q_norm, k_norm, sinks, w_out, norm_mix, norm_ffn, w_gate, w_up, w_down, ple_norm, w_ple_gate, w_ple_proj):
    depth = w_in.shape[0]
    t_len = x_prompt.shape[1]
    n_seq, n_tok = x_sample.shape[0], x_sample.shape[1]
    past_len = 16384
    assert x_prompt.shape[0] == 1 and t_len % TILE == 0

    lbs = jnp.cumsum(jax.nn.softmax(a_lower.astype(f32), axis=0), axis=0)
    lbs = lbs - lbs[0:1]
    consts_np = _np_constants()
    e256 = jnp.asarray(consts_np[0], bf16)
    consts = (e256, jnp.asarray(consts_np[1], bf16), jnp.asarray(consts_np[2], f32), jnp.asarray(consts_np[3], bf16),
              jnp.asarray(consts_np[4], f32))

    rope_p = _rope_prompt_tables(t_len)
    cos_s, sin_s = _rope_cos_sin(past_len + jnp.arange(n_tok, dtype=f32))
    cos_s, sin_s = jnp.tile(cos_s, (n_seq, 1)), jnp.tile(sin_s, (n_seq, 1))
    w_buf = cache_swa_k.shape[2]
    q_tok = (np.arange(N_HEADS * n_tok) % n_tok)[:, None]
    rel = q_tok + w_buf - np.arange(w_buf + n_tok)[None, :]
    bias_s = jnp.asarray(np.where((rel >= 0) & (rel <= WINDOW), 0.0, -np.inf), f32)

    wl = _stacked_weights(lbs, n_tok, w_in, a_onorm, conv_w, conv_b, conv_ln_g, conv_ln_b, q_norm, k_norm, sinks,
                          w_out, norm_mix, norm_ffn, w_gate, w_up, w_down, ple_norm, w_ple_gate, w_ple_proj)
    p_p = p_prompt.reshape(depth, t_len, PLE_DIM)
    p_s = p_sample.reshape(depth, n_seq * n_tok, PLE_DIM)
    s_hgrn = jnp.swapaxes(state_hgrn.reshape(depth, n_seq, -1), 1, 2)
    kc = cache_swa_k.reshape(depth, n_seq, w_buf, KV_W)
    vc = cache_swa_v.reshape(depth, n_seq, w_buf, KV_W)

    hp = x_prompt[0]
    hs = x_sample.reshape(n_seq * n_tok, D_MODEL)
    outs = [[] for _ in range(8)]
    conv_s = None
    for l in range(depth):
        hp, st, conv_new, k_new, v_new = _prompt_layer(hp, p_p, rope_p, wl, l, consts)
        s_new = jnp.stack([st[h * HEAD:(h + 1) * HEAD, h * HEAD:(h + 1) * HEAD].T for h in range(4)])
        outs[0].append(s_new[None])
        outs[1].append(conv_new[None])
        outs[2].append(k_new.reshape(1, WINDOW, 2, HEAD))
        outs[3].append(v_new.reshape(1, WINDOW, 2, HEAD))
        z, qx, k = _sample_in(hs, cos_s, sin_s, wl, l, e256, n_tok)
        oa, s_s = _sample_hgrn(z, s_hgrn, wl['lbp'], l, n_tok)
        ob0, ob1, conv_s = _sample_conv(z, state_conv, conv_s, wl, l, e256, n_tok)
        ox, k_s, v_s = _sample_attn(qx, k, z, kc, vc, wl['sink_col'], bias_s, l, n_tok)
        h_mid_s = _sample_out(hs, oa, z, ob0, ob1, ox, wl, l, e256, n_tok)
        hs = _ffn(h_mid_s, p_s, wl, l)
        outs[4].append(s_s.T.reshape(n_seq, 4, HEAD, HEAD))
        outs[6].append(k_s.reshape(n_seq, w_buf, 2, HEAD))
        outs[7].append(v_s.reshape(n_seq, w_buf, 2, HEAD))
    stacked = [conv_s if i == 5 else jnp.stack(o) for i, o in enumerate(outs)]
    return (hp[None], hs.reshape(n_seq, n_tok, D_MODEL)) + tuple(stacked)
```

```python
import functools

import numpy as np
import jax
import jax.numpy as jnp
from jax import lax
from jax.experimental import pallas as pl
from jax.experimental.pallas import tpu as pltpu

f32 = jnp.float32
bf16 = jnp.bfloat16

D_MODEL = 1024
HEAD = 64
A_W = 256
B_W = 256
C_W = 512
KV_W = 128
IN_COLS = 4 * A_W + 2 * B_W + C_W + 2 * KV_W
D_FF = 2816
PLE_DIM = 256
WINDOW = 128
CONV_WIDTH = 31
ROPE_THETA = 10000.0
EPS = 1e-6

TILE = 512
CUM_ROWS = 256
CHUNK = 64
LEVELS = (32, 16, 8, 4, 2, 1)
SAFE_EXPONENT = 80.0
HIST = 32
FFN_TILE = 512
FFN_SLABS = 4
LOG2E = 1.4426950408889634
VMEM_LIMIT = 52 * 1024 * 1024
LAYER_VMEM_LIMIT = 58 * 1024 * 1024


def _split2(x):
    hi = x.astype(bf16)
    lo = (x - hi.astype(f32)).astype(bf16)
    return hi, lo


def _split3(x):
    hi = x.astype(bf16)
    r = x - hi.astype(f32)
    mid = r.astype(bf16)
    lo = (r - mid.astype(f32)).astype(bf16)
    return hi, mid, lo


def _seg_sum(x, e):
    hi, lo = _split2(x)
    return (jnp.dot(hi, e, preferred_element_type=f32) + jnp.dot(lo, e, preferred_element_type=f32))


def _sigmoid(x):
    return 1.0 / (1.0 + jnp.exp(-x))


def _rmsnorm_rows(x, g):
    return x * lax.rsqrt(jnp.mean(x * x, axis=-1, keepdims=True) + EPS) * g


def _seg_stat(x, e):
    return jnp.dot(x.astype(bf16), e, preferred_element_type=f32)


def _head_norm(x, e, g):
    ms = _seg_stat(x * x, e) * (1.0 / HEAD)
    return x * lax.rsqrt(ms + EPS) * g


def _rope(x, cos, sin_signed):
    w = x.shape[-1]
    lane = lax.broadcasted_iota(jnp.int32, x.shape, 1)
    first = (lane % HEAD) < (HEAD // 2)
    swapped = jnp.where(first, pltpu.roll(x, w - HEAD // 2, 1), pltpu.roll(x, HEAD // 2, 1))
    return x * cos + swapped * sin_signed


def _log_forget(zf, log_lb, log_1m_lb):
    ls = jnp.minimum(zf, 0.0) - jnp.log(1.0 + jnp.exp(-jnp.abs(zf)))
    b = log_1m_lb + ls
    return jnp.maximum(log_lb, b) + jnp.log(1.0 + jnp.exp(-jnp.abs(log_lb - b))), ls


def _group_ln_silu(y, e, g, b):
    mu = _seg_sum(y, e) * (1.0 / HEAD)
    yc = y - mu
    var = _seg_stat(yc * yc, e) * (1.0 / HEAD)
    yn = yc * lax.rsqrt(var + EPS) * g + b
    return yn * _sigmoid(yn)


def _np_constants():
    lane = np.arange(256)
    e256 = (lane[:, None] // HEAD == lane[None, :] // HEAD).astype(np.float32)
    t = np.arange(CUM_ROWS)
    ltri = ((t[:, None] // CHUNK == t[None, :] // CHUNK) & (t[None, :] <= t[:, None])).astype(np.float32)
    tt = np.arange(CHUNK)[:, None]
    ss = (np.arange(256) % CHUNK)[None, :]
    lvl = []
    for m in LEVELS:
        lvl.append(((tt // (2 * m) == ss // (2 * m)) & (tt % (2 * m) >= m) & (ss % (2 * m) < m)).astype(np.float32))
    lvl.append((ss <= tt).astype(np.float32))
    lvl = np.stack(lvl)
    hm = np.stack([np.broadcast_to((lane // HEAD == h)[None, :], (CHUNK, 256)) for h in range(4)]).astype(np.float32)
    keys = np.arange(2 * WINDOW)[:, None]
    qry = np.arange(4 * WINDOW)[None, :] % WINDOW
    band = (keys >= qry) & (keys <= qry + WINDOW)
    bias = np.where(np.stack([band & (keys >= WINDOW), band]), 0.0, -np.inf).astype(np.float32)
    return e256, ltri, lvl, hm, bias


def _level_ref(g, m):
    n_rows = g.shape[0]
    if 2 * m >= 8:
        g3 = g.reshape(n_rows // (2 * m), 2 * m, g.shape[1])
        return jnp.broadcast_to(g3[:, m - 1:m, :], g3.shape).reshape(g.shape)
    g3 = g.reshape(n_rows // 8, 8, g.shape[1])
    sub = lax.broadcasted_iota(jnp.int32, g3.shape, 1)
    out = None
    for blk in range(8 // (2 * m)):
        r = blk * 2 * m + m - 1
        piece = jnp.broadcast_to(g3[:, r:r + 1, :], g3.shape)
        out = piece if out is None else jnp.where(sub >= blk * 2 * m, piece, out)
    return out.reshape(g.shape)


def _hgrn_tile(q, zf, v, lbp_ref, st_ref, st0_ref, o_ref, e, ltri_ref, lvl_ref, hm_ref):
    log_lb, log_1m_lb, one_m_lb = lbp_ref[0:1, :], lbp_ref[1:2, :], lbp_ref[2:3, :]
    logf, ls = _log_forget(zf, log_lb, log_1m_lb)
    kin = one_m_lb * jnp.exp(ls - zf)
    parts = jnp.concatenate(_split3(logf), axis=1)
    g3 = jnp.concatenate([jnp.dot(ltri_ref[...], parts[r:r + CUM_ROWS], preferred_element_type=f32)
                          for r in range(0, TILE, CUM_ROWS)], axis=0)
    g = g3[:, 0:A_W] + g3[:, A_W:2 * A_W] + g3[:, 2 * A_W:3 * A_W]
    v_b = v.astype(bf16)
    vt_b = v.T.astype(bf16)
    e_mid = g - _level_ref(g, CHUNK // 2)
    safe = jnp.max(jnp.abs(e_mid)) < SAFE_EXPONENT

    def expand(rows_b):
        return jnp.concatenate([rows_b * hm_ref[h] for h in range(4)], axis=0)

    def finish_chunk(c, att, q_in, k_hat):
        r0, r1 = c * CHUNK, (c + 1) * CHUNK
        st = st_ref[...]
        o = jnp.dot(att.astype(bf16), expand(v_b[r0:r1]), preferred_element_type=f32)
        o = o + lax.dot_general(q_in, st.astype(bf16) * e, (((1,), (1,)), ((), ())), preferred_element_type=f32)
        zero = jnp.zeros((CHUNK, A_W), bf16)
        pair = (c // 2) * 2 * CHUNK
        rhs = jnp.concatenate([k_hat, zero] if c % 2 == 0 else [zero, k_hat], axis=0)
        upd = jnp.dot(vt_b[:, pair:pair + 2 * CHUNK], rhs, preferred_element_type=f32)
        st_ref[...] = st * jnp.exp(g[r1 - 1:r1, :]) + upd
        o_ref[r0:r1, :] = o

    st0_ref[...] = st_ref[...]
    qw = q * jnp.exp(e_mid)
    kw = kin * jnp.exp(-e_mid)
    q_t, k_t = qw.astype(bf16), kw.astype(bf16)
    chunks = [(c * CHUNK, (c + 1) * CHUNK) for c in range(TILE // CHUNK)]
    atts = [lax.dot_general(q_t[r0:r1], expand(k_t[r0:r1]), (((1,), (1,)), ((), ())), preferred_element_type=f32)
            for r0, r1 in chunks]
    atts = [jnp.where(lvl_ref[len(LEVELS)] > 0.5, a, 0.0) for a in atts]
    for c, (r0, r1) in enumerate(chunks):
        g_mid, g_end = g[r0 + CHUNK // 2 - 1:r0 + CHUNK // 2, :], g[r1 - 1:r1, :]
        finish_chunk(c, atts[c], (qw[r0:r1] * jnp.exp(g_mid)).astype(bf16),
                     (kw[r0:r1] * jnp.exp(g_end - g_mid)).astype(bf16))

    def redo_if_unsafe():
        @pl.when(jnp.logical_not(safe))
        def _():
            st_ref[...] = st0_ref[...]
            q_levels, k_levels = [], []
            for m in LEVELS:
                w = jnp.exp(-jnp.abs(g - _level_ref(g, m)))
                q_levels.append((q * w).astype(bf16))
                k_levels.append((kin * w).astype(bf16))
            q_inter = (q * jnp.exp(g)).astype(bf16)
            for c in range(TILE // CHUNK):
                r0, r1 = c * CHUNK, (c + 1) * CHUNK
                att = jnp.zeros((CHUNK, 4 * CHUNK), f32)
                for li in range(len(LEVELS)):
                    a = lax.dot_general(q_levels[li][r0:r1], expand(k_levels[li][r0:r1]),
                                        (((1,), (1,)), ((), ())), preferred_element_type=f32)
                    att = att + a * lvl_ref[li]
                k_hat = (kin[r0:r1] * jnp.exp(g[r1 - 1:r1, :] - g[r0:r1])).astype(bf16)
                finish_chunk(c, att, q_inter[r0:r1], k_hat)
            o_ref[...] = o_ref[...] + _seg_sum(q * kin, e) * v

    return redo_if_unsafe


def _conv_tile(u, hist_ref, cw_ref, cvec_ref, e):
    n = u.shape[0]
    hist_ref[HIST:HIST + n, :] = u
    lo = HIST - (CONV_WIDTH - 1)
    y = cvec_ref[0:1, :]
    for r in range(8):
        rows = n if r == 0 else n + 8
        part = None
        for o in range(lo, lo + CONV_WIDTH):
            if o % 8 == r:
                term = hist_ref[o - r:o - r + rows, :] * cw_ref[o - lo:o - lo + 1, :]
                part = term if part is None else part + term
        y = y + part[r:r + n]
    return _group_ln_silu(y, e, cvec_ref[1:2, :], cvec_ref[2:3, :])


def _attn_tile(q_t, k_b, v_t, sink_ref, biases):
    zero = jnp.zeros((HEAD, WINDOW), f32)
    inst = [(b, g) for b in range(TILE // WINDOW) for g in range(2)]
    sinks = [jnp.concatenate([jnp.full((1, WINDOW), sink_ref[h] * LOG2E, f32) for h in range(4 * g, 4 * g + 4)],
                             axis=1) for g in range(2)]

    def rhs(b, g):
        cols = []
        for h in range(4 * g, 4 * g + 4):
            q_h = q_t[h * HEAD:(h + 1) * HEAD, b * WINDOW:(b + 1) * WINDOW]
            cols.append(jnp.concatenate([q_h, zero] if g == 0 else [zero, q_h], axis=0))
        return jnp.concatenate(cols, axis=1).astype(bf16)

    s = [jnp.dot(k_b[b * WINDOW:(b + 2) * WINDOW], rhs(b, g), preferred_element_type=f32) + biases[b]
         for b, g in inst]
    mx = [jnp.maximum(jnp.max(s[n], axis=0, keepdims=True), sinks[g]) for n, (b, g) in enumerate(inst)]
    pr = [jnp.exp2(s[n] - mx[n]) for n in range(len(inst))]
    rden = [1.0 / (jnp.sum(pr[n], axis=0, keepdims=True) + jnp.exp2(sinks[g] - mx[n]))
            for n, (b, g) in enumerate(inst)]
    o_t = [jnp.dot(v_t[g * HEAD:(g + 1) * HEAD, b * WINDOW:(b + 2) * WINDOW], pr[n].astype(bf16),
                   preferred_element_type=f32) * rden[n] for n, (b, g) in enumerate(inst)]
    rows = []
    for b in range(TILE // WINDOW):
        pairs = []
        for g in range(2):
            o = o_t[2 * b + g]
            for j in range(2):
                pair_t = jnp.concatenate([o[:, (2 * j) * WINDOW:(2 * j + 1) * WINDOW],
                                          o[:, (2 * j + 1) * WINDOW:(2 * j + 2) * WINDOW]], axis=0)
                pairs.append(pair_t.T)
        rows.append(jnp.concatenate(pairs, axis=1))
    return jnp.concatenate(rows, axis=0)


def _mixer_kernel(layer, x_ref, rope_row_ref, rope_tile_ref, win_ref, wout_ref, nmix_ref, lbp_ref, aon_ref, cw_ref,
                  cvec_ref, qk_ref, sinks_ref, e_ref, ltri_ref, lvl_ref, hm_ref, bias_ref,
                  h_ref, st_ref, conv_ref, kn_ref, vn_ref, z_ref, hist_ref, o_ref, st0_ref):
    i = pl.program_id(0)
    sink_ref = sinks_ref.at[layer]

    @pl.when(i == 0)
    def _():
        st_ref[...] = jnp.zeros_like(st_ref)
        hist_ref[0:HIST, :] = jnp.zeros((HIST, B_W), f32)
        kn_ref[...] = jnp.zeros_like(kn_ref)
        vn_ref[...] = jnp.zeros_like(vn_ref)

    x = x_ref[...]
    xn = _rmsnorm_rows(x, nmix_ref[...])
    z_ref[...] = jnp.dot(xn.astype(bf16), win_ref[...], preferred_element_type=f32)
    e = e_ref[...]

    redo_hgrn_if_unsafe = _hgrn_tile(z_ref[:, 0:A_W], z_ref[:, A_W:2 * A_W], z_ref[:, 2 * A_W:3 * A_W], lbp_ref,
                                     st_ref, st0_ref, o_ref, e, ltri_ref, lvl_ref, hm_ref)

    c0 = 4 * A_W
    u = z_ref[:, c0:c0 + B_W] * _sigmoid(z_ref[:, c0 + B_W:c0 + 2 * B_W])
    ob = _conv_tile(u, hist_ref, cw_ref, cvec_ref, e)
    conv_ref[...] = hist_ref[TILE + HIST - (CONV_WIDTH - 1):TILE + HIST, :]
    hist_ref[0:HIST, :] = hist_ref[TILE:TILE + HIST, :]

    c1 = c0 + 2 * B_W
    cos_a, sin_a = rope_tile_ref[0, pl.ds(i, 1), :], rope_tile_ref[1, pl.ds(i, 1), :]
    cos = cos_a * rope_row_ref[0] - sin_a * rope_row_ref[1]
    sin = sin_a * rope_row_ref[2] + cos_a * rope_row_ref[3]
    qn = jnp.concatenate([_head_norm(z_ref[:, c1 + j * 256:c1 + (j + 1) * 256], e,
                                     jnp.concatenate([qk_ref[0:1, :]] * 2, axis=1)) for j in range(2)], axis=1)
    q = _rope(qn, jnp.concatenate([cos] * 4, axis=1), jnp.concatenate([sin] * 4, axis=1)) * (LOG2E * HEAD ** -0.5)
    kk = _rope(_head_norm(z_ref[:, c1 + C_W:c1 + C_W + KV_W], e[0:KV_W, 0:KV_W], qk_ref[1:2, :]), cos, sin)
    vv = z_ref[:, c1 + C_W + KV_W:c1 + C_W + 2 * KV_W]
    kcat = jnp.concatenate([kn_ref[...], kk], axis=0)
    vcat = jnp.concatenate([vn_ref[...], vv], axis=0)
    q_t = q.T
    k_b = kcat.astype(bf16)
    v_t = vcat.T.astype(bf16)
    biases = [bias_ref[jnp.where(i > 0, 1, 0)]] + [bias_ref[1]] * (TILE // WINDOW - 1)
    oc = _attn_tile(q_t, k_b, v_t, sink_ref, biases)
    kn_ref[...] = kk[TILE - WINDOW:TILE]
    vn_ref[...] = vv[TILE - WINDOW:TILE]

    redo_hgrn_if_unsafe()
    gate = z_ref[:, 3 * A_W:4 * A_W]
    oa = _head_norm(o_ref[...], e, aon_ref[...]) * (gate * _sigmoid(gate))
    mix = jnp.concatenate([oa, ob, oc], axis=1).astype(bf16)
    h_ref[...] = x + jnp.dot(mix, wout_ref[...], preferred_element_type=f32)


def _const_spec(shape):
    nd = len(shape)
    return pl.BlockSpec(shape, lambda i, _n=nd: (0,) * _n)


def _layer_spec(arr, l, **kw):
    shape = arr.shape[1:]
    return pl.BlockSpec((None,) + shape, lambda i, _l=l, _n=len(shape): (_l,) + (0,) * _n, **kw)


def _ffn_kernel(h_ref, p_ref, wg_ref, wu_ref, wd_ref, wpg_ref, wpp_ref, norms_ref, y_ref):
    n = h_ref.shape[0]
    slabs = [slice(r, r + n // FFN_SLABS) for r in range(0, n, n // FFN_SLABS)]
    h = [h_ref[s, :] for s in slabs]
    hn = [_rmsnorm_rows(a, norms_ref[0:1, :]).astype(bf16) for a in h]
    gt = [jnp.dot(a, wg_ref[...], preferred_element_type=f32) for a in hn]
    up = [jnp.dot(a, wu_ref[...], preferred_element_type=f32) for a in hn]
    act = [(g * _sigmoid(g) * u).astype(bf16) for g, u in zip(gt, up)]
    h = [a + jnp.dot(b, wd_ref[...], preferred_element_type=f32) for a, b in zip(h, act)]
    pn = [_rmsnorm_rows(a, norms_ref[1:2, :]).astype(bf16) for a in h]
    gate = [_sigmoid(jnp.dot(a, wpg_ref[...], preferred_element_type=f32)) for a in pn]
    for s, a, g in zip(slabs, h, gate):
        y_ref[s, :] = a + g * jnp.dot(p_ref[s, :].astype(bf16), wpp_ref[...], preferred_element_type=f32)


def _ffn(h, p, wl, l):
    n = h.shape[0]
    tile = min(FFN_TILE, n)
    assert n % tile == 0
    row_spec = lambda w: pl.BlockSpec((tile, w), lambda i: (i, 0))
    ws = [wl['w_gate'], wl['w_up'], wl['w_down'], wl['w_ple_gate'], wl['w_ple_proj']]
    return pl.pallas_call(
        _ffn_kernel, grid=(n // tile,),
        in_specs=[row_spec(D_MODEL), pl.BlockSpec((None, tile, PLE_DIM), lambda i, _l=l: (_l, i, 0))]
        + [_layer_spec(a, l, pipeline_mode=pl.Buffered(1)) for a in ws] + [_layer_spec(wl['fnorms'], l)],
        out_specs=row_spec(D_MODEL), out_shape=jax.ShapeDtypeStruct((n, D_MODEL), f32),
        compiler_params=pltpu.CompilerParams(dimension_semantics=("arbitrary",), vmem_limit_bytes=VMEM_LIMIT),
        name="ffn_ple",
    )(h, p, *ws, wl['fnorms'])


N_MIXER_INPUTS = 17


def _layer_kernel(layer, *refs):
    mixer_in, (p_ref, wg_ref, wu_ref, wd_ref, wpg_ref, wpp_ref, norms_ref) = refs[:N_MIXER_INPUTS], refs[17:24]
    y_ref, st_ref, conv_ref, kn_ref, vn_ref = refs[24:29]
    z_ref, hist_ref, o_ref, st0_ref, mid_ref = refs[29:]
    _mixer_kernel(layer, *mixer_in, mid_ref, st_ref, conv_ref, kn_ref, vn_ref, z_ref, hist_ref, o_ref, st0_ref)
    _ffn_kernel(mid_ref, p_ref, wg_ref, wu_ref, wd_ref, wpg_ref, wpp_ref, norms_ref, y_ref)


def _prompt_layer(x, p, rope, wl, l, consts):
    t_len = x.shape[0]
    row_spec = lambda w: pl.BlockSpec((TILE, w), lambda i: (i, 0))
    once = dict(pipeline_mode=pl.Buffered(1))
    smalls = [wl['nmix'], wl['lbp'], wl['aon'], wl['cw'], wl['cvec'], wl['qk']]
    ffn_ws = [wl['w_gate'], wl['w_up'], wl['w_down'], wl['w_ple_gate'], wl['w_ple_proj']]
    in_specs = ([row_spec(D_MODEL), _const_spec(rope[0].shape), _const_spec(rope[1].shape),
                 _layer_spec(wl['w_in'], l, **once), _layer_spec(wl['w_out'], l, **once)]
                + [_layer_spec(a, l) for a in smalls] + [pl.BlockSpec(memory_space=pltpu.SMEM)]
                + [_const_spec(a.shape) for a in consts]
                + [pl.BlockSpec((None, TILE, PLE_DIM), lambda i, _l=l: (_l, i, 0))]
                + [_layer_spec(a, l, **once) for a in ffn_ws] + [_layer_spec(wl['fnorms'], l)])
    assert len(in_specs) == N_MIXER_INPUTS + 7
    out_shape = (jax.ShapeDtypeStruct((t_len, D_MODEL), f32),
                 jax.ShapeDtypeStruct((A_W, A_W), f32),
                 jax.ShapeDtypeStruct((CONV_WIDTH - 1, B_W), f32),
                 jax.ShapeDtypeStruct((WINDOW, KV_W), f32),
                 jax.ShapeDtypeStruct((WINDOW, KV_W), f32))
    out_specs = (row_spec(D_MODEL), _const_spec((A_W, A_W)), _const_spec((CONV_WIDTH - 1, B_W)),
                 _const_spec((WINDOW, KV_W)), _const_spec((WINDOW, KV_W)))
    return pl.pallas_call(
        functools.partial(_layer_kernel, l), grid=(t_len // TILE,), in_specs=in_specs, out_specs=out_specs,
        out_shape=out_shape,
        scratch_shapes=[pltpu.VMEM((TILE, IN_COLS), f32), pltpu.VMEM((TILE + HIST, B_W), f32),
                        pltpu.VMEM((TILE, A_W), f32), pltpu.VMEM((A_W, A_W), f32),
                        pltpu.VMEM((TILE, D_MODEL), f32)],
        compiler_params=pltpu.CompilerParams(dimension_semantics=("arbitrary",),
                                             vmem_limit_bytes=LAYER_VMEM_LIMIT),
        name="prompt_layer",
    )(x, rope[0], rope[1], wl['w_in'], wl['w_out'], *smalls, wl['sinks'], *consts, p, *ffn_ws, wl['fnorms'])


N_HEADS = C_W // HEAD


def _sample_in_kernel(n_tok, x_ref, cos_ref, sin_ref, win_ref, nmix_ref, qk_ref, e_ref, z_ref, qx_ref, k_ref,
                      piece_ref):
    n_seq = x_ref.shape[0] // n_tok
    xn = _rmsnorm_rows(x_ref[...], nmix_ref[...])
    z = jnp.dot(xn.astype(bf16), win_ref[...], preferred_element_type=f32)
    z_ref[...] = z
    e = e_ref[...]
    c1 = 4 * A_W + 2 * B_W
    cos, sin = cos_ref[...], sin_ref[...]
    qn = jnp.concatenate([_head_norm(z[:, c1 + j * 256:c1 + (j + 1) * 256], e,
                                     jnp.concatenate([qk_ref[0:1, :]] * 2, axis=1)) for j in range(2)], axis=1)
    q = _rope(qn, jnp.concatenate([cos] * 4, axis=1), jnp.concatenate([sin] * 4, axis=1)) * (HEAD ** -0.5)
    k_ref[...] = _rope(_head_norm(z[:, c1 + C_W:c1 + C_W + KV_W], e[0:KV_W, 0:KV_W], qk_ref[1:2, :]), cos, sin)
    lo_f = lax.broadcasted_iota(jnp.int32, (x_ref.shape[0], KV_W), 1) < HEAD
    for h in range(N_HEADS):
        pair, g = q[:, (h // 2) * KV_W:(h // 2 + 1) * KV_W], h // 4
        own = jnp.where(lo_f if h % 2 == 0 else jnp.logical_not(lo_f), pair, 0.0)
        piece_ref[...] = own if h % 2 == g else pltpu.roll(own, HEAD, 1)
        for t in range(n_tok):
            qx_ref[pl.ds(h * n_tok + t, n_seq, stride=N_HEADS * n_tok), :] = piece_ref[pl.ds(t, n_seq, stride=n_tok), :]


def _sample_in(x, cos, sin, wl, l, e256, n_tok):
    n = x.shape[0]
    args = (x, cos, sin, wl['w_in'], wl['nmix'], wl['qk'], e256)
    specs = [_const_spec(a.shape) for a in args]
    specs[3:6] = [_layer_spec(a, l) for a in args[3:6]]
    return pl.pallas_call(
        functools.partial(_sample_in_kernel, n_tok), grid=(1,), in_specs=specs,
        out_specs=(_const_spec((n, IN_COLS)), _const_spec((n * N_HEADS, KV_W)), _const_spec((n, KV_W))),
        out_shape=(jax.ShapeDtypeStruct((n, IN_COLS), f32), jax.ShapeDtypeStruct((n * N_HEADS, KV_W), f32),
                   jax.ShapeDtypeStruct((n, KV_W), f32)),
        scratch_shapes=[pltpu.VMEM((n, KV_W), f32)],
        compiler_params=pltpu.CompilerParams(dimension_semantics=("arbitrary",), vmem_limit_bytes=VMEM_LIMIT),
        name="sample_in",
    )(*args)


def _sample_hgrn_kernel(n_tok, q_ref, f_ref, v_ref, s_ref, lbp_ref, o_ref, so_ref, qt_scr, ft_scr, kt_scr, vt_scr,
                        ot_scr):
    n_seq = s_ref.shape[1]
    for t in range(n_tok):
        zf = f_ref[pl.ds(t, n_seq, stride=n_tok), :]
        logf, ls = _log_forget(zf, lbp_ref[0:1, :], lbp_ref[1:2, :])
        ft_scr[t] = jnp.exp(logf).T
        kt_scr[t] = (lbp_ref[2:3, :] * jnp.exp(ls - zf)).T
        qt_scr[t] = q_ref[pl.ds(t, n_seq, stride=n_tok), :].T
        vt_scr[t] = v_ref[pl.ds(t, n_seq, stride=n_tok), :].T
        ot_scr[t] = jnp.zeros((2 * HEAD, n_seq), f32)

    def body(hk, carry):
        r0 = pl.multiple_of(hk * HEAD, HEAD)
        v0 = pl.multiple_of((hk // HEAD) * HEAD, HEAD)
        blk = s_ref[pl.ds(r0, HEAD), :]
        for t in range(n_tok):
            blk = blk * ft_scr[t, pl.ds(hk, 1), :] + kt_scr[t, pl.ds(hk, 1), :] * vt_scr[t, pl.ds(v0, HEAD), :]
            ot_scr[t, pl.ds(v0, HEAD), :] = ot_scr[t, pl.ds(v0, HEAD), :] + qt_scr[t, pl.ds(hk, 1), :] * blk
        so_ref[pl.ds(r0, HEAD), :] = blk
        return carry

    lax.fori_loop(0, 2 * HEAD, body, 0, unroll=4)
    for t in range(n_tok):
        o_ref[pl.ds(t, n_seq, stride=n_tok), :] = ot_scr[t].T


def _sample_hgrn(z, state, lbp, l, n_tok):
    n = z.shape[0]
    n_seq = state.shape[2]
    blk = 2 * HEAD * HEAD
    col = lambda j0: pl.BlockSpec((n, 2 * HEAD), lambda i, _j=j0: (0, _j + i))
    return pl.pallas_call(
        functools.partial(_sample_hgrn_kernel, n_tok), grid=(2,),
        in_specs=[col(0), col(2), col(4), pl.BlockSpec((None, blk, n_seq), lambda i, _l=l: (_l, i, 0)),
                  pl.BlockSpec((None, 3, 2 * HEAD), lambda i, _l=l: (_l, 0, i))],
        out_specs=(pl.BlockSpec((n, 2 * HEAD), lambda i: (0, i)), pl.BlockSpec((blk, n_seq), lambda i: (i, 0))),
        out_shape=(jax.ShapeDtypeStruct((n, A_W), f32), jax.ShapeDtypeStruct(state.shape[1:], f32)),
        scratch_shapes=[pltpu.VMEM((n_tok, 2 * HEAD, n_seq), f32)] * 5,
        compiler_params=pltpu.CompilerParams(dimension_semantics=("arbitrary",), vmem_limit_bytes=VMEM_LIMIT),
        name="sample_hgrn",
    )(z, z, z, state, lbp)


def _sample_conv_kernel(n_tok, first, u0_ref, u1_ref, g0_ref, g1_ref, buf_ref, cw_ref, cvec_ref, e_ref, *rest):
    ob0_ref, ob1_ref, new_ref = rest[-3:]
    slots = [new_ref.at[d] for d in range(new_ref.shape[0])] if first else [new_ref]
    n_seq = buf_ref.shape[1]
    n_hist = CONV_WIDTH - 1
    us = []
    for t in range(n_tok):
        rows = pl.ds(t, n_seq, stride=n_tok)
        u = jnp.concatenate([u0_ref[rows, :], u1_ref[rows, :]], axis=1)
        g = jnp.concatenate([g0_ref[rows, :], g1_ref[rows, :]], axis=1)
        us.append(u * _sigmoid(g))

    def slab(j):
        return buf_ref[j] if j < n_hist else us[j - n_hist]

    for t in range(n_tok):
        y = cvec_ref[0:1, :]
        for j in range(CONV_WIDTH):
            y = y + slab(t + j) * cw_ref[j:j + 1, :]
        ob = _group_ln_silu(y, e_ref[...], cvec_ref[1:2, :], cvec_ref[2:3, :])
        ob0_ref[pl.ds(t, n_seq, stride=n_tok), :] = ob[:, 0:KV_W]
        ob1_ref[pl.ds(t, n_seq, stride=n_tok), :] = ob[:, KV_W:B_W]
    for dst in slots:
        for j in range(n_hist):
            dst[j] = slab(j + n_tok)


def _sample_conv(z, buf, new_all, wl, l, e256, n_tok):
    n = z.shape[0]
    col = lambda j: pl.BlockSpec((n, KV_W), lambda i, _j=j: (0, _j))
    half = jax.ShapeDtypeStruct((n, KV_W), f32)
    in_specs = [col(8), col(9), col(10), col(11), _layer_spec(buf, l), _layer_spec(wl['cw'], l),
                _layer_spec(wl['cvec'], l), _const_spec(e256.shape)]
    args = [z, z, z, z, buf, wl['cw'], wl['cvec'], e256]
    if new_all is None:
        new_spec, aliases = _const_spec(buf.shape), {}
    else:
        in_specs.append(pl.BlockSpec(memory_space=pl.ANY))
        args.append(new_all)
        new_spec, aliases = _layer_spec(buf, l), {len(args) - 1: 2}
    return pl.pallas_call(
        functools.partial(_sample_conv_kernel, n_tok, new_all is None), grid=(1,), in_specs=in_specs,
        out_specs=(_const_spec((n, KV_W)), _const_spec((n, KV_W)), new_spec),
        out_shape=(half, half, jax.ShapeDtypeStruct(buf.shape, f32)),
        input_output_aliases=aliases,
        compiler_params=pltpu.CompilerParams(dimension_semantics=("arbitrary",), vmem_limit_bytes=VMEM_LIMIT),
        name="sample_conv",
    )(*args)


SEQ_BLOCK = 32


def _sample_attn_kernel(n_tok, qx_ref, k_ref, v_ref, kc_ref, vc_ref, sink_ref, bias_ref, ox_ref, kn_ref, vn_ref):
    q_rows = N_HEADS * n_tok
    sink, bias = sink_ref[...], bias_ref[...]
    seqs = range(SEQ_BLOCK)
    k_all = [jnp.concatenate([kc_ref[b], k_ref[b * n_tok:(b + 1) * n_tok, :]], axis=0) for b in seqs]
    v_all = [jnp.concatenate([vc_ref[b], v_ref[b * n_tok:(b + 1) * n_tok, :]], axis=0) for b in seqs]
    s = [lax.dot_general(qx_ref[b * q_rows:(b + 1) * q_rows, :].astype(bf16), k_all[b].astype(bf16),
                         (((1,), (1,)), ((), ())), preferred_element_type=f32) + bias for b in seqs]
    mx = [jnp.maximum(jnp.max(s[b], axis=-1, keepdims=True), sink) for b in seqs]
    pr = [jnp.exp(s[b] - mx[b]) for b in seqs]
    rden = [1.0 / (jnp.sum(pr[b], axis=-1, keepdims=True) + jnp.exp(sink - mx[b])) for b in seqs]
    for b in seqs:
        ox_ref[b * q_rows:(b + 1) * q_rows, :] = jnp.dot(pr[b].astype(bf16), v_all[b].astype(bf16),
                                                         preferred_element_type=f32) * rden[b]
    for b in seqs:
        kn_ref[b] = k_all[b][n_tok:, :]
        vn_ref[b] = v_all[b][n_tok:, :]


def _sample_attn(qx, k, z, kc, vc, sink_col, bias, l, n_tok):
    n_seq = kc.shape[1]
    assert n_seq % SEQ_BLOCK == 0
    rows = SEQ_BLOCK * n_tok
    q_rows = SEQ_BLOCK * N_HEADS * n_tok
    v_col = (4 * A_W + 2 * B_W + C_W + KV_W) // KV_W
    cache_in = pl.BlockSpec((None, SEQ_BLOCK) + kc.shape[2:], lambda i, _l=l: (_l, i, 0, 0))
    cache_out = pl.BlockSpec((SEQ_BLOCK,) + kc.shape[2:], lambda i: (i, 0, 0))
    return pl.pallas_call(
        functools.partial(_sample_attn_kernel, n_tok), grid=(n_seq // SEQ_BLOCK,),
        in_specs=[pl.BlockSpec((q_rows, KV_W), lambda i: (i, 0)), pl.BlockSpec((rows, KV_W), lambda i: (i, 0)),
                  pl.BlockSpec((rows, KV_W), lambda i: (i, v_col)), cache_in, cache_in,
                  _layer_spec(sink_col, l), _const_spec(bias.shape)],
        out_specs=(pl.BlockSpec((q_rows, KV_W), lambda i: (i, 0)), cache_out, cache_out),
        out_shape=(jax.ShapeDtypeStruct(qx.shape, f32), jax.ShapeDtypeStruct(kc.shape[1:], f32),
                   jax.ShapeDtypeStruct(vc.shape[1:], f32)),
        compiler_params=pltpu.CompilerParams(dimension_semantics=("arbitrary",), vmem_limit_bytes=VMEM_LIMIT),
        name="sample_attn",
    )(qx, k, z, kc, vc, sink_col, bias)


def _sample_out_kernel(n_tok, x_ref, oa_ref, g_ref, ob0_ref, ob1_ref, ox_ref, wout_ref, aon_ref, e_ref, h_ref,
                       oc_ref):
    n_seq = x_ref.shape[0] // n_tok
    gate = g_ref[...]
    oa = _head_norm(oa_ref[...], e_ref[...], aon_ref[...]) * (gate * _sigmoid(gate))
    lo_f = lax.broadcasted_iota(jnp.int32, (n_seq, KV_W), 1) < HEAD
    for p in range(N_HEADS // 2):
        g = p // 2
        for t in range(n_tok):
            even = ox_ref[pl.ds((2 * p) * n_tok + t, n_seq, stride=N_HEADS * n_tok), :]
            odd = ox_ref[pl.ds((2 * p + 1) * n_tok + t, n_seq, stride=N_HEADS * n_tok), :]
            even = even if g == 0 else pltpu.roll(even, HEAD, 1)
            odd = odd if g == 1 else pltpu.roll(odd, HEAD, 1)
            oc_ref[p, pl.ds(t, n_seq, stride=n_tok), :] = jnp.where(lo_f, even, odd)
    mix = jnp.concatenate([oa, ob0_ref[...], ob1_ref[...]] + [oc_ref[p] for p in range(N_HEADS // 2)],
                          axis=1).astype(bf16)
    h_ref[...] = x_ref[...] + jnp.dot(mix, wout_ref[...], preferred_element_type=f32)


def _sample_out(x, oa, z, ob0, ob1, ox, wl, l, e256, n_tok):
    n = x.shape[0]
    gate_spec = pl.BlockSpec((n, A_W), lambda i: (0, 3))
    args = (x, oa, z, ob0, ob1, ox, wl['w_out'], wl['aon'], e256)
    specs = [_const_spec(a.shape) for a in args]
    specs[2] = gate_spec
    specs[6:8] = [_layer_spec(a, l) for a in args[6:8]]
    return pl.pallas_call(
        functools.partial(_sample_out_kernel, n_tok), grid=(1,), in_specs=specs,
        out_specs=_const_spec((n, D_MODEL)), out_shape=jax.ShapeDtypeStruct((n, D_MODEL), f32),
        scratch_shapes=[pltpu.VMEM((N_HEADS // 2, n, KV_W), f32)],
        compiler_params=pltpu.CompilerParams(dimension_semantics=("arbitrary",), vmem_limit_bytes=VMEM_LIMIT),
        name="sample_out",
    )(*args)


def _rope_cos_sin(pos):
    half = HEAD // 2
    inv = ROPE_THETA ** (-jnp.arange(half, dtype=f32) / half)
    ang = pos[:, None] * inv[None, :]
    cos, sin = jnp.cos(ang), jnp.sin(ang)
    return jnp.tile(cos, (1, 4)), jnp.concatenate([-sin, sin, -sin, sin], axis=1)


def _rope_prompt_tables(t_len):
    half = HEAD // 2
    inv = ROPE_THETA ** (-jnp.arange(half, dtype=f32) / half)
    sign = jnp.concatenate([-jnp.ones((half,), f32), jnp.ones((half,), f32)] * 2)
    ang_r = jnp.arange(TILE, dtype=f32)[:, None] * inv[None, :]
    ang_t = (jnp.arange(t_len // TILE, dtype=f32) * TILE)[:, None] * inv[None, :]
    cos_r, sin_r = jnp.tile(jnp.cos(ang_r), (1, 4)), jnp.tile(jnp.sin(ang_r), (1, 4))
    rows = jnp.stack([cos_r, sin_r, cos_r * sign, sin_r * sign])
    tiles = jnp.stack([jnp.tile(jnp.cos(ang_t), (1, 4)), jnp.tile(jnp.sin(ang_t), (1, 4))])
    return rows, tiles


def _stacked_weights(lbs, n_tok, w_in, a_onorm, conv_w, conv_b, conv_ln_g, conv_ln_b, q_norm, k_norm, sinks, w_out,
                     norm_mix, norm_ffn, w_gate, w_up, w_down, ple_norm, w_ple_gate, w_ple_proj):
    return dict(
        w_in=w_in.astype(bf16), w_out=w_out.astype(bf16), w_gate=w_gate.astype(bf16), w_up=w_up.astype(bf16),
        w_down=w_down.astype(bf16), w_ple_gate=w_ple_gate.astype(bf16), w_ple_proj=w_ple_proj.astype(bf16),
        nmix=norm_mix[:, None, :],
        lbp=jnp.stack([jnp.log(lbs), jnp.log1p(-lbs), 1.0 - lbs], axis=1),
        aon=jnp.tile(a_onorm, (1, 4))[:, None, :],
        cw=conv_w,
        cvec=jnp.stack([conv_b, conv_ln_g, conv_ln_b], axis=1),
        qk=jnp.stack([jnp.tile(q_norm, (1, 2)), jnp.tile(k_norm, (1, 2))], axis=1),
        sinks=sinks,
        sink_col=jnp.repeat(sinks, n_tok, axis=1)[:, :, None],
        fnorms=jnp.stack([norm_ffn, ple_norm], axis=1),
    )


def kernel(x_prompt, x_sample, state_hgrn, state_conv, cache_swa_k, cache_swa_v, p_prompt, p_sample, a_lower, w_in, a_onorm, conv_w, conv_b, conv_ln_g, conv_ln_b, q_norm, k_norm, sinks, w_out, norm_mix, norm_ffn, w_gate, w_up, w_down, ple_norm, w_ple_gate, w_ple_proj):
    depth = w_in.shape[0]
    t_len = x_prompt.shape[1]
    n_seq, n_tok = x_sample.shape[0], x_sample.shape[1]
    past_len = 16384
    assert x_prompt.shape[0] == 1 and t_len % TILE == 0

    lbs = jnp.cumsum(jax.nn.softmax(a_lower.astype(f32), axis=0), axis=0)
    lbs = lbs - lbs[0:1]
    consts_np = _np_constants()
    e256 = jnp.asarray(consts_np[0], bf16)
    consts = (e256, jnp.asarray(consts_np[1], bf16), jnp.asarray(consts_np[2], f32), jnp.asarray(consts_np[3], bf16),
              jnp.asarray(consts_np[4], f32))

    rope_p = _rope_prompt_tables(t_len)
    cos_s, sin_s = _rope_cos_sin(past_len + jnp.arange(n_tok, dtype=f32))
    cos_s, sin_s = jnp.tile(cos_s, (n_seq, 1)), jnp.tile(sin_s, (n_seq, 1))
    w_buf = cache_swa_k.shape[2]
    q_tok = (np.arange(N_HEADS * n_tok) % n_tok)[:, None]
    rel = q_tok + w_buf - np.arange(w_buf + n_tok)[None, :]
    bias_s = jnp.asarray(np.where((rel >= 0) & (rel <= WINDOW), 0.0, -np.inf), f32)

    wl = _stacked_weights(lbs, n_tok, w_in, a_onorm, conv_w, conv_b, conv_ln_g, conv_ln_b, q_norm, k_norm, sinks,
                          w_out, norm_mix, norm_ffn, w_gate, w_up, w_down, ple_norm, w_ple_gate, w_ple_proj)
    p_p = p_prompt.reshape(depth, t_len, PLE_DIM)
    p_s = p_sample.reshape(depth, n_seq * n_tok, PLE_DIM)
    s_hgrn = jnp.swapaxes(state_hgrn.reshape(depth, n_seq, -1), 1, 2)
    kc = cache_swa_k.reshape(depth, n_seq, w_buf, KV_W)
    vc = cache_swa_v.reshape(depth, n_seq, w_buf, KV_W)

    hp = x_prompt[0]
    hs = x_sample.reshape(n_seq * n_tok, D_MODEL)
    outs = [[] for _ in range(8)]
    conv_in = jnp.swapaxes(state_conv, 1, 2)
    conv_s = None
    for l in range(depth):
        hp, st, conv_new, k_new, v_new = _prompt_layer(hp, p_p, rope_p, wl, l, consts)
        s_new = jnp.stack([st[h * HEAD:(h + 1) * HEAD, h * HEAD:(h + 1) * HEAD].T for h in range(4)])
        outs[0].append(s_new[None])
        outs[1].append(conv_new[None])
        outs[2].append(k_new.reshape(1, WINDOW, 2, HEAD))
        outs[3].append(v_new.reshape(1, WINDOW, 2, HEAD))
        z, qx, k = _sample_in(hs, cos_s, sin_s, wl, l, e256, n_tok)
        oa, s_s = _sample_hgrn(z, s_hgrn, wl['lbp'], l, n_tok)
        ob0, ob1, conv_s = _sample_conv(z, conv_in, conv_s, wl, l, e256, n_tok)
        ox, k_s, v_s = _sample_attn(qx, k, z, kc, vc, wl['sink_col'], bias_s, l, n_tok)
        h_mid_s = _sample_out(hs, oa, z, ob0, ob1, ox, wl, l, e256, n_tok)
        hs = _ffn(h_mid_s, p_s, wl, l)
        outs[4].append(s_s.T.reshape(n_seq, 4, HEAD, HEAD))
        outs[6].append(k_s.reshape(n_seq, w_buf, 2, HEAD))
        outs[7].append(v_s.reshape(n_seq, w_buf, 2, HEAD))
    stacked = [jnp.swapaxes(conv_s, 1, 2) if i == 5 else jnp.stack(o) for i, o in enumerate(outs)]
    return (hp[None], hs.reshape(n_seq, n_tok, D_MODEL)) + tuple(stacked)
```

```python
import functools

import numpy as np
import jax
import jax.numpy as jnp
from jax import lax
from jax.experimental import pallas as pl
from jax.experimental.pallas import tpu as pltpu

f32 = jnp.float32
bf16 = jnp.bfloat16

D_MODEL = 1024
HEAD = 64
A_W = 256
B_W = 256
C_W = 512
KV_W = 128
IN_COLS = 4 * A_W + 2 * B_W + C_W + 2 * KV_W
D_FF = 2816
PLE_DIM = 256
WINDOW = 128
CONV_WIDTH = 31
ROPE_THETA = 10000.0
EPS = 1e-6

TILE = 512
CUM_ROWS = 256
CHUNK = 64
LEVELS = (32, 16, 8, 4, 2, 1)
SAFE_EXPONENT = 80.0
HIST = 32
FFN_TILE = 512
FFN_SLABS = 4
LOG2E = 1.4426950408889634
VMEM_LIMIT = 52 * 1024 * 1024
LAYER_VMEM_LIMIT = 58 * 1024 * 1024


def _split2(x):
    hi = x.astype(bf16)
    lo = (x - hi.astype(f32)).astype(bf16)
    return hi, lo


def _seg_sum(x, e):
    hi, lo = _split2(x)
    return (jnp.dot(hi, e, preferred_element_type=f32) + jnp.dot(lo, e, preferred_element_type=f32))


def _sigmoid(x):
    return 1.0 / (1.0 + jnp.exp(-x))


def _rmsnorm_rows(x, g):
    return x * lax.rsqrt(jnp.mean(x * x, axis=-1, keepdims=True) + EPS) * g


def _seg_stat(x, e):
    return jnp.dot(x.astype(bf16), e, preferred_element_type=f32)


def _mean_matrix(e):
    return e * (1.0 / HEAD)


def _head_norm(x, em, g):
    return x * lax.rsqrt(_seg_stat(x * x, em) + EPS) * g


def _rope(x, cos, sin_signed):
    w = x.shape[-1]
    lane = lax.broadcasted_iota(jnp.int32, x.shape, 1)
    first = (lane % HEAD) < (HEAD // 2)
    swapped = jnp.where(first, pltpu.roll(x, w - HEAD // 2, 1), pltpu.roll(x, HEAD // 2, 1))
    return x * cos + swapped * sin_signed


def _log_forget(zf, log_lb, log_1m_lb):
    ls = jnp.minimum(zf, 0.0) - jnp.log(1.0 + jnp.exp(-jnp.abs(zf)))
    b = log_1m_lb + ls
    return jnp.maximum(log_lb, b) + jnp.log(1.0 + jnp.exp(-jnp.abs(log_lb - b))), ls


def _group_ln_silu(y, em, g, b):
    yc = y - _seg_stat(y, em)
    var = _seg_stat(yc * yc, em)
    yn = yc * lax.rsqrt(var + EPS) * g + b
    return yn * _sigmoid(yn)


def _np_constants():
    lane = np.arange(256)
    e256 = (lane[:, None] // HEAD == lane[None, :] // HEAD).astype(np.float32)
    t = np.arange(CUM_ROWS)
    ltri = ((t[:, None] // CHUNK == t[None, :] // CHUNK) & (t[None, :] <= t[:, None])).astype(np.float32)
    tt = np.arange(CHUNK)[:, None]
    ss = (np.arange(256) % CHUNK)[None, :]
    lvl = []
    for m in LEVELS:
        lvl.append(((tt // (2 * m) == ss // (2 * m)) & (tt % (2 * m) >= m) & (ss % (2 * m) < m)).astype(np.float32))
    lvl.append((ss <= tt).astype(np.float32))
    lvl = np.stack(lvl)
    hm = np.stack([np.broadcast_to((lane // HEAD == h)[None, :], (CHUNK, 256)) for h in range(4)]).astype(np.float32)
    keys = np.arange(2 * WINDOW)[:, None]
    qry = np.arange(4 * WINDOW)[None, :] % WINDOW
    band = (keys >= qry) & (keys <= qry + WINDOW)
    bias = np.where(np.stack([band & (keys >= WINDOW), band]), 0.0, -np.inf).astype(np.float32)
    return e256, ltri, lvl, hm, bias


def _level_ref(g, m):
    n_rows = g.shape[0]
    if 2 * m >= 8:
        g3 = g.reshape(n_rows // (2 * m), 2 * m, g.shape[1])
        return jnp.broadcast_to(g3[:, m - 1:m, :], g3.shape).reshape(g.shape)
    g3 = g.reshape(n_rows // 8, 8, g.shape[1])
    sub = lax.broadcasted_iota(jnp.int32, g3.shape, 1)
    out = None
    for blk in range(8 // (2 * m)):
        r = blk * 2 * m + m - 1
        piece = jnp.broadcast_to(g3[:, r:r + 1, :], g3.shape)
        out = piece if out is None else jnp.where(sub >= blk * 2 * m, piece, out)
    return out.reshape(g.shape)


def _hgrn_tile(q, zf, v, lbp_ref, st_ref, st0_ref, o_ref, e, ltri_ref, lvl_ref, hm_ref):
    log_lb, log_1m_lb, one_m_lb = lbp_ref[0:1, :], lbp_ref[1:2, :], lbp_ref[2:3, :]
    logf, ls = _log_forget(zf, log_lb, log_1m_lb)
    kin = one_m_lb * jnp.exp(ls - zf)
    parts = jnp.concatenate(_split2(logf), axis=1)
    g2 = jnp.concatenate([jnp.dot(ltri_ref[...], parts[r:r + CUM_ROWS], preferred_element_type=f32)
                          for r in range(0, TILE, CUM_ROWS)], axis=0)
    g = g2[:, 0:A_W] + g2[:, A_W:2 * A_W]
    v_b = v.astype(bf16)
    vt_b = v.T.astype(bf16)
    e_mid = g - _level_ref(g, CHUNK // 2)
    safe = jnp.max(jnp.abs(e_mid)) < SAFE_EXPONENT

    def expand(rows_b):
        return jnp.concatenate([rows_b * hm_ref[h] for h in range(4)], axis=0)

    def finish_chunk(c, att, q_in, k_hat):
        r0, r1 = c * CHUNK, (c + 1) * CHUNK
        st = st_ref[...]
        o = jnp.dot(att.astype(bf16), expand(v_b[r0:r1]), preferred_element_type=f32)
        o = o + lax.dot_general(q_in, st.astype(bf16) * e, (((1,), (1,)), ((), ())), preferred_element_type=f32)
        zero = jnp.zeros((CHUNK, A_W), bf16)
        pair = (c // 2) * 2 * CHUNK
        rhs = jnp.concatenate([k_hat, zero] if c % 2 == 0 else [zero, k_hat], axis=0)
        upd = jnp.dot(vt_b[:, pair:pair + 2 * CHUNK], rhs, preferred_element_type=f32)
        st_ref[...] = st * jnp.exp(g[r1 - 1:r1, :]) + upd
        o_ref[r0:r1, :] = o

    st0_ref[...] = st_ref[...]
    qw = q * jnp.exp(e_mid)
    kw = kin * jnp.exp(-e_mid)
    q_t, k_t = qw.astype(bf16), kw.astype(bf16)
    chunks = [(c * CHUNK, (c + 1) * CHUNK) for c in range(TILE // CHUNK)]
    atts = [lax.dot_general(q_t[r0:r1], expand(k_t[r0:r1]), (((1,), (1,)), ((), ())), preferred_element_type=f32)
            for r0, r1 in chunks]
    atts = [jnp.where(lvl_ref[len(LEVELS)] > 0.5, a, 0.0) for a in atts]
    for c, (r0, r1) in enumerate(chunks):
        g_mid, g_end = g[r0 + CHUNK // 2 - 1:r0 + CHUNK // 2, :], g[r1 - 1:r1, :]
        finish_chunk(c, atts[c], (qw[r0:r1] * jnp.exp(g_mid)).astype(bf16),
                     (kw[r0:r1] * jnp.exp(g_end - g_mid)).astype(bf16))

    def redo_if_unsafe():
        @pl.when(jnp.logical_not(safe))
        def _():
            st_ref[...] = st0_ref[...]
            q_levels, k_levels = [], []
            for m in LEVELS:
                w = jnp.exp(-jnp.abs(g - _level_ref(g, m)))
                q_levels.append((q * w).astype(bf16))
                k_levels.append((kin * w).astype(bf16))
            q_inter = (q * jnp.exp(g)).astype(bf16)
            for c in range(TILE // CHUNK):
                r0, r1 = c * CHUNK, (c + 1) * CHUNK
                att = jnp.zeros((CHUNK, 4 * CHUNK), f32)
                for li in range(len(LEVELS)):
                    a = lax.dot_general(q_levels[li][r0:r1], expand(k_levels[li][r0:r1]),
                                        (((1,), (1,)), ((), ())), preferred_element_type=f32)
                    att = att + a * lvl_ref[li]
                k_hat = (kin[r0:r1] * jnp.exp(g[r1 - 1:r1, :] - g[r0:r1])).astype(bf16)
                finish_chunk(c, att, q_inter[r0:r1], k_hat)
            o_ref[...] = o_ref[...] + _seg_sum(q * kin, e) * v

    return redo_if_unsafe


def _conv_tile(u, hist_ref, cw_ref, cvec_ref, em):
    n = u.shape[0]
    hist_ref[HIST:HIST + n, :] = u
    lo = HIST - (CONV_WIDTH - 1)
    y = cvec_ref[0:1, :]
    for r in range(8):
        rows = n if r == 0 else n + 8
        part = None
        for o in range(lo, lo + CONV_WIDTH):
            if o % 8 == r:
                term = hist_ref[o - r:o - r + rows, :] * cw_ref[o - lo:o - lo + 1, :]
                part = term if part is None else part + term
        y = y + part[r:r + n]
    return _group_ln_silu(y, em, cvec_ref[1:2, :], cvec_ref[2:3, :])


def _attn_tile(q_t, k_b, v_t, sink_ref, biases):
    zero = jnp.zeros((HEAD, WINDOW), f32)
    inst = [(b, g) for b in range(TILE // WINDOW) for g in range(2)]
    sinks = [jnp.concatenate([jnp.full((1, WINDOW), sink_ref[h] * LOG2E, f32) for h in range(4 * g, 4 * g + 4)],
                             axis=1) for g in range(2)]

    def rhs(b, g):
        cols = []
        for h in range(4 * g, 4 * g + 4):
            q_h = q_t[h * HEAD:(h + 1) * HEAD, b * WINDOW:(b + 1) * WINDOW]
            cols.append(jnp.concatenate([q_h, zero] if g == 0 else [zero, q_h], axis=0))
        return jnp.concatenate(cols, axis=1).astype(bf16)

    s = [jnp.dot(k_b[b * WINDOW:(b + 2) * WINDOW], rhs(b, g), preferred_element_type=f32) + biases[b]
         for b, g in inst]
    mx = [jnp.maximum(jnp.max(s[n], axis=0, keepdims=True), sinks[g]) for n, (b, g) in enumerate(inst)]
    pr = [jnp.exp2(s[n] - mx[n]) for n in range(len(inst))]
    rden = [1.0 / (jnp.sum(pr[n], axis=0, keepdims=True) + jnp.exp2(sinks[g] - mx[n]))
            for n, (b, g) in enumerate(inst)]
    o_t = [jnp.dot(v_t[g * HEAD:(g + 1) * HEAD, b * WINDOW:(b + 2) * WINDOW], pr[n].astype(bf16),
                   preferred_element_type=f32) * rden[n] for n, (b, g) in enumerate(inst)]
    rows = []
    for b in range(TILE // WINDOW):
        pairs = []
        for g in range(2):
            o = o_t[2 * b + g]
            for j in range(2):
                pair_t = jnp.concatenate([o[:, (2 * j) * WINDOW:(2 * j + 1) * WINDOW],
                                          o[:, (2 * j + 1) * WINDOW:(2 * j + 2) * WINDOW]], axis=0)
                pairs.append(pair_t.T)
        rows.append(jnp.concatenate(pairs, axis=1))
    return jnp.concatenate(rows, axis=0)


def _mixer_kernel(layer, x_ref, rope_row_ref, rope_tile_ref, win_ref, wout_ref, nmix_ref, lbp_ref, aon_ref, cw_ref,
                  cvec_ref, qk_ref, sinks_ref, e_ref, ltri_ref, lvl_ref, hm_ref, bias_ref,
                  h_ref, st_ref, conv_ref, kn_ref, vn_ref, z_ref, hist_ref, o_ref, st0_ref):
    i = pl.program_id(0)
    sink_ref = sinks_ref.at[layer]

    @pl.when(i == 0)
    def _():
        st_ref[...] = jnp.zeros_like(st_ref)
        hist_ref[0:HIST, :] = jnp.zeros((HIST, B_W), f32)
        kn_ref[...] = jnp.zeros_like(kn_ref)
        vn_ref[...] = jnp.zeros_like(vn_ref)

    x = x_ref[...]
    xn = _rmsnorm_rows(x, nmix_ref[...])
    z_ref[...] = jnp.dot(xn.astype(bf16), win_ref[...], preferred_element_type=f32)
    e = e_ref[...]

    redo_hgrn_if_unsafe = _hgrn_tile(z_ref[:, 0:A_W], z_ref[:, A_W:2 * A_W], z_ref[:, 2 * A_W:3 * A_W], lbp_ref,
                                     st_ref, st0_ref, o_ref, e, ltri_ref, lvl_ref, hm_ref)

    c0 = 4 * A_W
    u = z_ref[:, c0:c0 + B_W] * _sigmoid(z_ref[:, c0 + B_W:c0 + 2 * B_W])
    em = _mean_matrix(e)
    ob = _conv_tile(u, hist_ref, cw_ref, cvec_ref, em)
    conv_ref[...] = hist_ref[TILE + HIST - (CONV_WIDTH - 1):TILE + HIST, :]
    hist_ref[0:HIST, :] = hist_ref[TILE:TILE + HIST, :]

    c1 = c0 + 2 * B_W
    cos_a, sin_a = rope_tile_ref[0, pl.ds(i, 1), :], rope_tile_ref[1, pl.ds(i, 1), :]
    cos = cos_a * rope_row_ref[0] - sin_a * rope_row_ref[1]
    sin = sin_a * rope_row_ref[2] + cos_a * rope_row_ref[3]
    qn = jnp.concatenate([_head_norm(z_ref[:, c1 + j * 256:c1 + (j + 1) * 256], em,
                                     jnp.concatenate([qk_ref[0:1, :]] * 2, axis=1)) for j in range(2)], axis=1)
    q = _rope(qn, jnp.concatenate([cos] * 4, axis=1), jnp.concatenate([sin] * 4, axis=1)) * (LOG2E * HEAD ** -0.5)
    kk = _rope(_head_norm(z_ref[:, c1 + C_W:c1 + C_W + KV_W], em[0:KV_W, 0:KV_W], qk_ref[1:2, :]), cos, sin)
    vv = z_ref[:, c1 + C_W + KV_W:c1 + C_W + 2 * KV_W]
    kcat = jnp.concatenate([kn_ref[...], kk], axis=0)
    vcat = jnp.concatenate([vn_ref[...], vv], axis=0)
    q_t = q.T
    k_b = kcat.astype(bf16)
    v_t = vcat.T.astype(bf16)
    biases = [bias_ref[jnp.where(i > 0, 1, 0)]] + [bias_ref[1]] * (TILE // WINDOW - 1)
    oc = _attn_tile(q_t, k_b, v_t, sink_ref, biases)
    kn_ref[...] = kk[TILE - WINDOW:TILE]
    vn_ref[...] = vv[TILE - WINDOW:TILE]

    redo_hgrn_if_unsafe()
    gate = z_ref[:, 3 * A_W:4 * A_W]
    oa = _head_norm(o_ref[...], em, aon_ref[...]) * (gate * _sigmoid(gate))
    mix = jnp.concatenate([oa, ob, oc], axis=1).astype(bf16)
    h_ref[...] = x + jnp.dot(mix, wout_ref[...], preferred_element_type=f32)


def _const_spec(shape):
    nd = len(shape)
    return pl.BlockSpec(shape, lambda i, _n=nd: (0,) * _n)


def _layer_spec(arr, l, **kw):
    shape = arr.shape[1:]
    return pl.BlockSpec((None,) + shape, lambda i, _l=l, _n=len(shape): (_l,) + (0,) * _n, **kw)


def _ffn_kernel(h_ref, p_ref, wg_ref, wu_ref, wd_ref, wpg_ref, wpp_ref, norms_ref, y_ref):
    n = h_ref.shape[0]
    slabs = [slice(r, r + n // FFN_SLABS) for r in range(0, n, n // FFN_SLABS)]
    h = [h_ref[s, :] for s in slabs]
    hn = [_rmsnorm_rows(a, norms_ref[0:1, :]).astype(bf16) for a in h]
    gt = [jnp.dot(a, wg_ref[...], preferred_element_type=f32) for a in hn]
    up = [jnp.dot(a, wu_ref[...], preferred_element_type=f32) for a in hn]
    act = [(g * _sigmoid(g) * u).astype(bf16) for g, u in zip(gt, up)]
    h = [a + jnp.dot(b, wd_ref[...], preferred_element_type=f32) for a, b in zip(h, act)]
    pn = [_rmsnorm_rows(a, norms_ref[1:2, :]).astype(bf16) for a in h]
    gate = [_sigmoid(jnp.dot(a, wpg_ref[...], preferred_element_type=f32)) for a in pn]
    for s, a, g in zip(slabs, h, gate):
        y_ref[s, :] = a + g * jnp.dot(p_ref[s, :].astype(bf16), wpp_ref[...], preferred_element_type=f32)


def _ffn(h, p, wl, l):
    n = h.shape[0]
    tile = min(FFN_TILE, n)
    assert n % tile == 0
    row_spec = lambda w: pl.BlockSpec((tile, w), lambda i: (i, 0))
    ws = [wl['w_gate'], wl['w_up'], wl['w_down'], wl['w_ple_gate'], wl['w_ple_proj']]
    return pl.pallas_call(
        _ffn_kernel, grid=(n // tile,),
        in_specs=[row_spec(D_MODEL), pl.BlockSpec((None, tile, PLE_DIM), lambda i, _l=l: (_l, i, 0))]
        + [_layer_spec(a, l, pipeline_mode=pl.Buffered(1)) for a in ws] + [_layer_spec(wl['fnorms'], l)],
        out_specs=row_spec(D_MODEL), out_shape=jax.ShapeDtypeStruct((n, D_MODEL), f32),
        compiler_params=pltpu.CompilerParams(dimension_semantics=("arbitrary",), vmem_limit_bytes=VMEM_LIMIT),
        name="ffn_ple",
    )(h, p, *ws, wl['fnorms'])


N_MIXER_INPUTS = 17


def _layer_kernel(layer, *refs):
    mixer_in, (p_ref, wg_ref, wu_ref, wd_ref, wpg_ref, wpp_ref, norms_ref) = refs[:N_MIXER_INPUTS], refs[17:24]
    y_ref, st_ref, conv_ref, kn_ref, vn_ref = refs[24:29]
    z_ref, hist_ref, o_ref, st0_ref, mid_ref = refs[29:]
    _mixer_kernel(layer, *mixer_in, mid_ref, st_ref, conv_ref, kn_ref, vn_ref, z_ref, hist_ref, o_ref, st0_ref)
    _ffn_kernel(mid_ref, p_ref, wg_ref, wu_ref, wd_ref, wpg_ref, wpp_ref, norms_ref, y_ref)


def _prompt_layer(x, p, rope, wl, l, consts):
    t_len = x.shape[0]
    row_spec = lambda w: pl.BlockSpec((TILE, w), lambda i: (i, 0))
    once = dict(pipeline_mode=pl.Buffered(1))
    smalls = [wl['nmix'], wl['lbp'], wl['aon'], wl['cw'], wl['cvec'], wl['qk']]
    ffn_ws = [wl['w_gate'], wl['w_up'], wl['w_down'], wl['w_ple_gate'], wl['w_ple_proj']]
    in_specs = ([row_spec(D_MODEL), _const_spec(rope[0].shape), _const_spec(rope[1].shape),
                 _layer_spec(wl['w_in'], l, **once), _layer_spec(wl['w_out'], l, **once)]
                + [_layer_spec(a, l) for a in smalls] + [pl.BlockSpec(memory_space=pltpu.SMEM)]
                + [_const_spec(a.shape) for a in consts]
                + [pl.BlockSpec((None, TILE, PLE_DIM), lambda i, _l=l: (_l, i, 0))]
                + [_layer_spec(a, l, **once) for a in ffn_ws] + [_layer_spec(wl['fnorms'], l)])
    assert len(in_specs) == N_MIXER_INPUTS + 7
    out_shape = (jax.ShapeDtypeStruct((t_len, D_MODEL), f32),
                 jax.ShapeDtypeStruct((A_W, A_W), f32),
                 jax.ShapeDtypeStruct((CONV_WIDTH - 1, B_W), f32),
                 jax.ShapeDtypeStruct((WINDOW, KV_W), f32),
                 jax.ShapeDtypeStruct((WINDOW, KV_W), f32))
    out_specs = (row_spec(D_MODEL), _const_spec((A_W, A_W)), _const_spec((CONV_WIDTH - 1, B_W)),
                 _const_spec((WINDOW, KV_W)), _const_spec((WINDOW, KV_W)))
    return pl.pallas_call(
        functools.partial(_layer_kernel, l), grid=(t_len // TILE,), in_specs=in_specs, out_specs=out_specs,
        out_shape=out_shape,
        scratch_shapes=[pltpu.VMEM((TILE, IN_COLS), f32), pltpu.VMEM((TILE + HIST, B_W), f32),
                        pltpu.VMEM((TILE, A_W), f32), pltpu.VMEM((A_W, A_W), f32),
                        pltpu.VMEM((TILE, D_MODEL), f32)],
        compiler_params=pltpu.CompilerParams(dimension_semantics=("arbitrary",),
                                             vmem_limit_bytes=LAYER_VMEM_LIMIT),
        name="prompt_layer",
    )(x, rope[0], rope[1], wl['w_in'], wl['w_out'], *smalls, wl['sinks'], *consts, p, *ffn_ws, wl['fnorms'])


N_HEADS = C_W // HEAD


def _sample_in_kernel(n_tok, x_ref, cos_ref, sin_ref, win_ref, nmix_ref, qk_ref, e_ref, z_ref, qx_ref, k_ref,
                      piece_ref):
    n_seq = x_ref.shape[0] // n_tok
    xn = _rmsnorm_rows(x_ref[...], nmix_ref[...])
    z = jnp.dot(xn.astype(bf16), win_ref[...], preferred_element_type=f32)
    z_ref[...] = z
    em = _mean_matrix(e_ref[...])
    c1 = 4 * A_W + 2 * B_W
    cos, sin = cos_ref[...], sin_ref[...]
    qn = jnp.concatenate([_head_norm(z[:, c1 + j * 256:c1 + (j + 1) * 256], em,
                                     jnp.concatenate([qk_ref[0:1, :]] * 2, axis=1)) for j in range(2)], axis=1)
    q = _rope(qn, jnp.concatenate([cos] * 4, axis=1), jnp.concatenate([sin] * 4, axis=1)) * (HEAD ** -0.5)
    k_ref[...] = _rope(_head_norm(z[:, c1 + C_W:c1 + C_W + KV_W], em[0:KV_W, 0:KV_W], qk_ref[1:2, :]), cos, sin)
    lo_f = lax.broadcasted_iota(jnp.int32, (x_ref.shape[0], KV_W), 1) < HEAD
    for h in range(N_HEADS):
        pair, g = q[:, (h // 2) * KV_W:(h // 2 + 1) * KV_W], h // 4
        own = jnp.where(lo_f if h % 2 == 0 else jnp.logical_not(lo_f), pair, 0.0)
        piece_ref[...] = own if h % 2 == g else pltpu.roll(own, HEAD, 1)
        for t in range(n_tok):
            qx_ref[pl.ds(h * n_tok + t, n_seq, stride=N_HEADS * n_tok), :] = piece_ref[pl.ds(t, n_seq, stride=n_tok), :]


def _sample_in(x, cos, sin, wl, l, e256, n_tok):
    n = x.shape[0]
    args = (x, cos, sin, wl['w_in'], wl['nmix'], wl['qk'], e256)
    specs = [_const_spec(a.shape) for a in args]
    specs[3:6] = [_layer_spec(a, l) for a in args[3:6]]
    return pl.pallas_call(
        functools.partial(_sample_in_kernel, n_tok), grid=(1,), in_specs=specs,
        out_specs=(_const_spec((n, IN_COLS)), _const_spec((n * N_HEADS, KV_W)), _const_spec((n, KV_W))),
        out_shape=(jax.ShapeDtypeStruct((n, IN_COLS), f32), jax.ShapeDtypeStruct((n * N_HEADS, KV_W), f32),
                   jax.ShapeDtypeStruct((n, KV_W), f32)),
        scratch_shapes=[pltpu.VMEM((n, KV_W), f32)],
        compiler_params=pltpu.CompilerParams(dimension_semantics=("arbitrary",), vmem_limit_bytes=VMEM_LIMIT),
        name="sample_in",
    )(*args)


def _sample_hgrn_kernel(n_tok, q_ref, f_ref, v_ref, s_ref, lbp_ref, o_ref, so_ref, qt_scr, ft_scr, kt_scr, vt_scr,
                        ot_scr):
    n_seq = s_ref.shape[1]
    for t in range(n_tok):
        zf = f_ref[pl.ds(t, n_seq, stride=n_tok), :]
        logf, ls = _log_forget(zf, lbp_ref[0:1, :], lbp_ref[1:2, :])
        ft_scr[t] = jnp.exp(logf).T
        kt_scr[t] = (lbp_ref[2:3, :] * jnp.exp(ls - zf)).T
        qt_scr[t] = q_ref[pl.ds(t, n_seq, stride=n_tok), :].T
        vt_scr[t] = v_ref[pl.ds(t, n_seq, stride=n_tok), :].T
        ot_scr[t] = jnp.zeros((2 * HEAD, n_seq), f32)

    def body(hk, carry):
        r0 = pl.multiple_of(hk * HEAD, HEAD)
        v0 = pl.multiple_of((hk // HEAD) * HEAD, HEAD)
        blk = s_ref[pl.ds(r0, HEAD), :]
        for t in range(n_tok):
            blk = blk * ft_scr[t, pl.ds(hk, 1), :] + kt_scr[t, pl.ds(hk, 1), :] * vt_scr[t, pl.ds(v0, HEAD), :]
            ot_scr[t, pl.ds(v0, HEAD), :] = ot_scr[t, pl.ds(v0, HEAD), :] + qt_scr[t, pl.ds(hk, 1), :] * blk
        so_ref[pl.ds(r0, HEAD), :] = blk
        return carry

    lax.fori_loop(0, 2 * HEAD, body, 0, unroll=4)
    for t in range(n_tok):
        o_ref[pl.ds(t, n_seq, stride=n_tok), :] = ot_scr[t].T


def _sample_hgrn(z, state, lbp, l, n_tok):
    n = z.shape[0]
    n_seq = state.shape[2]
    blk = 2 * HEAD * HEAD
    col = lambda j0: pl.BlockSpec((n, 2 * HEAD), lambda i, _j=j0: (0, _j + i))
    return pl.pallas_call(
        functools.partial(_sample_hgrn_kernel, n_tok), grid=(2,),
        in_specs=[col(0), col(2), col(4), pl.BlockSpec((None, blk, n_seq), lambda i, _l=l: (_l, i, 0)),
                  pl.BlockSpec((None, 3, 2 * HEAD), lambda i, _l=l: (_l, 0, i))],
        out_specs=(pl.BlockSpec((n, 2 * HEAD), lambda i: (0, i)), pl.BlockSpec((blk, n_seq), lambda i: (i, 0))),
        out_shape=(jax.ShapeDtypeStruct((n, A_W), f32), jax.ShapeDtypeStruct(state.shape[1:], f32)),
        scratch_shapes=[pltpu.VMEM((n_tok, 2 * HEAD, n_seq), f32)] * 5,
        compiler_params=pltpu.CompilerParams(dimension_semantics=("arbitrary",), vmem_limit_bytes=VMEM_LIMIT),
        name="sample_hgrn",
    )(z, z, z, state, lbp)


def _sample_conv_kernel(n_tok, first, u0_ref, u1_ref, g0_ref, g1_ref, buf_ref, cw_ref, cvec_ref, e_ref, *rest):
    ob0_ref, ob1_ref, new_ref = rest[-3:]
    slots = [new_ref.at[d] for d in range(new_ref.shape[0])] if first else [new_ref]
    n_seq = buf_ref.shape[1]
    n_hist = CONV_WIDTH - 1
    us = []
    for t in range(n_tok):
        rows = pl.ds(t, n_seq, stride=n_tok)
        u = jnp.concatenate([u0_ref[rows, :], u1_ref[rows, :]], axis=1)
        g = jnp.concatenate([g0_ref[rows, :], g1_ref[rows, :]], axis=1)
        us.append(u * _sigmoid(g))

    def slab(j):
        return buf_ref[j] if j < n_hist else us[j - n_hist]

    em = _mean_matrix(e_ref[...])
    for t in range(n_tok):
        y = cvec_ref[0:1, :]
        for j in range(CONV_WIDTH):
            y = y + slab(t + j) * cw_ref[j:j + 1, :]
        ob = _group_ln_silu(y, em, cvec_ref[1:2, :], cvec_ref[2:3, :])
        ob0_ref[pl.ds(t, n_seq, stride=n_tok), :] = ob[:, 0:KV_W]
        ob1_ref[pl.ds(t, n_seq, stride=n_tok), :] = ob[:, KV_W:B_W]
    for dst in slots:
        for j in range(n_hist):
            dst[j] = slab(j + n_tok)


def _sample_conv(z, buf, new_all, wl, l, e256, n_tok):
    n = z.shape[0]
    col = lambda j: pl.BlockSpec((n, KV_W), lambda i, _j=j: (0, _j))
    half = jax.ShapeDtypeStruct((n, KV_W), f32)
    in_specs = [col(8), col(9), col(10), col(11), _layer_spec(buf, l), _layer_spec(wl['cw'], l),
                _layer_spec(wl['cvec'], l), _const_spec(e256.shape)]
    args = [z, z, z, z, buf, wl['cw'], wl['cvec'], e256]
    if new_all is None:
        new_spec, aliases = _const_spec(buf.shape), {}
    else:
        in_specs.append(pl.BlockSpec(memory_space=pl.ANY))
        args.append(new_all)
        new_spec, aliases = _layer_spec(buf, l), {len(args) - 1: 2}
    return pl.pallas_call(
        functools.partial(_sample_conv_kernel, n_tok, new_all is None), grid=(1,), in_specs=in_specs,
        out_specs=(_const_spec((n, KV_W)), _const_spec((n, KV_W)), new_spec),
        out_shape=(half, half, jax.ShapeDtypeStruct(buf.shape, f32)),
        input_output_aliases=aliases,
        compiler_params=pltpu.CompilerParams(dimension_semantics=("arbitrary",), vmem_limit_bytes=VMEM_LIMIT),
        name="sample_conv",
    )(*args)


SEQ_BLOCK = 32


def _sample_attn_kernel(n_tok, qx_ref, k_ref, v_ref, kc_ref, vc_ref, sink_ref, bias_ref, ox_ref, kn_ref, vn_ref):
    q_rows = N_HEADS * n_tok
    sink, bias = sink_ref[...], bias_ref[...]
    seqs = range(SEQ_BLOCK)
    k_all = [jnp.concatenate([kc_ref[b], k_ref[b * n_tok:(b + 1) * n_tok, :]], axis=0) for b in seqs]
    v_all = [jnp.concatenate([vc_ref[b], v_ref[b * n_tok:(b + 1) * n_tok, :]], axis=0) for b in seqs]
    s = [lax.dot_general(qx_ref[b * q_rows:(b + 1) * q_rows, :].astype(bf16), k_all[b].astype(bf16),
                         (((1,), (1,)), ((), ())), preferred_element_type=f32) + bias for b in seqs]
    mx = [jnp.maximum(jnp.max(s[b], axis=-1, keepdims=True), sink) for b in seqs]
    pr = [jnp.exp(s[b] - mx[b]) for b in seqs]
    rden = [1.0 / (jnp.sum(pr[b], axis=-1, keepdims=True) + jnp.exp(sink - mx[b])) for b in seqs]
    for b in seqs:
        ox_ref[b * q_rows:(b + 1) * q_rows, :] = jnp.dot(pr[b].astype(bf16), v_all[b].astype(bf16),
                                                         preferred_element_type=f32) * rden[b]
    for b in seqs:
        kn_ref[b] = k_all[b][n_tok:, :]
        vn_ref[b] = v_all[b][n_tok:, :]


def _sample_attn(qx, k, z, kc, vc, sink_col, bias, l, n_tok):
    n_seq = kc.shape[1]
    assert n_seq % SEQ_BLOCK == 0
    rows = SEQ_BLOCK * n_tok
    q_rows = SEQ_BLOCK * N_HEADS * n_tok
    v_col = (4 * A_W + 2 * B_W + C_W + KV_W) // KV_W
    cache_in = pl.BlockSpec((None, SEQ_BLOCK) + kc.shape[2:], lambda i, _l=l: (_l, i, 0, 0))
    cache_out = pl.BlockSpec((SEQ_BLOCK,) + kc.shape[2:], lambda i: (i, 0, 0))
    return pl.pallas_call(
        functools.partial(_sample_attn_kernel, n_tok), grid=(n_seq // SEQ_BLOCK,),
        in_specs=[pl.BlockSpec((q_rows, KV_W), lambda i: (i, 0)), pl.BlockSpec((rows, KV_W), lambda i: (i, 0)),
                  pl.BlockSpec((rows, KV_W), lambda i: (i, v_col)), cache_in, cache_in,
                  _layer_spec(sink_col, l), _const_spec(bias.shape)],
        out_specs=(pl.BlockSpec((q_rows, KV_W), lambda i: (i, 0)), cache_out, cache_out),
        out_shape=(jax.ShapeDtypeStruct(qx.shape, f32), jax.ShapeDtypeStruct(kc.shape[1:], f32),
                   jax.ShapeDtypeStruct(vc.shape[1:], f32)),
        compiler_params=pltpu.CompilerParams(dimension_semantics=("arbitrary",), vmem_limit_bytes=VMEM_LIMIT),
        name="sample_attn",
    )(qx, k, z, kc, vc, sink_col, bias)


def _sample_out_kernel(n_tok, x_ref, oa_ref, g_ref, ob0_ref, ob1_ref, ox_ref, wout_ref, aon_ref, e_ref, h_ref,
                       oc_ref):
    n_seq = x_ref.shape[0] // n_tok
    gate = g_ref[...]
    oa = _head_norm(oa_ref[...], _mean_matrix(e_ref[...]), aon_ref[...]) * (gate * _sigmoid(gate))
    lo_f = lax.broadcasted_iota(jnp.int32, (n_seq, KV_W), 1) < HEAD
    for p in range(N_HEADS // 2):
        g = p // 2
        for t in range(n_tok):
            even = ox_ref[pl.ds((2 * p) * n_tok + t, n_seq, stride=N_HEADS * n_tok), :]
            odd = ox_ref[pl.ds((2 * p + 1) * n_tok + t, n_seq, stride=N_HEADS * n_tok), :]
            even = even if g == 0 else pltpu.roll(even, HEAD, 1)
            odd = odd if g == 1 else pltpu.roll(odd, HEAD, 1)
            oc_ref[p, pl.ds(t, n_seq, stride=n_tok), :] = jnp.where(lo_f, even, odd)
    mix = jnp.concatenate([oa, ob0_ref[...], ob1_ref[...]] + [oc_ref[p] for p in range(N_HEADS // 2)],
                          axis=1).astype(bf16)
    h_ref[...] = x_ref[...] + jnp.dot(mix, wout_ref[...], preferred_element_type=f32)


def _sample_out(x, oa, z, ob0, ob1, ox, wl, l, e256, n_tok):
    n = x.shape[0]
    gate_spec = pl.BlockSpec((n, A_W), lambda i: (0, 3))
    args = (x, oa, z, ob0, ob1, ox, wl['w_out'], wl['aon'], e256)
    specs = [_const_spec(a.shape) for a in args]
    specs[2] = gate_spec
    specs[6:8] = [_layer_spec(a, l) for a in args[6:8]]
    return pl.pallas_call(
        functools.partial(_sample_out_kernel, n_tok), grid=(1,), in_specs=specs,
        out_specs=_const_spec((n, D_MODEL)), out_shape=jax.ShapeDtypeStruct((n, D_MODEL), f32),
        scratch_shapes=[pltpu.VMEM((N_HEADS // 2, n, KV_W), f32)],
        compiler_params=pltpu.CompilerParams(dimension_semantics=("arbitrary",), vmem_limit_bytes=VMEM_LIMIT),
        name="sample_out",
    )(*args)


def _rope_cos_sin(pos):
    half = HEAD // 2
    inv = ROPE_THETA ** (-jnp.arange(half, dtype=f32) / half)
    ang = pos[:, None] * inv[None, :]
    cos, sin = jnp.cos(ang), jnp.sin(ang)
    return jnp.tile(cos, (1, 4)), jnp.concatenate([-sin, sin, -sin, sin], axis=1)


def _rope_prompt_tables(t_len):
    half = HEAD // 2
    inv = ROPE_THETA ** (-jnp.arange(half, dtype=f32) / half)
    sign = jnp.concatenate([-jnp.ones((half,), f32), jnp.ones((half,), f32)] * 2)
    ang_r = jnp.arange(TILE, dtype=f32)[:, None] * inv[None, :]
    ang_t = (jnp.arange(t_len // TILE, dtype=f32) * TILE)[:, None] * inv[None, :]
    cos_r, sin_r = jnp.tile(jnp.cos(ang_r), (1, 4)), jnp.tile(jnp.sin(ang_r), (1, 4))
    rows = jnp.stack([cos_r, sin_r, cos_r * sign, sin_r * sign])
    tiles = jnp.stack([jnp.tile(jnp.cos(ang_t), (1, 4)), jnp.tile(jnp.sin(ang_t), (1, 4))])
    return rows, tiles


def _stacked_weights(lbs, n_tok, w_in, a_onorm, conv_w, conv_b, conv_ln_g, conv_ln_b, q_norm, k_norm, sinks, w_out,
                     norm_mix, norm_ffn, w_gate, w_up, w_down, ple_norm, w_ple_gate, w_ple_proj):
    return dict(
        w_in=w_in.astype(bf16), w_out=w_out.astype(bf16), w_gate=w_gate.astype(bf16), w_up=w_up.astype(bf16),
        w_down=w_down.astype(bf16), w_ple_gate=w_ple_gate.astype(bf16), w_ple_proj=w_ple_proj.astype(bf16),
        nmix=norm_mix[:, None, :],
        lbp=jnp.stack([jnp.log(lbs), jnp.log1p(-lbs), 1.0 - lbs], axis=1),
        aon=jnp.tile(a_onorm, (1, 4))[:, None, :],
        cw=conv_w,
        cvec=jnp.stack([conv_b, conv_ln_g, conv_ln_b], axis=1),
        qk=jnp.stack([jnp.tile(q_norm, (1, 2)), jnp.tile(k_norm, (1, 2))], axis=1),
        sinks=sinks,
        sink_col=jnp.repeat(sinks, n_tok, axis=1)[:, :, None],
        fnorms=jnp.stack([norm_ffn, ple_norm], axis=1),
    )


def kernel(x_prompt, x_sample, state_hgrn, state_conv, cache_swa_k, cache_swa_v, p_prompt, p_sample, a_lower, w_in, a_onorm, conv_w, conv_b, conv_ln_g, conv_ln_b, q_norm, k_norm, sinks, w_out, norm_mix, norm_ffn, w_gate, w_up, w_down, ple_norm, w_ple_gate, w_ple_proj):
    depth = w_in.shape[0]
    t_len = x_prompt.shape[1]
    n_seq, n_tok = x_sample.shape[0], x_sample.shape[1]
    past_len = 16384
    assert x_prompt.shape[0] == 1 and t_len % TILE == 0

    lbs = jnp.cumsum(jax.nn.softmax(a_lower.astype(f32), axis=0), axis=0)
    lbs = lbs - lbs[0:1]
    consts_np = _np_constants()
    e256 = jnp.asarray(consts_np[0], bf16)
    consts = (e256, jnp.asarray(consts_np[1], bf16), jnp.asarray(consts_np[2], f32), jnp.asarray(consts_np[3], bf16),
              jnp.asarray(consts_np[4], f32))

    rope_p = _rope_prompt_tables(t_len)
    cos_s, sin_s = _rope_cos_sin(past_len + jnp.arange(n_tok, dtype=f32))
    cos_s, sin_s = jnp.tile(cos_s, (n_seq, 1)), jnp.tile(sin_s, (n_seq, 1))
    w_buf = cache_swa_k.shape[2]
    q_tok = (np.arange(N_HEADS * n_tok) % n_tok)[:, None]
    rel = q_tok + w_buf - np.arange(w_buf + n_tok)[None, :]
    bias_s = jnp.asarray(np.where((rel >= 0) & (rel <= WINDOW), 0.0, -np.inf), f32)

    wl = _stacked_weights(lbs, n_tok, w_in, a_onorm, conv_w, conv_b, conv_ln_g, conv_ln_b, q_norm, k_norm, sinks,
                          w_out, norm_mix, norm_ffn, w_gate, w_up, w_down, ple_norm, w_ple_gate, w_ple_proj)
    p_p = p_prompt.reshape(depth, t_len, PLE_DIM)
    p_s = p_sample.reshape(depth, n_seq * n_tok, PLE_DIM)
    s_hgrn = jnp.swapaxes(state_hgrn.reshape(depth, n_seq, -1), 1, 2)
    kc = cache_swa_k.reshape(depth, n_seq, w_buf, KV_W)
    vc = cache_swa_v.reshape(depth, n_seq, w_buf, KV_W)

    hp = x_prompt[0]
    hs = x_sample.reshape(n_seq * n_tok, D_MODEL)
    outs = [[] for _ in range(8)]
    conv_in = jnp.swapaxes(state_conv, 1, 2)
    conv_s = None
    for l in range(depth):
        hp, st, conv_new, k_new, v_new = _prompt_layer(hp, p_p, rope_p, wl, l, consts)
        s_new = jnp.stack([st[h * HEAD:(h + 1) * HEAD, h * HEAD:(h + 1) * HEAD].T for h in range(4)])
        outs[0].append(s_new[None])
        outs[1].append(conv_new[None])
        outs[2].append(k_new.reshape(1, WINDOW, 2, HEAD))
        outs[3].append(v_new.reshape(1, WINDOW, 2, HEAD))
        z, qx, k = _sample_in(hs, cos_s, sin_s, wl, l, e256, n_tok)
        oa, s_s = _sample_hgrn(z, s_hgrn, wl['lbp'], l, n_tok)
        ob0, ob1, conv_s = _sample_conv(z, conv_in, conv_s, wl, l, e256, n_tok)
        ox, k_s, v_s = _sample_attn(qx, k, z, kc, vc, wl['sink_col'], bias_s, l, n_tok)
        h_mid_s = _sample_out(hs, oa, z, ob0, ob1, ox, wl, l, e256, n_tok)
        hs = _ffn(h_mid_s, p_s, wl, l)
        outs[4].append(s_s.T.reshape(n_seq, 4, HEAD, HEAD))
        outs[6].append(k_s.reshape(n_seq, w_buf, 2, HEAD))
        outs[7].append(v_s.reshape(n_seq, w_buf, 2, HEAD))
    stacked = [jnp.swapaxes(conv_s, 1, 2) if i == 5 else jnp.stack(o) for i, o in enumerate(outs)]
    return (hp[None], hs.reshape(n_seq, n_tok, D_MODEL)) + tuple(stacked)
```

```python
import functools

import numpy as np
import jax
import jax.numpy as jnp
from jax import lax
from jax.experimental import pallas as pl
from jax.experimental.pallas import tpu as pltpu

f32 = jnp.float32
bf16 = jnp.bfloat16

D_MODEL = 1024
HEAD = 64
A_W = 256
B_W = 256
C_W = 512
KV_W = 128
IN_COLS = 4 * A_W + 2 * B_W + C_W + 2 * KV_W
PLE_DIM = 256
WINDOW = 128
CONV_WIDTH = 31
ROPE_THETA = 10000.0
EPS = 1e-6
PAST_LEN = 16384

TILE = 512
CUM_ROWS = 256
CHUNK = 64
LEVELS = (32, 16, 8, 4, 2, 1)
SAFE_EXPONENT = 80.0
HIST = 32
FFN_TILE = 512
FFN_SLABS = 4
LOG2E = 1.4426950408889634
VMEM_LIMIT = 52 * 1024 * 1024
LAYER_VMEM_LIMIT = 58 * 1024 * 1024


def _split2(x):
    hi = x.astype(bf16)
    lo = (x - hi.astype(f32)).astype(bf16)
    return hi, lo


def _seg_sum(x, e):
    hi, lo = _split2(x)
    return (jnp.dot(hi, e, preferred_element_type=f32) + jnp.dot(lo, e, preferred_element_type=f32))


def _sigmoid(x):
    return 1.0 / (1.0 + jnp.exp(-x))


def _rmsnorm_rows(x, g):
    return x * lax.rsqrt(jnp.mean(x * x, axis=-1, keepdims=True) + EPS) * g


def _seg_stat(x, e):
    return jnp.dot(x.astype(bf16), e, preferred_element_type=f32)


def _mean_matrix(e):
    return e * (1.0 / HEAD)


def _head_norm(x, em, g):
    return x * lax.rsqrt(_seg_stat(x * x, em) + EPS) * g


def _rope(x, cos, sin_signed):
    w = x.shape[-1]
    lane = lax.broadcasted_iota(jnp.int32, x.shape, 1)
    first = (lane % HEAD) < (HEAD // 2)
    swapped = jnp.where(first, pltpu.roll(x, w - HEAD // 2, 1), pltpu.roll(x, HEAD // 2, 1))
    return x * cos + swapped * sin_signed


def _log_forget(zf, log_lb, log_1m_lb):
    ls = jnp.minimum(zf, 0.0) - jnp.log(1.0 + jnp.exp(-jnp.abs(zf)))
    b = log_1m_lb + ls
    return jnp.maximum(log_lb, b) + jnp.log(1.0 + jnp.exp(-jnp.abs(log_lb - b))), ls


def _group_ln_silu(y, em, g, b):
    yc = y - _seg_stat(y, em)
    var = _seg_stat(yc * yc, em)
    yn = yc * lax.rsqrt(var + EPS) * g + b
    return yn * _sigmoid(yn)


def _np_constants():
    lane = np.arange(256)
    e256 = (lane[:, None] // HEAD == lane[None, :] // HEAD).astype(np.float32)
    t = np.arange(CUM_ROWS)
    ltri = ((t[:, None] // CHUNK == t[None, :] // CHUNK) & (t[None, :] <= t[:, None])).astype(np.float32)
    tt = np.arange(CHUNK)[:, None]
    ss = (np.arange(256) % CHUNK)[None, :]
    lvl = []
    for m in LEVELS:
        lvl.append(((tt // (2 * m) == ss // (2 * m)) & (tt % (2 * m) >= m) & (ss % (2 * m) < m)).astype(np.float32))
    lvl.append((ss <= tt).astype(np.float32))
    lvl = np.stack(lvl)
    hm = np.stack([np.broadcast_to((lane // HEAD == h)[None, :], (CHUNK, 256)) for h in range(4)]).astype(np.float32)
    keys = np.arange(2 * WINDOW)[:, None]
    qry = np.arange(4 * WINDOW)[None, :] % WINDOW
    band = (keys >= qry) & (keys <= qry + WINDOW)
    bias = np.where(np.stack([band & (keys >= WINDOW), band]), 0.0, -np.inf).astype(np.float32)
    return e256, ltri, lvl, hm, bias


def _level_ref(g, m):
    n_rows = g.shape[0]
    if 2 * m >= 8:
        g3 = g.reshape(n_rows // (2 * m), 2 * m, g.shape[1])
        return jnp.broadcast_to(g3[:, m - 1:m, :], g3.shape).reshape(g.shape)
    g3 = g.reshape(n_rows // 8, 8, g.shape[1])
    sub = lax.broadcasted_iota(jnp.int32, g3.shape, 1)
    out = None
    for blk in range(8 // (2 * m)):
        r = blk * 2 * m + m - 1
        piece = jnp.broadcast_to(g3[:, r:r + 1, :], g3.shape)
        out = piece if out is None else jnp.where(sub >= blk * 2 * m, piece, out)
    return out.reshape(g.shape)


def _hgrn_tile(q, zf, v, lbp_ref, st_ref, st0_ref, o_ref, e, ltri_ref, lvl_ref, hm_ref):
    log_lb, log_1m_lb, one_m_lb = lbp_ref[0:1, :], lbp_ref[1:2, :], lbp_ref[2:3, :]
    logf, ls = _log_forget(zf, log_lb, log_1m_lb)
    kin = one_m_lb * jnp.exp(ls - zf)
    parts = jnp.concatenate(_split2(logf), axis=1)
    g2 = jnp.concatenate([jnp.dot(ltri_ref[...], parts[r:r + CUM_ROWS], preferred_element_type=f32)
                          for r in range(0, TILE, CUM_ROWS)], axis=0)
    g = g2[:, 0:A_W] + g2[:, A_W:2 * A_W]
    v_b = v.astype(bf16)
    vt_b = v.T.astype(bf16)
    e_mid = g - _level_ref(g, CHUNK // 2)
    safe = jnp.max(jnp.abs(e_mid)) < SAFE_EXPONENT

    def expand(rows_b):
        return jnp.concatenate([rows_b * hm_ref[h] for h in range(4)], axis=0)

    def finish_chunk(c, att, q_in, k_hat):
        r0, r1 = c * CHUNK, (c + 1) * CHUNK
        st = st_ref[...]
        o = jnp.dot(att.astype(bf16), expand(v_b[r0:r1]), preferred_element_type=f32)
        o = o + lax.dot_general(q_in, st.astype(bf16) * e, (((1,), (1,)), ((), ())), preferred_element_type=f32)
        zero = jnp.zeros((CHUNK, A_W), bf16)
        pair = (c // 2) * 2 * CHUNK
        rhs = jnp.concatenate([k_hat, zero] if c % 2 == 0 else [zero, k_hat], axis=0)
        upd = jnp.dot(vt_b[:, pair:pair + 2 * CHUNK], rhs, preferred_element_type=f32)
        st_ref[...] = st * jnp.exp(g[r1 - 1:r1, :]) + upd
        o_ref[r0:r1, :] = o

    st0_ref[...] = st_ref[...]
    qw = q * jnp.exp(e_mid)
    kw = kin * jnp.exp(-e_mid)
    q_t, k_t = qw.astype(bf16), kw.astype(bf16)
    chunks = [(c * CHUNK, (c + 1) * CHUNK) for c in range(TILE // CHUNK)]
    atts = [lax.dot_general(q_t[r0:r1], expand(k_t[r0:r1]), (((1,), (1,)), ((), ())), preferred_element_type=f32)
            for r0, r1 in chunks]
    atts = [jnp.where(lvl_ref[len(LEVELS)] > 0.5, a, 0.0) for a in atts]
    for c, (r0, r1) in enumerate(chunks):
        g_mid, g_end = g[r0 + CHUNK // 2 - 1:r0 + CHUNK // 2, :], g[r1 - 1:r1, :]
        finish_chunk(c, atts[c], (qw[r0:r1] * jnp.exp(g_mid)).astype(bf16),
                     (kw[r0:r1] * jnp.exp(g_end - g_mid)).astype(bf16))

    def redo_if_unsafe():
        @pl.when(jnp.logical_not(safe))
        def _():
            st_ref[...] = st0_ref[...]
            q_levels, k_levels = [], []
            for m in LEVELS:
                w = jnp.exp(-jnp.abs(g - _level_ref(g, m)))
                q_levels.append((q * w).astype(bf16))
                k_levels.append((kin * w).astype(bf16))
            q_inter = (q * jnp.exp(g)).astype(bf16)
            for c in range(TILE // CHUNK):
                r0, r1 = c * CHUNK, (c + 1) * CHUNK
                att = jnp.zeros((CHUNK, 4 * CHUNK), f32)
                for li in range(len(LEVELS)):
                    a = lax.dot_general(q_levels[li][r0:r1], expand(k_levels[li][r0:r1]),
                                        (((1,), (1,)), ((), ())), preferred_element_type=f32)
                    att = att + a * lvl_ref[li]
                k_hat = (kin[r0:r1] * jnp.exp(g[r1 - 1:r1, :] - g[r0:r1])).astype(bf16)
                finish_chunk(c, att, q_inter[r0:r1], k_hat)
            o_ref[...] = o_ref[...] + _seg_sum(q * kin, e) * v

    return redo_if_unsafe


def _conv_tile(u, hist_ref, cw_ref, cvec_ref, em):
    n = u.shape[0]
    hist_ref[HIST:HIST + n, :] = u
    lo = HIST - (CONV_WIDTH - 1)
    y = cvec_ref[0:1, :]
    for r in range(8):
        rows = n if r == 0 else n + 8
        part = None
        for o in range(lo, lo + CONV_WIDTH):
            if o % 8 == r:
                term = hist_ref[o - r:o - r + rows, :] * cw_ref[o - lo:o - lo + 1, :]
                part = term if part is None else part + term
        y = y + part[r:r + n]
    return _group_ln_silu(y, em, cvec_ref[1:2, :], cvec_ref[2:3, :])


def _attn_tile(q_t, k_b, v_t, sink_ref, biases):
    zero = jnp.zeros((HEAD, WINDOW), f32)
    inst = [(b, g) for b in range(TILE // WINDOW) for g in range(2)]
    sinks = [jnp.concatenate([jnp.full((1, WINDOW), sink_ref[h] * LOG2E, f32) for h in range(4 * g, 4 * g + 4)],
                             axis=1) for g in range(2)]

    def rhs(b, g):
        cols = []
        for h in range(4 * g, 4 * g + 4):
            q_h = q_t[h * HEAD:(h + 1) * HEAD, b * WINDOW:(b + 1) * WINDOW]
            cols.append(jnp.concatenate([q_h, zero] if g == 0 else [zero, q_h], axis=0))
        return jnp.concatenate(cols, axis=1).astype(bf16)

    s = [jnp.dot(k_b[b * WINDOW:(b + 2) * WINDOW], rhs(b, g), preferred_element_type=f32) + biases[b]
         for b, g in inst]
    mx = [jnp.maximum(jnp.max(s[n], axis=0, keepdims=True), sinks[g]) for n, (b, g) in enumerate(inst)]
    pr = [jnp.exp2(s[n] - mx[n]) for n in range(len(inst))]
    rden = [1.0 / (jnp.sum(pr[n], axis=0, keepdims=True) + jnp.exp2(sinks[g] - mx[n]))
            for n, (b, g) in enumerate(inst)]
    o_t = [jnp.dot(v_t[g * HEAD:(g + 1) * HEAD, b * WINDOW:(b + 2) * WINDOW], pr[n].astype(bf16),
                   preferred_element_type=f32) * rden[n] for n, (b, g) in enumerate(inst)]
    rows = []
    for b in range(TILE // WINDOW):
        pairs = []
        for g in range(2):
            o = o_t[2 * b + g]
            for j in range(2):
                pair_t = jnp.concatenate([o[:, (2 * j) * WINDOW:(2 * j + 1) * WINDOW],
                                          o[:, (2 * j + 1) * WINDOW:(2 * j + 2) * WINDOW]], axis=0)
                pairs.append(pair_t.T)
        rows.append(jnp.concatenate(pairs, axis=1))
    return jnp.concatenate(rows, axis=0)


def _mixer_kernel(layer, x_ref, rope_row_ref, rope_tile_ref, win_ref, wout_ref, nmix_ref, lbp_ref, aon_ref, cw_ref,
                  cvec_ref, qk_ref, sinks_ref, e_ref, ltri_ref, lvl_ref, hm_ref, bias_ref,
                  h_ref, st_ref, conv_ref, kn_ref, vn_ref, z_ref, hist_ref, o_ref, st0_ref):
    i = pl.program_id(0)
    sink_ref = sinks_ref.at[layer]

    @pl.when(i == 0)
    def _():
        st_ref[...] = jnp.zeros_like(st_ref)
        hist_ref[0:HIST, :] = jnp.zeros((HIST, B_W), f32)
        kn_ref[...] = jnp.zeros_like(kn_ref)
        vn_ref[...] = jnp.zeros_like(vn_ref)

    x = x_ref[...]
    xn = _rmsnorm_rows(x, nmix_ref[...])
    z_ref[...] = jnp.dot(xn.astype(bf16), win_ref[...], preferred_element_type=f32)
    e = e_ref[...]

    redo_hgrn_if_unsafe = _hgrn_tile(z_ref[:, 0:A_W], z_ref[:, A_W:2 * A_W], z_ref[:, 2 * A_W:3 * A_W], lbp_ref,
                                     st_ref, st0_ref, o_ref, e, ltri_ref, lvl_ref, hm_ref)

    c0 = 4 * A_W
    u = z_ref[:, c0:c0 + B_W] * _sigmoid(z_ref[:, c0 + B_W:c0 + 2 * B_W])
    em = _mean_matrix(e)
    ob = _conv_tile(u, hist_ref, cw_ref, cvec_ref, em)
    conv_ref[...] = hist_ref[TILE + HIST - (CONV_WIDTH - 1):TILE + HIST, :]
    hist_ref[0:HIST, :] = hist_ref[TILE:TILE + HIST, :]

    c1 = c0 + 2 * B_W
    cos_a, sin_a = rope_tile_ref[0, pl.ds(i, 1), :], rope_tile_ref[1, pl.ds(i, 1), :]
    cos = cos_a * rope_row_ref[0] - sin_a * rope_row_ref[1]
    sin = sin_a * rope_row_ref[2] + cos_a * rope_row_ref[3]
    qn = jnp.concatenate([_head_norm(z_ref[:, c1 + j * 256:c1 + (j + 1) * 256], em,
                                     jnp.concatenate([qk_ref[0:1, :]] * 2, axis=1)) for j in range(2)], axis=1)
    q = _rope(qn, jnp.concatenate([cos] * 4, axis=1), jnp.concatenate([sin] * 4, axis=1)) * (LOG2E * HEAD ** -0.5)
    kk = _rope(_head_norm(z_ref[:, c1 + C_W:c1 + C_W + KV_W], em[0:KV_W, 0:KV_W], qk_ref[1:2, :]), cos, sin)
    vv = z_ref[:, c1 + C_W + KV_W:c1 + C_W + 2 * KV_W]
    kcat = jnp.concatenate([kn_ref[...], kk], axis=0)
    vcat = jnp.concatenate([vn_ref[...], vv], axis=0)
    q_t = q.T
    k_b = kcat.astype(bf16)
    v_t = vcat.T.astype(bf16)
    biases = [bias_ref[jnp.where(i > 0, 1, 0)]] + [bias_ref[1]] * (TILE // WINDOW - 1)
    oc = _attn_tile(q_t, k_b, v_t, sink_ref, biases)
    kn_ref[...] = kk[TILE - WINDOW:TILE]
    vn_ref[...] = vv[TILE - WINDOW:TILE]

    redo_hgrn_if_unsafe()
    gate = z_ref[:, 3 * A_W:4 * A_W]
    oa = _head_norm(o_ref[...], em, aon_ref[...]) * (gate * _sigmoid(gate))
    mix = jnp.concatenate([oa, ob, oc], axis=1).astype(bf16)
    h_ref[...] = x + jnp.dot(mix, wout_ref[...], preferred_element_type=f32)


def _const_spec(shape):
    nd = len(shape)
    return pl.BlockSpec(shape, lambda i, _n=nd: (0,) * _n)


def _layer_spec(arr, l, **kw):
    shape = arr.shape[1:]
    return pl.BlockSpec((None,) + shape, lambda i, _l=l, _n=len(shape): (_l,) + (0,) * _n, **kw)


def _ffn_kernel(h_ref, p_ref, wg_ref, wu_ref, wd_ref, wpg_ref, wpp_ref, norms_ref, y_ref):
    n = h_ref.shape[0]
    slabs = [slice(r, r + n // FFN_SLABS) for r in range(0, n, n // FFN_SLABS)]
    h = [h_ref[s, :] for s in slabs]
    hn = [_rmsnorm_rows(a, norms_ref[0:1, :]).astype(bf16) for a in h]
    gt = [jnp.dot(a, wg_ref[...], preferred_element_type=f32) for a in hn]
    up = [jnp.dot(a, wu_ref[...], preferred_element_type=f32) for a in hn]
    act = [(g * _sigmoid(g) * u).astype(bf16) for g, u in zip(gt, up)]
    h = [a + jnp.dot(b, wd_ref[...], preferred_element_type=f32) for a, b in zip(h, act)]
    pn = [_rmsnorm_rows(a, norms_ref[1:2, :]).astype(bf16) for a in h]
    gate = [_sigmoid(jnp.dot(a, wpg_ref[...], preferred_element_type=f32)) for a in pn]
    for s, a, g in zip(slabs, h, gate):
        y_ref[s, :] = a + g * jnp.dot(p_ref[s, :].astype(bf16), wpp_ref[...], preferred_element_type=f32)


def _ffn(h, p, wl, l):
    n = h.shape[0]
    tile = min(FFN_TILE, n)
    assert n % tile == 0
    row_spec = lambda w: pl.BlockSpec((tile, w), lambda i: (i, 0))
    ws = [wl['w_gate'], wl['w_up'], wl['w_down'], wl['w_ple_gate'], wl['w_ple_proj']]
    return pl.pallas_call(
        _ffn_kernel, grid=(n // tile,),
        in_specs=[row_spec(D_MODEL), pl.BlockSpec((None, tile, PLE_DIM), lambda i, _l=l: (_l, i, 0))]
        + [_layer_spec(a, l, pipeline_mode=pl.Buffered(1)) for a in ws] + [_layer_spec(wl['fnorms'], l)],
        out_specs=row_spec(D_MODEL), out_shape=jax.ShapeDtypeStruct((n, D_MODEL), f32),
        compiler_params=pltpu.CompilerParams(dimension_semantics=("arbitrary",), vmem_limit_bytes=VMEM_LIMIT),
        name="ffn_ple",
    )(h, p, *ws, wl['fnorms'])


N_MIXER_INPUTS = 17
N_FFN_INPUTS = 7


def _layer_kernel(layer, *refs):
    n_in, n_out = N_MIXER_INPUTS + N_FFN_INPUTS, 5
    mixer_in, ffn_in = refs[:N_MIXER_INPUTS], refs[N_MIXER_INPUTS:n_in]
    y_ref, st_ref, conv_ref, kn_ref, vn_ref = refs[n_in:n_in + n_out]
    z_ref, hist_ref, o_ref, st0_ref, mid_ref = refs[n_in + n_out:]
    _mixer_kernel(layer, *mixer_in, mid_ref, st_ref, conv_ref, kn_ref, vn_ref, z_ref, hist_ref, o_ref, st0_ref)
    _ffn_kernel(mid_ref, *ffn_in, y_ref)


def _prompt_layer(x, p, rope, wl, l, consts):
    t_len = x.shape[0]
    row_spec = lambda w: pl.BlockSpec((TILE, w), lambda i: (i, 0))
    once = dict(pipeline_mode=pl.Buffered(1))
    smalls = [wl['nmix'], wl['lbp'], wl['aon'], wl['cw'], wl['cvec'], wl['qk']]
    ffn_ws = [wl['w_gate'], wl['w_up'], wl['w_down'], wl['w_ple_gate'], wl['w_ple_proj']]
    in_specs = ([row_spec(D_MODEL), _const_spec(rope[0].shape), _const_spec(rope[1].shape),
                 _layer_spec(wl['w_in'], l, **once), _layer_spec(wl['w_out'], l, **once)]
                + [_layer_spec(a, l) for a in smalls] + [pl.BlockSpec(memory_space=pltpu.SMEM)]
                + [_const_spec(a.shape) for a in consts]
                + [pl.BlockSpec((None, TILE, PLE_DIM), lambda i, _l=l: (_l, i, 0))]
                + [_layer_spec(a, l, **once) for a in ffn_ws] + [_layer_spec(wl['fnorms'], l)])
    assert len(in_specs) == N_MIXER_INPUTS + N_FFN_INPUTS
    out_shape = (jax.ShapeDtypeStruct((t_len, D_MODEL), f32),
                 jax.ShapeDtypeStruct((A_W, A_W), f32),
                 jax.ShapeDtypeStruct((CONV_WIDTH - 1, B_W), f32),
                 jax.ShapeDtypeStruct((WINDOW, KV_W), f32),
                 jax.ShapeDtypeStruct((WINDOW, KV_W), f32))
    out_specs = (row_spec(D_MODEL), _const_spec((A_W, A_W)), _const_spec((CONV_WIDTH - 1, B_W)),
                 _const_spec((WINDOW, KV_W)), _const_spec((WINDOW, KV_W)))
    return pl.pallas_call(
        functools.partial(_layer_kernel, l), grid=(t_len // TILE,), in_specs=in_specs, out_specs=out_specs,
        out_shape=out_shape,
        scratch_shapes=[pltpu.VMEM((TILE, IN_COLS), f32), pltpu.VMEM((TILE + HIST, B_W), f32),
                        pltpu.VMEM((TILE, A_W), f32), pltpu.VMEM((A_W, A_W), f32),
                        pltpu.VMEM((TILE, D_MODEL), f32)],
        compiler_params=pltpu.CompilerParams(dimension_semantics=("arbitrary",),
                                             vmem_limit_bytes=LAYER_VMEM_LIMIT),
        name="prompt_layer",
    )(x, rope[0], rope[1], wl['w_in'], wl['w_out'], *smalls, wl['sinks'], *consts, p, *ffn_ws, wl['fnorms'])


N_HEADS = C_W // HEAD


def _sample_in_kernel(n_tok, x_ref, cos_ref, sin_ref, win_ref, nmix_ref, qk_ref, e_ref, z_ref, qx_ref, k_ref,
                      piece_ref):
    n_seq = x_ref.shape[0] // n_tok
    xn = _rmsnorm_rows(x_ref[...], nmix_ref[...])
    z = jnp.dot(xn.astype(bf16), win_ref[...], preferred_element_type=f32)
    z_ref[...] = z
    em = _mean_matrix(e_ref[...])
    c1 = 4 * A_W + 2 * B_W
    cos, sin = cos_ref[...], sin_ref[...]
    qn = jnp.concatenate([_head_norm(z[:, c1 + j * 256:c1 + (j + 1) * 256], em,
                                     jnp.concatenate([qk_ref[0:1, :]] * 2, axis=1)) for j in range(2)], axis=1)
    q = _rope(qn, jnp.concatenate([cos] * 4, axis=1), jnp.concatenate([sin] * 4, axis=1)) * (HEAD ** -0.5)
    k_ref[...] = _rope(_head_norm(z[:, c1 + C_W:c1 + C_W + KV_W], em[0:KV_W, 0:KV_W], qk_ref[1:2, :]), cos, sin)
    lo_f = lax.broadcasted_iota(jnp.int32, (x_ref.shape[0], KV_W), 1) < HEAD
    for h in range(N_HEADS):
        pair, g = q[:, (h // 2) * KV_W:(h // 2 + 1) * KV_W], h // 4
        own = jnp.where(lo_f if h % 2 == 0 else jnp.logical_not(lo_f), pair, 0.0)
        piece_ref[...] = own if h % 2 == g else pltpu.roll(own, HEAD, 1)
        for t in range(n_tok):
            qx_ref[pl.ds(h * n_tok + t, n_seq, stride=N_HEADS * n_tok), :] = piece_ref[pl.ds(t, n_seq, stride=n_tok), :]


def _sample_in(x, cos, sin, wl, l, e256, n_tok):
    n = x.shape[0]
    args = (x, cos, sin, wl['w_in'], wl['nmix'], wl['qk'], e256)
    specs = [_const_spec(a.shape) for a in args]
    specs[3:6] = [_layer_spec(a, l) for a in args[3:6]]
    return pl.pallas_call(
        functools.partial(_sample_in_kernel, n_tok), grid=(1,), in_specs=specs,
        out_specs=(_const_spec((n, IN_COLS)), _const_spec((n * N_HEADS, KV_W)), _const_spec((n, KV_W))),
        out_shape=(jax.ShapeDtypeStruct((n, IN_COLS), f32), jax.ShapeDtypeStruct((n * N_HEADS, KV_W), f32),
                   jax.ShapeDtypeStruct((n, KV_W), f32)),
        scratch_shapes=[pltpu.VMEM((n, KV_W), f32)],
        compiler_params=pltpu.CompilerParams(dimension_semantics=("arbitrary",), vmem_limit_bytes=VMEM_LIMIT),
        name="sample_in",
    )(*args)


def _sample_hgrn_kernel(n_tok, q_ref, f_ref, v_ref, s_ref, lbp_ref, o_ref, so_ref, qt_scr, ft_scr, kt_scr, vt_scr,
                        ot_scr):
    n_seq = s_ref.shape[1]
    for t in range(n_tok):
        zf = f_ref[pl.ds(t, n_seq, stride=n_tok), :]
        logf, ls = _log_forget(zf, lbp_ref[0:1, :], lbp_ref[1:2, :])
        ft_scr[t] = jnp.exp(logf).T
        kt_scr[t] = (lbp_ref[2:3, :] * jnp.exp(ls - zf)).T
        qt_scr[t] = q_ref[pl.ds(t, n_seq, stride=n_tok), :].T
        vt_scr[t] = v_ref[pl.ds(t, n_seq, stride=n_tok), :].T
        ot_scr[t] = jnp.zeros((2 * HEAD, n_seq), f32)

    def body(hk, carry):
        r0 = pl.multiple_of(hk * HEAD, HEAD)
        v0 = pl.multiple_of((hk // HEAD) * HEAD, HEAD)
        blk = s_ref[pl.ds(r0, HEAD), :]
        for t in range(n_tok):
            blk = blk * ft_scr[t, pl.ds(hk, 1), :] + kt_scr[t, pl.ds(hk, 1), :] * vt_scr[t, pl.ds(v0, HEAD), :]
            ot_scr[t, pl.ds(v0, HEAD), :] = ot_scr[t, pl.ds(v0, HEAD), :] + qt_scr[t, pl.ds(hk, 1), :] * blk
        so_ref[pl.ds(r0, HEAD), :] = blk
        return carry

    lax.fori_loop(0, 2 * HEAD, body, 0, unroll=4)
    for t in range(n_tok):
        o_ref[pl.ds(t, n_seq, stride=n_tok), :] = ot_scr[t].T


def _sample_hgrn(z, state, lbp, l, n_tok):
    n = z.shape[0]
    n_seq = state.shape[2]
    blk = 2 * HEAD * HEAD
    col = lambda j0: pl.BlockSpec((n, 2 * HEAD), lambda i, _j=j0: (0, _j + i))
    return pl.pallas_call(
        functools.partial(_sample_hgrn_kernel, n_tok), grid=(2,),
        in_specs=[col(0), col(2), col(4), pl.BlockSpec((None, blk, n_seq), lambda i, _l=l: (_l, i, 0)),
                  pl.BlockSpec((None, 3, 2 * HEAD), lambda i, _l=l: (_l, 0, i))],
        out_specs=(pl.BlockSpec((n, 2 * HEAD), lambda i: (0, i)), pl.BlockSpec((blk, n_seq), lambda i: (i, 0))),
        out_shape=(jax.ShapeDtypeStruct((n, A_W), f32), jax.ShapeDtypeStruct(state.shape[1:], f32)),
        scratch_shapes=[pltpu.VMEM((n_tok, 2 * HEAD, n_seq), f32)] * 5,
        compiler_params=pltpu.CompilerParams(dimension_semantics=("arbitrary",), vmem_limit_bytes=VMEM_LIMIT),
        name="sample_hgrn",
    )(z, z, z, state, lbp)


def _sample_conv_kernel(n_tok, first, u0_ref, u1_ref, g0_ref, g1_ref, buf_ref, cw_ref, cvec_ref, e_ref, *rest):
    ob0_ref, ob1_ref, new_ref = rest[-3:]
    slots = [new_ref.at[d] for d in range(new_ref.shape[0])] if first else [new_ref]
    n_seq = buf_ref.shape[1]
    n_hist = CONV_WIDTH - 1
    us = []
    for t in range(n_tok):
        rows = pl.ds(t, n_seq, stride=n_tok)
        u = jnp.concatenate([u0_ref[rows, :], u1_ref[rows, :]], axis=1)
        g = jnp.concatenate([g0_ref[rows, :], g1_ref[rows, :]], axis=1)
        us.append(u * _sigmoid(g))

    def slab(j):
        return buf_ref[j] if j < n_hist else us[j - n_hist]

    em = _mean_matrix(e_ref[...])
    for t in range(n_tok):
        y = cvec_ref[0:1, :]
        for j in range(CONV_WIDTH):
            y = y + slab(t + j) * cw_ref[j:j + 1, :]
        ob = _group_ln_silu(y, em, cvec_ref[1:2, :], cvec_ref[2:3, :])
        ob0_ref[pl.ds(t, n_seq, stride=n_tok), :] = ob[:, 0:KV_W]
        ob1_ref[pl.ds(t, n_seq, stride=n_tok), :] = ob[:, KV_W:B_W]
    for dst in slots:
        for j in range(n_hist):
            dst[j] = slab(j + n_tok)


def _sample_conv(z, buf, new_all, wl, l, e256, n_tok):
    n = z.shape[0]
    col = lambda j: pl.BlockSpec((n, KV_W), lambda i, _j=j: (0, _j))
    half = jax.ShapeDtypeStruct((n, KV_W), f32)
    in_specs = [col(8), col(9), col(10), col(11), _layer_spec(buf, l), _layer_spec(wl['cw'], l),
                _layer_spec(wl['cvec'], l), _const_spec(e256.shape)]
    args = [z, z, z, z, buf, wl['cw'], wl['cvec'], e256]
    if new_all is None:
        new_spec, aliases = _const_spec(buf.shape), {}
    else:
        in_specs.append(pl.BlockSpec(memory_space=pl.ANY))
        args.append(new_all)
        new_spec, aliases = _layer_spec(buf, l), {len(args) - 1: 2}
    return pl.pallas_call(
        functools.partial(_sample_conv_kernel, n_tok, new_all is None), grid=(1,), in_specs=in_specs,
        out_specs=(_const_spec((n, KV_W)), _const_spec((n, KV_W)), new_spec),
        out_shape=(half, half, jax.ShapeDtypeStruct(buf.shape, f32)),
        input_output_aliases=aliases,
        compiler_params=pltpu.CompilerParams(dimension_semantics=("arbitrary",), vmem_limit_bytes=VMEM_LIMIT),
        name="sample_conv",
    )(*args)


SEQ_BLOCK = 32


def _sample_attn_kernel(n_tok, qx_ref, k_ref, v_ref, kc_ref, vc_ref, sink_ref, bias_ref, ox_ref, kn_ref, vn_ref):
    q_rows = N_HEADS * n_tok
    sink, bias = sink_ref[...], bias_ref[...]
    seqs = range(SEQ_BLOCK)
    k_all = [jnp.concatenate([kc_ref[b], k_ref[b * n_tok:(b + 1) * n_tok, :]], axis=0) for b in seqs]
    v_all = [jnp.concatenate([vc_ref[b], v_ref[b * n_tok:(b + 1) * n_tok, :]], axis=0) for b in seqs]
    s = [lax.dot_general(qx_ref[b * q_rows:(b + 1) * q_rows, :].astype(bf16), k_all[b].astype(bf16),
                         (((1,), (1,)), ((), ())), preferred_element_type=f32) + bias for b in seqs]
    mx = [jnp.maximum(jnp.max(s[b], axis=-1, keepdims=True), sink) for b in seqs]
    pr = [jnp.exp(s[b] - mx[b]) for b in seqs]
    rden = [1.0 / (jnp.sum(pr[b], axis=-1, keepdims=True) + jnp.exp(sink - mx[b])) for b in seqs]
    for b in seqs:
        ox_ref[b * q_rows:(b + 1) * q_rows, :] = jnp.dot(pr[b].astype(bf16), v_all[b].astype(bf16),
                                                         preferred_element_type=f32) * rden[b]
    for b in seqs:
        kn_ref[b] = k_all[b][n_tok:, :]
        vn_ref[b] = v_all[b][n_tok:, :]


def _sample_attn(qx, k, z, kc, vc, sink_col, bias, l, n_tok):
    n_seq = kc.shape[1]
    assert n_seq % SEQ_BLOCK == 0
    rows = SEQ_BLOCK * n_tok
    q_rows = SEQ_BLOCK * N_HEADS * n_tok
    v_col = (4 * A_W + 2 * B_W + C_W + KV_W) // KV_W
    cache_in = pl.BlockSpec((None, SEQ_BLOCK) + kc.shape[2:], lambda i, _l=l: (_l, i, 0, 0))
    cache_out = pl.BlockSpec((SEQ_BLOCK,) + kc.shape[2:], lambda i: (i, 0, 0))
    return pl.pallas_call(
        functools.partial(_sample_attn_kernel, n_tok), grid=(n_seq // SEQ_BLOCK,),
        in_specs=[pl.BlockSpec((q_rows, KV_W), lambda i: (i, 0)), pl.BlockSpec((rows, KV_W), lambda i: (i, 0)),
                  pl.BlockSpec((rows, KV_W), lambda i: (i, v_col)), cache_in, cache_in,
                  _layer_spec(sink_col, l), _const_spec(bias.shape)],
        out_specs=(pl.BlockSpec((q_rows, KV_W), lambda i: (i, 0)), cache_out, cache_out),
        out_shape=(jax.ShapeDtypeStruct(qx.shape, f32), jax.ShapeDtypeStruct(kc.shape[1:], f32),
                   jax.ShapeDtypeStruct(vc.shape[1:], f32)),
        compiler_params=pltpu.CompilerParams(dimension_semantics=("arbitrary",), vmem_limit_bytes=VMEM_LIMIT),
        name="sample_attn",
    )(qx, k, z, kc, vc, sink_col, bias)


def _sample_out_kernel(n_tok, x_ref, oa_ref, g_ref, ob0_ref, ob1_ref, ox_ref, wout_ref, aon_ref, e_ref, h_ref,
                       oc_ref):
    n_seq = x_ref.shape[0] // n_tok
    gate = g_ref[...]
    oa = _head_norm(oa_ref[...], _mean_matrix(e_ref[...]), aon_ref[...]) * (gate * _sigmoid(gate))
    lo_f = lax.broadcasted_iota(jnp.int32, (n_seq, KV_W), 1) < HEAD
    for p in range(N_HEADS // 2):
        g = p // 2
        for t in range(n_tok):
            even = ox_ref[pl.ds((2 * p) * n_tok + t, n_seq, stride=N_HEADS * n_tok), :]
            odd = ox_ref[pl.ds((2 * p + 1) * n_tok + t, n_seq, stride=N_HEADS * n_tok), :]
            even = even if g == 0 else pltpu.roll(even, HEAD, 1)
            odd = odd if g == 1 else pltpu.roll(odd, HEAD, 1)
            oc_ref[p, pl.ds(t, n_seq, stride=n_tok), :] = jnp.where(lo_f, even, odd)
    mix = jnp.concatenate([oa, ob0_ref[...], ob1_ref[...]] + [oc_ref[p] for p in range(N_HEADS // 2)],
                          axis=1).astype(bf16)
    h_ref[...] = x_ref[...] + jnp.dot(mix, wout_ref[...], preferred_element_type=f32)


def _sample_out(x, oa, z, ob0, ob1, ox, wl, l, e256, n_tok):
    n = x.shape[0]
    gate_spec = pl.BlockSpec((n, A_W), lambda i: (0, 3))
    args = (x, oa, z, ob0, ob1, ox, wl['w_out'], wl['aon'], e256)
    specs = [_const_spec(a.shape) for a in args]
    specs[2] = gate_spec
    specs[6:8] = [_layer_spec(a, l) for a in args[6:8]]
    return pl.pallas_call(
        functools.partial(_sample_out_kernel, n_tok), grid=(1,), in_specs=specs,
        out_specs=_const_spec((n, D_MODEL)), out_shape=jax.ShapeDtypeStruct((n, D_MODEL), f32),
        scratch_shapes=[pltpu.VMEM((N_HEADS // 2, n, KV_W), f32)],
        compiler_params=pltpu.CompilerParams(dimension_semantics=("arbitrary",), vmem_limit_bytes=VMEM_LIMIT),
        name="sample_out",
    )(*args)


def _rope_cos_sin(pos):
    half = HEAD // 2
    inv = ROPE_THETA ** (-jnp.arange(half, dtype=f32) / half)
    ang = pos[:, None] * inv[None, :]
    cos, sin = jnp.cos(ang), jnp.sin(ang)
    return jnp.tile(cos, (1, 4)), jnp.concatenate([-sin, sin, -sin, sin], axis=1)


def _rope_prompt_tables(t_len):
    half = HEAD // 2
    inv = ROPE_THETA ** (-jnp.arange(half, dtype=f32) / half)
    sign = jnp.concatenate([-jnp.ones((half,), f32), jnp.ones((half,), f32)] * 2)
    ang_r = jnp.arange(TILE, dtype=f32)[:, None] * inv[None, :]
    ang_t = (jnp.arange(t_len // TILE, dtype=f32) * TILE)[:, None] * inv[None, :]
    cos_r, sin_r = jnp.tile(jnp.cos(ang_r), (1, 4)), jnp.tile(jnp.sin(ang_r), (1, 4))
    rows = jnp.stack([cos_r, sin_r, cos_r * sign, sin_r * sign])
    tiles = jnp.stack([jnp.tile(jnp.cos(ang_t), (1, 4)), jnp.tile(jnp.sin(ang_t), (1, 4))])
    return rows, tiles


def _stacked_weights(lbs, n_tok, w_in, a_onorm, conv_w, conv_b, conv_ln_g, conv_ln_b, q_norm, k_norm, sinks, w_out,
                     norm_mix, norm_ffn, w_gate, w_up, w_down, ple_norm, w_ple_gate, w_ple_proj):
    return dict(
        w_in=w_in.astype(bf16), w_out=w_out.astype(bf16), w_gate=w_gate.astype(bf16), w_up=w_up.astype(bf16),
        w_down=w_down.astype(bf16), w_ple_gate=w_ple_gate.astype(bf16), w_ple_proj=w_ple_proj.astype(bf16),
        nmix=norm_mix[:, None, :],
        lbp=jnp.stack([jnp.log(lbs), jnp.log1p(-lbs), 1.0 - lbs], axis=1),
        aon=jnp.tile(a_onorm, (1, 4))[:, None, :],
        cw=conv_w,
        cvec=jnp.stack([conv_b, conv_ln_g, conv_ln_b], axis=1),
        qk=jnp.stack([jnp.tile(q_norm, (1, 2)), jnp.tile(k_norm, (1, 2))], axis=1),
        sinks=sinks,
        sink_col=jnp.repeat(sinks, n_tok, axis=1)[:, :, None],
        fnorms=jnp.stack([norm_ffn, ple_norm], axis=1),
    )


def kernel(x_prompt, x_sample, state_hgrn, state_conv, cache_swa_k, cache_swa_v, p_prompt, p_sample, a_lower, w_in, a_onorm, conv_w, conv_b, conv_ln_g, conv_ln_b, q_norm, k_norm, sinks, w_out, norm_mix, norm_ffn, w_gate, w_up, w_down, ple_norm, w_ple_gate, w_ple_proj):
    depth = w_in.shape[0]
    t_len = x_prompt.shape[1]
    n_seq, n_tok = x_sample.shape[0], x_sample.shape[1]
    past_len = PAST_LEN
    assert x_prompt.shape[0] == 1 and t_len % TILE == 0

    lbs = jnp.cumsum(jax.nn.softmax(a_lower.astype(f32), axis=0), axis=0)
    lbs = lbs - lbs[0:1]
    consts_np = _np_constants()
    e256 = jnp.asarray(consts_np[0], bf16)
    consts = (e256, jnp.asarray(consts_np[1], bf16), jnp.asarray(consts_np[2], f32), jnp.asarray(consts_np[3], bf16),
              jnp.asarray(consts_np[4], f32))

    rope_p = _rope_prompt_tables(t_len)
    cos_s, sin_s = _rope_cos_sin(past_len + jnp.arange(n_tok, dtype=f32))
    cos_s, sin_s = jnp.tile(cos_s, (n_seq, 1)), jnp.tile(sin_s, (n_seq, 1))
    w_buf = cache_swa_k.shape[2]
    q_tok = (np.arange(N_HEADS * n_tok) % n_tok)[:, None]
    rel = q_tok + w_buf - np.arange(w_buf + n_tok)[None, :]
    bias_s = jnp.asarray(np.where((rel >= 0) & (rel <= WINDOW), 0.0, -np.inf), f32)

    wl = _stacked_weights(lbs, n_tok, w_in, a_onorm, conv_w, conv_b, conv_ln_g, conv_ln_b, q_norm, k_norm, sinks,
                          w_out, norm_mix, norm_ffn, w_gate, w_up, w_down, ple_norm, w_ple_gate, w_ple_proj)
    p_p = p_prompt.reshape(depth, t_len, PLE_DIM)
    p_s = p_sample.reshape(depth, n_seq * n_tok, PLE_DIM)
    s_hgrn = jnp.swapaxes(state_hgrn.reshape(depth, n_seq, -1), 1, 2)
    kc = cache_swa_k.reshape(depth, n_seq, w_buf, KV_W)
    vc = cache_swa_v.reshape(depth, n_seq, w_buf, KV_W)

    hp = x_prompt[0]
    hs = x_sample.reshape(n_seq * n_tok, D_MODEL)
    outs = [[] for _ in range(8)]
    conv_in = jnp.swapaxes(state_conv, 1, 2)
    conv_s = None
    for l in range(depth):
        hp, st, conv_new, k_new, v_new = _prompt_layer(hp, p_p, rope_p, wl, l, consts)
        s_new = jnp.stack([st[h * HEAD:(h + 1) * HEAD, h * HEAD:(h + 1) * HEAD].T for h in range(4)])
        outs[0].append(s_new[None])
        outs[1].append(conv_new[None])
        outs[2].append(k_new.reshape(1, WINDOW, 2, HEAD))
        outs[3].append(v_new.reshape(1, WINDOW, 2, HEAD))
        z, qx, k = _sample_in(hs, cos_s, sin_s, wl, l, e256, n_tok)
        oa, s_s = _sample_hgrn(z, s_hgrn, wl['lbp'], l, n_tok)
        ob0, ob1, conv_s = _sample_conv(z, conv_in, conv_s, wl, l, e256, n_tok)
        ox, k_s, v_s = _sample_attn(qx, k, z, kc, vc, wl['sink_col'], bias_s, l, n_tok)
        h_mid_s = _sample_out(hs, oa, z, ob0, ob1, ox, wl, l, e256, n_tok)
        hs = _ffn(h_mid_s, p_s, wl, l)
        outs[4].append(s_s.T.reshape(n_seq, 4, HEAD, HEAD))
        outs[6].append(k_s.reshape(n_seq, w_buf, 2, HEAD))
        outs[7].append(v_s.reshape(n_seq, w_buf, 2, HEAD))
    stacked = [jnp.swapaxes(conv_s, 1, 2) if i == 5 else jnp.stack(o) for i, o in enumerate(outs)]
    return (hp[None], hs.reshape(n_seq, n_tok, D_MODEL)) + tuple(stacked)
```

```python
import functools

import numpy as np
import jax
import jax.numpy as jnp
from jax import lax
from jax.experimental import pallas as pl
from jax.experimental.pallas import tpu as pltpu

f32 = jnp.float32
bf16 = jnp.bfloat16

D_MODEL = 1024
HEAD = 64
A_W = 256
B_W = 256
C_W = 512
KV_W = 128
IN_COLS = 4 * A_W + 2 * B_W + C_W + 2 * KV_W
PLE_DIM = 256
WINDOW = 128
CONV_WIDTH = 31
ROPE_THETA = 10000.0
EPS = 1e-6
PAST_LEN = 16384

TILE = 512
CUM_ROWS = 256
CHUNK = 64
LEVELS = (32, 16, 8, 4, 2, 1)
SAFE_EXPONENT = 80.0
HIST = 32
FFN_SLABS = 4
LOG2E = 1.4426950408889634
VMEM_LIMIT = 52 * 1024 * 1024
LAYER_VMEM_LIMIT = 58 * 1024 * 1024


def _split2(x):
    hi = x.astype(bf16)
    lo = (x - hi.astype(f32)).astype(bf16)
    return hi, lo


def _seg_sum(x, e):
    hi, lo = _split2(x)
    return (jnp.dot(hi, e, preferred_element_type=f32) + jnp.dot(lo, e, preferred_element_type=f32))


def _sigmoid(x):
    return 1.0 / (1.0 + jnp.exp(-x))


def _rmsnorm_rows(x, g):
    return x * lax.rsqrt(jnp.mean(x * x, axis=-1, keepdims=True) + EPS) * g


def _seg_stat(x, e):
    return jnp.dot(x.astype(bf16), e, preferred_element_type=f32)


def _mean_matrix(e):
    return e * (1.0 / HEAD)


def _head_norm(x, em, g):
    return x * lax.rsqrt(_seg_stat(x * x, em) + EPS) * g


def _rope(x, cos, sin_signed):
    w = x.shape[-1]
    lane = lax.broadcasted_iota(jnp.int32, x.shape, 1)
    first = (lane % HEAD) < (HEAD // 2)
    swapped = jnp.where(first, pltpu.roll(x, w - HEAD // 2, 1), pltpu.roll(x, HEAD // 2, 1))
    return x * cos + swapped * sin_signed


def _log_forget(zf, log_lb, log_1m_lb):
    ls = jnp.minimum(zf, 0.0) - jnp.log(1.0 + jnp.exp(-jnp.abs(zf)))
    b = log_1m_lb + ls
    return jnp.maximum(log_lb, b) + jnp.log(1.0 + jnp.exp(-jnp.abs(log_lb - b))), ls


def _group_ln_silu(y, em, g, b):
    yc = y - _seg_stat(y, em)
    var = _seg_stat(yc * yc, em)
    yn = yc * lax.rsqrt(var + EPS) * g + b
    return yn * _sigmoid(yn)


def _np_constants():
    lane = np.arange(256)
    e256 = (lane[:, None] // HEAD == lane[None, :] // HEAD).astype(np.float32)
    t = np.arange(CUM_ROWS)
    ltri = ((t[:, None] // CHUNK == t[None, :] // CHUNK) & (t[None, :] <= t[:, None])).astype(np.float32)
    tt = np.arange(CHUNK)[:, None]
    ss = (np.arange(256) % CHUNK)[None, :]
    lvl = []
    for m in LEVELS:
        lvl.append(((tt // (2 * m) == ss // (2 * m)) & (tt % (2 * m) >= m) & (ss % (2 * m) < m)).astype(np.float32))
    lvl.append((ss <= tt).astype(np.float32))
    lvl = np.stack(lvl)
    hm = np.stack([np.broadcast_to((lane // HEAD == h)[None, :], (CHUNK, 256)) for h in range(4)]).astype(np.float32)
    keys = np.arange(2 * WINDOW)[:, None]
    qry = np.arange(4 * WINDOW)[None, :] % WINDOW
    band = (keys >= qry) & (keys <= qry + WINDOW)
    bias = np.where(np.stack([band & (keys >= WINDOW), band]), 0.0, -np.inf).astype(np.float32)
    return e256, ltri, lvl, hm, bias


def _level_ref(g, m):
    n_rows = g.shape[0]
    if 2 * m >= 8:
        g3 = g.reshape(n_rows // (2 * m), 2 * m, g.shape[1])
        return jnp.broadcast_to(g3[:, m - 1:m, :], g3.shape).reshape(g.shape)
    g3 = g.reshape(n_rows // 8, 8, g.shape[1])
    sub = lax.broadcasted_iota(jnp.int32, g3.shape, 1)
    out = None
    for blk in range(8 // (2 * m)):
        r = blk * 2 * m + m - 1
        piece = jnp.broadcast_to(g3[:, r:r + 1, :], g3.shape)
        out = piece if out is None else jnp.where(sub >= blk * 2 * m, piece, out)
    return out.reshape(g.shape)


def _hgrn_tile(q, zf, v, lbp_ref, st_ref, st0_ref, o_ref, e, ltri_ref, lvl_ref, hm_ref):
    log_lb, log_1m_lb, one_m_lb = lbp_ref[0:1, :], lbp_ref[1:2, :], lbp_ref[2:3, :]
    logf, ls = _log_forget(zf, log_lb, log_1m_lb)
    kin = one_m_lb * jnp.exp(ls - zf)
    parts = jnp.concatenate(_split2(logf), axis=1)
    g2 = jnp.concatenate([jnp.dot(ltri_ref[...], parts[r:r + CUM_ROWS], preferred_element_type=f32)
                          for r in range(0, TILE, CUM_ROWS)], axis=0)
    g = g2[:, 0:A_W] + g2[:, A_W:2 * A_W]
    v_b = v.astype(bf16)
    vt_b = v.T.astype(bf16)
    e_mid = g - _level_ref(g, CHUNK // 2)
    safe = jnp.max(jnp.abs(e_mid)) < SAFE_EXPONENT

    def expand(rows_b):
        return jnp.concatenate([rows_b * hm_ref[h] for h in range(4)], axis=0)

    def finish_chunk(c, att, q_in, k_hat):
        r0, r1 = c * CHUNK, (c + 1) * CHUNK
        st = st_ref[...]
        o = jnp.dot(att.astype(bf16), expand(v_b[r0:r1]), preferred_element_type=f32)
        o = o + lax.dot_general(q_in, st.astype(bf16) * e, (((1,), (1,)), ((), ())), preferred_element_type=f32)
        zero = jnp.zeros((CHUNK, A_W), bf16)
        pair = (c // 2) * 2 * CHUNK
        rhs = jnp.concatenate([k_hat, zero] if c % 2 == 0 else [zero, k_hat], axis=0)
        upd = jnp.dot(vt_b[:, pair:pair + 2 * CHUNK], rhs, preferred_element_type=f32)
        st_ref[...] = st * jnp.exp(g[r1 - 1:r1, :]) + upd
        o_ref[r0:r1, :] = o

    st0_ref[...] = st_ref[...]
    qw = q * jnp.exp(e_mid)
    kw = kin * jnp.exp(-e_mid)
    q_t, k_t = qw.astype(bf16), kw.astype(bf16)
    chunks = [(c * CHUNK, (c + 1) * CHUNK) for c in range(TILE // CHUNK)]
    atts = [lax.dot_general(q_t[r0:r1], expand(k_t[r0:r1]), (((1,), (1,)), ((), ())), preferred_element_type=f32)
            for r0, r1 in chunks]
    atts = [jnp.where(lvl_ref[len(LEVELS)] > 0.5, a, 0.0) for a in atts]
    for c, (r0, r1) in enumerate(chunks):
        g_mid, g_end = g[r0 + CHUNK // 2 - 1:r0 + CHUNK // 2, :], g[r1 - 1:r1, :]
        finish_chunk(c, atts[c], (qw[r0:r1] * jnp.exp(g_mid)).astype(bf16),
                     (kw[r0:r1] * jnp.exp(g_end - g_mid)).astype(bf16))

    def redo_if_unsafe():
        @pl.when(jnp.logical_not(safe))
        def _():
            st_ref[...] = st0_ref[...]
            q_levels, k_levels = [], []
            for m in LEVELS:
                w = jnp.exp(-jnp.abs(g - _level_ref(g, m)))
                q_levels.append((q * w).astype(bf16))
                k_levels.append((kin * w).astype(bf16))
            q_inter = (q * jnp.exp(g)).astype(bf16)
            for c in range(TILE // CHUNK):
                r0, r1 = c * CHUNK, (c + 1) * CHUNK
                att = jnp.zeros((CHUNK, 4 * CHUNK), f32)
                for li in range(len(LEVELS)):
                    a = lax.dot_general(q_levels[li][r0:r1], expand(k_levels[li][r0:r1]),
                                        (((1,), (1,)), ((), ())), preferred_element_type=f32)
                    att = att + a * lvl_ref[li]
                k_hat = (kin[r0:r1] * jnp.exp(g[r1 - 1:r1, :] - g[r0:r1])).astype(bf16)
                finish_chunk(c, att, q_inter[r0:r1], k_hat)
            o_ref[...] = o_ref[...] + _seg_sum(q * kin, e) * v

    return redo_if_unsafe


def _conv_tile(u, hist_ref, cw_ref, cvec_ref, em):
    n = u.shape[0]
    hist_ref[HIST:HIST + n, :] = u
    lo = HIST - (CONV_WIDTH - 1)
    y = cvec_ref[0:1, :]
    for r in range(8):
        rows = n if r == 0 else n + 8
        part = None
        for o in range(lo, lo + CONV_WIDTH):
            if o % 8 == r:
                term = hist_ref[o - r:o - r + rows, :] * cw_ref[o - lo:o - lo + 1, :]
                part = term if part is None else part + term
        y = y + part[r:r + n]
    return _group_ln_silu(y, em, cvec_ref[1:2, :], cvec_ref[2:3, :])


def _attn_tile(q_t, k_b, v_t, sink_ref, biases):
    zero = jnp.zeros((HEAD, WINDOW), f32)
    inst = [(b, g) for b in range(TILE // WINDOW) for g in range(2)]
    sinks = [jnp.concatenate([jnp.full((1, WINDOW), sink_ref[h] * LOG2E, f32) for h in range(4 * g, 4 * g + 4)],
                             axis=1) for g in range(2)]

    def rhs(b, g):
        cols = []
        for h in range(4 * g, 4 * g + 4):
            q_h = q_t[h * HEAD:(h + 1) * HEAD, b * WINDOW:(b + 1) * WINDOW]
            cols.append(jnp.concatenate([q_h, zero] if g == 0 else [zero, q_h], axis=0))
        return jnp.concatenate(cols, axis=1).astype(bf16)

    s = [jnp.dot(k_b[b * WINDOW:(b + 2) * WINDOW], rhs(b, g), preferred_element_type=f32) + biases[b]
         for b, g in inst]
    mx = [jnp.maximum(jnp.max(s[n], axis=0, keepdims=True), sinks[g]) for n, (b, g) in enumerate(inst)]
    pr = [jnp.exp2(s[n] - mx[n]) for n in range(len(inst))]
    rden = [1.0 / (jnp.sum(pr[n], axis=0, keepdims=True) + jnp.exp2(sinks[g] - mx[n]))
            for n, (b, g) in enumerate(inst)]
    o_t = [jnp.dot(v_t[g * HEAD:(g + 1) * HEAD, b * WINDOW:(b + 2) * WINDOW], pr[n].astype(bf16),
                   preferred_element_type=f32) * rden[n] for n, (b, g) in enumerate(inst)]
    rows = []
    for b in range(TILE // WINDOW):
        pairs = []
        for g in range(2):
            o = o_t[2 * b + g]
            for j in range(2):
                pair_t = jnp.concatenate([o[:, (2 * j) * WINDOW:(2 * j + 1) * WINDOW],
                                          o[:, (2 * j + 1) * WINDOW:(2 * j + 2) * WINDOW]], axis=0)
                pairs.append(pair_t.T)
        rows.append(jnp.concatenate(pairs, axis=1))
    return jnp.concatenate(rows, axis=0)


def _mixer_kernel(layer, x_ref, rope_row_ref, rope_tile_ref, win_ref, wout_ref, nmix_ref, lbp_ref, aon_ref, cw_ref,
                  cvec_ref, qk_ref, sinks_ref, e_ref, ltri_ref, lvl_ref, hm_ref, bias_ref,
                  h_ref, st_ref, conv_ref, kn_ref, vn_ref, z_ref, hist_ref, o_ref, st0_ref):
    i = pl.program_id(0)
    sink_ref = sinks_ref.at[layer]

    @pl.when(i == 0)
    def _():
        st_ref[...] = jnp.zeros_like(st_ref)
        hist_ref[0:HIST, :] = jnp.zeros((HIST, B_W), f32)
        kn_ref[...] = jnp.zeros_like(kn_ref)
        vn_ref[...] = jnp.zeros_like(vn_ref)

    x = x_ref[...]
    xn = _rmsnorm_rows(x, nmix_ref[...])
    z_ref[...] = jnp.dot(xn.astype(bf16), win_ref[...], preferred_element_type=f32)
    e = e_ref[...]

    redo_hgrn_if_unsafe = _hgrn_tile(z_ref[:, 0:A_W], z_ref[:, A_W:2 * A_W], z_ref[:, 2 * A_W:3 * A_W], lbp_ref,
                                     st_ref, st0_ref, o_ref, e, ltri_ref, lvl_ref, hm_ref)

    c0 = 4 * A_W
    u = z_ref[:, c0:c0 + B_W] * _sigmoid(z_ref[:, c0 + B_W:c0 + 2 * B_W])
    em = _mean_matrix(e)
    ob = _conv_tile(u, hist_ref, cw_ref, cvec_ref, em)
    conv_ref[...] = hist_ref[TILE + HIST - (CONV_WIDTH - 1):TILE + HIST, :]
    hist_ref[0:HIST, :] = hist_ref[TILE:TILE + HIST, :]

    c1 = c0 + 2 * B_W
    cos_a, sin_a = rope_tile_ref[0, pl.ds(i, 1), :], rope_tile_ref[1, pl.ds(i, 1), :]
    cos = cos_a * rope_row_ref[0] - sin_a * rope_row_ref[1]
    sin = sin_a * rope_row_ref[2] + cos_a * rope_row_ref[3]
    qn = jnp.concatenate([_head_norm(z_ref[:, c1 + j * 256:c1 + (j + 1) * 256], em,
                                     jnp.concatenate([qk_ref[0:1, :]] * 2, axis=1)) for j in range(2)], axis=1)
    q = _rope(qn, jnp.concatenate([cos] * 4, axis=1), jnp.concatenate([sin] * 4, axis=1)) * (LOG2E * HEAD ** -0.5)
    kk = _rope(_head_norm(z_ref[:, c1 + C_W:c1 + C_W + KV_W], em[0:KV_W, 0:KV_W], qk_ref[1:2, :]), cos, sin)
    vv = z_ref[:, c1 + C_W + KV_W:c1 + C_W + 2 * KV_W]
    kcat = jnp.concatenate([kn_ref[...], kk], axis=0)
    vcat = jnp.concatenate([vn_ref[...], vv], axis=0)
    q_t = q.T
    k_b = kcat.astype(bf16)
    v_t = vcat.T.astype(bf16)
    biases = [bias_ref[jnp.where(i > 0, 1, 0)]] + [bias_ref[1]] * (TILE // WINDOW - 1)
    oc = _attn_tile(q_t, k_b, v_t, sink_ref, biases)
    kn_ref[...] = kk[TILE - WINDOW:TILE]
    vn_ref[...] = vv[TILE - WINDOW:TILE]

    redo_hgrn_if_unsafe()
    gate = z_ref[:, 3 * A_W:4 * A_W]
    oa = _head_norm(o_ref[...], em, aon_ref[...]) * (gate * _sigmoid(gate))
    mix = jnp.concatenate([oa, ob, oc], axis=1).astype(bf16)
    h_ref[...] = x + jnp.dot(mix, wout_ref[...], preferred_element_type=f32)


def _const_spec(shape):
    nd = len(shape)
    return pl.BlockSpec(shape, lambda i, _n=nd: (0,) * _n)


def _layer_spec(arr, l, **kw):
    shape = arr.shape[1:]
    return pl.BlockSpec((None,) + shape, lambda i, _l=l, _n=len(shape): (_l,) + (0,) * _n, **kw)


def _ffn_kernel(h_ref, p_ref, wg_ref, wu_ref, wd_ref, wpg_ref, wpp_ref, norms_ref, y_ref):
    n = h_ref.shape[0]
    slabs = [slice(r, r + n // FFN_SLABS) for r in range(0, n, n // FFN_SLABS)]
    h = [h_ref[s, :] for s in slabs]
    hn = [_rmsnorm_rows(a, norms_ref[0:1, :]).astype(bf16) for a in h]
    gt = [jnp.dot(a, wg_ref[...], preferred_element_type=f32) for a in hn]
    up = [jnp.dot(a, wu_ref[...], preferred_element_type=f32) for a in hn]
    act = [(g * _sigmoid(g) * u).astype(bf16) for g, u in zip(gt, up)]
    h = [a + jnp.dot(b, wd_ref[...], preferred_element_type=f32) for a, b in zip(h, act)]
    pn = [_rmsnorm_rows(a, norms_ref[1:2, :]).astype(bf16) for a in h]
    gate = [_sigmoid(jnp.dot(a, wpg_ref[...], preferred_element_type=f32)) for a in pn]
    for s, a, g in zip(slabs, h, gate):
        y_ref[s, :] = a + g * jnp.dot(p_ref[s, :].astype(bf16), wpp_ref[...], preferred_element_type=f32)


N_MIXER_INPUTS = 17
N_FFN_INPUTS = 7


def _layer_kernel(layer, *refs):
    n_in, n_out = N_MIXER_INPUTS + N_FFN_INPUTS, 5
    mixer_in, ffn_in = refs[:N_MIXER_INPUTS], refs[N_MIXER_INPUTS:n_in]
    y_ref, st_ref, conv_ref, kn_ref, vn_ref = refs[n_in:n_in + n_out]
    z_ref, hist_ref, o_ref, st0_ref, mid_ref = refs[n_in + n_out:]
    _mixer_kernel(layer, *mixer_in, mid_ref, st_ref, conv_ref, kn_ref, vn_ref, z_ref, hist_ref, o_ref, st0_ref)
    _ffn_kernel(mid_ref, *ffn_in, y_ref)


def _prompt_layer(x, p, rope, wl, l, consts):
    t_len = x.shape[0]
    row_spec = lambda w: pl.BlockSpec((TILE, w), lambda i: (i, 0))
    once = dict(pipeline_mode=pl.Buffered(1))
    smalls = [wl['nmix'], wl['lbp'], wl['aon'], wl['cw'], wl['cvec'], wl['qk']]
    ffn_ws = [wl['w_gate'], wl['w_up'], wl['w_down'], wl['w_ple_gate'], wl['w_ple_proj']]
    in_specs = ([row_spec(D_MODEL), _const_spec(rope[0].shape), _const_spec(rope[1].shape),
                 _layer_spec(wl['w_in'], l, **once), _layer_spec(wl['w_out'], l, **once)]
                + [_layer_spec(a, l) for a in smalls] + [pl.BlockSpec(memory_space=pltpu.SMEM)]
                + [_const_spec(a.shape) for a in consts]
                + [pl.BlockSpec((None, TILE, PLE_DIM), lambda i, _l=l: (_l, i, 0))]
                + [_layer_spec(a, l, **once) for a in ffn_ws] + [_layer_spec(wl['fnorms'], l)])
    assert len(in_specs) == N_MIXER_INPUTS + N_FFN_INPUTS
    out_shape = (jax.ShapeDtypeStruct((t_len, D_MODEL), f32),
                 jax.ShapeDtypeStruct((A_W, A_W), f32),
                 jax.ShapeDtypeStruct((CONV_WIDTH - 1, B_W), f32),
                 jax.ShapeDtypeStruct((WINDOW, KV_W), f32),
                 jax.ShapeDtypeStruct((WINDOW, KV_W), f32))
    out_specs = (row_spec(D_MODEL), _const_spec((A_W, A_W)), _const_spec((CONV_WIDTH - 1, B_W)),
                 _const_spec((WINDOW, KV_W)), _const_spec((WINDOW, KV_W)))
    return pl.pallas_call(
        functools.partial(_layer_kernel, l), grid=(t_len // TILE,), in_specs=in_specs, out_specs=out_specs,
        out_shape=out_shape,
        scratch_shapes=[pltpu.VMEM((TILE, IN_COLS), f32), pltpu.VMEM((TILE + HIST, B_W), f32),
                        pltpu.VMEM((TILE, A_W), f32), pltpu.VMEM((A_W, A_W), f32),
                        pltpu.VMEM((TILE, D_MODEL), f32)],
        compiler_params=pltpu.CompilerParams(dimension_semantics=("arbitrary",),
                                             vmem_limit_bytes=LAYER_VMEM_LIMIT),
        name="prompt_layer",
    )(x, rope[0], rope[1], wl['w_in'], wl['w_out'], *smalls, wl['sinks'], *consts, p, *ffn_ws, wl['fnorms'])


N_HEADS = C_W // HEAD


def _sample_in_kernel(n_tok, x_ref, cos_ref, sin_ref, win_ref, nmix_ref, qk_ref, e_ref, z_ref, qx_ref, k_ref,
                      piece_ref):
    n_seq = x_ref.shape[0] // n_tok
    xn = _rmsnorm_rows(x_ref[...], nmix_ref[...])
    z = jnp.dot(xn.astype(bf16), win_ref[...], preferred_element_type=f32)
    z_ref[...] = z
    em = _mean_matrix(e_ref[...])
    c1 = 4 * A_W + 2 * B_W
    cos, sin = cos_ref[...], sin_ref[...]
    qn = jnp.concatenate([_head_norm(z[:, c1 + j * 256:c1 + (j + 1) * 256], em,
                                     jnp.concatenate([qk_ref[0:1, :]] * 2, axis=1)) for j in range(2)], axis=1)
    q = _rope(qn, jnp.concatenate([cos] * 4, axis=1), jnp.concatenate([sin] * 4, axis=1)) * (HEAD ** -0.5)
    k_ref[...] = _rope(_head_norm(z[:, c1 + C_W:c1 + C_W + KV_W], em[0:KV_W, 0:KV_W], qk_ref[1:2, :]), cos, sin)
    lo_f = lax.broadcasted_iota(jnp.int32, (x_ref.shape[0], KV_W), 1) < HEAD
    for h in range(N_HEADS):
        pair, g = q[:, (h // 2) * KV_W:(h // 2 + 1) * KV_W], h // 4
        own = jnp.where(lo_f if h % 2 == 0 else jnp.logical_not(lo_f), pair, 0.0)
        piece_ref[...] = own if h % 2 == g else pltpu.roll(own, HEAD, 1)
        for t in range(n_tok):
            qx_ref[pl.ds(h * n_tok + t, n_seq, stride=N_HEADS * n_tok), :] = piece_ref[pl.ds(t, n_seq, stride=n_tok), :]


def _sample_in(x, cos, sin, wl, l, e256, n_tok):
    n = x.shape[0]
    args = (x, cos, sin, wl['w_in'], wl['nmix'], wl['qk'], e256)
    specs = [_const_spec(a.shape) for a in args]
    specs[3:6] = [_layer_spec(a, l) for a in args[3:6]]
    return pl.pallas_call(
        functools.partial(_sample_in_kernel, n_tok), grid=(1,), in_specs=specs,
        out_specs=(_const_spec((n, IN_COLS)), _const_spec((n * N_HEADS, KV_W)), _const_spec((n, KV_W))),
        out_shape=(jax.ShapeDtypeStruct((n, IN_COLS), f32), jax.ShapeDtypeStruct((n * N_HEADS, KV_W), f32),
                   jax.ShapeDtypeStruct((n, KV_W), f32)),
        scratch_shapes=[pltpu.VMEM((n, KV_W), f32)],
        compiler_params=pltpu.CompilerParams(dimension_semantics=("arbitrary",), vmem_limit_bytes=VMEM_LIMIT),
        name="sample_in",
    )(*args)


def _sample_hgrn_kernel(n_tok, q_ref, f_ref, v_ref, s_ref, lbp_ref, o_ref, so_ref, qt_scr, ft_scr, kt_scr, vt_scr,
                        ot_scr):
    n_seq = s_ref.shape[1]
    for t in range(n_tok):
        zf = f_ref[pl.ds(t, n_seq, stride=n_tok), :]
        logf, ls = _log_forget(zf, lbp_ref[0:1, :], lbp_ref[1:2, :])
        ft_scr[t] = jnp.exp(logf).T
        kt_scr[t] = (lbp_ref[2:3, :] * jnp.exp(ls - zf)).T
        qt_scr[t] = q_ref[pl.ds(t, n_seq, stride=n_tok), :].T
        vt_scr[t] = v_ref[pl.ds(t, n_seq, stride=n_tok), :].T
        ot_scr[t] = jnp.zeros((2 * HEAD, n_seq), f32)

    def body(hk, carry):
        r0 = pl.multiple_of(hk * HEAD, HEAD)
        v0 = pl.multiple_of((hk // HEAD) * HEAD, HEAD)
        blk = s_ref[pl.ds(r0, HEAD), :]
        for t in range(n_tok):
            blk = blk * ft_scr[t, pl.ds(hk, 1), :] + kt_scr[t, pl.ds(hk, 1), :] * vt_scr[t, pl.ds(v0, HEAD), :]
            ot_scr[t, pl.ds(v0, HEAD), :] = ot_scr[t, pl.ds(v0, HEAD), :] + qt_scr[t, pl.ds(hk, 1), :] * blk
        so_ref[pl.ds(r0, HEAD), :] = blk
        return carry

    lax.fori_loop(0, 2 * HEAD, body, 0, unroll=4)
    for t in range(n_tok):
        o_ref[pl.ds(t, n_seq, stride=n_tok), :] = ot_scr[t].T


def _sample_hgrn(z, state, lbp, l, n_tok):
    n = z.shape[0]
    n_seq = state.shape[2]
    blk = 2 * HEAD * HEAD
    col = lambda j0: pl.BlockSpec((n, 2 * HEAD), lambda i, _j=j0: (0, _j + i))
    return pl.pallas_call(
        functools.partial(_sample_hgrn_kernel, n_tok), grid=(2,),
        in_specs=[col(0), col(2), col(4), pl.BlockSpec((None, blk, n_seq), lambda i, _l=l: (_l, i, 0)),
                  pl.BlockSpec((None, 3, 2 * HEAD), lambda i, _l=l: (_l, 0, i))],
        out_specs=(pl.BlockSpec((n, 2 * HEAD), lambda i: (0, i)), pl.BlockSpec((blk, n_seq), lambda i: (i, 0))),
        out_shape=(jax.ShapeDtypeStruct((n, A_W), f32), jax.ShapeDtypeStruct(state.shape[1:], f32)),
        scratch_shapes=[pltpu.VMEM((n_tok, 2 * HEAD, n_seq), f32)] * 5,
        compiler_params=pltpu.CompilerParams(dimension_semantics=("arbitrary",), vmem_limit_bytes=VMEM_LIMIT),
        name="sample_hgrn",
    )(z, z, z, state, lbp)


def _sample_conv_kernel(n_tok, first, u0_ref, u1_ref, g0_ref, g1_ref, buf_ref, cw_ref, cvec_ref, e_ref, *rest):
    ob0_ref, ob1_ref, new_ref = rest[-3:]
    slots = [new_ref.at[d] for d in range(new_ref.shape[0])] if first else [new_ref]
    n_seq = buf_ref.shape[1]
    n_hist = CONV_WIDTH - 1
    us = []
    for t in range(n_tok):
        rows = pl.ds(t, n_seq, stride=n_tok)
        u = jnp.concatenate([u0_ref[rows, :], u1_ref[rows, :]], axis=1)
        g = jnp.concatenate([g0_ref[rows, :], g1_ref[rows, :]], axis=1)
        us.append(u * _sigmoid(g))

    def slab(j):
        return buf_ref[j] if j < n_hist else us[j - n_hist]

    em = _mean_matrix(e_ref[...])
    for t in range(n_tok):
        y = cvec_ref[0:1, :]
        for j in range(CONV_WIDTH):
            y = y + slab(t + j) * cw_ref[j:j + 1, :]
        ob = _group_ln_silu(y, em, cvec_ref[1:2, :], cvec_ref[2:3, :])
        ob0_ref[pl.ds(t, n_seq, stride=n_tok), :] = ob[:, 0:KV_W]
        ob1_ref[pl.ds(t, n_seq, stride=n_tok), :] = ob[:, KV_W:B_W]
    for dst in slots:
        for j in range(n_hist):
            dst[j] = slab(j + n_tok)


def _sample_conv(z, buf, new_all, wl, l, e256, n_tok):
    n = z.shape[0]
    col = lambda j: pl.BlockSpec((n, KV_W), lambda i, _j=j: (0, _j))
    half = jax.ShapeDtypeStruct((n, KV_W), f32)
    in_specs = [col(8), col(9), col(10), col(11), _layer_spec(buf, l), _layer_spec(wl['cw'], l),
                _layer_spec(wl['cvec'], l), _const_spec(e256.shape)]
    args = [z, z, z, z, buf, wl['cw'], wl['cvec'], e256]
    if new_all is None:
        new_spec, aliases = _const_spec(buf.shape), {}
    else:
        in_specs.append(pl.BlockSpec(memory_space=pl.ANY))
        args.append(new_all)
        new_spec, aliases = _layer_spec(buf, l), {len(args) - 1: 2}
    return pl.pallas_call(
        functools.partial(_sample_conv_kernel, n_tok, new_all is None), grid=(1,), in_specs=in_specs,
        out_specs=(_const_spec((n, KV_W)), _const_spec((n, KV_W)), new_spec),
        out_shape=(half, half, jax.ShapeDtypeStruct(buf.shape, f32)),
        input_output_aliases=aliases,
        compiler_params=pltpu.CompilerParams(dimension_semantics=("arbitrary",), vmem_limit_bytes=VMEM_LIMIT),
        name="sample_conv",
    )(*args)


SEQ_BLOCK = 32


def _sample_attn_kernel(n_tok, qx_ref, k_ref, v_ref, kc_ref, vc_ref, sink_ref, bias_ref, ox_ref, kn_ref, vn_ref):
    q_rows = N_HEADS * n_tok
    sink, bias = sink_ref[...], bias_ref[...]
    seqs = range(SEQ_BLOCK)
    k_all = [jnp.concatenate([kc_ref[b], k_ref[b * n_tok:(b + 1) * n_tok, :]], axis=0) for b in seqs]
    v_all = [jnp.concatenate([vc_ref[b], v_ref[b * n_tok:(b + 1) * n_tok, :]], axis=0) for b in seqs]
    s = [lax.dot_general(qx_ref[b * q_rows:(b + 1) * q_rows, :].astype(bf16), k_all[b].astype(bf16),
                         (((1,), (1,)), ((), ())), preferred_element_type=f32) + bias for b in seqs]
    mx = [jnp.maximum(jnp.max(s[b], axis=-1, keepdims=True), sink) for b in seqs]
    pr = [jnp.exp(s[b] - mx[b]) for b in seqs]
    rden = [1.0 / (jnp.sum(pr[b], axis=-1, keepdims=True) + jnp.exp(sink - mx[b])) for b in seqs]
    for b in seqs:
        ox_ref[b * q_rows:(b + 1) * q_rows, :] = jnp.dot(pr[b].astype(bf16), v_all[b].astype(bf16),
                                                         preferred_element_type=f32) * rden[b]
    for b in seqs:
        kn_ref[b] = k_all[b][n_tok:, :]
        vn_ref[b] = v_all[b][n_tok:, :]


def _sample_attn(qx, k, z, kc, vc, sink_col, bias, l, n_tok):
    n_seq = kc.shape[1]
    assert n_seq % SEQ_BLOCK == 0
    rows = SEQ_BLOCK * n_tok
    q_rows = SEQ_BLOCK * N_HEADS * n_tok
    v_col = (4 * A_W + 2 * B_W + C_W + KV_W) // KV_W
    cache_in = pl.BlockSpec((None, SEQ_BLOCK) + kc.shape[2:], lambda i, _l=l: (_l, i, 0, 0))
    cache_out = pl.BlockSpec((SEQ_BLOCK,) + kc.shape[2:], lambda i: (i, 0, 0))
    return pl.pallas_call(
        functools.partial(_sample_attn_kernel, n_tok), grid=(n_seq // SEQ_BLOCK,),
        in_specs=[pl.BlockSpec((q_rows, KV_W), lambda i: (i, 0)), pl.BlockSpec((rows, KV_W), lambda i: (i, 0)),
                  pl.BlockSpec((rows, KV_W), lambda i: (i, v_col)), cache_in, cache_in,
                  _layer_spec(sink_col, l), _const_spec(bias.shape)],
        out_specs=(pl.BlockSpec((q_rows, KV_W), lambda i: (i, 0)), cache_out, cache_out),
        out_shape=(jax.ShapeDtypeStruct(qx.shape, f32), jax.ShapeDtypeStruct(kc.shape[1:], f32),
                   jax.ShapeDtypeStruct(vc.shape[1:], f32)),
        compiler_params=pltpu.CompilerParams(dimension_semantics=("arbitrary",), vmem_limit_bytes=VMEM_LIMIT),
        name="sample_attn",
    )(qx, k, z, kc, vc, sink_col, bias)


def _sample_out_kernel(n_tok, x_ref, oa_ref, g_ref, ob0_ref, ob1_ref, ox_ref, wout_ref, aon_ref, e_ref, h_ref,
                       oc_ref):
    n_seq = x_ref.shape[0] // n_tok
    gate = g_ref[...]
    oa = _head_norm(oa_ref[...], _mean_matrix(e_ref[...]), aon_ref[...]) * (gate * _sigmoid(gate))
    lo_f = lax.broadcasted_iota(jnp.int32, (n_seq, KV_W), 1) < HEAD
    for p in range(N_HEADS // 2):
        g = p // 2
        for t in range(n_tok):
            even = ox_ref[pl.ds((2 * p) * n_tok + t, n_seq, stride=N_HEADS * n_tok), :]
            odd = ox_ref[pl.ds((2 * p + 1) * n_tok + t, n_seq, stride=N_HEADS * n_tok), :]
            even = even if g == 0 else pltpu.roll(even, HEAD, 1)
            odd = odd if g == 1 else pltpu.roll(odd, HEAD, 1)
            oc_ref[p, pl.ds(t, n_seq, stride=n_tok), :] = jnp.where(lo_f, even, odd)
    mix = jnp.concatenate([oa, ob0_ref[...], ob1_ref[...]] + [oc_ref[p] for p in range(N_HEADS // 2)],
                          axis=1).astype(bf16)
    h_ref[...] = x_ref[...] + jnp.dot(mix, wout_ref[...], preferred_element_type=f32)


N_OUT_INPUTS = 9


def _sample_tail_kernel(n_tok, *refs):
    n_in = N_OUT_INPUTS + N_FFN_INPUTS
    out_in, ffn_in = refs[:N_OUT_INPUTS], refs[N_OUT_INPUTS:n_in]
    y_ref, oc_ref, mid_ref = refs[n_in:]
    _sample_out_kernel(n_tok, *out_in, mid_ref, oc_ref)
    _ffn_kernel(mid_ref, *ffn_in, y_ref)


def _sample_tail(x, oa, z, ob0, ob1, ox, p, wl, l, e256, n_tok):
    n = x.shape[0]
    once = dict(pipeline_mode=pl.Buffered(1))
    args = (x, oa, z, ob0, ob1, ox, wl['w_out'], wl['aon'], e256)
    specs = [_const_spec(a.shape) for a in args]
    specs[2] = pl.BlockSpec((n, A_W), lambda i: (0, 3))
    specs[6:8] = [_layer_spec(a, l) for a in args[6:8]]
    ffn_ws = [wl['w_gate'], wl['w_up'], wl['w_down'], wl['w_ple_gate'], wl['w_ple_proj']]
    specs += ([_layer_spec(p, l)] + [_layer_spec(a, l, **once) for a in ffn_ws] + [_layer_spec(wl['fnorms'], l)])
    assert len(specs) == N_OUT_INPUTS + N_FFN_INPUTS
    return pl.pallas_call(
        functools.partial(_sample_tail_kernel, n_tok), grid=(1,), in_specs=specs,
        out_specs=_const_spec((n, D_MODEL)), out_shape=jax.ShapeDtypeStruct((n, D_MODEL), f32),
        scratch_shapes=[pltpu.VMEM((N_HEADS // 2, n, KV_W), f32), pltpu.VMEM((n, D_MODEL), f32)],
        compiler_params=pltpu.CompilerParams(dimension_semantics=("arbitrary",), vmem_limit_bytes=VMEM_LIMIT),
        name="sample_tail",
    )(*args, p, *ffn_ws, wl['fnorms'])


def _rope_cos_sin(pos):
    half = HEAD // 2
    inv = ROPE_THETA ** (-jnp.arange(half, dtype=f32) / half)
    ang = pos[:, None] * inv[None, :]
    cos, sin = jnp.cos(ang), jnp.sin(ang)
    return jnp.tile(cos, (1, 4)), jnp.concatenate([-sin, sin, -sin, sin], axis=1)


def _rope_prompt_tables(t_len):
    half = HEAD // 2
    inv = ROPE_THETA ** (-jnp.arange(half, dtype=f32) / half)
    sign = jnp.concatenate([-jnp.ones((half,), f32), jnp.ones((half,), f32)] * 2)
    ang_r = jnp.arange(TILE, dtype=f32)[:, None] * inv[None, :]
    ang_t = (jnp.arange(t_len // TILE, dtype=f32) * TILE)[:, None] * inv[None, :]
    cos_r, sin_r = jnp.tile(jnp.cos(ang_r), (1, 4)), jnp.tile(jnp.sin(ang_r), (1, 4))
    rows = jnp.stack([cos_r, sin_r, cos_r * sign, sin_r * sign])
    tiles = jnp.stack([jnp.tile(jnp.cos(ang_t), (1, 4)), jnp.tile(jnp.sin(ang_t), (1, 4))])
    return rows, tiles


def _stacked_weights(lbs, n_tok, w_in, a_onorm, conv_w, conv_b, conv_ln_g, conv_ln_b, q_norm, k_norm, sinks, w_out,
                     norm_mix, norm_ffn, w_gate, w_up, w_down, ple_norm, w_ple_gate, w_ple_proj):
    return dict(
        w_in=w_in.astype(bf16), w_out=w_out.astype(bf16), w_gate=w_gate.astype(bf16), w_up=w_up.astype(bf16),
        w_down=w_down.astype(bf16), w_ple_gate=w_ple_gate.astype(bf16), w_ple_proj=w_ple_proj.astype(bf16),
        nmix=norm_mix[:, None, :],
        lbp=jnp.stack([jnp.log(lbs), jnp.log1p(-lbs), 1.0 - lbs], axis=1),
        aon=jnp.tile(a_onorm, (1, 4))[:, None, :],
        cw=conv_w,
        cvec=jnp.stack([conv_b, conv_ln_g, conv_ln_b], axis=1),
        qk=jnp.stack([jnp.tile(q_norm, (1, 2)), jnp.tile(k_norm, (1, 2))], axis=1),
        sinks=sinks,
        sink_col=jnp.repeat(sinks, n_tok, axis=1)[:, :, None],
        fnorms=jnp.stack([norm_ffn, ple_norm], axis=1),
    )


def kernel(x_prompt, x_sample, state_hgrn, state_conv, cache_swa_k, cache_swa_v, p_prompt, p_sample, a_lower, w_in, a_onorm, conv_w, conv_b, conv_ln_g, conv_ln_b, q_norm, k_norm, sinks, w_out, norm_mix, norm_ffn, w_gate, w_up, w_down, ple_norm, w_ple_gate, w_ple_proj):
    depth = w_in.shape[0]
    t_len = x_prompt.shape[1]
    n_seq, n_tok = x_sample.shape[0], x_sample.shape[1]
    past_len = PAST_LEN
    assert x_prompt.shape[0] == 1 and t_len % TILE == 0

    lbs = jnp.cumsum(jax.nn.softmax(a_lower.astype(f32), axis=0), axis=0)
    lbs = lbs - lbs[0:1]
    consts_np = _np_constants()
    e256 = jnp.asarray(consts_np[0], bf16)
    consts = (e256, jnp.asarray(consts_np[1], bf16), jnp.asarray(consts_np[2], f32), jnp.asarray(consts_np[3], bf16),
              jnp.asarray(consts_np[4], f32))

    rope_p = _rope_prompt_tables(t_len)
    cos_s, sin_s = _rope_cos_sin(past_len + jnp.arange(n_tok, dtype=f32))
    cos_s, sin_s = jnp.tile(cos_s, (n_seq, 1)), jnp.tile(sin_s, (n_seq, 1))
    w_buf = cache_swa_k.shape[2]
    q_tok = (np.arange(N_HEADS * n_tok) % n_tok)[:, None]
    rel = q_tok + w_buf - np.arange(w_buf + n_tok)[None, :]
    bias_s = jnp.asarray(np.where((rel >= 0) & (rel <= WINDOW), 0.0, -np.inf), f32)

    wl = _stacked_weights(lbs, n_tok, w_in, a_onorm, conv_w, conv_b, conv_ln_g, conv_ln_b, q_norm, k_norm, sinks,
                          w_out, norm_mix, norm_ffn, w_gate, w_up, w_down, ple_norm, w_ple_gate, w_ple_proj)
    p_p = p_prompt.reshape(depth, t_len, PLE_DIM)
    p_s = p_sample.reshape(depth, n_seq * n_tok, PLE_DIM)
    s_hgrn = jnp.swapaxes(state_hgrn.reshape(depth, n_seq, -1), 1, 2)
    kc = cache_swa_k.reshape(depth, n_seq, w_buf, KV_W)
    vc = cache_swa_v.reshape(depth, n_seq, w_buf, KV_W)

    hp = x_prompt[0]
    hs = x_sample.reshape(n_seq * n_tok, D_MODEL)
    outs = [[] for _ in range(8)]
    conv_in = jnp.swapaxes(state_conv, 1, 2)
    conv_s = None
    for l in range(depth):
        hp, st, conv_new, k_new, v_new = _prompt_layer(hp, p_p, rope_p, wl, l, consts)
        s_new = jnp.stack([st[h * HEAD:(h + 1) * HEAD, h * HEAD:(h + 1) * HEAD].T for h in range(4)])
        outs[0].append(s_new[None])
        outs[1].append(conv_new[None])
        outs[2].append(k_new.reshape(1, WINDOW, 2, HEAD))
        outs[3].append(v_new.reshape(1, WINDOW, 2, HEAD))
        z, qx, k = _sample_in(hs, cos_s, sin_s, wl, l, e256, n_tok)
        oa, s_s = _sample_hgrn(z, s_hgrn, wl['lbp'], l, n_tok)
        ob0, ob1, conv_s = _sample_conv(z, conv_in, conv_s, wl, l, e256, n_tok)
        ox, k_s, v_s = _sample_attn(qx, k, z, kc, vc, wl['sink_col'], bias_s, l, n_tok)
        hs = _sample_tail(hs, oa, z, ob0, ob1, ox, p_s, wl, l, e256, n_tok)
        outs[4].append(s_s.T.reshape(n_seq, 4, HEAD, HEAD))
        outs[6].append(k_s.reshape(n_seq, w_buf, 2, HEAD))
        outs[7].append(v_s.reshape(n_seq, w_buf, 2, HEAD))
    stacked = [jnp.swapaxes(conv_s, 1, 2) if i == 5 else jnp.stack(o) for i, o in enumerate(outs)]
    return (hp[None], hs.reshape(n_seq, n_tok, D_MODEL)) + tuple(stacked)
```

```python
import functools

import numpy as np
import jax
import jax.numpy as jnp
from jax import lax
from jax.experimental import pallas as pl
from jax.experimental.pallas import tpu as pltpu

f32 = jnp.float32
bf16 = jnp.bfloat16

D_MODEL = 1024
HEAD = 64
A_W = 256
B_W = 256
C_W = 512
KV_W = 128
IN_COLS = 4 * A_W + 2 * B_W + C_W + 2 * KV_W
PLE_DIM = 256
WINDOW = 128
CONV_WIDTH = 31
ROPE_THETA = 10000.0
EPS = 1e-6
PAST_LEN = 16384

TILE = 512
CUM_ROWS = 256
CHUNK = 64
LEVELS = (32, 16, 8, 4, 2, 1)
SAFE_EXPONENT = 80.0
HIST = 32
FFN_SLABS = 4
LOG2E = 1.4426950408889634
VMEM_LIMIT = 52 * 1024 * 1024
LAYER_VMEM_LIMIT = 58 * 1024 * 1024


def _split2(x):
    hi = x.astype(bf16)
    lo = (x - hi.astype(f32)).astype(bf16)
    return hi, lo


def _seg_sum(x, e):
    hi, lo = _split2(x)
    return (jnp.dot(hi, e, preferred_element_type=f32) + jnp.dot(lo, e, preferred_element_type=f32))


def _sigmoid(x):
    return 1.0 / (1.0 + jnp.exp(-x))


def _rmsnorm_rows(x, g):
    return x * lax.rsqrt(jnp.mean(x * x, axis=-1, keepdims=True) + EPS) * g


def _seg_stat(x, e):
    return jnp.dot(x.astype(bf16), e, preferred_element_type=f32)


def _mean_matrix(e):
    return e * (1.0 / HEAD)


def _head_norm(x, em, g):
    return x * lax.rsqrt(_seg_stat(x * x, em) + EPS) * g


def _rope(x, cos, sin_signed):
    w = x.shape[-1]
    lane = lax.broadcasted_iota(jnp.int32, x.shape, 1)
    first = (lane % HEAD) < (HEAD // 2)
    swapped = jnp.where(first, pltpu.roll(x, w - HEAD // 2, 1), pltpu.roll(x, HEAD // 2, 1))
    return x * cos + swapped * sin_signed


def _log_forget(zf, log_lb, log_1m_lb):
    ls = jnp.minimum(zf, 0.0) - jnp.log(1.0 + jnp.exp(-jnp.abs(zf)))
    b = log_1m_lb + ls
    return jnp.maximum(log_lb, b) + jnp.log(1.0 + jnp.exp(-jnp.abs(log_lb - b))), ls


def _group_ln_silu(y, em, g, b):
    yc = y - _seg_stat(y, em)
    var = _seg_stat(yc * yc, em)
    yn = yc * lax.rsqrt(var + EPS) * g + b
    return yn * _sigmoid(yn)


def _np_constants():
    lane = np.arange(256)
    e256 = (lane[:, None] // HEAD == lane[None, :] // HEAD).astype(np.float32)
    t = np.arange(CUM_ROWS)
    ltri = ((t[:, None] // CHUNK == t[None, :] // CHUNK) & (t[None, :] <= t[:, None])).astype(np.float32)
    tt = np.arange(CHUNK)[:, None]
    ss = (np.arange(256) % CHUNK)[None, :]
    lvl = []
    for m in LEVELS:
        lvl.append(((tt // (2 * m) == ss // (2 * m)) & (tt % (2 * m) >= m) & (ss % (2 * m) < m)).astype(np.float32))
    lvl.append((ss <= tt).astype(np.float32))
    lvl = np.stack(lvl)
    hm = np.stack([np.broadcast_to((lane // HEAD == h)[None, :], (CHUNK, 256)) for h in range(4)]).astype(np.float32)
    keys = np.arange(2 * WINDOW)[:, None]
    qry = np.arange(4 * WINDOW)[None, :] % WINDOW
    band = (keys >= qry) & (keys <= qry + WINDOW)
    bias = np.where(np.stack([band & (keys >= WINDOW), band]), 0.0, -np.inf).astype(np.float32)
    return e256, ltri, lvl, hm, bias


def _level_ref(g, m):
    n_rows = g.shape[0]
    if 2 * m >= 8:
        g3 = g.reshape(n_rows // (2 * m), 2 * m, g.shape[1])
        return jnp.broadcast_to(g3[:, m - 1:m, :], g3.shape).reshape(g.shape)
    g3 = g.reshape(n_rows // 8, 8, g.shape[1])
    sub = lax.broadcasted_iota(jnp.int32, g3.shape, 1)
    out = None
    for blk in range(8 // (2 * m)):
        r = blk * 2 * m + m - 1
        piece = jnp.broadcast_to(g3[:, r:r + 1, :], g3.shape)
        out = piece if out is None else jnp.where(sub >= blk * 2 * m, piece, out)
    return out.reshape(g.shape)


def _hgrn_tile(q, zf, v, lbp_ref, st_ref, st0_ref, o_ref, e, ltri_ref, lvl_ref, hm_ref):
    log_lb, log_1m_lb, one_m_lb = lbp_ref[0:1, :], lbp_ref[1:2, :], lbp_ref[2:3, :]
    logf, ls = _log_forget(zf, log_lb, log_1m_lb)
    kin = one_m_lb * jnp.exp(ls - zf)
    parts = jnp.concatenate(_split2(logf), axis=1)
    g2 = jnp.concatenate([jnp.dot(ltri_ref[...], parts[r:r + CUM_ROWS], preferred_element_type=f32)
                          for r in range(0, TILE, CUM_ROWS)], axis=0)
    g = g2[:, 0:A_W] + g2[:, A_W:2 * A_W]
    v_b = v.astype(bf16)
    vt_b = v.T.astype(bf16)
    e_mid = g - _level_ref(g, CHUNK // 2)
    safe = jnp.max(jnp.abs(e_mid)) < SAFE_EXPONENT

    def expand(rows_b):
        return jnp.concatenate([rows_b * hm_ref[h] for h in range(4)], axis=0)

    def finish_chunk(c, att, q_in, k_hat):
        r0, r1 = c * CHUNK, (c + 1) * CHUNK
        st = st_ref[...]
        o = jnp.dot(att.astype(bf16), expand(v_b[r0:r1]), preferred_element_type=f32)
        o = o + lax.dot_general(q_in, st.astype(bf16) * e, (((1,), (1,)), ((), ())), preferred_element_type=f32)
        zero = jnp.zeros((CHUNK, A_W), bf16)
        pair = (c // 2) * 2 * CHUNK
        rhs = jnp.concatenate([k_hat, zero] if c % 2 == 0 else [zero, k_hat], axis=0)
        upd = jnp.dot(vt_b[:, pair:pair + 2 * CHUNK], rhs, preferred_element_type=f32)
        st_ref[...] = st * jnp.exp(g[r1 - 1:r1, :]) + upd
        o_ref[r0:r1, :] = o

    st0_ref[...] = st_ref[...]
    qw = q * jnp.exp(e_mid)
    kw = kin * jnp.exp(-e_mid)
    q_t, k_t = qw.astype(bf16), kw.astype(bf16)
    chunks = [(c * CHUNK, (c + 1) * CHUNK) for c in range(TILE // CHUNK)]
    atts = [lax.dot_general(q_t[r0:r1], expand(k_t[r0:r1]), (((1,), (1,)), ((), ())), preferred_element_type=f32)
            for r0, r1 in chunks]
    atts = [jnp.where(lvl_ref[len(LEVELS)] > 0.5, a, 0.0) for a in atts]
    for c, (r0, r1) in enumerate(chunks):
        g_mid, g_end = g[r0 + CHUNK // 2 - 1:r0 + CHUNK // 2, :], g[r1 - 1:r1, :]
        finish_chunk(c, atts[c], (qw[r0:r1] * jnp.exp(g_mid)).astype(bf16),
                     (kw[r0:r1] * jnp.exp(g_end - g_mid)).astype(bf16))

    def redo_if_unsafe():
        @pl.when(jnp.logical_not(safe))
        def _():
            st_ref[...] = st0_ref[...]
            q_levels, k_levels = [], []
            for m in LEVELS:
                w = jnp.exp(-jnp.abs(g - _level_ref(g, m)))
                q_levels.append((q * w).astype(bf16))
                k_levels.append((kin * w).astype(bf16))
            q_inter = (q * jnp.exp(g)).astype(bf16)
            for c in range(TILE // CHUNK):
                r0, r1 = c * CHUNK, (c + 1) * CHUNK
                att = jnp.zeros((CHUNK, 4 * CHUNK), f32)
                for li in range(len(LEVELS)):
                    a = lax.dot_general(q_levels[li][r0:r1], expand(k_levels[li][r0:r1]),
                                        (((1,), (1,)), ((), ())), preferred_element_type=f32)
                    att = att + a * lvl_ref[li]
                k_hat = (kin[r0:r1] * jnp.exp(g[r1 - 1:r1, :] - g[r0:r1])).astype(bf16)
                finish_chunk(c, att, q_inter[r0:r1], k_hat)
            o_ref[...] = o_ref[...] + _seg_sum(q * kin, e) * v

    return redo_if_unsafe


def _conv_tile(u, hist_ref, cw_ref, cvec_ref, em):
    n = u.shape[0]
    hist_ref[HIST:HIST + n, :] = u
    lo = HIST - (CONV_WIDTH - 1)
    y = cvec_ref[0:1, :]
    for r in range(8):
        rows = n if r == 0 else n + 8
        part = None
        for o in range(lo, lo + CONV_WIDTH):
            if o % 8 == r:
                term = hist_ref[o - r:o - r + rows, :] * cw_ref[o - lo:o - lo + 1, :]
                part = term if part is None else part + term
        y = y + part[r:r + n]
    return _group_ln_silu(y, em, cvec_ref[1:2, :], cvec_ref[2:3, :])


def _attn_tile(q_t, k_b, v_t, sink_ref, biases):
    zero = jnp.zeros((HEAD, WINDOW), f32)
    inst = [(b, g) for b in range(TILE // WINDOW) for g in range(2)]
    sinks = [jnp.concatenate([jnp.full((1, WINDOW), sink_ref[h] * LOG2E, f32) for h in range(4 * g, 4 * g + 4)],
                             axis=1) for g in range(2)]

    def rhs(b, g):
        cols = []
        for h in range(4 * g, 4 * g + 4):
            q_h = q_t[h * HEAD:(h + 1) * HEAD, b * WINDOW:(b + 1) * WINDOW]
            cols.append(jnp.concatenate([q_h, zero] if g == 0 else [zero, q_h], axis=0))
        return jnp.concatenate(cols, axis=1).astype(bf16)

    s = [jnp.dot(k_b[b * WINDOW:(b + 2) * WINDOW], rhs(b, g), preferred_element_type=f32) + biases[b]
         for b, g in inst]
    mx = [jnp.maximum(jnp.max(s[n], axis=0, keepdims=True), sinks[g]) for n, (b, g) in enumerate(inst)]
    pr = [jnp.exp2(s[n] - mx[n]) for n in range(len(inst))]
    rden = [1.0 / (jnp.sum(pr[n], axis=0, keepdims=True) + jnp.exp2(sinks[g] - mx[n]))
            for n, (b, g) in enumerate(inst)]
    o_t = [jnp.dot(v_t[g * HEAD:(g + 1) * HEAD, b * WINDOW:(b + 2) * WINDOW], pr[n].astype(bf16),
                   preferred_element_type=f32) * rden[n] for n, (b, g) in enumerate(inst)]
    rows = []
    for b in range(TILE // WINDOW):
        pairs = []
        for g in range(2):
            o = o_t[2 * b + g]
            for j in range(2):
                pair_t = jnp.concatenate([o[:, (2 * j) * WINDOW:(2 * j + 1) * WINDOW],
                                          o[:, (2 * j + 1) * WINDOW:(2 * j + 2) * WINDOW]], axis=0)
                pairs.append(pair_t.T)
        rows.append(jnp.concatenate(pairs, axis=1))
    return jnp.concatenate(rows, axis=0)


def _mixer_kernel(layer, x_ref, rope_row_ref, rope_tile_ref, win_ref, wout_ref, nmix_ref, lbp_ref, aon_ref, cw_ref,
                  cvec_ref, qk_ref, sinks_ref, e_ref, ltri_ref, lvl_ref, hm_ref, bias_ref,
                  h_ref, st_ref, conv_ref, kn_ref, vn_ref, z_ref, hist_ref, o_ref, st0_ref):
    i = pl.program_id(0)
    sink_ref = sinks_ref.at[layer]

    @pl.when(i == 0)
    def _():
        st_ref[...] = jnp.zeros_like(st_ref)
        hist_ref[0:HIST, :] = jnp.zeros((HIST, B_W), f32)
        kn_ref[...] = jnp.zeros_like(kn_ref)
        vn_ref[...] = jnp.zeros_like(vn_ref)

    x = x_ref[...]
    xn = _rmsnorm_rows(x, nmix_ref[...])
    z_ref[...] = jnp.dot(xn.astype(bf16), win_ref[...], preferred_element_type=f32)
    e = e_ref[...]

    redo_hgrn_if_unsafe = _hgrn_tile(z_ref[:, 0:A_W], z_ref[:, A_W:2 * A_W], z_ref[:, 2 * A_W:3 * A_W], lbp_ref,
                                     st_ref, st0_ref, o_ref, e, ltri_ref, lvl_ref, hm_ref)

    c0 = 4 * A_W
    u = z_ref[:, c0:c0 + B_W] * _sigmoid(z_ref[:, c0 + B_W:c0 + 2 * B_W])
    em = _mean_matrix(e)
    ob = _conv_tile(u, hist_ref, cw_ref, cvec_ref, em)
    conv_ref[...] = hist_ref[TILE + HIST - (CONV_WIDTH - 1):TILE + HIST, :]
    hist_ref[0:HIST, :] = hist_ref[TILE:TILE + HIST, :]

    c1 = c0 + 2 * B_W
    cos_a, sin_a = rope_tile_ref[0, pl.ds(i, 1), :], rope_tile_ref[1, pl.ds(i, 1), :]
    cos = cos_a * rope_row_ref[0] - sin_a * rope_row_ref[1]
    sin = sin_a * rope_row_ref[2] + cos_a * rope_row_ref[3]
    qn = jnp.concatenate([_head_norm(z_ref[:, c1 + j * 256:c1 + (j + 1) * 256], em,
                                     jnp.concatenate([qk_ref[0:1, :]] * 2, axis=1)) for j in range(2)], axis=1)
    q = _rope(qn, jnp.concatenate([cos] * 4, axis=1), jnp.concatenate([sin] * 4, axis=1)) * (LOG2E * HEAD ** -0.5)
    kk = _rope(_head_norm(z_ref[:, c1 + C_W:c1 + C_W + KV_W], em[0:KV_W, 0:KV_W], qk_ref[1:2, :]), cos, sin)
    vv = z_ref[:, c1 + C_W + KV_W:c1 + C_W + 2 * KV_W]
    kcat = jnp.concatenate([kn_ref[...], kk], axis=0)
    vcat = jnp.concatenate([vn_ref[...], vv], axis=0)
    q_t = q.T
    k_b = kcat.astype(bf16)
    v_t = vcat.T.astype(bf16)
    biases = [bias_ref[jnp.where(i > 0, 1, 0)]] + [bias_ref[1]] * (TILE // WINDOW - 1)
    oc = _attn_tile(q_t, k_b, v_t, sink_ref, biases)
    kn_ref[...] = kk[TILE - WINDOW:TILE]
    vn_ref[...] = vv[TILE - WINDOW:TILE]

    redo_hgrn_if_unsafe()
    gate = z_ref[:, 3 * A_W:4 * A_W]
    oa = _head_norm(o_ref[...], em, aon_ref[...]) * (gate * _sigmoid(gate))
    mix = jnp.concatenate([oa, ob, oc], axis=1).astype(bf16)
    h_ref[...] = x + jnp.dot(mix, wout_ref[...], preferred_element_type=f32)


def _const_spec(shape):
    nd = len(shape)
    return pl.BlockSpec(shape, lambda i, _n=nd: (0,) * _n)


def _layer_spec(arr, l, **kw):
    shape = arr.shape[1:]
    return pl.BlockSpec((None,) + shape, lambda i, _l=l, _n=len(shape): (_l,) + (0,) * _n, **kw)


def _ffn_kernel(h_ref, p_ref, wg_ref, wu_ref, wd_ref, wpg_ref, wpp_ref, norms_ref, y_ref):
    n = h_ref.shape[0]
    slabs = [slice(r, r + n // FFN_SLABS) for r in range(0, n, n // FFN_SLABS)]
    h = [h_ref[s, :] for s in slabs]
    hn = [_rmsnorm_rows(a, norms_ref[0:1, :]).astype(bf16) for a in h]
    gt = [jnp.dot(a, wg_ref[...], preferred_element_type=f32) for a in hn]
    up = [jnp.dot(a, wu_ref[...], preferred_element_type=f32) for a in hn]
    act = [(g * _sigmoid(g) * u).astype(bf16) for g, u in zip(gt, up)]
    h = [a + jnp.dot(b, wd_ref[...], preferred_element_type=f32) for a, b in zip(h, act)]
    pn = [_rmsnorm_rows(a, norms_ref[1:2, :]).astype(bf16) for a in h]
    gate = [_sigmoid(jnp.dot(a, wpg_ref[...], preferred_element_type=f32)) for a in pn]
    for s, a, g in zip(slabs, h, gate):
        y_ref[s, :] = a + g * jnp.dot(p_ref[s, :].astype(bf16), wpp_ref[...], preferred_element_type=f32)


N_MIXER_INPUTS = 17
N_FFN_INPUTS = 7


def _layer_kernel(layer, *refs):
    n_in, n_out = N_MIXER_INPUTS + N_FFN_INPUTS, 5
    mixer_in, (p_ref, *w_hbm, norms_ref) = refs[:N_MIXER_INPUTS], refs[N_MIXER_INPUTS:n_in]
    y_ref, st_ref, conv_ref, kn_ref, vn_ref = refs[n_in:n_in + n_out]
    z_ref, hist_ref, o_ref, st0_ref, mid_ref, *w_vmem, sems = refs[n_in + n_out:]
    copies = [pltpu.make_async_copy(src.at[layer], dst, sems.at[k]) for k, (src, dst) in enumerate(zip(w_hbm, w_vmem))]
    first_step = pl.program_id(0) == 0

    @pl.when(first_step)
    def _():
        for c in copies:
            c.start()

    _mixer_kernel(layer, *mixer_in, mid_ref, st_ref, conv_ref, kn_ref, vn_ref, z_ref, hist_ref, o_ref, st0_ref)

    @pl.when(first_step)
    def _():
        for c in copies:
            c.wait()

    _ffn_kernel(mid_ref, p_ref, *w_vmem, norms_ref, y_ref)


def _prompt_layer(x, p, rope, wl, l, consts):
    t_len = x.shape[0]
    row_spec = lambda w: pl.BlockSpec((TILE, w), lambda i: (i, 0))
    once = dict(pipeline_mode=pl.Buffered(1))
    smalls = [wl['nmix'], wl['lbp'], wl['aon'], wl['cw'], wl['cvec'], wl['qk']]
    ffn_ws = [wl['w_gate'], wl['w_up'], wl['w_down'], wl['w_ple_gate'], wl['w_ple_proj']]
    in_specs = ([row_spec(D_MODEL), _const_spec(rope[0].shape), _const_spec(rope[1].shape),
                 _layer_spec(wl['w_in'], l, **once), _layer_spec(wl['w_out'], l, **once)]
                + [_layer_spec(a, l) for a in smalls] + [pl.BlockSpec(memory_space=pltpu.SMEM)]
                + [_const_spec(a.shape) for a in consts]
                + [pl.BlockSpec((None, TILE, PLE_DIM), lambda i, _l=l: (_l, i, 0))]
                + [pl.BlockSpec(memory_space=pl.ANY) for a in ffn_ws] + [_layer_spec(wl['fnorms'], l)])
    assert len(in_specs) == N_MIXER_INPUTS + N_FFN_INPUTS
    out_shape = (jax.ShapeDtypeStruct((t_len, D_MODEL), f32),
                 jax.ShapeDtypeStruct((A_W, A_W), f32),
                 jax.ShapeDtypeStruct((CONV_WIDTH - 1, B_W), f32),
                 jax.ShapeDtypeStruct((WINDOW, KV_W), f32),
                 jax.ShapeDtypeStruct((WINDOW, KV_W), f32))
    out_specs = (row_spec(D_MODEL), _const_spec((A_W, A_W)), _const_spec((CONV_WIDTH - 1, B_W)),
                 _const_spec((WINDOW, KV_W)), _const_spec((WINDOW, KV_W)))
    return pl.pallas_call(
        functools.partial(_layer_kernel, l), grid=(t_len // TILE,), in_specs=in_specs, out_specs=out_specs,
        out_shape=out_shape,
        scratch_shapes=[pltpu.VMEM((TILE, IN_COLS), f32), pltpu.VMEM((TILE + HIST, B_W), f32),
                        pltpu.VMEM((TILE, A_W), f32), pltpu.VMEM((A_W, A_W), f32),
                        pltpu.VMEM((TILE, D_MODEL), f32)]
        + [pltpu.VMEM(a.shape[1:], a.dtype) for a in ffn_ws] + [pltpu.SemaphoreType.DMA((len(ffn_ws),))],
        compiler_params=pltpu.CompilerParams(dimension_semantics=("arbitrary",),
                                             vmem_limit_bytes=LAYER_VMEM_LIMIT),
        name="prompt_layer",
    )(x, rope[0], rope[1], wl['w_in'], wl['w_out'], *smalls, wl['sinks'], *consts, p, *ffn_ws, wl['fnorms'])


N_HEADS = C_W // HEAD


def _sample_in_kernel(n_tok, x_ref, cos_ref, sin_ref, win_ref, nmix_ref, qk_ref, e_ref, z_ref, qx_ref, k_ref,
                      piece_ref):
    n_seq = x_ref.shape[0] // n_tok
    xn = _rmsnorm_rows(x_ref[...], nmix_ref[...])
    z = jnp.dot(xn.astype(bf16), win_ref[...], preferred_element_type=f32)
    z_ref[...] = z
    em = _mean_matrix(e_ref[...])
    c1 = 4 * A_W + 2 * B_W
    cos, sin = cos_ref[...], sin_ref[...]
    qn = jnp.concatenate([_head_norm(z[:, c1 + j * 256:c1 + (j + 1) * 256], em,
                                     jnp.concatenate([qk_ref[0:1, :]] * 2, axis=1)) for j in range(2)], axis=1)
    q = _rope(qn, jnp.concatenate([cos] * 4, axis=1), jnp.concatenate([sin] * 4, axis=1)) * (HEAD ** -0.5)
    k_ref[...] = _rope(_head_norm(z[:, c1 + C_W:c1 + C_W + KV_W], em[0:KV_W, 0:KV_W], qk_ref[1:2, :]), cos, sin)
    lo_f = lax.broadcasted_iota(jnp.int32, (x_ref.shape[0], KV_W), 1) < HEAD
    for h in range(N_HEADS):
        pair, g = q[:, (h // 2) * KV_W:(h // 2 + 1) * KV_W], h // 4
        own = jnp.where(lo_f if h % 2 == 0 else jnp.logical_not(lo_f), pair, 0.0)
        piece_ref[...] = own if h % 2 == g else pltpu.roll(own, HEAD, 1)
        for t in range(n_tok):
            qx_ref[pl.ds(h * n_tok + t, n_seq, stride=N_HEADS * n_tok), :] = piece_ref[pl.ds(t, n_seq, stride=n_tok), :]


def _sample_in(x, cos, sin, wl, l, e256, n_tok):
    n = x.shape[0]
    args = (x, cos, sin, wl['w_in'], wl['nmix'], wl['qk'], e256)
    specs = [_const_spec(a.shape) for a in args]
    specs[3:6] = [_layer_spec(a, l) for a in args[3:6]]
    return pl.pallas_call(
        functools.partial(_sample_in_kernel, n_tok), grid=(1,), in_specs=specs,
        out_specs=(_const_spec((n, IN_COLS)), _const_spec((n * N_HEADS, KV_W)), _const_spec((n, KV_W))),
        out_shape=(jax.ShapeDtypeStruct((n, IN_COLS), f32), jax.ShapeDtypeStruct((n * N_HEADS, KV_W), f32),
                   jax.ShapeDtypeStruct((n, KV_W), f32)),
        scratch_shapes=[pltpu.VMEM((n, KV_W), f32)],
        compiler_params=pltpu.CompilerParams(dimension_semantics=("arbitrary",), vmem_limit_bytes=VMEM_LIMIT),
        name="sample_in",
    )(*args)


def _sample_hgrn_kernel(n_tok, q_ref, f_ref, v_ref, s_ref, lbp_ref, o_ref, so_ref, qt_scr, ft_scr, kt_scr, vt_scr,
                        ot_scr):
    n_seq = s_ref.shape[1]
    for t in range(n_tok):
        zf = f_ref[pl.ds(t, n_seq, stride=n_tok), :]
        logf, ls = _log_forget(zf, lbp_ref[0:1, :], lbp_ref[1:2, :])
        ft_scr[t] = jnp.exp(logf).T
        kt_scr[t] = (lbp_ref[2:3, :] * jnp.exp(ls - zf)).T
        qt_scr[t] = q_ref[pl.ds(t, n_seq, stride=n_tok), :].T
        vt_scr[t] = v_ref[pl.ds(t, n_seq, stride=n_tok), :].T
        ot_scr[t] = jnp.zeros((2 * HEAD, n_seq), f32)

    def body(hk, carry):
        r0 = pl.multiple_of(hk * HEAD, HEAD)
        v0 = pl.multiple_of((hk // HEAD) * HEAD, HEAD)
        blk = s_ref[pl.ds(r0, HEAD), :]
        for t in range(n_tok):
            blk = blk * ft_scr[t, pl.ds(hk, 1), :] + kt_scr[t, pl.ds(hk, 1), :] * vt_scr[t, pl.ds(v0, HEAD), :]
            ot_scr[t, pl.ds(v0, HEAD), :] = ot_scr[t, pl.ds(v0, HEAD), :] + qt_scr[t, pl.ds(hk, 1), :] * blk
        so_ref[pl.ds(r0, HEAD), :] = blk
        return carry

    lax.fori_loop(0, 2 * HEAD, body, 0, unroll=4)
    for t in range(n_tok):
        o_ref[pl.ds(t, n_seq, stride=n_tok), :] = ot_scr[t].T


def _sample_hgrn(z, state, lbp, l, n_tok):
    n = z.shape[0]
    n_seq = state.shape[2]
    blk = 2 * HEAD * HEAD
    col = lambda j0: pl.BlockSpec((n, 2 * HEAD), lambda i, _j=j0: (0, _j + i))
    return pl.pallas_call(
        functools.partial(_sample_hgrn_kernel, n_tok), grid=(2,),
        in_specs=[col(0), col(2), col(4), pl.BlockSpec((None, blk, n_seq), lambda i, _l=l: (_l, i, 0)),
                  pl.BlockSpec((None, 3, 2 * HEAD), lambda i, _l=l: (_l, 0, i))],
        out_specs=(pl.BlockSpec((n, 2 * HEAD), lambda i: (0, i)), pl.BlockSpec((blk, n_seq), lambda i: (i, 0))),
        out_shape=(jax.ShapeDtypeStruct((n, A_W), f32), jax.ShapeDtypeStruct(state.shape[1:], f32)),
        scratch_shapes=[pltpu.VMEM((n_tok, 2 * HEAD, n_seq), f32)] * 5,
        compiler_params=pltpu.CompilerParams(dimension_semantics=("arbitrary",), vmem_limit_bytes=VMEM_LIMIT),
        name="sample_hgrn",
    )(z, z, z, state, lbp)


def _sample_conv_kernel(n_tok, first, u0_ref, u1_ref, g0_ref, g1_ref, buf_ref, cw_ref, cvec_ref, e_ref, *rest):
    ob0_ref, ob1_ref, new_ref = rest[-3:]
    slots = [new_ref.at[d] for d in range(new_ref.shape[0])] if first else [new_ref]
    n_seq = buf_ref.shape[1]
    n_hist = CONV_WIDTH - 1
    us = []
    for t in range(n_tok):
        rows = pl.ds(t, n_seq, stride=n_tok)
        u = jnp.concatenate([u0_ref[rows, :], u1_ref[rows, :]], axis=1)
        g = jnp.concatenate([g0_ref[rows, :], g1_ref[rows, :]], axis=1)
        us.append(u * _sigmoid(g))

    def slab(j):
        return buf_ref[j] if j < n_hist else us[j - n_hist]

    em = _mean_matrix(e_ref[...])
    for t in range(n_tok):
        y = cvec_ref[0:1, :]
        for j in range(CONV_WIDTH):
            y = y + slab(t + j) * cw_ref[j:j + 1, :]
        ob = _group_ln_silu(y, em, cvec_ref[1:2, :], cvec_ref[2:3, :])
        ob0_ref[pl.ds(t, n_seq, stride=n_tok), :] = ob[:, 0:KV_W]
        ob1_ref[pl.ds(t, n_seq, stride=n_tok), :] = ob[:, KV_W:B_W]
    for dst in slots:
        for j in range(n_hist):
            dst[j] = slab(j + n_tok)


def _sample_conv(z, buf, new_all, wl, l, e256, n_tok):
    n = z.shape[0]
    col = lambda j: pl.BlockSpec((n, KV_W), lambda i, _j=j: (0, _j))
    half = jax.ShapeDtypeStruct((n, KV_W), f32)
    in_specs = [col(8), col(9), col(10), col(11), _layer_spec(buf, l), _layer_spec(wl['cw'], l),
                _layer_spec(wl['cvec'], l), _const_spec(e256.shape)]
    args = [z, z, z, z, buf, wl['cw'], wl['cvec'], e256]
    if new_all is None:
        new_spec, aliases = _const_spec(buf.shape), {}
    else:
        in_specs.append(pl.BlockSpec(memory_space=pl.ANY))
        args.append(new_all)
        new_spec, aliases = _layer_spec(buf, l), {len(args) - 1: 2}
    return pl.pallas_call(
        functools.partial(_sample_conv_kernel, n_tok, new_all is None), grid=(1,), in_specs=in_specs,
        out_specs=(_const_spec((n, KV_W)), _const_spec((n, KV_W)), new_spec),
        out_shape=(half, half, jax.ShapeDtypeStruct(buf.shape, f32)),
        input_output_aliases=aliases,
        compiler_params=pltpu.CompilerParams(dimension_semantics=("arbitrary",), vmem_limit_bytes=VMEM_LIMIT),
        name="sample_conv",
    )(*args)


SEQ_BLOCK = 32


def _sample_attn_kernel(n_tok, qx_ref, k_ref, v_ref, kc_ref, vc_ref, sink_ref, bias_ref, ox_ref, kn_ref, vn_ref):
    q_rows = N_HEADS * n_tok
    sink, bias = sink_ref[...], bias_ref[...]
    seqs = range(SEQ_BLOCK)
    k_all = [jnp.concatenate([kc_ref[b], k_ref[b * n_tok:(b + 1) * n_tok, :]], axis=0) for b in seqs]
    v_all = [jnp.concatenate([vc_ref[b], v_ref[b * n_tok:(b + 1) * n_tok, :]], axis=0) for b in seqs]
    s = [lax.dot_general(qx_ref[b * q_rows:(b + 1) * q_rows, :].astype(bf16), k_all[b].astype(bf16),
                         (((1,), (1,)), ((), ())), preferred_element_type=f32) + bias for b in seqs]
    mx = [jnp.maximum(jnp.max(s[b], axis=-1, keepdims=True), sink) for b in seqs]
    pr = [jnp.exp(s[b] - mx[b]) for b in seqs]
    rden = [1.0 / (jnp.sum(pr[b], axis=-1, keepdims=True) + jnp.exp(sink - mx[b])) for b in seqs]
    for b in seqs:
        ox_ref[b * q_rows:(b + 1) * q_rows, :] = jnp.dot(pr[b].astype(bf16), v_all[b].astype(bf16),
                                                         preferred_element_type=f32) * rden[b]
    for b in seqs:
        kn_ref[b] = k_all[b][n_tok:, :]
        vn_ref[b] = v_all[b][n_tok:, :]


def _sample_attn(qx, k, z, kc, vc, sink_col, bias, l, n_tok):
    n_seq = kc.shape[1]
    assert n_seq % SEQ_BLOCK == 0
    rows = SEQ_BLOCK * n_tok
    q_rows = SEQ_BLOCK * N_HEADS * n_tok
    v_col = (4 * A_W + 2 * B_W + C_W + KV_W) // KV_W
    cache_in = pl.BlockSpec((None, SEQ_BLOCK) + kc.shape[2:], lambda i, _l=l: (_l, i, 0, 0))
    cache_out = pl.BlockSpec((SEQ_BLOCK,) + kc.shape[2:], lambda i: (i, 0, 0))
    return pl.pallas_call(
        functools.partial(_sample_attn_kernel, n_tok), grid=(n_seq // SEQ_BLOCK,),
        in_specs=[pl.BlockSpec((q_rows, KV_W), lambda i: (i, 0)), pl.BlockSpec((rows, KV_W), lambda i: (i, 0)),
                  pl.BlockSpec((rows, KV_W), lambda i: (i, v_col)), cache_in, cache_in,
                  _layer_spec(sink_col, l), _const_spec(bias.shape)],
        out_specs=(pl.BlockSpec((q_rows, KV_W), lambda i: (i, 0)), cache_out, cache_out),
        out_shape=(jax.ShapeDtypeStruct(qx.shape, f32), jax.ShapeDtypeStruct(kc.shape[1:], f32),
                   jax.ShapeDtypeStruct(vc.shape[1:], f32)),
        compiler_params=pltpu.CompilerParams(dimension_semantics=("arbitrary",), vmem_limit_bytes=VMEM_LIMIT),
        name="sample_attn",
    )(qx, k, z, kc, vc, sink_col, bias)


def _sample_out_kernel(n_tok, x_ref, oa_ref, g_ref, ob0_ref, ob1_ref, ox_ref, wout_ref, aon_ref, e_ref, h_ref,
                       oc_ref):
    n_seq = x_ref.shape[0] // n_tok
    gate = g_ref[...]
    oa = _head_norm(oa_ref[...], _mean_matrix(e_ref[...]), aon_ref[...]) * (gate * _sigmoid(gate))
    lo_f = lax.broadcasted_iota(jnp.int32, (n_seq, KV_W), 1) < HEAD
    for p in range(N_HEADS // 2):
        g = p // 2
        for t in range(n_tok):
            even = ox_ref[pl.ds((2 * p) * n_tok + t, n_seq, stride=N_HEADS * n_tok), :]
            odd = ox_ref[pl.ds((2 * p + 1) * n_tok + t, n_seq, stride=N_HEADS * n_tok), :]
            even = even if g == 0 else pltpu.roll(even, HEAD, 1)
            odd = odd if g == 1 else pltpu.roll(odd, HEAD, 1)
            oc_ref[p, pl.ds(t, n_seq, stride=n_tok), :] = jnp.where(lo_f, even, odd)
    mix = jnp.concatenate([oa, ob0_ref[...], ob1_ref[...]] + [oc_ref[p] for p in range(N_HEADS // 2)],
                          axis=1).astype(bf16)
    h_ref[...] = x_ref[...] + jnp.dot(mix, wout_ref[...], preferred_element_type=f32)


N_OUT_INPUTS = 9


def _sample_tail_kernel(n_tok, *refs):
    n_in = N_OUT_INPUTS + N_FFN_INPUTS
    out_in, ffn_in = refs[:N_OUT_INPUTS], refs[N_OUT_INPUTS:n_in]
    y_ref, oc_ref, mid_ref = refs[n_in:]
    _sample_out_kernel(n_tok, *out_in, mid_ref, oc_ref)
    _ffn_kernel(mid_ref, *ffn_in, y_ref)


def _sample_tail(x, oa, z, ob0, ob1, ox, p, wl, l, e256, n_tok):
    n = x.shape[0]
    once = dict(pipeline_mode=pl.Buffered(1))
    args = (x, oa, z, ob0, ob1, ox, wl['w_out'], wl['aon'], e256)
    specs = [_const_spec(a.shape) for a in args]
    specs[2] = pl.BlockSpec((n, A_W), lambda i: (0, 3))
    specs[6:8] = [_layer_spec(a, l) for a in args[6:8]]
    ffn_ws = [wl['w_gate'], wl['w_up'], wl['w_down'], wl['w_ple_gate'], wl['w_ple_proj']]
    specs += ([_layer_spec(p, l)] + [_layer_spec(a, l, **once) for a in ffn_ws] + [_layer_spec(wl['fnorms'], l)])
    assert len(specs) == N_OUT_INPUTS + N_FFN_INPUTS
    return pl.pallas_call(
        functools.partial(_sample_tail_kernel, n_tok), grid=(1,), in_specs=specs,
        out_specs=_const_spec((n, D_MODEL)), out_shape=jax.ShapeDtypeStruct((n, D_MODEL), f32),
        scratch_shapes=[pltpu.VMEM((N_HEADS // 2, n, KV_W), f32), pltpu.VMEM((n, D_MODEL), f32)],
        compiler_params=pltpu.CompilerParams(dimension_semantics=("arbitrary",), vmem_limit_bytes=VMEM_LIMIT),
        name="sample_tail",
    )(*args, p, *ffn_ws, wl['fnorms'])


def _rope_cos_sin(pos):
    half = HEAD // 2
    inv = ROPE_THETA ** (-jnp.arange(half, dtype=f32) / half)
    ang = pos[:, None] * inv[None, :]
    cos, sin = jnp.cos(ang), jnp.sin(ang)
    return jnp.tile(cos, (1, 4)), jnp.concatenate([-sin, sin, -sin, sin], axis=1)


def _rope_prompt_tables(t_len):
    half = HEAD // 2
    inv = ROPE_THETA ** (-jnp.arange(half, dtype=f32) / half)
    sign = jnp.concatenate([-jnp.ones((half,), f32), jnp.ones((half,), f32)] * 2)
    ang_r = jnp.arange(TILE, dtype=f32)[:, None] * inv[None, :]
    ang_t = (jnp.arange(t_len // TILE, dtype=f32) * TILE)[:, None] * inv[None, :]
    cos_r, sin_r = jnp.tile(jnp.cos(ang_r), (1, 4)), jnp.tile(jnp.sin(ang_r), (1, 4))
    rows = jnp.stack([cos_r, sin_r, cos_r * sign, sin_r * sign])
    tiles = jnp.stack([jnp.tile(jnp.cos(ang_t), (1, 4)), jnp.tile(jnp.sin(ang_t), (1, 4))])
    return rows, tiles


def _stacked_weights(lbs, n_tok, w_in, a_onorm, conv_w, conv_b, conv_ln_g, conv_ln_b, q_norm, k_norm, sinks, w_out,
                     norm_mix, norm_ffn, w_gate, w_up, w_down, ple_norm, w_ple_gate, w_ple_proj):
    return dict(
        w_in=w_in.astype(bf16), w_out=w_out.astype(bf16), w_gate=w_gate.astype(bf16), w_up=w_up.astype(bf16),
        w_down=w_down.astype(bf16), w_ple_gate=w_ple_gate.astype(bf16), w_ple_proj=w_ple_proj.astype(bf16),
        nmix=norm_mix[:, None, :],
        lbp=jnp.stack([jnp.log(lbs), jnp.log1p(-lbs), 1.0 - lbs], axis=1),
        aon=jnp.tile(a_onorm, (1, 4))[:, None, :],
        cw=conv_w,
        cvec=jnp.stack([conv_b, conv_ln_g, conv_ln_b], axis=1),
        qk=jnp.stack([jnp.tile(q_norm, (1, 2)), jnp.tile(k_norm, (1, 2))], axis=1),
        sinks=sinks,
        sink_col=jnp.repeat(sinks, n_tok, axis=1)[:, :, None],
        fnorms=jnp.stack([norm_ffn, ple_norm], axis=1),
    )


def kernel(x_prompt, x_sample, state_hgrn, state_conv, cache_swa_k, cache_swa_v, p_prompt, p_sample, a_lower, w_in, a_onorm, conv_w, conv_b, conv_ln_g, conv_ln_b, q_norm, k_norm, sinks, w_out, norm_mix, norm_ffn, w_gate, w_up, w_down, ple_norm, w_ple_gate, w_ple_proj):
    depth = w_in.shape[0]
    t_len = x_prompt.shape[1]
    n_seq, n_tok = x_sample.shape[0], x_sample.shape[1]
    past_len = PAST_LEN
    assert x_prompt.shape[0] == 1 and t_len % TILE == 0

    lbs = jnp.cumsum(jax.nn.softmax(a_lower.astype(f32), axis=0), axis=0)
    lbs = lbs - lbs[0:1]
    consts_np = _np_constants()
    e256 = jnp.asarray(consts_np[0], bf16)
    consts = (e256, jnp.asarray(consts_np[1], bf16), jnp.asarray(consts_np[2], f32), jnp.asarray(consts_np[3], bf16),
              jnp.asarray(consts_np[4], f32))

    rope_p = _rope_prompt_tables(t_len)
    cos_s, sin_s = _rope_cos_sin(past_len + jnp.arange(n_tok, dtype=f32))
    cos_s, sin_s = jnp.tile(cos_s, (n_seq, 1)), jnp.tile(sin_s, (n_seq, 1))
    w_buf = cache_swa_k.shape[2]
    q_tok = (np.arange(N_HEADS * n_tok) % n_tok)[:, None]
    rel = q_tok + w_buf - np.arange(w_buf + n_tok)[None, :]
    bias_s = jnp.asarray(np.where((rel >= 0) & (rel <= WINDOW), 0.0, -np.inf), f32)

    wl = _stacked_weights(lbs, n_tok, w_in, a_onorm, conv_w, conv_b, conv_ln_g, conv_ln_b, q_norm, k_norm, sinks,
                          w_out, norm_mix, norm_ffn, w_gate, w_up, w_down, ple_norm, w_ple_gate, w_ple_proj)
    p_p = p_prompt.reshape(depth, t_len, PLE_DIM)
    p_s = p_sample.reshape(depth, n_seq * n_tok, PLE_DIM)
    s_hgrn = jnp.swapaxes(state_hgrn.reshape(depth, n_seq, -1), 1, 2)
    kc = cache_swa_k.reshape(depth, n_seq, w_buf, KV_W)
    vc = cache_swa_v.reshape(depth, n_seq, w_buf, KV_W)

    hp = x_prompt[0]
    hs = x_sample.reshape(n_seq * n_tok, D_MODEL)
    outs = [[] for _ in range(8)]
    conv_in = jnp.swapaxes(state_conv, 1, 2)
    conv_s = None
    for l in range(depth):
        hp, st, conv_new, k_new, v_new = _prompt_layer(hp, p_p, rope_p, wl, l, consts)
        s_new = jnp.stack([st[h * HEAD:(h + 1) * HEAD, h * HEAD:(h + 1) * HEAD].T for h in range(4)])
        outs[0].append(s_new[None])
        outs[1].append(conv_new[None])
        outs[2].append(k_new.reshape(1, WINDOW, 2, HEAD))
        outs[3].append(v_new.reshape(1, WINDOW, 2, HEAD))
        z, qx, k = _sample_in(hs, cos_s, sin_s, wl, l, e256, n_tok)
        oa, s_s = _sample_hgrn(z, s_hgrn, wl['lbp'], l, n_tok)
        ob0, ob1, conv_s = _sample_conv(z, conv_in, conv_s, wl, l, e256, n_tok)
        ox, k_s, v_s = _sample_attn(qx, k, z, kc, vc, wl['sink_col'], bias_s, l, n_tok)
        hs = _sample_tail(hs, oa, z, ob0, ob1, ox, p_s, wl, l, e256, n_tok)
        outs[4].append(s_s.T.reshape(n_seq, 4, HEAD, HEAD))
        outs[6].append(k_s.reshape(n_seq, w_buf, 2, HEAD))
        outs[7].append(v_s.reshape(n_seq, w_buf, 2, HEAD))
    stacked = [jnp.swapaxes(conv_s, 1, 2) if i == 5 else jnp.stack(o) for i, o in enumerate(outs)]
    return (hp[None], hs.reshape(n_seq, n_tok, D_MODEL)) + tuple(stacked)
```

```python
import functools

import numpy as np
import jax
import jax.numpy as jnp
from jax import lax
from jax.experimental import pallas as pl
from jax.experimental.pallas import tpu as pltpu

f32 = jnp.float32
bf16 = jnp.bfloat16

D_MODEL = 1024
HEAD = 64
A_W = 256
B_W = 256
C_W = 512
KV_W = 128
IN_COLS = 4 * A_W + 2 * B_W + C_W + 2 * KV_W
PLE_DIM = 256
WINDOW = 128
CONV_WIDTH = 31
ROPE_THETA = 10000.0
EPS = 1e-6
PAST_LEN = 16384

TILE = 512
CUM_ROWS = 256
CHUNK = 64
LEVELS = (32, 16, 8, 4, 2, 1)
SAFE_EXPONENT = 80.0
HIST = 32
FFN_SLABS = 4
LOG2E = 1.4426950408889634
VMEM_LIMIT = 52 * 1024 * 1024
LAYER_VMEM_LIMIT = 58 * 1024 * 1024


def _split2(x):
    hi = x.astype(bf16)
    lo = (x - hi.astype(f32)).astype(bf16)
    return hi, lo


def _seg_sum(x, e):
    hi, lo = _split2(x)
    return (jnp.dot(hi, e, preferred_element_type=f32) + jnp.dot(lo, e, preferred_element_type=f32))


def _sigmoid(x):
    return 1.0 / (1.0 + jnp.exp(-x))


def _rmsnorm_rows(x, g):
    return x * lax.rsqrt(jnp.mean(x * x, axis=-1, keepdims=True) + EPS) * g


def _seg_stat(x, e):
    return jnp.dot(x.astype(bf16), e, preferred_element_type=f32)


def _mean_matrix(e):
    return e * (1.0 / HEAD)


def _head_norm(x, em, g):
    return x * lax.rsqrt(_seg_stat(x * x, em) + EPS) * g


def _rope(x, cos, sin_signed):
    w = x.shape[-1]
    lane = lax.broadcasted_iota(jnp.int32, x.shape, 1)
    first = (lane % HEAD) < (HEAD // 2)
    swapped = jnp.where(first, pltpu.roll(x, w - HEAD // 2, 1), pltpu.roll(x, HEAD // 2, 1))
    return x * cos + swapped * sin_signed


def _log_forget(zf, log_lb, log_1m_lb):
    ls = jnp.minimum(zf, 0.0) - jnp.log(1.0 + jnp.exp(-jnp.abs(zf)))
    b = log_1m_lb + ls
    return jnp.maximum(log_lb, b) + jnp.log(1.0 + jnp.exp(-jnp.abs(log_lb - b))), ls


def _group_ln_silu(y, em, g, b):
    yc = y - _seg_stat(y, em)
    var = _seg_stat(yc * yc, em)
    yn = yc * lax.rsqrt(var + EPS) * g + b
    return yn * _sigmoid(yn)


def _np_constants():
    lane = np.arange(256)
    e256 = (lane[:, None] // HEAD == lane[None, :] // HEAD).astype(np.float32)
    t = np.arange(CUM_ROWS)
    ltri = ((t[:, None] // CHUNK == t[None, :] // CHUNK) & (t[None, :] <= t[:, None])).astype(np.float32)
    tt = np.arange(CHUNK)[:, None]
    ss = (np.arange(256) % CHUNK)[None, :]
    lvl = []
    for m in LEVELS:
        lvl.append(((tt // (2 * m) == ss // (2 * m)) & (tt % (2 * m) >= m) & (ss % (2 * m) < m)).astype(np.float32))
    lvl.append((ss <= tt).astype(np.float32))
    lvl = np.stack(lvl)
    hm = np.stack([np.broadcast_to((lane // HEAD == h)[None, :], (CHUNK, 256)) for h in range(4)]).astype(np.float32)
    keys = np.arange(2 * WINDOW)[:, None]
    qry = np.arange(4 * WINDOW)[None, :] % WINDOW
    band = (keys >= qry) & (keys <= qry + WINDOW)
    bias = np.where(np.stack([band & (keys >= WINDOW), band]), 0.0, -np.inf).astype(np.float32)
    return e256, ltri, lvl, hm, bias


def _level_ref(g, m):
    n_rows = g.shape[0]
    if 2 * m >= 8:
        g3 = g.reshape(n_rows // (2 * m), 2 * m, g.shape[1])
        return jnp.broadcast_to(g3[:, m - 1:m, :], g3.shape).reshape(g.shape)
    g3 = g.reshape(n_rows // 8, 8, g.shape[1])
    sub = lax.broadcasted_iota(jnp.int32, g3.shape, 1)
    out = None
    for blk in range(8 // (2 * m)):
        r = blk * 2 * m + m - 1
        piece = jnp.broadcast_to(g3[:, r:r + 1, :], g3.shape)
        out = piece if out is None else jnp.where(sub >= blk * 2 * m, piece, out)
    return out.reshape(g.shape)


def _hgrn_tile(q, zf, v, lbp_ref, st_ref, st0_ref, o_ref, e, ltri_ref, lvl_ref, hm_ref):
    log_lb, log_1m_lb, one_m_lb = lbp_ref[0:1, :], lbp_ref[1:2, :], lbp_ref[2:3, :]
    logf, ls = _log_forget(zf, log_lb, log_1m_lb)
    kin = one_m_lb * jnp.exp(ls - zf)
    parts = jnp.concatenate(_split2(logf), axis=1)
    g2 = jnp.concatenate([jnp.dot(ltri_ref[...], parts[r:r + CUM_ROWS], preferred_element_type=f32)
                          for r in range(0, TILE, CUM_ROWS)], axis=0)
    g = g2[:, 0:A_W] + g2[:, A_W:2 * A_W]
    v_b = v.astype(bf16)
    vt_b = v.T.astype(bf16)
    e_mid = g - _level_ref(g, CHUNK // 2)
    safe = jnp.max(jnp.abs(e_mid)) < SAFE_EXPONENT

    def expand(rows_b):
        return jnp.concatenate([rows_b * hm_ref[h] for h in range(4)], axis=0)

    def finish_chunk(c, att, q_in, k_hat):
        r0, r1 = c * CHUNK, (c + 1) * CHUNK
        st = st_ref[...]
        o = jnp.dot(att.astype(bf16), expand(v_b[r0:r1]), preferred_element_type=f32)
        o = o + lax.dot_general(q_in, st.astype(bf16) * e, (((1,), (1,)), ((), ())), preferred_element_type=f32)
        zero = jnp.zeros((CHUNK, A_W), bf16)
        pair = (c // 2) * 2 * CHUNK
        rhs = jnp.concatenate([k_hat, zero] if c % 2 == 0 else [zero, k_hat], axis=0)
        upd = jnp.dot(vt_b[:, pair:pair + 2 * CHUNK], rhs, preferred_element_type=f32)
        st_ref[...] = st * jnp.exp(g[r1 - 1:r1, :]) + upd
        o_ref[r0:r1, :] = o

    st0_ref[...] = st_ref[...]
    qw = q * jnp.exp(e_mid)
    kw = kin * jnp.exp(-e_mid)
    q_t, k_t = qw.astype(bf16), kw.astype(bf16)
    chunks = [(c * CHUNK, (c + 1) * CHUNK) for c in range(TILE // CHUNK)]
    atts = [lax.dot_general(q_t[r0:r1], expand(k_t[r0:r1]), (((1,), (1,)), ((), ())), preferred_element_type=f32)
            for r0, r1 in chunks]
    atts = [jnp.where(lvl_ref[len(LEVELS)] > 0.5, a, 0.0) for a in atts]
    for c, (r0, r1) in enumerate(chunks):
        g_mid, g_end = g[r0 + CHUNK // 2 - 1:r0 + CHUNK // 2, :], g[r1 - 1:r1, :]
        finish_chunk(c, atts[c], (qw[r0:r1] * jnp.exp(g_mid)).astype(bf16),
                     (kw[r0:r1] * jnp.exp(g_end - g_mid)).astype(bf16))

    def redo_if_unsafe():
        @pl.when(jnp.logical_not(safe))
        def _():
            st_ref[...] = st0_ref[...]
            q_levels, k_levels = [], []
            for m in LEVELS:
                w = jnp.exp(-jnp.abs(g - _level_ref(g, m)))
                q_levels.append((q * w).astype(bf16))
                k_levels.append((kin * w).astype(bf16))
            q_inter = (q * jnp.exp(g)).astype(bf16)
            for c in range(TILE // CHUNK):
                r0, r1 = c * CHUNK, (c + 1) * CHUNK
                att = jnp.zeros((CHUNK, 4 * CHUNK), f32)
                for li in range(len(LEVELS)):
                    a = lax.dot_general(q_levels[li][r0:r1], expand(k_levels[li][r0:r1]),
                                        (((1,), (1,)), ((), ())), preferred_element_type=f32)
                    att = att + a * lvl_ref[li]
                k_hat = (kin[r0:r1] * jnp.exp(g[r1 - 1:r1, :] - g[r0:r1])).astype(bf16)
                finish_chunk(c, att, q_inter[r0:r1], k_hat)
            o_ref[...] = o_ref[...] + _seg_sum(q * kin, e) * v

    return redo_if_unsafe


def _conv_tile(u, hist_ref, cw_ref, cvec_ref, em):
    n = u.shape[0]
    hist_ref[HIST:HIST + n, :] = u
    lo = HIST - (CONV_WIDTH - 1)
    y = cvec_ref[0:1, :]
    for r in range(8):
        rows = n if r == 0 else n + 8
        part = None
        for o in range(lo, lo + CONV_WIDTH):
            if o % 8 == r:
                term = hist_ref[o - r:o - r + rows, :] * cw_ref[o - lo:o - lo + 1, :]
                part = term if part is None else part + term
        y = y + part[r:r + n]
    return _group_ln_silu(y, em, cvec_ref[1:2, :], cvec_ref[2:3, :])


def _attn_tile(q_t, k_b, v_t, sink_ref, biases):
    zero = jnp.zeros((HEAD, WINDOW), f32)
    inst = [(b, g) for b in range(TILE // WINDOW) for g in range(2)]
    sinks = [jnp.concatenate([jnp.full((1, WINDOW), sink_ref[h] * LOG2E, f32) for h in range(4 * g, 4 * g + 4)],
                             axis=1) for g in range(2)]

    def rhs(b, g):
        cols = []
        for h in range(4 * g, 4 * g + 4):
            q_h = q_t[h * HEAD:(h + 1) * HEAD, b * WINDOW:(b + 1) * WINDOW]
            cols.append(jnp.concatenate([q_h, zero] if g == 0 else [zero, q_h], axis=0))
        return jnp.concatenate(cols, axis=1).astype(bf16)

    s = [jnp.dot(k_b[b * WINDOW:(b + 2) * WINDOW], rhs(b, g), preferred_element_type=f32) + biases[b]
         for b, g in inst]
    mx = [jnp.maximum(jnp.max(s[n], axis=0, keepdims=True), sinks[g]) for n, (b, g) in enumerate(inst)]
    pr = [jnp.exp2(s[n] - mx[n]) for n in range(len(inst))]
    rden = [1.0 / (jnp.sum(pr[n], axis=0, keepdims=True) + jnp.exp2(sinks[g] - mx[n]))
            for n, (b, g) in enumerate(inst)]
    o_t = [jnp.dot(v_t[g * HEAD:(g + 1) * HEAD, b * WINDOW:(b + 2) * WINDOW], pr[n].astype(bf16),
                   preferred_element_type=f32) * rden[n] for n, (b, g) in enumerate(inst)]
    rows = []
    for b in range(TILE // WINDOW):
        pairs = []
        for g in range(2):
            o = o_t[2 * b + g]
            for j in range(2):
                pair_t = jnp.concatenate([o[:, (2 * j) * WINDOW:(2 * j + 1) * WINDOW],
                                          o[:, (2 * j + 1) * WINDOW:(2 * j + 2) * WINDOW]], axis=0)
                pairs.append(pair_t.T)
        rows.append(jnp.concatenate(pairs, axis=1))
    return jnp.concatenate(rows, axis=0)


def _mixer_kernel(layer, x_ref, rope_row_ref, rope_tile_ref, win_ref, wout_ref, nmix_ref, lbp_ref, aon_ref, cw_ref,
                  cvec_ref, qk_ref, sinks_ref, e_ref, ltri_ref, lvl_ref, hm_ref, bias_ref,
                  h_ref, st_ref, conv_ref, kn_ref, vn_ref, z_ref, hist_ref, o_ref, st0_ref):
    i = pl.program_id(0)
    sink_ref = sinks_ref.at[layer]

    @pl.when(i == 0)
    def _():
        st_ref[...] = jnp.zeros_like(st_ref)
        hist_ref[0:HIST, :] = jnp.zeros((HIST, B_W), f32)
        kn_ref[...] = jnp.zeros_like(kn_ref)
        vn_ref[...] = jnp.zeros_like(vn_ref)

    x = x_ref[...]
    xn = _rmsnorm_rows(x, nmix_ref[...])
    z_ref[...] = jnp.dot(xn.astype(bf16), win_ref[...], preferred_element_type=f32)
    e = e_ref[...]

    redo_hgrn_if_unsafe = _hgrn_tile(z_ref[:, 0:A_W], z_ref[:, A_W:2 * A_W], z_ref[:, 2 * A_W:3 * A_W], lbp_ref,
                                     st_ref, st0_ref, o_ref, e, ltri_ref, lvl_ref, hm_ref)

    c0 = 4 * A_W
    u = z_ref[:, c0:c0 + B_W] * _sigmoid(z_ref[:, c0 + B_W:c0 + 2 * B_W])
    em = _mean_matrix(e)
    ob = _conv_tile(u, hist_ref, cw_ref, cvec_ref, em)
    conv_ref[...] = hist_ref[TILE + HIST - (CONV_WIDTH - 1):TILE + HIST, :]
    hist_ref[0:HIST, :] = hist_ref[TILE:TILE + HIST, :]

    c1 = c0 + 2 * B_W
    cos_a, sin_a = rope_tile_ref[0, pl.ds(i, 1), :], rope_tile_ref[1, pl.ds(i, 1), :]
    cos = cos_a * rope_row_ref[0] - sin_a * rope_row_ref[1]
    sin = sin_a * rope_row_ref[2] + cos_a * rope_row_ref[3]
    qn = jnp.concatenate([_head_norm(z_ref[:, c1 + j * 256:c1 + (j + 1) * 256], em,
                                     jnp.concatenate([qk_ref[0:1, :]] * 2, axis=1)) for j in range(2)], axis=1)
    q = _rope(qn, jnp.concatenate([cos] * 4, axis=1), jnp.concatenate([sin] * 4, axis=1)) * (LOG2E * HEAD ** -0.5)
    kk = _rope(_head_norm(z_ref[:, c1 + C_W:c1 + C_W + KV_W], em[0:KV_W, 0:KV_W], qk_ref[1:2, :]), cos, sin)
    vv = z_ref[:, c1 + C_W + KV_W:c1 + C_W + 2 * KV_W]
    kcat = jnp.concatenate([kn_ref[...], kk], axis=0)
    vcat = jnp.concatenate([vn_ref[...], vv], axis=0)
    q_t = q.T
    k_b = kcat.astype(bf16)
    v_t = vcat.T.astype(bf16)
    biases = [bias_ref[jnp.where(i > 0, 1, 0)]] + [bias_ref[1]] * (TILE // WINDOW - 1)
    oc = _attn_tile(q_t, k_b, v_t, sink_ref, biases)
    kn_ref[...] = kk[TILE - WINDOW:TILE]
    vn_ref[...] = vv[TILE - WINDOW:TILE]

    redo_hgrn_if_unsafe()
    gate = z_ref[:, 3 * A_W:4 * A_W]
    oa = _head_norm(o_ref[...], em, aon_ref[...]) * (gate * _sigmoid(gate))
    mix = jnp.concatenate([oa, ob, oc], axis=1).astype(bf16)
    h_ref[...] = x + jnp.dot(mix, wout_ref[...], preferred_element_type=f32)


def _const_spec(shape):
    nd = len(shape)
    return pl.BlockSpec(shape, lambda i, _n=nd: (0,) * _n)


def _layer_spec(arr, l, **kw):
    shape = arr.shape[1:]
    return pl.BlockSpec((None,) + shape, lambda i, _l=l, _n=len(shape): (_l,) + (0,) * _n, **kw)


def _ffn_kernel(h_ref, p_ref, wg_ref, wu_ref, wd_ref, wpg_ref, wpp_ref, norms_ref, y_ref, before_use=None):
    ready = before_use or [lambda: None] * 5
    n = h_ref.shape[0]
    slabs = [slice(r, r + n // FFN_SLABS) for r in range(0, n, n // FFN_SLABS)]
    h = [h_ref[s, :] for s in slabs]
    hn = [_rmsnorm_rows(a, norms_ref[0:1, :]).astype(bf16) for a in h]
    ready[0]()
    gt = [jnp.dot(a, wg_ref[...], preferred_element_type=f32) for a in hn]
    ready[1]()
    up = [jnp.dot(a, wu_ref[...], preferred_element_type=f32) for a in hn]
    act = [(g * _sigmoid(g) * u).astype(bf16) for g, u in zip(gt, up)]
    ready[2]()
    h = [a + jnp.dot(b, wd_ref[...], preferred_element_type=f32) for a, b in zip(h, act)]
    pn = [_rmsnorm_rows(a, norms_ref[1:2, :]).astype(bf16) for a in h]
    ready[3]()
    gate = [_sigmoid(jnp.dot(a, wpg_ref[...], preferred_element_type=f32)) for a in pn]
    ready[4]()
    for s, a, g in zip(slabs, h, gate):
        y_ref[s, :] = a + g * jnp.dot(p_ref[s, :].astype(bf16), wpp_ref[...], preferred_element_type=f32)


N_MIXER_INPUTS = 17
N_FFN_INPUTS = 7


def _layer_kernel(layer, *refs):
    n_in, n_out = N_MIXER_INPUTS + N_FFN_INPUTS, 5
    mixer_in, (p_ref, *w_hbm, norms_ref) = refs[:N_MIXER_INPUTS], refs[N_MIXER_INPUTS:n_in]
    y_ref, st_ref, conv_ref, kn_ref, vn_ref = refs[n_in:n_in + n_out]
    z_ref, hist_ref, o_ref, st0_ref, mid_ref, *w_vmem, sems = refs[n_in + n_out:]
    copies = [pltpu.make_async_copy(src.at[layer], dst, sems.at[k]) for k, (src, dst) in enumerate(zip(w_hbm, w_vmem))]
    first_step = pl.program_id(0) == 0

    @pl.when(first_step)
    def _():
        for c in copies:
            c.start()

    _mixer_kernel(layer, *mixer_in, mid_ref, st_ref, conv_ref, kn_ref, vn_ref, z_ref, hist_ref, o_ref, st0_ref)

    @pl.when(first_step)
    def _():
        for c in copies:
            c.wait()

    _ffn_kernel(mid_ref, p_ref, *w_vmem, norms_ref, y_ref)


def _prompt_layer(x, p, rope, wl, l, consts):
    t_len = x.shape[0]
    row_spec = lambda w: pl.BlockSpec((TILE, w), lambda i: (i, 0))
    once = dict(pipeline_mode=pl.Buffered(1))
    smalls = [wl['nmix'], wl['lbp'], wl['aon'], wl['cw'], wl['cvec'], wl['qk']]
    ffn_ws = [wl['w_gate'], wl['w_up'], wl['w_down'], wl['w_ple_gate'], wl['w_ple_proj']]
    in_specs = ([row_spec(D_MODEL), _const_spec(rope[0].shape), _const_spec(rope[1].shape),
                 _layer_spec(wl['w_in'], l, **once), _layer_spec(wl['w_out'], l, **once)]
                + [_layer_spec(a, l) for a in smalls] + [pl.BlockSpec(memory_space=pltpu.SMEM)]
                + [_const_spec(a.shape) for a in consts]
                + [pl.BlockSpec((None, TILE, PLE_DIM), lambda i, _l=l: (_l, i, 0))]
                + [pl.BlockSpec(memory_space=pl.ANY) for a in ffn_ws] + [_layer_spec(wl['fnorms'], l)])
    assert len(in_specs) == N_MIXER_INPUTS + N_FFN_INPUTS
    out_shape = (jax.ShapeDtypeStruct((t_len, D_MODEL), f32),
                 jax.ShapeDtypeStruct((A_W, A_W), f32),
                 jax.ShapeDtypeStruct((CONV_WIDTH - 1, B_W), f32),
                 jax.ShapeDtypeStruct((WINDOW, KV_W), f32),
                 jax.ShapeDtypeStruct((WINDOW, KV_W), f32))
    out_specs = (row_spec(D_MODEL), _const_spec((A_W, A_W)), _const_spec((CONV_WIDTH - 1, B_W)),
                 _const_spec((WINDOW, KV_W)), _const_spec((WINDOW, KV_W)))
    return pl.pallas_call(
        functools.partial(_layer_kernel, l), grid=(t_len // TILE,), in_specs=in_specs, out_specs=out_specs,
        out_shape=out_shape,
        scratch_shapes=[pltpu.VMEM((TILE, IN_COLS), f32), pltpu.VMEM((TILE + HIST, B_W), f32),
                        pltpu.VMEM((TILE, A_W), f32), pltpu.VMEM((A_W, A_W), f32),
                        pltpu.VMEM((TILE, D_MODEL), f32)]
        + [pltpu.VMEM(a.shape[1:], a.dtype) for a in ffn_ws] + [pltpu.SemaphoreType.DMA((len(ffn_ws),))],
        compiler_params=pltpu.CompilerParams(dimension_semantics=("arbitrary",),
                                             vmem_limit_bytes=LAYER_VMEM_LIMIT),
        name="prompt_layer",
    )(x, rope[0], rope[1], wl['w_in'], wl['w_out'], *smalls, wl['sinks'], *consts, p, *ffn_ws, wl['fnorms'])


N_HEADS = C_W // HEAD


def _sample_in_kernel(n_tok, x_ref, cos_ref, sin_ref, win_ref, nmix_ref, qk_ref, e_ref, z_ref, qx_ref, k_ref,
                      piece_ref):
    n_seq = x_ref.shape[0] // n_tok
    xn = _rmsnorm_rows(x_ref[...], nmix_ref[...])
    z = jnp.dot(xn.astype(bf16), win_ref[...], preferred_element_type=f32)
    z_ref[...] = z
    em = _mean_matrix(e_ref[...])
    c1 = 4 * A_W + 2 * B_W
    cos, sin = cos_ref[...], sin_ref[...]
    qn = jnp.concatenate([_head_norm(z[:, c1 + j * 256:c1 + (j + 1) * 256], em,
                                     jnp.concatenate([qk_ref[0:1, :]] * 2, axis=1)) for j in range(2)], axis=1)
    q = _rope(qn, jnp.concatenate([cos] * 4, axis=1), jnp.concatenate([sin] * 4, axis=1)) * (HEAD ** -0.5)
    k_ref[...] = _rope(_head_norm(z[:, c1 + C_W:c1 + C_W + KV_W], em[0:KV_W, 0:KV_W], qk_ref[1:2, :]), cos, sin)
    lo_f = lax.broadcasted_iota(jnp.int32, (x_ref.shape[0], KV_W), 1) < HEAD
    for h in range(N_HEADS):
        pair, g = q[:, (h // 2) * KV_W:(h // 2 + 1) * KV_W], h // 4
        own = jnp.where(lo_f if h % 2 == 0 else jnp.logical_not(lo_f), pair, 0.0)
        piece_ref[...] = own if h % 2 == g else pltpu.roll(own, HEAD, 1)
        for t in range(n_tok):
            qx_ref[pl.ds(h * n_tok + t, n_seq, stride=N_HEADS * n_tok), :] = piece_ref[pl.ds(t, n_seq, stride=n_tok), :]


def _sample_in(x, cos, sin, wl, l, e256, n_tok):
    n = x.shape[0]
    args = (x, cos, sin, wl['w_in'], wl['nmix'], wl['qk'], e256)
    specs = [_const_spec(a.shape) for a in args]
    specs[3:6] = [_layer_spec(a, l) for a in args[3:6]]
    return pl.pallas_call(
        functools.partial(_sample_in_kernel, n_tok), grid=(1,), in_specs=specs,
        out_specs=(_const_spec((n, IN_COLS)), _const_spec((n * N_HEADS, KV_W)), _const_spec((n, KV_W))),
        out_shape=(jax.ShapeDtypeStruct((n, IN_COLS), f32), jax.ShapeDtypeStruct((n * N_HEADS, KV_W), f32),
                   jax.ShapeDtypeStruct((n, KV_W), f32)),
        scratch_shapes=[pltpu.VMEM((n, KV_W), f32)],
        compiler_params=pltpu.CompilerParams(dimension_semantics=("arbitrary",), vmem_limit_bytes=VMEM_LIMIT),
        name="sample_in",
    )(*args)


def _sample_hgrn_kernel(n_tok, q_ref, f_ref, v_ref, s_ref, lbp_ref, o_ref, so_ref, qt_scr, ft_scr, kt_scr, vt_scr,
                        ot_scr):
    n_seq = s_ref.shape[1]
    for t in range(n_tok):
        zf = f_ref[pl.ds(t, n_seq, stride=n_tok), :]
        logf, ls = _log_forget(zf, lbp_ref[0:1, :], lbp_ref[1:2, :])
        ft_scr[t] = jnp.exp(logf).T
        kt_scr[t] = (lbp_ref[2:3, :] * jnp.exp(ls - zf)).T
        qt_scr[t] = q_ref[pl.ds(t, n_seq, stride=n_tok), :].T
        vt_scr[t] = v_ref[pl.ds(t, n_seq, stride=n_tok), :].T
        ot_scr[t] = jnp.zeros((2 * HEAD, n_seq), f32)

    def body(hk, carry):
        r0 = pl.multiple_of(hk * HEAD, HEAD)
        v0 = pl.multiple_of((hk // HEAD) * HEAD, HEAD)
        blk = s_ref[pl.ds(r0, HEAD), :]
        for t in range(n_tok):
            blk = blk * ft_scr[t, pl.ds(hk, 1), :] + kt_scr[t, pl.ds(hk, 1), :] * vt_scr[t, pl.ds(v0, HEAD), :]
            ot_scr[t, pl.ds(v0, HEAD), :] = ot_scr[t, pl.ds(v0, HEAD), :] + qt_scr[t, pl.ds(hk, 1), :] * blk
        so_ref[pl.ds(r0, HEAD), :] = blk
        return carry

    lax.fori_loop(0, 2 * HEAD, body, 0, unroll=4)
    for t in range(n_tok):
        o_ref[pl.ds(t, n_seq, stride=n_tok), :] = ot_scr[t].T


def _sample_hgrn(z, state, lbp, l, n_tok):
    n = z.shape[0]
    n_seq = state.shape[2]
    blk = 2 * HEAD * HEAD
    col = lambda j0: pl.BlockSpec((n, 2 * HEAD), lambda i, _j=j0: (0, _j + i))
    return pl.pallas_call(
        functools.partial(_sample_hgrn_kernel, n_tok), grid=(2,),
        in_specs=[col(0), col(2), col(4), pl.BlockSpec((None, blk, n_seq), lambda i, _l=l: (_l, i, 0)),
                  pl.BlockSpec((None, 3, 2 * HEAD), lambda i, _l=l: (_l, 0, i))],
        out_specs=(pl.BlockSpec((n, 2 * HEAD), lambda i: (0, i)), pl.BlockSpec((blk, n_seq), lambda i: (i, 0))),
        out_shape=(jax.ShapeDtypeStruct((n, A_W), f32), jax.ShapeDtypeStruct(state.shape[1:], f32)),
        scratch_shapes=[pltpu.VMEM((n_tok, 2 * HEAD, n_seq), f32)] * 5,
        compiler_params=pltpu.CompilerParams(dimension_semantics=("arbitrary",), vmem_limit_bytes=VMEM_LIMIT),
        name="sample_hgrn",
    )(z, z, z, state, lbp)


def _sample_conv_kernel(n_tok, first, u0_ref, u1_ref, g0_ref, g1_ref, buf_ref, cw_ref, cvec_ref, e_ref, *rest):
    ob0_ref, ob1_ref, new_ref = rest[-3:]
    slots = [new_ref.at[d] for d in range(new_ref.shape[0])] if first else [new_ref]
    n_seq = buf_ref.shape[1]
    n_hist = CONV_WIDTH - 1
    us = []
    for t in range(n_tok):
        rows = pl.ds(t, n_seq, stride=n_tok)
        u = jnp.concatenate([u0_ref[rows, :], u1_ref[rows, :]], axis=1)
        g = jnp.concatenate([g0_ref[rows, :], g1_ref[rows, :]], axis=1)
        us.append(u * _sigmoid(g))

    def slab(j):
        return buf_ref[j] if j < n_hist else us[j - n_hist]

    em = _mean_matrix(e_ref[...])
    for t in range(n_tok):
        y = cvec_ref[0:1, :]
        for j in range(CONV_WIDTH):
            y = y + slab(t + j) * cw_ref[j:j + 1, :]
        ob = _group_ln_silu(y, em, cvec_ref[1:2, :], cvec_ref[2:3, :])
        ob0_ref[pl.ds(t, n_seq, stride=n_tok), :] = ob[:, 0:KV_W]
        ob1_ref[pl.ds(t, n_seq, stride=n_tok), :] = ob[:, KV_W:B_W]
    for dst in slots:
        for j in range(n_hist):
            dst[j] = slab(j + n_tok)


def _sample_conv(z, buf, new_all, wl, l, e256, n_tok):
    n = z.shape[0]
    col = lambda j: pl.BlockSpec((n, KV_W), lambda i, _j=j: (0, _j))
    half = jax.ShapeDtypeStruct((n, KV_W), f32)
    in_specs = [col(8), col(9), col(10), col(11), _layer_spec(buf, l), _layer_spec(wl['cw'], l),
                _layer_spec(wl['cvec'], l), _const_spec(e256.shape)]
    args = [z, z, z, z, buf, wl['cw'], wl['cvec'], e256]
    if new_all is None:
        new_spec, aliases = _const_spec(buf.shape), {}
    else:
        in_specs.append(pl.BlockSpec(memory_space=pl.ANY))
        args.append(new_all)
        new_spec, aliases = _layer_spec(buf, l), {len(args) - 1: 2}
    return pl.pallas_call(
        functools.partial(_sample_conv_kernel, n_tok, new_all is None), grid=(1,), in_specs=in_specs,
        out_specs=(_const_spec((n, KV_W)), _const_spec((n, KV_W)), new_spec),
        out_shape=(half, half, jax.ShapeDtypeStruct(buf.shape, f32)),
        input_output_aliases=aliases,
        compiler_params=pltpu.CompilerParams(dimension_semantics=("arbitrary",), vmem_limit_bytes=VMEM_LIMIT),
        name="sample_conv",
    )(*args)


SEQ_BLOCK = 32


def _sample_attn_kernel(n_tok, qx_ref, k_ref, v_ref, kc_ref, vc_ref, sink_ref, bias_ref, ox_ref, kn_ref, vn_ref):
    q_rows = N_HEADS * n_tok
    sink, bias = sink_ref[...], bias_ref[...]
    seqs = range(SEQ_BLOCK)
    k_all = [jnp.concatenate([kc_ref[b], k_ref[b * n_tok:(b + 1) * n_tok, :]], axis=0) for b in seqs]
    v_all = [jnp.concatenate([vc_ref[b], v_ref[b * n_tok:(b + 1) * n_tok, :]], axis=0) for b in seqs]
    s = [lax.dot_general(qx_ref[b * q_rows:(b + 1) * q_rows, :].astype(bf16), k_all[b].astype(bf16),
                         (((1,), (1,)), ((), ())), preferred_element_type=f32) + bias for b in seqs]
    mx = [jnp.maximum(jnp.max(s[b], axis=-1, keepdims=True), sink) for b in seqs]
    pr = [jnp.exp(s[b] - mx[b]) for b in seqs]
    rden = [1.0 / (jnp.sum(pr[b], axis=-1, keepdims=True) + jnp.exp(sink - mx[b])) for b in seqs]
    for b in seqs:
        ox_ref[b * q_rows:(b + 1) * q_rows, :] = jnp.dot(pr[b].astype(bf16), v_all[b].astype(bf16),
                                                         preferred_element_type=f32) * rden[b]
    for b in seqs:
        kn_ref[b] = k_all[b][n_tok:, :]
        vn_ref[b] = v_all[b][n_tok:, :]


def _sample_attn(qx, k, z, kc, vc, sink_col, bias, l, n_tok):
    n_seq = kc.shape[1]
    assert n_seq % SEQ_BLOCK == 0
    rows = SEQ_BLOCK * n_tok
    q_rows = SEQ_BLOCK * N_HEADS * n_tok
    v_col = (4 * A_W + 2 * B_W + C_W + KV_W) // KV_W
    cache_in = pl.BlockSpec((None, SEQ_BLOCK) + kc.shape[2:], lambda i, _l=l: (_l, i, 0, 0))
    cache_out = pl.BlockSpec((SEQ_BLOCK,) + kc.shape[2:], lambda i: (i, 0, 0))
    return pl.pallas_call(
        functools.partial(_sample_attn_kernel, n_tok), grid=(n_seq // SEQ_BLOCK,),
        in_specs=[pl.BlockSpec((q_rows, KV_W), lambda i: (i, 0)), pl.BlockSpec((rows, KV_W), lambda i: (i, 0)),
                  pl.BlockSpec((rows, KV_W), lambda i: (i, v_col)), cache_in, cache_in,
                  _layer_spec(sink_col, l), _const_spec(bias.shape)],
        out_specs=(pl.BlockSpec((q_rows, KV_W), lambda i: (i, 0)), cache_out, cache_out),
        out_shape=(jax.ShapeDtypeStruct(qx.shape, f32), jax.ShapeDtypeStruct(kc.shape[1:], f32),
                   jax.ShapeDtypeStruct(vc.shape[1:], f32)),
        compiler_params=pltpu.CompilerParams(dimension_semantics=("arbitrary",), vmem_limit_bytes=VMEM_LIMIT),
        name="sample_attn",
    )(qx, k, z, kc, vc, sink_col, bias)


def _sample_out_kernel(n_tok, x_ref, oa_ref, g_ref, ob0_ref, ob1_ref, ox_ref, wout_ref, aon_ref, e_ref, h_ref,
                       oc_ref):
    n_seq = x_ref.shape[0] // n_tok
    gate = g_ref[...]
    oa = _head_norm(oa_ref[...], _mean_matrix(e_ref[...]), aon_ref[...]) * (gate * _sigmoid(gate))
    lo_f = lax.broadcasted_iota(jnp.int32, (n_seq, KV_W), 1) < HEAD
    for p in range(N_HEADS // 2):
        g = p // 2
        for t in range(n_tok):
            even = ox_ref[pl.ds((2 * p) * n_tok + t, n_seq, stride=N_HEADS * n_tok), :]
            odd = ox_ref[pl.ds((2 * p + 1) * n_tok + t, n_seq, stride=N_HEADS * n_tok), :]
            even = even if g == 0 else pltpu.roll(even, HEAD, 1)
            odd = odd if g == 1 else pltpu.roll(odd, HEAD, 1)
            oc_ref[p, pl.ds(t, n_seq, stride=n_tok), :] = jnp.where(lo_f, even, odd)
    mix = jnp.concatenate([oa, ob0_ref[...], ob1_ref[...]] + [oc_ref[p] for p in range(N_HEADS // 2)],
                          axis=1).astype(bf16)
    h_ref[...] = x_ref[...] + jnp.dot(mix, wout_ref[...], preferred_element_type=f32)


N_OUT_INPUTS = 9


def _sample_tail_kernel(n_tok, layer, *refs):
    n_in = N_OUT_INPUTS + N_FFN_INPUTS
    out_in, (p_ref, *w_hbm, norms_ref) = refs[:N_OUT_INPUTS], refs[N_OUT_INPUTS:n_in]
    y_ref, oc_ref, mid_ref, *w_vmem, sems = refs[n_in:]
    copies = [pltpu.make_async_copy(src.at[layer], dst, sems.at[k]) for k, (src, dst) in enumerate(zip(w_hbm, w_vmem))]
    for c in copies:
        c.start()
    _sample_out_kernel(n_tok, *out_in, mid_ref, oc_ref)
    _ffn_kernel(mid_ref, p_ref, *w_vmem, norms_ref, y_ref, before_use=[c.wait for c in copies])


def _sample_tail(x, oa, z, ob0, ob1, ox, p, wl, l, e256, n_tok):
    n = x.shape[0]
    args = (x, oa, z, ob0, ob1, ox, wl['w_out'], wl['aon'], e256)
    specs = [_const_spec(a.shape) for a in args]
    specs[2] = pl.BlockSpec((n, A_W), lambda i: (0, 3))
    specs[6:8] = [_layer_spec(a, l) for a in args[6:8]]
    ffn_ws = [wl['w_gate'], wl['w_up'], wl['w_down'], wl['w_ple_gate'], wl['w_ple_proj']]
    specs += ([_layer_spec(p, l)] + [pl.BlockSpec(memory_space=pl.ANY) for a in ffn_ws]
              + [_layer_spec(wl['fnorms'], l)])
    assert len(specs) == N_OUT_INPUTS + N_FFN_INPUTS
    return pl.pallas_call(
        functools.partial(_sample_tail_kernel, n_tok, l), grid=(1,), in_specs=specs,
        out_specs=_const_spec((n, D_MODEL)), out_shape=jax.ShapeDtypeStruct((n, D_MODEL), f32),
        scratch_shapes=[pltpu.VMEM((N_HEADS // 2, n, KV_W), f32), pltpu.VMEM((n, D_MODEL), f32)]
        + [pltpu.VMEM(a.shape[1:], a.dtype) for a in ffn_ws] + [pltpu.SemaphoreType.DMA((len(ffn_ws),))],
        compiler_params=pltpu.CompilerParams(dimension_semantics=("arbitrary",), vmem_limit_bytes=VMEM_LIMIT),
        name="sample_tail",
    )(*args, p, *ffn_ws, wl['fnorms'])


def _rope_cos_sin(pos):
    half = HEAD // 2
    inv = ROPE_THETA ** (-jnp.arange(half, dtype=f32) / half)
    ang = pos[:, None] * inv[None, :]
    cos, sin = jnp.cos(ang), jnp.sin(ang)
    return jnp.tile(cos, (1, 4)), jnp.concatenate([-sin, sin, -sin, sin], axis=1)


def _rope_prompt_tables(t_len):
    half = HEAD // 2
    inv = ROPE_THETA ** (-jnp.arange(half, dtype=f32) / half)
    sign = jnp.concatenate([-jnp.ones((half,), f32), jnp.ones((half,), f32)] * 2)
    ang_r = jnp.arange(TILE, dtype=f32)[:, None] * inv[None, :]
    ang_t = (jnp.arange(t_len // TILE, dtype=f32) * TILE)[:, None] * inv[None, :]
    cos_r, sin_r = jnp.tile(jnp.cos(ang_r), (1, 4)), jnp.tile(jnp.sin(ang_r), (1, 4))
    rows = jnp.stack([cos_r, sin_r, cos_r * sign, sin_r * sign])
    tiles = jnp.stack([jnp.tile(jnp.cos(ang_t), (1, 4)), jnp.tile(jnp.sin(ang_t), (1, 4))])
    return rows, tiles


def _stacked_weights(lbs, n_tok, w_in, a_onorm, conv_w, conv_b, conv_ln_g, conv_ln_b, q_norm, k_norm, sinks, w_out,
                     norm_mix, norm_ffn, w_gate, w_up, w_down, ple_norm, w_ple_gate, w_ple_proj):
    return dict(
        w_in=w_in.astype(bf16), w_out=w_out.astype(bf16), w_gate=w_gate.astype(bf16), w_up=w_up.astype(bf16),
        w_down=w_down.astype(bf16), w_ple_gate=w_ple_gate.astype(bf16), w_ple_proj=w_ple_proj.astype(bf16),
        nmix=norm_mix[:, None, :],
        lbp=jnp.stack([jnp.log(lbs), jnp.log1p(-lbs), 1.0 - lbs], axis=1),
        aon=jnp.tile(a_onorm, (1, 4))[:, None, :],
        cw=conv_w,
        cvec=jnp.stack([conv_b, conv_ln_g, conv_ln_b], axis=1),
        qk=jnp.stack([jnp.tile(q_norm, (1, 2)), jnp.tile(k_norm, (1, 2))], axis=1),
        sinks=sinks,
        sink_col=jnp.repeat(sinks, n_tok, axis=1)[:, :, None],
        fnorms=jnp.stack([norm_ffn, ple_norm], axis=1),
    )


def kernel(x_prompt, x_sample, state_hgrn, state_conv, cache_swa_k, cache_swa_v, p_prompt, p_sample, a_lower, w_in, a_onorm, conv_w, conv_b, conv_ln_g, conv_ln_b, q_norm, k_norm, sinks, w_out, norm_mix, norm_ffn, w_gate, w_up, w_down, ple_norm, w_ple_gate, w_ple_proj):
    depth = w_in.shape[0]
    t_len = x_prompt.shape[1]
    n_seq, n_tok = x_sample.shape[0], x_sample.shape[1]
    past_len = PAST_LEN
    assert x_prompt.shape[0] == 1 and t_len % TILE == 0

    lbs = jnp.cumsum(jax.nn.softmax(a_lower.astype(f32), axis=0), axis=0)
    lbs = lbs - lbs[0:1]
    consts_np = _np_constants()
    e256 = jnp.asarray(consts_np[0], bf16)
    consts = (e256, jnp.asarray(consts_np[1], bf16), jnp.asarray(consts_np[2], f32), jnp.asarray(consts_np[3], bf16),
              jnp.asarray(consts_np[4], f32))

    rope_p = _rope_prompt_tables(t_len)
    cos_s, sin_s = _rope_cos_sin(past_len + jnp.arange(n_tok, dtype=f32))
    cos_s, sin_s = jnp.tile(cos_s, (n_seq, 1)), jnp.tile(sin_s, (n_seq, 1))
    w_buf = cache_swa_k.shape[2]
    q_tok = (np.arange(N_HEADS * n_tok) % n_tok)[:, None]
    rel = q_tok + w_buf - np.arange(w_buf + n_tok)[None, :]
    bias_s = jnp.asarray(np.where((rel >= 0) & (rel <= WINDOW), 0.0, -np.inf), f32)

    wl = _stacked_weights(lbs, n_tok, w_in, a_onorm, conv_w, conv_b, conv_ln_g, conv_ln_b, q_norm, k_norm, sinks,
                          w_out, norm_mix, norm_ffn, w_gate, w_up, w_down, ple_norm, w_ple_gate, w_ple_proj)
    p_p = p_prompt.reshape(depth, t_len, PLE_DIM)
    p_s = p_sample.reshape(depth, n_seq * n_tok, PLE_DIM)
    s_hgrn = jnp.swapaxes(state_hgrn.reshape(depth, n_seq, -1), 1, 2)
    kc = cache_swa_k.reshape(depth, n_seq, w_buf, KV_W)
    vc = cache_swa_v.reshape(depth, n_seq, w_buf, KV_W)

    hp = x_prompt[0]
    hs = x_sample.reshape(n_seq * n_tok, D_MODEL)
    outs = [[] for _ in range(8)]
    conv_in = jnp.swapaxes(state_conv, 1, 2)
    conv_s = None
    for l in range(depth):
        hp, st, conv_new, k_new, v_new = _prompt_layer(hp, p_p, rope_p, wl, l, consts)
        s_new = jnp.stack([st[h * HEAD:(h + 1) * HEAD, h * HEAD:(h + 1) * HEAD].T for h in range(4)])
        outs[0].append(s_new[None])
        outs[1].append(conv_new[None])
        outs[2].append(k_new.reshape(1, WINDOW, 2, HEAD))
        outs[3].append(v_new.reshape(1, WINDOW, 2, HEAD))
        z, qx, k = _sample_in(hs, cos_s, sin_s, wl, l, e256, n_tok)
        oa, s_s = _sample_hgrn(z, s_hgrn, wl['lbp'], l, n_tok)
        ob0, ob1, conv_s = _sample_conv(z, conv_in, conv_s, wl, l, e256, n_tok)
        ox, k_s, v_s = _sample_attn(qx, k, z, kc, vc, wl['sink_col'], bias_s, l, n_tok)
        hs = _sample_tail(hs, oa, z, ob0, ob1, ox, p_s, wl, l, e256, n_tok)
        outs[4].append(s_s.T.reshape(n_seq, 4, HEAD, HEAD))
        outs[6].append(k_s.reshape(n_seq, w_buf, 2, HEAD))
        outs[7].append(v_s.reshape(n_seq, w_buf, 2, HEAD))
    stacked = [jnp.swapaxes(conv_s, 1, 2) if i == 5 else jnp.stack(o) for i, o in enumerate(outs)]
    return (hp[None], hs.reshape(n_seq, n_tok, D_MODEL)) + tuple(stacked)
```
